```python
import math
import jax, jax.numpy as jnp
from jax import lax
import numpy as np

D_MODEL = 2048
BATCH = 4
SEQ = 2048
DEPTH = 1
DEC_BATCH = 16
DEC_SEQ = 16
PAST_LEN = 1024

CHUNK = 64
Q_BLOCK = 128
A_HEADS = 8
A_QK_DIM = 64
A_V_DIM = 2 * A_QK_DIM
A_WIDTH = A_HEADS * A_V_DIM
A_QK_COLS = A_HEADS * 2 * A_QK_DIM
A_SCALE = A_QK_DIM ** -0.5
ROT_DIM = A_QK_DIM // 4
ROPE_THETA = 500000.0
SUBLN_EPS = 1e-5
P_WINDOWS = (2, 4, 8, 16)
P_GROUPS = len(P_WINDOWS)
P_GROUP_DIM = 256
P_WIDTH = P_GROUPS * P_GROUP_DIM
P_HIST = max(P_WINDOWS) - 1
M_TOKENS = 256
M_HEADS = 4
M_HEAD_DIM = 256
M_WIDTH = M_HEADS * M_HEAD_DIM
M_SCALE = M_HEAD_DIM ** -0.5
N_BRANCH = 3
DEEPNORM_ALPHA = (2.0 * DEPTH) ** 0.25
DEEPNORM_BETA = (8.0 * DEPTH) ** -0.25
LN_EPS = 1e-5
IN_SIZES = (A_QK_COLS, A_QK_COLS, A_WIDTH, A_WIDTH, P_WIDTH, P_WIDTH, M_WIDTH, M_WIDTH, N_BRANCH * D_MODEL)
SPLIT_POINTS = tuple(int(s) for s in np.cumsum(IN_SIZES)[:-1])
IN_COLS = int(sum(IN_SIZES))

kernel_name = "hybrid_diffattn_pool_memxattn_streaming_step"


def _lambda_init(layer_idx):
    return 0.8 - 0.6 * math.exp(-0.3 * layer_idx)


def _layer_norm(x, g, b):
    xf = x.astype(jnp.float32)
    mu = jnp.mean(xf, axis=-1, keepdims=True)
    var = jnp.mean(jnp.square(xf - mu), axis=-1, keepdims=True)
    y = (xf - mu) * lax.rsqrt(var + LN_EPS) * g.astype(jnp.float32) + b.astype(jnp.float32)
    return y.astype(x.dtype)


def _rotary(x, pos):
    half = ROT_DIM // 2
    inv = ROPE_THETA ** (-(jnp.arange(half, dtype=jnp.float32) * 2.0) / ROT_DIM)
    ang = pos.astype(jnp.float32)[:, None] * inv[None, :]
    shape = (pos.shape[0],) + (1,) * (x.ndim - 3) + (half,)
    cos = jnp.cos(ang).reshape(shape)
    sin = jnp.sin(ang).reshape(shape)
    xf = x.astype(jnp.float32)
    x1 = xf[..., :half]
    x2 = xf[..., half:ROT_DIM]
    out = jnp.concatenate([x1 * cos - x2 * sin, x2 * cos + x1 * sin, xf[..., ROT_DIM:]], axis=-1)
    return out.astype(x.dtype)


def _diff_attn(q, q_pos, k, v, k_pos, lam):
    s = jnp.einsum('bqhcd,bkhcd->bhcqk', q, k).astype(jnp.float32) * A_SCALE
    visible = (k_pos[None, :] // CHUNK) <= (q_pos[:, None] // CHUNK)
    s = jnp.where(visible, s, -jnp.inf)
    p = jax.nn.softmax(s, axis=-1)
    a = p[:, :, 0] - lam * p[:, :, 1]
    return jnp.einsum('bhqk,bkhe->bqhe', a.astype(v.dtype), v)


def _pool_mixer(u, hist, pos, pool_w, pool_scale):
    b, t, _ = u.shape
    ext = jnp.concatenate([hist, u], axis=1)
    cs = jnp.cumsum(jnp.pad(ext.astype(jnp.float32), ((0, 0), (1, 0), (0, 0))), axis=1)
    base = P_HIST + 1
    end = cs[:, base:base + t]
    parts = []
    for g, w in enumerate(P_WINDOWS):
        sl = slice(g * P_GROUP_DIM, (g + 1) * P_GROUP_DIM)
        tot = end[..., sl] - cs[:, base - w:base - w + t, sl]
        cnt = jnp.minimum(pos + 1, w).astype(jnp.float32)[:, None]
        parts.append(tot / cnt)
    pooled = jnp.concatenate(parts, axis=-1) - u.astype(jnp.float32)
    mixed = jnp.einsum('btgc,gcd->btgd', pooled.reshape(b, t, P_GROUPS, P_GROUP_DIM),
                       pool_w.astype(jnp.float32)).reshape(b, t, P_WIDTH)
    mixed = mixed * pool_scale.astype(jnp.float32)
    return mixed.astype(u.dtype), ext[:, -P_HIST:]


def _mem_attn(q, mk, mv):
    s = jnp.einsum('bqhd,bmhd->bhqm', q, mk).astype(jnp.float32) * M_SCALE
    p = jax.nn.softmax(s, axis=-1)
    return jnp.einsum('bhqm,bmhd->bqhd', p.astype(mv.dtype), mv)


def _mixer_layer(x, pos, k_hist, v_hist, pool_hist, mem_k, mem_v, w_in, lam, lam_init,
                 subln_g, pool_w, pool_scale, w_branch, w_out, ln_g, ln_b):
    b, t, _ = x.shape
    h = jnp.einsum('btd,dn->btn', x, w_in)
    qa, ka, va, za, up, zp, qm, zm, gm = jnp.split(h, SPLIT_POINTS, axis=-1)

    q = _rotary(qa.reshape(b, t, A_HEADS, 2, A_QK_DIM), pos)
    k = _rotary(ka.reshape(b, t, A_HEADS, 2, A_QK_DIM), pos)
    v = va.reshape(b, t, A_HEADS, A_V_DIM)
    k_all = k if k_hist is None else jnp.concatenate([k_hist, k], axis=1)
    v_all = v if v_hist is None else jnp.concatenate([v_hist, v], axis=1)
    k_pos = jnp.arange(k_all.shape[1])
    if t > Q_BLOCK:
        nb = t // Q_BLOCK
        qb = jnp.moveaxis(q.reshape(b, nb, Q_BLOCK, A_HEADS, 2, A_QK_DIM), 1, 0)
        pb = pos.reshape(nb, Q_BLOCK)
        ob = lax.map(lambda a: _diff_attn(a[0], a[1], k_all, v_all, k_pos, lam), (qb, pb))
        o = jnp.moveaxis(ob, 0, 1).reshape(b, t, A_HEADS, A_V_DIM)
    else:
        o = _diff_attn(q, pos, k_all, v_all, k_pos, lam)
    of = o.astype(jnp.float32)
    of = of * lax.rsqrt(jnp.mean(jnp.square(of), axis=-1, keepdims=True) + SUBLN_EPS)
    of = of * subln_g.astype(jnp.float32) * (1.0 - lam_init)
    o_a = of.astype(x.dtype).reshape(b, t, A_WIDTH) * jax.nn.silu(za)

    if pool_hist is None:
        pool_hist = jnp.zeros((b, P_HIST, P_WIDTH), up.dtype)
    o_p_raw, pool_state = _pool_mixer(up, pool_hist, pos, pool_w, pool_scale)
    o_p = o_p_raw * jax.nn.silu(zp)

    o_m = _mem_attn(qm.reshape(b, t, M_HEADS, M_HEAD_DIM), mem_k, mem_v).reshape(b, t, M_WIDTH)
    o_m = o_m * jax.nn.silu(zm)

    branches = jnp.stack([o_a, o_p, o_m], axis=2)
    proj = jnp.einsum('btnc,ncd->btnd', branches, w_branch)
    gate = jax.nn.sigmoid(gm.reshape(b, t, N_BRANCH, D_MODEL))
    merged = jnp.sum(gate * proj, axis=2)
    out = jnp.einsum('btd,de->bte', merged, w_out)
    y = _layer_norm(DEEPNORM_ALPHA * x + out, ln_g, ln_b)
    return y, k, v, pool_state


def setup_inputs(seed: int = 0) -> dict:
    key = jax.random.key(seed)
    ks = jax.random.split(key, 24)
    nrm = lambda k, s: jax.random.normal(k, s, dtype=jnp.float32)
    return {
        "x_prompt": nrm(ks[0], (BATCH, SEQ, D_MODEL)),
        "x_sample": nrm(ks[1], (DEC_BATCH, DEC_SEQ, D_MODEL)),
        "cache_attn_k": nrm(ks[2], (DEPTH, DEC_BATCH, PAST_LEN, A_HEADS, 2, A_QK_DIM)),
        "cache_attn_v": nrm(ks[3], (DEPTH, DEC_BATCH, PAST_LEN, A_HEADS, A_V_DIM)),
        "cache_mem_k": nrm(ks[4], (DEPTH, DEC_BATCH, M_TOKENS, M_HEADS, M_HEAD_DIM)),
        "cache_mem_v": nrm(ks[5], (DEPTH, DEC_BATCH, M_TOKENS, M_HEADS, M_HEAD_DIM)),
        "state_pool": nrm(ks[6], (DEPTH, DEC_BATCH, P_HIST, P_WIDTH)),
        "mem_prompt": nrm(ks[7], (BATCH, M_TOKENS, D_MODEL)),
        "w_in": nrm(ks[8], (DEPTH, D_MODEL, IN_COLS)) * D_MODEL ** -0.5,
        "w_mem_kv": nrm(ks[9], (DEPTH, D_MODEL, 2 * M_WIDTH)) * D_MODEL ** -0.5,
        "lambda_q1": nrm(ks[10], (DEPTH, A_QK_DIM)) * 0.1,
        "lambda_k1": nrm(ks[11], (DEPTH, A_QK_DIM)) * 0.1,
        "lambda_q2": nrm(ks[12], (DEPTH, A_QK_DIM)) * 0.1,
        "lambda_k2": nrm(ks[13], (DEPTH, A_QK_DIM)) * 0.1,
        "subln_gain": 1.0 + 0.01 * nrm(ks[14], (DEPTH, A_V_DIM)),
        "pool_w": nrm(ks[15], (DEPTH, P_GROUPS, P_GROUP_DIM, P_GROUP_DIM)) * P_GROUP_DIM ** -0.5,
        "pool_scale": 1.0 + 0.1 * nrm(ks[16], (DEPTH, P_WIDTH)),
        "w_branch": nrm(ks[17], (DEPTH, N_BRANCH, A_WIDTH, D_MODEL)) * (A_WIDTH ** -0.5 * DEEPNORM_BETA),
        "w_out": nrm(ks[18], (DEPTH, D_MODEL, D_MODEL)) * (D_MODEL ** -0.5 * DEEPNORM_BETA),
        "ln_gain": 1.0 + 0.01 * nrm(ks[19], (DEPTH, D_MODEL)),
        "ln_bias": 0.01 * nrm(ks[20], (DEPTH, D_MODEL)),
    }


def reference(x_prompt, x_sample, cache_attn_k, cache_attn_v, cache_mem_k, cache_mem_v, state_pool,
              mem_prompt, w_in, w_mem_kv, lambda_q1, lambda_k1, lambda_q2, lambda_k2, subln_gain,
              pool_w, pool_scale, w_branch, w_out, ln_gain, ln_bias):
    t_p = x_prompt.shape[1]
    t_s = x_sample.shape[1]
    past = cache_attn_k.shape[2]
    pos_p = jnp.arange(t_p)
    pos_s = past + jnp.arange(t_s)
    b_p = mem_prompt.shape[0]

    y_p = x_prompt
    y_s = x_sample
    kp_l, vp_l, mkp_l, mvp_l, pp_l, ks_l, vs_l, ps_l = [], [], [], [], [], [], [], []
    for l in range(DEPTH):
        lam_init = _lambda_init(l)
        lam = (jnp.exp(jnp.sum(lambda_q1[l] * lambda_k1[l]).astype(jnp.float32))
               - jnp.exp(jnp.sum(lambda_q2[l] * lambda_k2[l]).astype(jnp.float32)) + lam_init)
        mkv = jnp.einsum('bmd,dn->bmn', mem_prompt, w_mem_kv[l])
        mk, mv = jnp.split(mkv, 2, axis=-1)
        mk = mk.reshape(b_p, M_TOKENS, M_HEADS, M_HEAD_DIM)
        mv = mv.reshape(b_p, M_TOKENS, M_HEADS, M_HEAD_DIM)
        y_p, kp, vp, pp = _mixer_layer(y_p, pos_p, None, None, None, mk, mv, w_in[l], lam, lam_init,
                                       subln_gain[l], pool_w[l], pool_scale[l], w_branch[l], w_out[l],
                                       ln_gain[l], ln_bias[l])
        y_s, ks, vs, ps = _mixer_layer(y_s, pos_s, cache_attn_k[l], cache_attn_v[l], state_pool[l],
                                       cache_mem_k[l], cache_mem_v[l], w_in[l], lam, lam_init,
                                       subln_gain[l], pool_w[l], pool_scale[l], w_branch[l], w_out[l],
                                       ln_gain[l], ln_bias[l])
        kp_l.append(kp); vp_l.append(vp); mkp_l.append(mk); mvp_l.append(mv); pp_l.append(pp)
        ks_l.append(ks); vs_l.append(vs); ps_l.append(ps)

    return (y_p, y_s, jnp.stack(kp_l), jnp.stack(vp_l), jnp.stack(mkp_l), jnp.stack(mvp_l), jnp.stack(pp_l),
            jnp.stack(ks_l), jnp.stack(vs_l), jnp.stack(ps_l))
```

```python
import functools
import math

import numpy as np
import jax
import jax.numpy as jnp
from jax import lax
from jax.experimental import pallas as pl
from jax.experimental.pallas import tpu as pltpu

F32 = jnp.float32
BF16 = jnp.bfloat16

V7X_VMEM_LIMIT_BYTES = 56 * 1024 * 1024
LANES = 128

D_MODEL = 2048
CHUNK = 64
A_HEADS = 8
A_QK_DIM = 64
A_V_DIM = 128
A_WIDTH = A_HEADS * A_V_DIM
A_SCALE = A_QK_DIM ** -0.5
ROT_DIM = A_QK_DIM // 4
ROPE_THETA = 500000.0
SUBLN_EPS = 1e-5
P_WINDOWS = (2, 4, 8, 16)
P_GROUP_DIM = 256
P_WIDTH = len(P_WINDOWS) * P_GROUP_DIM
P_HIST = max(P_WINDOWS) - 1
M_TOKENS = 256
M_HEADS = 4
M_HEAD_DIM = 256
M_WIDTH = M_HEADS * M_HEAD_DIM
M_SCALE = M_HEAD_DIM ** -0.5
N_BRANCH = 3
DEPTH = 1
DEEPNORM_ALPHA = (2.0 * DEPTH) ** 0.25
LN_EPS = 1e-5
LAMBDA_INIT = 0.8 - 0.6 * math.exp(-0.3 * 0)

COL_Q, COL_K, COL_V, COL_ZA, COL_UP, COL_ZP = 0, 1024, 2048, 3072, 4096, 5120
IN_COLS = 14336
H_ZA, H_ZP, H_QM, H_ZM, H_GM = 0, 1024, 2048, 3072, 4096
H_COLS = 10240


def _params(*sem):
    return pltpu.CompilerParams(dimension_semantics=sem, vmem_limit_bytes=V7X_VMEM_LIMIT_BYTES)


def _sigmoid(z):
    return 1.0 / (1.0 + jnp.exp(-z))


def _silu(z):
    return z * _sigmoid(z)


def _proj_kernel(x_ref, w_ref, *rest, rotary, scale):
    acc = jnp.dot(x_ref[...], w_ref[...], preferred_element_type=F32)
    if rotary:
        cos_ref, sa_ref, sb_ref, o_ref = rest
        c, sa, sb = cos_ref[...], sa_ref[...], sb_ref[...]
        half = ROT_DIM // 2
        parts = []
        for k in range(acc.shape[1] // LANES):
            blk = acc[:, k * LANES:(k + 1) * LANES]
            lo = pltpu.roll(blk, half, 1)
            hi = pltpu.roll(blk, LANES - half, 1)
            parts.append(blk * c + lo * sa + hi * sb)
        acc = jnp.concatenate(parts, axis=1)
    else:
        (o_ref,) = rest
    if scale != 1.0:
        acc = acc * scale
    o_ref[...] = acc.astype(o_ref.dtype)


def _proj(x, w, col_block, ncols, out_dtype, tm, tn, rot=None, scale=1.0, name="proj"):
    n, kdim = x.shape
    grid = (n // tm, ncols // tn)
    in_specs = [
        pl.BlockSpec((tm, kdim), lambda i, j: (i, 0)),
        pl.BlockSpec((kdim, tn), lambda i, j: (0, col_block(j))),
    ]
    args = [x, w]
    if rot is not None:
        nrep = rot[0].shape[0] // tm
        for t in rot:
            in_specs.append(pl.BlockSpec((tm, LANES), lambda i, j: (i % nrep, 0)))
            args.append(t)
    return pl.pallas_call(
        functools.partial(_proj_kernel, rotary=rot is not None, scale=scale),
        out_shape=jax.ShapeDtypeStruct((n, ncols), out_dtype),
        grid=grid,
        in_specs=in_specs,
        out_specs=pl.BlockSpec((tm, tn), lambda i, j: (i, j)),
        compiler_params=_params("parallel", "arbitrary"),
        name=name,
    )(*args)


def _rotary_tables(pos):
    half = ROT_DIM // 2
    inv = ROPE_THETA ** (-(jnp.arange(half, dtype=F32) * 2.0) / ROT_DIM)
    ang = pos.astype(F32)[:, None] * inv[None, :]
    cos, sin = jnp.cos(ang), jnp.sin(ang)
    t = pos.shape[0]
    ones = jnp.ones((t, A_QK_DIM - ROT_DIM), F32)
    zeros = jnp.zeros((t, A_QK_DIM - ROT_DIM), F32)
    zh = jnp.zeros((t, half), F32)
    c64 = jnp.concatenate([cos, cos, ones], axis=1)
    sa64 = jnp.concatenate([zh, sin, zeros], axis=1)
    sb64 = jnp.concatenate([-sin, zh, zeros], axis=1)
    tile = lambda a: jnp.concatenate([a, a], axis=1)
    return tile(c64), tile(sa64), tile(sb64)


def _lambda_value(lam_ref):
    lv = lam_ref[...]
    a = jnp.sum(lv[0:1] * lv[1:2], axis=1, keepdims=True)
    b = jnp.sum(lv[2:3] * lv[3:4], axis=1, keepdims=True)
    return jnp.exp(a) - jnp.exp(b) + LAMBDA_INIT


def _split_components(q):
    lane = lax.broadcasted_iota(jnp.int32, q.shape, 1)
    qf = q.astype(F32)
    return jnp.concatenate([jnp.where(lane < A_QK_DIM, qf, 0.0),
                            jnp.where(lane >= A_QK_DIM, qf, 0.0)], axis=0).astype(q.dtype)


def _attn_finish(acc, l, tq, lam, gain, za):
    o = acc[:tq] / l[:tq] - lam * (acc[tq:] / l[tq:])
    o = o * lax.rsqrt(jnp.mean(o * o, axis=1, keepdims=True) + SUBLN_EPS)
    o = o * gain * (1.0 - LAMBDA_INIT)
    return o * _silu(za)


def _attn_prompt_kernel(q_ref, k_ref, v_ref, za_ref, gain_ref, lam_ref, o_ref, kb_ref, vb_ref, *, tq):
    i = pl.program_id(2)

    @pl.when(i == 0)
    def _():
        kb_ref[...] = k_ref[...].astype(BF16)
        vb_ref[...] = v_ref[...].astype(BF16)

    qq = _split_components(q_ref[...])
    nt = (((1,), (1,)), ((), ()))

    def step(j, carry, mask):
        m, l, acc = carry
        start = pl.multiple_of(j * tq, tq)
        s = lax.dot_general(qq, kb_ref[pl.ds(start, tq), :], nt, preferred_element_type=F32)
        if mask is not None:
            s = jnp.where(mask, s, -jnp.inf)
        m_new = jnp.maximum(m, jnp.max(s, axis=1, keepdims=True))
        alpha = jnp.exp(m - m_new)
        p = jnp.exp(s - m_new)
        l = alpha * l + jnp.sum(p, axis=1, keepdims=True)
        acc = alpha * acc + jnp.dot(p.astype(BF16), vb_ref[pl.ds(start, tq), :],
                                    preferred_element_type=F32)
        return m_new, l, acc

    init = (jnp.full((2 * tq, 1), -jnp.inf, F32), jnp.zeros((2 * tq, 1), F32),
            jnp.zeros((2 * tq, A_V_DIM), F32))
    carry = lax.fori_loop(0, i, lambda j, c: step(j, c, None), init)
    qrow = lax.broadcasted_iota(jnp.int32, (2 * tq, tq), 0) % tq
    kcol = lax.broadcasted_iota(jnp.int32, (2 * tq, tq), 1)
    _, l, acc = step(i, carry, (kcol // CHUNK) <= (qrow // CHUNK))
    out = _attn_finish(acc, l, tq, _lambda_value(lam_ref), gain_ref[...], za_ref[...].astype(F32))
    o_ref[...] = out.astype(o_ref.dtype)


def _attn_prompt(q, k, v, h, gain, lamv, batch, t, tq=256):
    nq = t // tq
    n = batch * t
    return pl.pallas_call(
        functools.partial(_attn_prompt_kernel, tq=tq),
        out_shape=jax.ShapeDtypeStruct((n, A_WIDTH), BF16),
        grid=(batch, A_HEADS, nq),
        in_specs=[
            pl.BlockSpec((tq, LANES), lambda b, hh, i: (b * nq + i, hh)),
            pl.BlockSpec((t, LANES), lambda b, hh, i: (b, hh)),
            pl.BlockSpec((t, LANES), lambda b, hh, i: (b, hh)),
            pl.BlockSpec((tq, LANES), lambda b, hh, i: (b * nq + i, H_ZA // LANES + hh)),
            pl.BlockSpec((1, LANES), lambda b, hh, i: (0, 0)),
            pl.BlockSpec((4, A_QK_DIM), lambda b, hh, i: (0, 0)),
        ],
        out_specs=pl.BlockSpec((tq, LANES), lambda b, hh, i: (b * nq + i, hh)),
        scratch_shapes=[pltpu.VMEM((t, LANES), BF16), pltpu.VMEM((t, LANES), BF16)],
        compiler_params=_params("arbitrary", "arbitrary", "arbitrary"),
        name="attn_prompt",
    )(q, k, v, h, gain, lamv)


def _chunk_mask(s, ts, q_pos0, k_pos0):
    q_pos = q_pos0 + lax.broadcasted_iota(jnp.int32, s.shape, 0) % ts
    k_pos = k_pos0 + lax.broadcasted_iota(jnp.int32, s.shape, 1)
    return jnp.where((k_pos // CHUNK) <= (q_pos // CHUNK), s, -jnp.inf)


def _attn_sample_kernel(q_ref, kc_ref, vc_ref, kn_ref, vn_ref, za_ref, gain_ref, lam_ref, o_ref, *,
                        ts, past, mask_cache, mask_new):
    qq = _split_components(q_ref[...])
    nt = (((1,), (1,)), ((), ()))
    s_c = lax.dot_general(qq, kc_ref[...].astype(BF16), nt, preferred_element_type=F32)
    s_n = lax.dot_general(qq, kn_ref[...].astype(BF16), nt, preferred_element_type=F32)
    if mask_cache:
        s_c = _chunk_mask(s_c, ts, past, 0)
    if mask_new:
        s_n = _chunk_mask(s_n, ts, past, past)
    m = jnp.maximum(jnp.max(s_c, axis=1, keepdims=True), jnp.max(s_n, axis=1, keepdims=True))
    p_c = jnp.exp(s_c - m)
    p_n = jnp.exp(s_n - m)
    l = jnp.sum(p_c, axis=1, keepdims=True) + jnp.sum(p_n, axis=1, keepdims=True)
    acc = (jnp.dot(p_c.astype(BF16), vc_ref[...].astype(BF16), preferred_element_type=F32)
           + jnp.dot(p_n.astype(BF16), vn_ref[...].astype(BF16), preferred_element_type=F32))
    out = _attn_finish(acc, l, ts, _lambda_value(lam_ref), gain_ref[...], za_ref[...].astype(F32))
    o_ref[...] = out.astype(o_ref.dtype)


def _needs_mask(q_pos, k_pos):
    return not bool(((k_pos[None, :] // CHUNK) <= (q_pos[:, None] // CHUNK)).all())


def _attn_sample(q, kc, vc, kn, vn, h, gain, lamv, batch, ts, past):
    q_pos = past + np.arange(ts)
    return pl.pallas_call(
        functools.partial(_attn_sample_kernel, ts=ts, past=past,
                          mask_cache=_needs_mask(q_pos, np.arange(past)),
                          mask_new=_needs_mask(q_pos, q_pos)),
        out_shape=jax.ShapeDtypeStruct((batch * ts, A_WIDTH), BF16),
        grid=(batch, A_HEADS),
        in_specs=[
            pl.BlockSpec((ts, LANES), lambda b, hh: (b, hh)),
            pl.BlockSpec((past, LANES), lambda b, hh: (b, hh)),
            pl.BlockSpec((past, LANES), lambda b, hh: (b, hh)),
            pl.BlockSpec((ts, LANES), lambda b, hh: (b, hh)),
            pl.BlockSpec((ts, LANES), lambda b, hh: (b, hh)),
            pl.BlockSpec((ts, LANES), lambda b, hh: (b, H_ZA // LANES + hh)),
            pl.BlockSpec((1, LANES), lambda b, hh: (0, 0)),
            pl.BlockSpec((4, A_QK_DIM), lambda b, hh: (0, 0)),
        ],
        out_specs=pl.BlockSpec((ts, LANES), lambda b, hh: (b, hh)),
        compiler_params=_params("arbitrary", "arbitrary"),
        name="attn_sample",
    )(q, kc, vc, kn, vn, h, gain, lamv)


def _pool_kernel(u_ref, uprev_ref, hist_ref, zp_ref, pw_ref, ps_ref, o_ref, *, tm, pos0):
    i = pl.program_id(1)
    u = u_ref[...]
    prev = jnp.where(i == 0, hist_ref[...], uprev_ref[...])
    ext = jnp.concatenate([prev, u], axis=0)
    hist_rows = prev.shape[0]
    pos = pos0 + i * tm + lax.broadcasted_iota(jnp.int32, (tm, 1), 0)
    parts = []
    for g, w in enumerate(P_WINDOWS):
        sl = slice(g * P_GROUP_DIM, (g + 1) * P_GROUP_DIM)
        tot = ext[:, sl]
        span = 1
        while span < w:
            tot = tot + pltpu.roll(tot, span, 0)
            span *= 2
        tot = tot[hist_rows:]
        cnt = jnp.minimum(pos + 1, w).astype(F32)
        pooled = tot / cnt - u[:, sl]
        parts.append(jnp.dot(pooled.astype(BF16), pw_ref[g], preferred_element_type=F32))
    mixed = jnp.concatenate(parts, axis=1) * ps_ref[...]
    o_ref[...] = (mixed * _silu(zp_ref[...].astype(F32))).astype(o_ref.dtype)


def _pool(u, hist, h, pool_w, pool_scale, batch, t, tm, pos0):
    nt = t // tm
    hr = hist.shape[1]
    per = tm // hr
    return pl.pallas_call(
        functools.partial(_pool_kernel, tm=tm, pos0=pos0),
        out_shape=jax.ShapeDtypeStruct((batch * t, P_WIDTH), BF16),
        grid=(batch, nt),
        in_specs=[
            pl.BlockSpec((tm, P_WIDTH), lambda b, i: (b * nt + i, 0)),
            pl.BlockSpec((hr, P_WIDTH), lambda b, i: (jnp.maximum((b * nt + i) * per - 1, 0), 0)),
            pl.BlockSpec((None, hr, P_WIDTH), lambda b, i: (b, 0, 0)),
            pl.BlockSpec((tm, P_WIDTH), lambda b, i: (b * nt + i, H_ZP // P_WIDTH)),
            pl.BlockSpec((len(P_WINDOWS), P_GROUP_DIM, P_GROUP_DIM), lambda b, i: (0, 0, 0)),
            pl.BlockSpec((1, P_WIDTH), lambda b, i: (0, 0)),
        ],
        out_specs=pl.BlockSpec((tm, P_WIDTH), lambda b, i: (b * nt + i, 0)),
        compiler_params=_params("arbitrary", "arbitrary"),
        name="pool",
    )(u, u, hist, h, pool_w, pool_scale)


def _mem_attn_kernel(q_ref, mk_ref, mv_ref, zm_ref, o_ref):
    nt = (((1,), (1,)), ((), ()))
    s = lax.dot_general(q_ref[...], mk_ref[...].astype(BF16), nt, preferred_element_type=F32) * M_SCALE
    m = jnp.max(s, axis=1, keepdims=True)
    p = jnp.exp(s - m)
    l = jnp.sum(p, axis=1, keepdims=True)
    o = jnp.dot(p.astype(BF16), mv_ref[...].astype(BF16), preferred_element_type=F32) / l
    o_ref[...] = (o * _silu(zm_ref[...].astype(F32))).astype(o_ref.dtype)


def _mem_attn(h, mk, mv, batch, t, tq):
    nq = t // tq
    d = M_HEAD_DIM
    return pl.pallas_call(
        _mem_attn_kernel,
        out_shape=jax.ShapeDtypeStruct((batch * t, M_WIDTH), BF16),
        grid=(batch, M_HEADS, nq),
        in_specs=[
            pl.BlockSpec((tq, d), lambda b, hh, i: (b * nq + i, H_QM // d + hh)),
            pl.BlockSpec((M_TOKENS, d), lambda b, hh, i: (b, hh)),
            pl.BlockSpec((M_TOKENS, d), lambda b, hh, i: (b, hh)),
            pl.BlockSpec((tq, d), lambda b, hh, i: (b * nq + i, H_ZM // d + hh)),
        ],
        out_specs=pl.BlockSpec((tq, d), lambda b, hh, i: (b * nq + i, hh)),
        compiler_params=_params("arbitrary", "arbitrary", "arbitrary"),
        name="mem_attn",
    )(h, mk, mv, h)


def _merge_kernel(ba_ref, bp_ref, bm_ref, g0_ref, g1_ref, g2_ref, wb_ref, o_ref):
    acc = None
    for n, (b_ref, g_ref) in enumerate(((ba_ref, g0_ref), (bp_ref, g1_ref), (bm_ref, g2_ref))):
        proj = jnp.dot(b_ref[...], wb_ref[n], preferred_element_type=F32)
        term = _sigmoid(g_ref[...].astype(F32)) * proj
        acc = term if acc is None else acc + term
    o_ref[...] = acc.astype(o_ref.dtype)


def _merge(ba, bp, bm, h, wb, tm, tn):
    n = ba.shape[0]
    width = ba.shape[1]
    gspec = lambda k: pl.BlockSpec((tm, tn), lambda j, i: (i, (H_GM + k * D_MODEL) // tn + j))
    bspec = pl.BlockSpec((tm, width), lambda j, i: (i, 0))
    return pl.pallas_call(
        _merge_kernel,
        out_shape=jax.ShapeDtypeStruct((n, D_MODEL), BF16),
        grid=(D_MODEL // tn, n // tm),
        in_specs=[bspec, bspec, bspec, gspec(0), gspec(1), gspec(2),
                  pl.BlockSpec((N_BRANCH, width, tn), lambda j, i: (0, 0, j))],
        out_specs=pl.BlockSpec((tm, tn), lambda j, i: (i, j)),
        compiler_params=_params("arbitrary", "arbitrary"),
        name="merge",
    )(ba, bp, bm, h, h, h, wb)


def _out_kernel(m_ref, w_ref, x_ref, g_ref, b_ref, o_ref):
    z = DEEPNORM_ALPHA * x_ref[...] + jnp.dot(m_ref[...], w_ref[...], preferred_element_type=F32)
    mu = jnp.mean(z, axis=1, keepdims=True)
    zc = z - mu
    var = jnp.mean(zc * zc, axis=1, keepdims=True)
    o_ref[...] = zc * lax.rsqrt(var + LN_EPS) * g_ref[...] + b_ref[...]


def _out(merged, w_out, x, ln_g, ln_b, tm):
    n = merged.shape[0]
    row = pl.BlockSpec((tm, D_MODEL), lambda i: (i, 0))
    vec = pl.BlockSpec((1, D_MODEL), lambda i: (0, 0))
    return pl.pallas_call(
        _out_kernel,
        out_shape=jax.ShapeDtypeStruct((n, D_MODEL), F32),
        grid=(n // tm,),
        in_specs=[row, pl.BlockSpec((D_MODEL, D_MODEL), lambda i: (0, 0)), row, vec, vec],
        out_specs=row,
        compiler_params=_params("arbitrary"),
        name="out_ln",
    )(merged, w_out, x, ln_g, ln_b)


def _rest_col_block(tn):
    za_blocks = (COL_UP - COL_ZA) // tn
    return lambda j: jnp.where(j < za_blocks, COL_ZA // tn + j, COL_ZP // tn + (j - za_blocks))


def _layer(x2d, batch, t, pos0, w, k_hist, v_hist, pool_hist, mk, mv, tm):
    xb = x2d.astype(BF16)
    pos = pos0 + jnp.arange(t)
    reps = max(tm // t, 1)
    rot = tuple(jnp.tile(a, (reps, 1)) for a in _rotary_tables(pos))
    tn = 1024
    seg = lambda start: (lambda j: start // tn + j)
    q = _proj(xb, w["w_in"], seg(COL_Q), 1024, BF16, tm, tn, rot=rot, scale=A_SCALE, name="proj_q")
    k = _proj(xb, w["w_in"], seg(COL_K), 1024, F32, tm, tn, rot=rot, name="proj_k")
    v = _proj(xb, w["w_in"], seg(COL_V), 1024, F32, tm, tn, name="proj_v")
    u = _proj(xb, w["w_in"], seg(COL_UP), 1024, F32, tm, tn, name="proj_u")
    h = _proj(xb, w["w_in"], _rest_col_block(tn), H_COLS, BF16, tm, tn, name="proj_rest")

    if k_hist is None:
        b_a = _attn_prompt(q, k, v, h, w["gain"], w["lamv"], batch, t)
    else:
        b_a = _attn_sample(q, k_hist, v_hist, k, v, h, w["gain"], w["lamv"], batch, t, k_hist.shape[0] // batch)
    b_p = _pool(u, pool_hist, h, w["pool_w"], w["pool_scale"], batch, t, min(t, 512), pos0)
    b_m = _mem_attn(h, mk, mv, batch, t, min(t, 512))
    merged = _merge(b_a, b_p, b_m, h, w["w_branch"], min(tm, 512), 1024)
    y = _out(merged, w["w_out"], x2d, w["ln_g"], w["ln_b"], min(tm, 512))
    return y, k, v, u


def kernel(x_prompt, x_sample, cache_attn_k, cache_attn_v, cache_mem_k, cache_mem_v, state_pool, mem_prompt, w_in, w_mem_kv, lambda_q1, lambda_k1, lambda_q2, lambda_k2, subln_gain, pool_w, pool_scale, w_branch, w_out, ln_gain, ln_bias):
    bp, tp, d = x_prompt.shape
    bs, ts, _ = x_sample.shape
    past = cache_attn_k.shape[2]
    hist_rows = P_HIST + 1

    def run_layer(l, xp2d, xs2d):
        w = {
            "w_in": w_in[l].astype(BF16),
            "gain": subln_gain[l].reshape(1, A_V_DIM),
            "lamv": jnp.stack([lambda_q1[l], lambda_k1[l], lambda_q2[l], lambda_k2[l]]),
            "pool_w": pool_w[l].astype(BF16),
            "pool_scale": pool_scale[l].reshape(1, P_WIDTH),
            "w_branch": w_branch[l].astype(BF16),
            "w_out": w_out[l].astype(BF16),
            "ln_g": ln_gain[l].reshape(1, d),
            "ln_b": ln_bias[l].reshape(1, d),
        }
        memb = mem_prompt.reshape(bp * M_TOKENS, d).astype(BF16)
        wkv = w_mem_kv[l].astype(BF16)
        mk = _proj(memb, wkv, lambda j: j, M_WIDTH, F32, 512, 1024, name="proj_mk")
        mv = _proj(memb, wkv, lambda j: M_WIDTH // 1024 + j, M_WIDTH, F32, 512, 1024, name="proj_mv")

        zero_hist = jnp.zeros((bp, hist_rows, P_WIDTH), F32)
        y_p, k_p, v_p, u_p = _layer(xp2d, bp, tp, 0, w, None, None, zero_hist, mk, mv, 1024)

        hist_s = jnp.pad(state_pool[l], ((0, 0), (hist_rows - P_HIST, 0), (0, 0)))
        y_s, k_s, v_s, u_s = _layer(
            xs2d, bs, ts, past, w,
            cache_attn_k[l].reshape(bs * past, A_WIDTH), cache_attn_v[l].reshape(bs * past, A_WIDTH),
            hist_s, cache_mem_k[l].reshape(bs * M_TOKENS, M_WIDTH), cache_mem_v[l].reshape(bs * M_TOKENS, M_WIDTH),
            bs * ts)

        pool_p = jnp.concatenate([zero_hist, u_p.reshape(bp, tp, P_WIDTH)], axis=1)[:, -P_HIST:]
        pool_s = jnp.concatenate([hist_s, u_s.reshape(bs, ts, P_WIDTH)], axis=1)[:, -P_HIST:]
        outs = (k_p.reshape(bp, tp, A_HEADS, 2, A_QK_DIM), v_p.reshape(bp, tp, A_HEADS, A_V_DIM),
                mk.reshape(bp, M_TOKENS, M_HEADS, M_HEAD_DIM), mv.reshape(bp, M_TOKENS, M_HEADS, M_HEAD_DIM),
                pool_p,
                k_s.reshape(bs, ts, A_HEADS, 2, A_QK_DIM), v_s.reshape(bs, ts, A_HEADS, A_V_DIM), pool_s)
        return y_p, y_s, outs

    yp = x_prompt.reshape(bp * tp, d)
    ys = x_sample.reshape(bs * ts, d)
    per_layer = []
    for l in range(w_in.shape[0]):
        yp, ys, outs = run_layer(l, yp, ys)
        per_layer.append(outs)
    stacked = [jnp.stack([o[n] for o in per_layer]) for n in range(8)]
    return (yp.reshape(bp, tp, d), ys.reshape(bs, ts, d), *stacked)
```

```python
import functools
import math

import numpy as np
import jax
import jax.numpy as jnp
from jax import lax
from jax.experimental import pallas as pl
from jax.experimental.pallas import tpu as pltpu

F32 = jnp.float32
BF16 = jnp.bfloat16

V7X_VMEM_LIMIT_BYTES = 56 * 1024 * 1024
LANES = 128

D_MODEL = 2048
CHUNK = 64
A_HEADS = 8
A_QK_DIM = 64
A_V_DIM = 128
A_WIDTH = A_HEADS * A_V_DIM
A_SCALE = A_QK_DIM ** -0.5
ROT_DIM = A_QK_DIM // 4
ROPE_THETA = 500000.0
SUBLN_EPS = 1e-5
P_WINDOWS = (2, 4, 8, 16)
P_GROUP_DIM = 256
P_WIDTH = len(P_WINDOWS) * P_GROUP_DIM
P_HIST = max(P_WINDOWS) - 1
M_TOKENS = 256
M_HEADS = 4
M_HEAD_DIM = 256
M_WIDTH = M_HEADS * M_HEAD_DIM
M_SCALE = M_HEAD_DIM ** -0.5
N_BRANCH = 3
DEPTH = 1
DEEPNORM_ALPHA = (2.0 * DEPTH) ** 0.25
LN_EPS = 1e-5
LAMBDA_INIT = 0.8 - 0.6 * math.exp(-0.3 * 0)

COL_Q, COL_K, COL_V, COL_ZA, COL_UP, COL_ZP = 0, 1024, 2048, 3072, 4096, 5120
IN_COLS = 14336
H_ZA, H_ZP, H_QM, H_ZM, H_GM = 0, 1024, 2048, 3072, 4096
H_COLS = 10240


def _params(*sem):
    return pltpu.CompilerParams(dimension_semantics=sem, vmem_limit_bytes=V7X_VMEM_LIMIT_BYTES)


def _sigmoid(z):
    return 1.0 / (1.0 + jnp.exp(-z))


def _silu(z):
    return z * _sigmoid(z)


def _proj_kernel(x_ref, w_ref, *rest, rotary, scale):
    acc = jnp.dot(x_ref[...], w_ref[...], preferred_element_type=F32)
    if rotary:
        cos_ref, sa_ref, sb_ref, o_ref = rest
        c, sa, sb = cos_ref[...], sa_ref[...], sb_ref[...]
        half = ROT_DIM // 2
        parts = []
        for k in range(acc.shape[1] // LANES):
            blk = acc[:, k * LANES:(k + 1) * LANES]
            lo = pltpu.roll(blk, half, 1)
            hi = pltpu.roll(blk, LANES - half, 1)
            parts.append(blk * c + lo * sa + hi * sb)
        acc = jnp.concatenate(parts, axis=1)
    else:
        (o_ref,) = rest
    if scale != 1.0:
        acc = acc * scale
    o_ref[...] = acc.astype(o_ref.dtype)


def _proj(x, w, col_block, ncols, out_dtype, tm, tn, rot=None, scale=1.0, name="proj"):
    n, kdim = x.shape
    grid = (n // tm, ncols // tn)
    in_specs = [
        pl.BlockSpec((tm, kdim), lambda i, j: (i, 0)),
        pl.BlockSpec((kdim, tn), lambda i, j: (0, col_block(j))),
    ]
    args = [x, w]
    if rot is not None:
        nrep = rot[0].shape[0] // tm
        for t in rot:
            in_specs.append(pl.BlockSpec((tm, LANES), lambda i, j: (i % nrep, 0)))
            args.append(t)
    return pl.pallas_call(
        functools.partial(_proj_kernel, rotary=rot is not None, scale=scale),
        out_shape=jax.ShapeDtypeStruct((n, ncols), out_dtype),
        grid=grid,
        in_specs=in_specs,
        out_specs=pl.BlockSpec((tm, tn), lambda i, j: (i, j)),
        compiler_params=_params("parallel", "arbitrary"),
        name=name,
    )(*args)


def _proj_t_kernel(x_ref, w_ref, *rest, rotary, scale, std_out, t_out, key_tile):
    rest = list(rest)
    acc = jnp.dot(x_ref[...], w_ref[...], preferred_element_type=F32)
    if rotary:
        c, sa, sb = rest.pop(0)[...], rest.pop(0)[...], rest.pop(0)[...]
        half = ROT_DIM // 2
        parts = []
        for k in range(acc.shape[1] // LANES):
            blk = acc[:, k * LANES:(k + 1) * LANES]
            lo = pltpu.roll(blk, half, 1)
            hi = pltpu.roll(blk, LANES - half, 1)
            parts.append(blk * c + lo * sa + hi * sb)
        acc = jnp.concatenate(parts, axis=1)
    if scale != 1.0:
        acc = acc * scale
    if std_out:
        o_ref = rest.pop(0)
        o_ref[...] = acc.astype(o_ref.dtype)
    if t_out == "plain":
        t_ref = rest.pop(0)
        t_ref[...] = acc.T.astype(t_ref.dtype)
    elif t_out == "blocked":
        t_ref = rest.pop(0)
        for c in range(acc.shape[0] // key_tile):
            t_ref[c] = acc[c * key_tile:(c + 1) * key_tile, :].T.astype(t_ref.dtype)


def _proj_t(x, w, col_start, batch, t, tm, rot=None, scale=1.0, std_dtype=None, t_out=None, t_dtype=None,
            key_tile=256, name="proj_t"):
    n, kdim = x.shape
    width = 1024
    nt = t // tm
    in_specs = [
        pl.BlockSpec((tm, kdim), lambda i: (i, 0)),
        pl.BlockSpec((kdim, width), lambda i: (0, col_start // width)),
    ]
    args = [x, w]
    if rot is not None:
        nrep = rot[0].shape[0] // tm
        for tab in rot:
            in_specs.append(pl.BlockSpec((tm, LANES), lambda i: (i % nrep, 0)))
            args.append(tab)
    out_shape, out_specs = [], []
    if std_dtype is not None:
        out_shape.append(jax.ShapeDtypeStruct((n, width), std_dtype))
        out_specs.append(pl.BlockSpec((tm, width), lambda i: (i, 0)))
    if t_out == "plain":
        out_shape.append(jax.ShapeDtypeStruct((batch, width, t), t_dtype))
        out_specs.append(pl.BlockSpec((None, width, tm), lambda i: (i // nt, 0, i % nt)))
    elif t_out == "blocked":
        out_shape.append(jax.ShapeDtypeStruct((batch, t // key_tile, width, key_tile), t_dtype))
        out_specs.append(pl.BlockSpec((None, tm // key_tile, width, key_tile), lambda i: (i // nt, i % nt, 0, 0)))
    return pl.pallas_call(
        functools.partial(_proj_t_kernel, rotary=rot is not None, scale=scale,
                          std_out=std_dtype is not None, t_out=t_out, key_tile=key_tile),
        out_shape=out_shape,
        grid=(n // tm,),
        in_specs=in_specs,
        out_specs=out_specs,
        compiler_params=_params("arbitrary"),
        name=name,
    )(*args)


def _rotary_tables(pos):
    half = ROT_DIM // 2
    inv = ROPE_THETA ** (-(jnp.arange(half, dtype=F32) * 2.0) / ROT_DIM)
    ang = pos.astype(F32)[:, None] * inv[None, :]
    cos, sin = jnp.cos(ang), jnp.sin(ang)
    t = pos.shape[0]
    ones = jnp.ones((t, A_QK_DIM - ROT_DIM), F32)
    zeros = jnp.zeros((t, A_QK_DIM - ROT_DIM), F32)
    zh = jnp.zeros((t, half), F32)
    c64 = jnp.concatenate([cos, cos, ones], axis=1)
    sa64 = jnp.concatenate([zh, sin, zeros], axis=1)
    sb64 = jnp.concatenate([-sin, zh, zeros], axis=1)
    tile = lambda a: jnp.concatenate([a, a], axis=1)
    return tile(c64), tile(sa64), tile(sb64)


def _lambda_value(lam_ref):
    lv = lam_ref[...]
    a = jnp.sum(lv[0:1] * lv[1:2], axis=1, keepdims=True)
    b = jnp.sum(lv[2:3] * lv[3:4], axis=1, keepdims=True)
    return jnp.exp(a) - jnp.exp(b) + LAMBDA_INIT


def _split_components(q):
    lane = lax.broadcasted_iota(jnp.int32, q.shape, 1)
    qf = q.astype(F32)
    return jnp.concatenate([jnp.where(lane < A_QK_DIM, qf, 0.0),
                            jnp.where(lane >= A_QK_DIM, qf, 0.0)], axis=0).astype(q.dtype)


def _attn_finish(acc, l, tq, lam, gain, za):
    o = acc[:tq] / l[:tq] - lam * (acc[tq:] / l[tq:])
    o = o * lax.rsqrt(jnp.mean(o * o, axis=1, keepdims=True) + SUBLN_EPS)
    o = o * gain * (1.0 - LAMBDA_INIT)
    return o * _silu(za)


def _attn_prompt_kernel(qt_ref, k_ref, vt_ref, za_ref, gain_ref, lam_ref, o_ref, *, tq):
    i = pl.program_id(2)
    qt = qt_ref[...].astype(F32)
    row = lax.broadcasted_iota(jnp.int32, qt.shape, 0)
    qqt = jnp.concatenate([jnp.where(row < A_QK_DIM, qt, 0.0),
                           jnp.where(row >= A_QK_DIM, qt, 0.0)], axis=1).astype(BF16)

    def step(j, carry, mask):
        m, l, acc = carry
        start = pl.multiple_of(j * tq, tq)
        s = jnp.dot(k_ref[pl.ds(start, tq), :], qqt, preferred_element_type=F32)
        if mask is not None:
            s = jnp.where(mask, s, -jnp.inf)
        m_new = jnp.maximum(m, jnp.max(s, axis=0, keepdims=True))
        alpha = jnp.exp(m - m_new)
        p = jnp.exp(s - m_new)
        l = alpha * l + jnp.sum(p, axis=0, keepdims=True)
        acc = alpha * acc + jnp.dot(vt_ref[j], p.astype(BF16), preferred_element_type=F32)
        return m_new, l, acc

    init = (jnp.full((1, 2 * tq), -jnp.inf, F32), jnp.zeros((1, 2 * tq), F32),
            jnp.zeros((A_V_DIM, 2 * tq), F32))
    carry = lax.fori_loop(0, i, lambda j, c: step(j, c, None), init)
    krow = lax.broadcasted_iota(jnp.int32, (tq, 2 * tq), 0)
    qcol = lax.broadcasted_iota(jnp.int32, (tq, 2 * tq), 1) % tq
    _, l, acc = step(i, carry, (krow // CHUNK) <= (qcol // CHUNK))
    lam = _lambda_value(lam_ref)
    ot = acc[:, :tq] / l[:, :tq] - lam * (acc[:, tq:] / l[:, tq:])
    o = ot.T
    o = o * lax.rsqrt(jnp.mean(o * o, axis=1, keepdims=True) + SUBLN_EPS)
    o = o * gain_ref[...] * (1.0 - LAMBDA_INIT)
    o_ref[...] = (o * _silu(za_ref[...].astype(F32))).astype(o_ref.dtype)


def _attn_prompt(qt, k, vt, h, gain, lamv, batch, t, tq=256):
    nq = t // tq
    n = batch * t
    return pl.pallas_call(
        functools.partial(_attn_prompt_kernel, tq=tq),
        out_shape=jax.ShapeDtypeStruct((n, A_WIDTH), BF16),
        grid=(batch, A_HEADS, nq),
        in_specs=[
            pl.BlockSpec((None, LANES, tq), lambda b, hh, i: (b, hh, i)),
            pl.BlockSpec((t, LANES), lambda b, hh, i: (b, hh)),
            pl.BlockSpec((None, nq, LANES, tq), lambda b, hh, i: (b, 0, hh, 0)),
            pl.BlockSpec((tq, LANES), lambda b, hh, i: (b * nq + i, H_ZA // LANES + hh)),
            pl.BlockSpec((1, LANES), lambda b, hh, i: (0, 0)),
            pl.BlockSpec((4, A_QK_DIM), lambda b, hh, i: (0, 0)),
        ],
        out_specs=pl.BlockSpec((tq, LANES), lambda b, hh, i: (b * nq + i, hh)),
        compiler_params=_params("arbitrary", "arbitrary", "arbitrary"),
        name="attn_prompt",
    )(qt, k, vt, h, gain, lamv)


def _chunk_mask(s, ts, q_pos0, k_pos0):
    q_pos = q_pos0 + lax.broadcasted_iota(jnp.int32, s.shape, 0) % ts
    k_pos = k_pos0 + lax.broadcasted_iota(jnp.int32, s.shape, 1)
    return jnp.where((k_pos // CHUNK) <= (q_pos // CHUNK), s, -jnp.inf)


def _attn_sample_kernel(q_ref, kc_ref, vc_ref, kn_ref, vn_ref, za_ref, gain_ref, lam_ref, o_ref, *,
                        ts, past, mask_cache, mask_new):
    qq = _split_components(q_ref[...])
    nt = (((1,), (1,)), ((), ()))
    s_c = jnp.dot(qq, kc_ref[...].astype(BF16), preferred_element_type=F32)
    s_n = lax.dot_general(qq, kn_ref[...].astype(BF16), nt, preferred_element_type=F32)
    if mask_cache:
        s_c = _chunk_mask(s_c, ts, past, 0)
    if mask_new:
        s_n = _chunk_mask(s_n, ts, past, past)
    m = jnp.maximum(jnp.max(s_c, axis=1, keepdims=True), jnp.max(s_n, axis=1, keepdims=True))
    p_c = jnp.exp(s_c - m)
    p_n = jnp.exp(s_n - m)
    l = jnp.sum(p_c, axis=1, keepdims=True) + jnp.sum(p_n, axis=1, keepdims=True)
    acc = (jnp.dot(p_c.astype(BF16), vc_ref[...].astype(BF16), preferred_element_type=F32)
           + jnp.dot(p_n.astype(BF16), vn_ref[...].astype(BF16), preferred_element_type=F32))
    out = _attn_finish(acc, l, ts, _lambda_value(lam_ref), gain_ref[...], za_ref[...].astype(F32))
    o_ref[...] = out.astype(o_ref.dtype)


def _needs_mask(q_pos, k_pos):
    return not bool(((k_pos[None, :] // CHUNK) <= (q_pos[:, None] // CHUNK)).all())


def _attn_sample(q, kc, vc, kn, vn, h, gain, lamv, batch, ts, past):
    q_pos = past + np.arange(ts)
    return pl.pallas_call(
        functools.partial(_attn_sample_kernel, ts=ts, past=past,
                          mask_cache=_needs_mask(q_pos, np.arange(past)),
                          mask_new=_needs_mask(q_pos, q_pos)),
        out_shape=jax.ShapeDtypeStruct((batch * ts, A_WIDTH), BF16),
        grid=(batch, A_HEADS),
        in_specs=[
            pl.BlockSpec((ts, LANES), lambda b, hh: (b, hh)),
            pl.BlockSpec((None, None, LANES, past), lambda b, hh: (b, hh, 0, 0)),
            pl.BlockSpec((past, LANES), lambda b, hh: (b, hh)),
            pl.BlockSpec((ts, LANES), lambda b, hh: (b, hh)),
            pl.BlockSpec((ts, LANES), lambda b, hh: (b, hh)),
            pl.BlockSpec((ts, LANES), lambda b, hh: (b, H_ZA // LANES + hh)),
            pl.BlockSpec((1, LANES), lambda b, hh: (0, 0)),
            pl.BlockSpec((4, A_QK_DIM), lambda b, hh: (0, 0)),
        ],
        out_specs=pl.BlockSpec((ts, LANES), lambda b, hh: (b, hh)),
        compiler_params=_params("arbitrary", "arbitrary"),
        name="attn_sample",
    )(q, kc, vc, kn, vn, h, gain, lamv)


def _pool_kernel(u_ref, uprev_ref, hist_ref, zp_ref, pw_ref, ps_ref, o_ref, *, tm, pos0):
    i = pl.program_id(1)
    u = u_ref[...]
    prev = jnp.where(i == 0, hist_ref[...], uprev_ref[...])
    ext = jnp.concatenate([prev, u], axis=0)
    hist_rows = prev.shape[0]
    pos = pos0 + i * tm + lax.broadcasted_iota(jnp.int32, (tm, 1), 0)
    parts = []
    for g, w in enumerate(P_WINDOWS):
        sl = slice(g * P_GROUP_DIM, (g + 1) * P_GROUP_DIM)
        tot = ext[:, sl]
        span = 1
        while span < w:
            tot = tot + pltpu.roll(tot, span, 0)
            span *= 2
        tot = tot[hist_rows:]
        cnt = jnp.minimum(pos + 1, w).astype(F32)
        pooled = tot / cnt - u[:, sl]
        parts.append(jnp.dot(pooled.astype(BF16), pw_ref[g], preferred_element_type=F32))
    mixed = jnp.concatenate(parts, axis=1) * ps_ref[...]
    o_ref[...] = (mixed * _silu(zp_ref[...].astype(F32))).astype(o_ref.dtype)


def _pool(u, hist, h, pool_w, pool_scale, batch, t, tm, pos0):
    nt = t // tm
    hr = hist.shape[1]
    per = tm // hr
    return pl.pallas_call(
        functools.partial(_pool_kernel, tm=tm, pos0=pos0),
        out_shape=jax.ShapeDtypeStruct((batch * t, P_WIDTH), BF16),
        grid=(batch, nt),
        in_specs=[
            pl.BlockSpec((tm, P_WIDTH), lambda b, i: (b * nt + i, 0)),
            pl.BlockSpec((hr, P_WIDTH), lambda b, i: (jnp.maximum((b * nt + i) * per - 1, 0), 0)),
            pl.BlockSpec((None, hr, P_WIDTH), lambda b, i: (b, 0, 0)),
            pl.BlockSpec((tm, P_WIDTH), lambda b, i: (b * nt + i, H_ZP // P_WIDTH)),
            pl.BlockSpec((len(P_WINDOWS), P_GROUP_DIM, P_GROUP_DIM), lambda b, i: (0, 0, 0)),
            pl.BlockSpec((1, P_WIDTH), lambda b, i: (0, 0)),
        ],
        out_specs=pl.BlockSpec((tm, P_WIDTH), lambda b, i: (b * nt + i, 0)),
        compiler_params=_params("arbitrary", "arbitrary"),
        name="pool",
    )(u, u, hist, h, pool_w, pool_scale)


def _mem_attn_kernel(q_ref, mk_ref, mv_ref, zm_ref, o_ref):
    nt = (((1,), (1,)), ((), ()))
    s = lax.dot_general(q_ref[...], mk_ref[...].astype(BF16), nt, preferred_element_type=F32) * M_SCALE
    m = jnp.max(s, axis=1, keepdims=True)
    p = jnp.exp(s - m)
    l = jnp.sum(p, axis=1, keepdims=True)
    o = jnp.dot(p.astype(BF16), mv_ref[...].astype(BF16), preferred_element_type=F32) / l
    o_ref[...] = (o * _silu(zm_ref[...].astype(F32))).astype(o_ref.dtype)


def _mem_attn(h, mk, mv, batch, t, tq):
    nq = t // tq
    d = M_HEAD_DIM
    return pl.pallas_call(
        _mem_attn_kernel,
        out_shape=jax.ShapeDtypeStruct((batch * t, M_WIDTH), BF16),
        grid=(batch, M_HEADS, nq),
        in_specs=[
            pl.BlockSpec((tq, d), lambda b, hh, i: (b * nq + i, H_QM // d + hh)),
            pl.BlockSpec((M_TOKENS, d), lambda b, hh, i: (b, hh)),
            pl.BlockSpec((M_TOKENS, d), lambda b, hh, i: (b, hh)),
            pl.BlockSpec((tq, d), lambda b, hh, i: (b * nq + i, H_ZM // d + hh)),
        ],
        out_specs=pl.BlockSpec((tq, d), lambda b, hh, i: (b * nq + i, hh)),
        compiler_params=_params("arbitrary", "arbitrary", "arbitrary"),
        name="mem_attn",
    )(h, mk, mv, h)


def _merge_kernel(ba_ref, bp_ref, bm_ref, g0_ref, g1_ref, g2_ref, wb_ref, o_ref):
    acc = None
    for n, (b_ref, g_ref) in enumerate(((ba_ref, g0_ref), (bp_ref, g1_ref), (bm_ref, g2_ref))):
        proj = jnp.dot(b_ref[...], wb_ref[n], preferred_element_type=F32)
        term = _sigmoid(g_ref[...].astype(F32)) * proj
        acc = term if acc is None else acc + term
    o_ref[...] = acc.astype(o_ref.dtype)


def _merge(ba, bp, bm, h, wb, tm, tn):
    n = ba.shape[0]
    width = ba.shape[1]
    gspec = lambda k: pl.BlockSpec((tm, tn), lambda j, i: (i, (H_GM + k * D_MODEL) // tn + j))
    bspec = pl.BlockSpec((tm, width), lambda j, i: (i, 0))
    return pl.pallas_call(
        _merge_kernel,
        out_shape=jax.ShapeDtypeStruct((n, D_MODEL), BF16),
        grid=(D_MODEL // tn, n // tm),
        in_specs=[bspec, bspec, bspec, gspec(0), gspec(1), gspec(2),
                  pl.BlockSpec((N_BRANCH, width, tn), lambda j, i: (0, 0, j))],
        out_specs=pl.BlockSpec((tm, tn), lambda j, i: (i, j)),
        compiler_params=_params("arbitrary", "arbitrary"),
        name="merge",
    )(ba, bp, bm, h, h, h, wb)


def _out_kernel(m_ref, w_ref, x_ref, g_ref, b_ref, o_ref):
    z = DEEPNORM_ALPHA * x_ref[...] + jnp.dot(m_ref[...], w_ref[...], preferred_element_type=F32)
    mu = jnp.mean(z, axis=1, keepdims=True)
    zc = z - mu
    var = jnp.mean(zc * zc, axis=1, keepdims=True)
    o_ref[...] = zc * lax.rsqrt(var + LN_EPS) * g_ref[...] + b_ref[...]


def _out(merged, w_out, x, ln_g, ln_b, tm):
    n = merged.shape[0]
    row = pl.BlockSpec((tm, D_MODEL), lambda i: (i, 0))
    vec = pl.BlockSpec((1, D_MODEL), lambda i: (0, 0))
    return pl.pallas_call(
        _out_kernel,
        out_shape=jax.ShapeDtypeStruct((n, D_MODEL), F32),
        grid=(n // tm,),
        in_specs=[row, pl.BlockSpec((D_MODEL, D_MODEL), lambda i: (0, 0)), row, vec, vec],
        out_specs=row,
        compiler_params=_params("arbitrary"),
        name="out_ln",
    )(merged, w_out, x, ln_g, ln_b)


def _rest_col_block(tn):
    za_blocks = (COL_UP - COL_ZA) // tn
    return lambda j: jnp.where(j < za_blocks, COL_ZA // tn + j, COL_ZP // tn + (j - za_blocks))


def _layer(x2d, batch, t, pos0, w, kt_hist, v_hist, pool_hist, mk, mv, tm):
    xb = x2d.astype(BF16)
    pos = pos0 + jnp.arange(t)
    reps = max(tm // t, 1)
    rot = tuple(jnp.tile(a, (reps, 1)) for a in _rotary_tables(pos))
    tn = 1024
    seg = lambda start: (lambda j: start // tn + j)
    u = _proj(xb, w["w_in"], seg(COL_UP), 1024, F32, tm, tn, name="proj_u")
    h = _proj(xb, w["w_in"], _rest_col_block(tn), H_COLS, BF16, tm, tn, name="proj_rest")

    if kt_hist is None:
        tp = min(tm, 512)
        (qt,) = _proj_t(xb, w["w_in"], COL_Q, batch, t, tp, rot=rot, scale=A_SCALE,
                        t_out="plain", t_dtype=BF16, name="proj_qt")
        kb, k = _proj_t(xb, w["w_in"], COL_K, batch, t, tp, rot=rot, std_dtype=BF16,
                        t_out="plain", t_dtype=F32, name="proj_kt")
        v, vt = _proj_t(xb, w["w_in"], COL_V, batch, t, tp, std_dtype=F32,
                        t_out="blocked", t_dtype=BF16, name="proj_vt")
        b_a = _attn_prompt(qt, kb, vt, h, w["gain"], w["lamv"], batch, t)
    else:
        q = _proj(xb, w["w_in"], seg(COL_Q), 1024, BF16, tm, tn, rot=rot, scale=A_SCALE, name="proj_q")
        k = _proj(xb, w["w_in"], seg(COL_K), 1024, F32, tm, tn, rot=rot, name="proj_k")
        v = _proj(xb, w["w_in"], seg(COL_V), 1024, F32, tm, tn, name="proj_v")
        b_a = _attn_sample(q, kt_hist, v_hist, k, v, h, w["gain"], w["lamv"], batch, t, kt_hist.shape[-1])
    b_p = _pool(u, pool_hist, h, w["pool_w"], w["pool_scale"], batch, t, min(t, 512), pos0)
    b_m = _mem_attn(h, mk, mv, batch, t, min(t, 512))
    merged = _merge(b_a, b_p, b_m, h, w["w_branch"], min(tm, 512), 1024)
    y = _out(merged, w["w_out"], x2d, w["ln_g"], w["ln_b"], min(tm, 512))
    return y, k, v, u


def kernel(x_prompt, x_sample, cache_attn_k, cache_attn_v, cache_mem_k, cache_mem_v, state_pool, mem_prompt, w_in, w_mem_kv, lambda_q1, lambda_k1, lambda_q2, lambda_k2, subln_gain, pool_w, pool_scale, w_branch, w_out, ln_gain, ln_bias):
    bp, tp, d = x_prompt.shape
    bs, ts, _ = x_sample.shape
    past = cache_attn_k.shape[2]
    hist_rows = P_HIST + 1

    def run_layer(l, xp2d, xs2d):
        w = {
            "w_in": w_in[l].astype(BF16),
            "gain": subln_gain[l].reshape(1, A_V_DIM),
            "lamv": jnp.stack([lambda_q1[l], lambda_k1[l], lambda_q2[l], lambda_k2[l]]),
            "pool_w": pool_w[l].astype(BF16),
            "pool_scale": pool_scale[l].reshape(1, P_WIDTH),
            "w_branch": w_branch[l].astype(BF16),
            "w_out": w_out[l].astype(BF16),
            "ln_g": ln_gain[l].reshape(1, d),
            "ln_b": ln_bias[l].reshape(1, d),
        }
        memb = mem_prompt.reshape(bp * M_TOKENS, d).astype(BF16)
        wkv = w_mem_kv[l].astype(BF16)
        mk = _proj(memb, wkv, lambda j: j, M_WIDTH, F32, 512, 1024, name="proj_mk")
        mv = _proj(memb, wkv, lambda j: M_WIDTH // 1024 + j, M_WIDTH, F32, 512, 1024, name="proj_mv")

        zero_hist = jnp.zeros((bp, hist_rows, P_WIDTH), F32)
        y_p, k_p, v_p, u_p = _layer(xp2d, bp, tp, 0, w, None, None, zero_hist, mk, mv, 1024)

        hist_s = jnp.pad(state_pool[l], ((0, 0), (hist_rows - P_HIST, 0), (0, 0)))
        kt_hist = jnp.transpose(cache_attn_k[l], (0, 2, 3, 4, 1)).reshape(bs, A_HEADS, 2 * A_QK_DIM, past)
        y_s, k_s, v_s, u_s = _layer(
            xs2d, bs, ts, past, w,
            kt_hist, cache_attn_v[l].reshape(bs * past, A_WIDTH),
            hist_s, cache_mem_k[l].reshape(bs * M_TOKENS, M_WIDTH), cache_mem_v[l].reshape(bs * M_TOKENS, M_WIDTH),
            bs * ts)

        pool_p = jnp.concatenate([zero_hist, u_p.reshape(bp, tp, P_WIDTH)], axis=1)[:, -P_HIST:]
        pool_s = jnp.concatenate([hist_s, u_s.reshape(bs, ts, P_WIDTH)], axis=1)[:, -P_HIST:]
        k_p = jnp.transpose(k_p.reshape(bp, A_HEADS, 2, A_QK_DIM, tp), (0, 4, 1, 2, 3))
        outs = (k_p, v_p.reshape(bp, tp, A_HEADS, A_V_DIM),
                mk.reshape(bp, M_TOKENS, M_HEADS, M_HEAD_DIM), mv.reshape(bp, M_TOKENS, M_HEADS, M_HEAD_DIM),
                pool_p,
                k_s.reshape(bs, ts, A_HEADS, 2, A_QK_DIM), v_s.reshape(bs, ts, A_HEADS, A_V_DIM), pool_s)
        return y_p, y_s, outs

    yp = x_prompt.reshape(bp * tp, d)
    ys = x_sample.reshape(bs * ts, d)
    per_layer = []
    for l in range(w_in.shape[0]):
        yp, ys, outs = run_layer(l, yp, ys)
        per_layer.append(outs)
    stacked = [jnp.stack([o[n] for o in per_layer]) for n in range(8)]
    return (yp.reshape(bp, tp, d), ys.reshape(bs, ts, d), *stacked)
```

```python
import functools
import math

import numpy as np
import jax
import jax.numpy as jnp
from jax import lax
from jax.experimental import pallas as pl
from jax.experimental.pallas import tpu as pltpu

F32 = jnp.float32
BF16 = jnp.bfloat16

V7X_VMEM_LIMIT_BYTES = 56 * 1024 * 1024
LANES = 128

D_MODEL = 2048
CHUNK = 64
A_HEADS = 8
A_QK_DIM = 64
A_V_DIM = 128
A_WIDTH = A_HEADS * A_V_DIM
A_SCALE = A_QK_DIM ** -0.5
ROT_DIM = A_QK_DIM // 4
ROPE_THETA = 500000.0
SUBLN_EPS = 1e-5
P_WINDOWS = (2, 4, 8, 16)
P_GROUP_DIM = 256
P_WIDTH = len(P_WINDOWS) * P_GROUP_DIM
P_HIST = max(P_WINDOWS) - 1
M_TOKENS = 256
M_HEADS = 4
M_HEAD_DIM = 256
M_WIDTH = M_HEADS * M_HEAD_DIM
M_SCALE = M_HEAD_DIM ** -0.5
N_BRANCH = 3
DEPTH = 1
DEEPNORM_ALPHA = (2.0 * DEPTH) ** 0.25
LN_EPS = 1e-5
LAMBDA_INIT = 0.8 - 0.6 * math.exp(-0.3 * 0)

COL_Q, COL_K, COL_V, COL_ZA, COL_UP, COL_ZP = 0, 1024, 2048, 3072, 4096, 5120
IN_COLS = 14336
H_ZA, H_ZP, H_QM, H_ZM, H_GM = 0, 1024, 2048, 3072, 4096
H_COLS = 10240


def _params(*sem):
    return pltpu.CompilerParams(dimension_semantics=sem, vmem_limit_bytes=V7X_VMEM_LIMIT_BYTES)


def _sigmoid(z):
    return 1.0 / (1.0 + jnp.exp(-z))


def _silu(z):
    return z * _sigmoid(z)


def _proj_kernel(x_ref, w_ref, *rest, rotary, scale):
    acc = jnp.dot(x_ref[...], w_ref[...], preferred_element_type=F32)
    if rotary:
        cos_ref, sa_ref, sb_ref, o_ref = rest
        c, sa, sb = cos_ref[...], sa_ref[...], sb_ref[...]
        half = ROT_DIM // 2
        parts = []
        for k in range(acc.shape[1] // LANES):
            blk = acc[:, k * LANES:(k + 1) * LANES]
            lo = pltpu.roll(blk, half, 1)
            hi = pltpu.roll(blk, LANES - half, 1)
            parts.append(blk * c + lo * sa + hi * sb)
        acc = jnp.concatenate(parts, axis=1)
    else:
        (o_ref,) = rest
    if scale != 1.0:
        acc = acc * scale
    o_ref[...] = acc.astype(o_ref.dtype)


def _proj(x, w, col_block, ncols, out_dtype, tm, tn, rot=None, scale=1.0, name="proj"):
    n, kdim = x.shape
    grid = (n // tm, ncols // tn)
    in_specs = [
        pl.BlockSpec((tm, kdim), lambda i, j: (i, 0)),
        pl.BlockSpec((kdim, tn), lambda i, j: (0, col_block(j))),
    ]
    args = [x, w]
    if rot is not None:
        nrep = rot[0].shape[0] // tm
        for t in rot:
            in_specs.append(pl.BlockSpec((tm, LANES), lambda i, j: (i % nrep, 0)))
            args.append(t)
    return pl.pallas_call(
        functools.partial(_proj_kernel, rotary=rot is not None, scale=scale),
        out_shape=jax.ShapeDtypeStruct((n, ncols), out_dtype),
        grid=grid,
        in_specs=in_specs,
        out_specs=pl.BlockSpec((tm, tn), lambda i, j: (i, j)),
        compiler_params=_params("parallel", "arbitrary"),
        name=name,
    )(*args)


def _proj_t_kernel(x_ref, w_ref, *rest, rotary, scale, std_out, t_out, key_tile):
    rest = list(rest)
    acc = jnp.dot(x_ref[...], w_ref[...], preferred_element_type=F32)
    if rotary:
        c, sa, sb = rest.pop(0)[...], rest.pop(0)[...], rest.pop(0)[...]
        half = ROT_DIM // 2
        parts = []
        for k in range(acc.shape[1] // LANES):
            blk = acc[:, k * LANES:(k + 1) * LANES]
            lo = pltpu.roll(blk, half, 1)
            hi = pltpu.roll(blk, LANES - half, 1)
            parts.append(blk * c + lo * sa + hi * sb)
        acc = jnp.concatenate(parts, axis=1)
    if scale != 1.0:
        acc = acc * scale
    if std_out:
        o_ref = rest.pop(0)
        o_ref[...] = acc.astype(o_ref.dtype)
    if t_out == "plain":
        t_ref = rest.pop(0)
        t_ref[...] = acc.T.astype(t_ref.dtype)
    elif t_out == "blocked":
        t_ref = rest.pop(0)
        for c in range(acc.shape[0] // key_tile):
            t_ref[c] = acc[c * key_tile:(c + 1) * key_tile, :].T.astype(t_ref.dtype)


def _proj_t(x, w, col_start, batch, t, tm, rot=None, scale=1.0, std_dtype=None, t_out=None, t_dtype=None,
            key_tile=256, name="proj_t"):
    n, kdim = x.shape
    width = 1024
    nt = t // tm
    in_specs = [
        pl.BlockSpec((tm, kdim), lambda i: (i, 0)),
        pl.BlockSpec((kdim, width), lambda i: (0, col_start // width)),
    ]
    args = [x, w]
    if rot is not None:
        nrep = rot[0].shape[0] // tm
        for tab in rot:
            in_specs.append(pl.BlockSpec((tm, LANES), lambda i: (i % nrep, 0)))
            args.append(tab)
    out_shape, out_specs = [], []
    if std_dtype is not None:
        out_shape.append(jax.ShapeDtypeStruct((n, width), std_dtype))
        out_specs.append(pl.BlockSpec((tm, width), lambda i: (i, 0)))
    if t_out == "plain":
        out_shape.append(jax.ShapeDtypeStruct((batch, width, t), t_dtype))
        out_specs.append(pl.BlockSpec((None, width, tm), lambda i: (i // nt, 0, i % nt)))
    elif t_out == "blocked":
        out_shape.append(jax.ShapeDtypeStruct((batch, t // key_tile, width, key_tile), t_dtype))
        out_specs.append(pl.BlockSpec((None, tm // key_tile, width, key_tile), lambda i: (i // nt, i % nt, 0, 0)))
    return pl.pallas_call(
        functools.partial(_proj_t_kernel, rotary=rot is not None, scale=scale,
                          std_out=std_dtype is not None, t_out=t_out, key_tile=key_tile),
        out_shape=out_shape,
        grid=(n // tm,),
        in_specs=in_specs,
        out_specs=out_specs,
        compiler_params=_params("arbitrary"),
        name=name,
    )(*args)


def _rotary_tables(pos):
    half = ROT_DIM // 2
    inv = ROPE_THETA ** (-(jnp.arange(half, dtype=F32) * 2.0) / ROT_DIM)
    ang = pos.astype(F32)[:, None] * inv[None, :]
    cos, sin = jnp.cos(ang), jnp.sin(ang)
    t = pos.shape[0]
    ones = jnp.ones((t, A_QK_DIM - ROT_DIM), F32)
    zeros = jnp.zeros((t, A_QK_DIM - ROT_DIM), F32)
    zh = jnp.zeros((t, half), F32)
    c64 = jnp.concatenate([cos, cos, ones], axis=1)
    sa64 = jnp.concatenate([zh, sin, zeros], axis=1)
    sb64 = jnp.concatenate([-sin, zh, zeros], axis=1)
    tile = lambda a: jnp.concatenate([a, a], axis=1)
    return tile(c64), tile(sa64), tile(sb64)


def _lambda_value(lam_ref):
    lv = lam_ref[...]
    a = jnp.sum(lv[0:1] * lv[1:2], axis=1, keepdims=True)
    b = jnp.sum(lv[2:3] * lv[3:4], axis=1, keepdims=True)
    return jnp.exp(a) - jnp.exp(b) + LAMBDA_INIT


def _split_components(q):
    lane = lax.broadcasted_iota(jnp.int32, q.shape, 1)
    qf = q.astype(F32)
    return jnp.concatenate([jnp.where(lane < A_QK_DIM, qf, 0.0),
                            jnp.where(lane >= A_QK_DIM, qf, 0.0)], axis=0).astype(q.dtype)


def _attn_finish(acc, l, tq, lam, gain, za):
    o = acc[:tq] / l[:tq] - lam * (acc[tq:] / l[tq:])
    o = o * lax.rsqrt(jnp.mean(o * o, axis=1, keepdims=True) + SUBLN_EPS)
    o = o * gain * (1.0 - LAMBDA_INIT)
    return o * _silu(za)


def _attn_prompt_kernel(qt_ref, k_ref, vt_ref, za_ref, gain_ref, lam_ref, o_ref, *, tq, heads):
    i = pl.program_id(2)
    hs = [slice(n * LANES, (n + 1) * LANES) for n in range(heads)]
    qqts = []
    for sl in hs:
        qt = qt_ref[sl, :].astype(F32)
        row = lax.broadcasted_iota(jnp.int32, qt.shape, 0)
        qqts.append(jnp.concatenate([jnp.where(row < A_QK_DIM, qt, 0.0),
                                     jnp.where(row >= A_QK_DIM, qt, 0.0)], axis=1).astype(BF16))

    def step(j, carries, mask):
        start = pl.multiple_of(j * tq, tq)
        out = []
        scores = [jnp.dot(k_ref[pl.ds(start, tq), hs[n]], qqts[n], preferred_element_type=F32)
                  for n in range(heads)]
        for n, (m, l, acc) in enumerate(carries):
            s = scores[n]
            if mask is not None:
                s = jnp.where(mask, s, -jnp.inf)
            m_new = jnp.maximum(m, jnp.max(s, axis=0, keepdims=True))
            alpha = jnp.exp(m - m_new)
            p = jnp.exp(s - m_new)
            l = alpha * l + jnp.sum(p, axis=0, keepdims=True)
            acc = alpha * acc + jnp.dot(vt_ref[j, hs[n], :], p.astype(BF16), preferred_element_type=F32)
            out.append((m_new, l, acc))
        return tuple(out)

    init = tuple((jnp.full((1, 2 * tq), -jnp.inf, F32), jnp.zeros((1, 2 * tq), F32),
                  jnp.zeros((A_V_DIM, 2 * tq), F32)) for _ in range(heads))
    carries = lax.fori_loop(0, i, lambda j, c: step(j, c, None), init)
    krow = lax.broadcasted_iota(jnp.int32, (tq, 2 * tq), 0)
    qcol = lax.broadcasted_iota(jnp.int32, (tq, 2 * tq), 1) % tq
    carries = step(i, carries, (krow // CHUNK) <= (qcol // CHUNK))
    lam = _lambda_value(lam_ref)
    for n, (_, l, acc) in enumerate(carries):
        ot = acc[:, :tq] / l[:, :tq] - lam * (acc[:, tq:] / l[:, tq:])
        o = ot.T
        o = o * lax.rsqrt(jnp.mean(o * o, axis=1, keepdims=True) + SUBLN_EPS)
        o = o * gain_ref[...] * (1.0 - LAMBDA_INIT)
        o_ref[:, hs[n]] = (o * _silu(za_ref[:, hs[n]].astype(F32))).astype(o_ref.dtype)


def _attn_prompt(qt, k, vt, h, gain, lamv, batch, t, tq=256, heads=8):
    nq = t // tq
    n = batch * t
    hw = heads * LANES
    return pl.pallas_call(
        functools.partial(_attn_prompt_kernel, tq=tq, heads=heads),
        out_shape=jax.ShapeDtypeStruct((n, A_WIDTH), BF16),
        grid=(batch, A_HEADS // heads, nq),
        in_specs=[
            pl.BlockSpec((None, hw, tq), lambda b, hh, i: (b, hh, i)),
            pl.BlockSpec((t, hw), lambda b, hh, i: (b, hh)),
            pl.BlockSpec((None, nq, hw, tq), lambda b, hh, i: (b, 0, hh, 0)),
            pl.BlockSpec((tq, hw), lambda b, hh, i: (b * nq + i, H_ZA // hw + hh)),
            pl.BlockSpec((1, LANES), lambda b, hh, i: (0, 0)),
            pl.BlockSpec((4, A_QK_DIM), lambda b, hh, i: (0, 0)),
        ],
        out_specs=pl.BlockSpec((tq, hw), lambda b, hh, i: (b * nq + i, hh)),
        compiler_params=_params("arbitrary", "arbitrary", "arbitrary"),
        name="attn_prompt",
    )(qt, k, vt, h, gain, lamv)


def _chunk_mask(s, ts, q_pos0, k_pos0):
    q_pos = q_pos0 + lax.broadcasted_iota(jnp.int32, s.shape, 0) % ts
    k_pos = k_pos0 + lax.broadcasted_iota(jnp.int32, s.shape, 1)
    return jnp.where((k_pos // CHUNK) <= (q_pos // CHUNK), s, -jnp.inf)


def _attn_sample_kernel(q_ref, kc_ref, vc_ref, kn_ref, vn_ref, za_ref, gain_ref, lam_ref, o_ref, *,
                        ts, past, mask_cache, mask_new):
    qq = _split_components(q_ref[...])
    nt = (((1,), (1,)), ((), ()))
    s_c = jnp.dot(qq, kc_ref[...].astype(BF16), preferred_element_type=F32)
    s_n = lax.dot_general(qq, kn_ref[...].astype(BF16), nt, preferred_element_type=F32)
    if mask_cache:
        s_c = _chunk_mask(s_c, ts, past, 0)
    if mask_new:
        s_n = _chunk_mask(s_n, ts, past, past)
    m = jnp.maximum(jnp.max(s_c, axis=1, keepdims=True), jnp.max(s_n, axis=1, keepdims=True))
    p_c = jnp.exp(s_c - m)
    p_n = jnp.exp(s_n - m)
    l = jnp.sum(p_c, axis=1, keepdims=True) + jnp.sum(p_n, axis=1, keepdims=True)
    acc = (jnp.dot(p_c.astype(BF16), vc_ref[...].astype(BF16), preferred_element_type=F32)
           + jnp.dot(p_n.astype(BF16), vn_ref[...].astype(BF16), preferred_element_type=F32))
    out = _attn_finish(acc, l, ts, _lambda_value(lam_ref), gain_ref[...], za_ref[...].astype(F32))
    o_ref[...] = out.astype(o_ref.dtype)


def _needs_mask(q_pos, k_pos):
    return not bool(((k_pos[None, :] // CHUNK) <= (q_pos[:, None] // CHUNK)).all())


def _attn_sample(q, kc, vc, kn, vn, h, gain, lamv, batch, ts, past):
    q_pos = past + np.arange(ts)
    return pl.pallas_call(
        functools.partial(_attn_sample_kernel, ts=ts, past=past,
                          mask_cache=_needs_mask(q_pos, np.arange(past)),
                          mask_new=_needs_mask(q_pos, q_pos)),
        out_shape=jax.ShapeDtypeStruct((batch * ts, A_WIDTH), BF16),
        grid=(batch, A_HEADS),
        in_specs=[
            pl.BlockSpec((ts, LANES), lambda b, hh: (b, hh)),
            pl.BlockSpec((None, None, LANES, past), lambda b, hh: (b, hh, 0, 0)),
            pl.BlockSpec((past, LANES), lambda b, hh: (b, hh)),
            pl.BlockSpec((ts, LANES), lambda b, hh: (b, hh)),
            pl.BlockSpec((ts, LANES), lambda b, hh: (b, hh)),
            pl.BlockSpec((ts, LANES), lambda b, hh: (b, H_ZA // LANES + hh)),
            pl.BlockSpec((1, LANES), lambda b, hh: (0, 0)),
            pl.BlockSpec((4, A_QK_DIM), lambda b, hh: (0, 0)),
        ],
        out_specs=pl.BlockSpec((ts, LANES), lambda b, hh: (b, hh)),
        compiler_params=_params("arbitrary", "arbitrary"),
        name="attn_sample",
    )(q, kc, vc, kn, vn, h, gain, lamv)


def _pool_kernel(u_ref, uprev_ref, hist_ref, zp_ref, pw_ref, ps_ref, o_ref, *, tm, pos0):
    i = pl.program_id(1)
    u = u_ref[...]
    prev = jnp.where(i == 0, hist_ref[...], uprev_ref[...])
    ext = jnp.concatenate([prev, u], axis=0)
    hist_rows = prev.shape[0]
    pos = pos0 + i * tm + lax.broadcasted_iota(jnp.int32, (tm, 1), 0)
    parts = []
    for g, w in enumerate(P_WINDOWS):
        sl = slice(g * P_GROUP_DIM, (g + 1) * P_GROUP_DIM)
        tot = ext[:, sl]
        span = 1
        while span < w:
            tot = tot + pltpu.roll(tot, span, 0)
            span *= 2
        tot = tot[hist_rows:]
        cnt = jnp.minimum(pos + 1, w).astype(F32)
        pooled = tot / cnt - u[:, sl]
        parts.append(jnp.dot(pooled.astype(BF16), pw_ref[g], preferred_element_type=F32))
    mixed = jnp.concatenate(parts, axis=1) * ps_ref[...]
    o_ref[...] = (mixed * _silu(zp_ref[...].astype(F32))).astype(o_ref.dtype)


def _pool(u, hist, h, pool_w, pool_scale, batch, t, tm, pos0):
    nt = t // tm
    hr = hist.shape[1]
    per = tm // hr
    return pl.pallas_call(
        functools.partial(_pool_kernel, tm=tm, pos0=pos0),
        out_shape=jax.ShapeDtypeStruct((batch * t, P_WIDTH), BF16),
        grid=(batch, nt),
        in_specs=[
            pl.BlockSpec((tm, P_WIDTH), lambda b, i: (b * nt + i, 0)),
            pl.BlockSpec((hr, P_WIDTH), lambda b, i: (jnp.maximum((b * nt + i) * per - 1, 0), 0)),
            pl.BlockSpec((None, hr, P_WIDTH), lambda b, i: (b, 0, 0)),
            pl.BlockSpec((tm, P_WIDTH), lambda b, i: (b * nt + i, H_ZP // P_WIDTH)),
            pl.BlockSpec((len(P_WINDOWS), P_GROUP_DIM, P_GROUP_DIM), lambda b, i: (0, 0, 0)),
            pl.BlockSpec((1, P_WIDTH), lambda b, i: (0, 0)),
        ],
        out_specs=pl.BlockSpec((tm, P_WIDTH), lambda b, i: (b * nt + i, 0)),
        compiler_params=_params("arbitrary", "arbitrary"),
        name="pool",
    )(u, u, hist, h, pool_w, pool_scale)


def _mem_attn_kernel(q_ref, mk_ref, mv_ref, zm_ref, o_ref):
    nt = (((1,), (1,)), ((), ()))
    s = lax.dot_general(q_ref[...], mk_ref[...].astype(BF16), nt, preferred_element_type=F32) * M_SCALE
    m = jnp.max(s, axis=1, keepdims=True)
    p = jnp.exp(s - m)
    l = jnp.sum(p, axis=1, keepdims=True)
    o = jnp.dot(p.astype(BF16), mv_ref[...].astype(BF16), preferred_element_type=F32) / l
    o_ref[...] = (o * _silu(zm_ref[...].astype(F32))).astype(o_ref.dtype)


def _mem_attn(h, mk, mv, batch, t, tq):
    nq = t // tq
    d = M_HEAD_DIM
    return pl.pallas_call(
        _mem_attn_kernel,
        out_shape=jax.ShapeDtypeStruct((batch * t, M_WIDTH), BF16),
        grid=(batch, M_HEADS, nq),
        in_specs=[
            pl.BlockSpec((tq, d), lambda b, hh, i: (b * nq + i, H_QM // d + hh)),
            pl.BlockSpec((M_TOKENS, d), lambda b, hh, i: (b, hh)),
            pl.BlockSpec((M_TOKENS, d), lambda b, hh, i: (b, hh)),
            pl.BlockSpec((tq, d), lambda b, hh, i: (b * nq + i, H_ZM // d + hh)),
        ],
        out_specs=pl.BlockSpec((tq, d), lambda b, hh, i: (b * nq + i, hh)),
        compiler_params=_params("arbitrary", "arbitrary", "arbitrary"),
        name="mem_attn",
    )(h, mk, mv, h)


def _merge_kernel(ba_ref, bp_ref, bm_ref, g0_ref, g1_ref, g2_ref, wb_ref, o_ref):
    acc = None
    for n, (b_ref, g_ref) in enumerate(((ba_ref, g0_ref), (bp_ref, g1_ref), (bm_ref, g2_ref))):
        proj = jnp.dot(b_ref[...], wb_ref[n], preferred_element_type=F32)
        term = _sigmoid(g_ref[...].astype(F32)) * proj
        acc = term if acc is None else acc + term
    o_ref[...] = acc.astype(o_ref.dtype)


def _merge(ba, bp, bm, h, wb, tm, tn):
    n = ba.shape[0]
    width = ba.shape[1]
    gspec = lambda k: pl.BlockSpec((tm, tn), lambda j, i: (i, (H_GM + k * D_MODEL) // tn + j))
    bspec = pl.BlockSpec((tm, width), lambda j, i: (i, 0))
    return pl.pallas_call(
        _merge_kernel,
        out_shape=jax.ShapeDtypeStruct((n, D_MODEL), BF16),
        grid=(D_MODEL // tn, n // tm),
        in_specs=[bspec, bspec, bspec, gspec(0), gspec(1), gspec(2),
                  pl.BlockSpec((N_BRANCH, width, tn), lambda j, i: (0, 0, j))],
        out_specs=pl.BlockSpec((tm, tn), lambda j, i: (i, j)),
        compiler_params=_params("arbitrary", "arbitrary"),
        name="merge",
    )(ba, bp, bm, h, h, h, wb)


def _out_kernel(m_ref, w_ref, x_ref, g_ref, b_ref, o_ref):
    z = DEEPNORM_ALPHA * x_ref[...] + jnp.dot(m_ref[...], w_ref[...], preferred_element_type=F32)
    mu = jnp.mean(z, axis=1, keepdims=True)
    zc = z - mu
    var = jnp.mean(zc * zc, axis=1, keepdims=True)
    o_ref[...] = zc * lax.rsqrt(var + LN_EPS) * g_ref[...] + b_ref[...]


def _out(merged, w_out, x, ln_g, ln_b, tm):
    n = merged.shape[0]
    row = pl.BlockSpec((tm, D_MODEL), lambda i: (i, 0))
    vec = pl.BlockSpec((1, D_MODEL), lambda i: (0, 0))
    return pl.pallas_call(
        _out_kernel,
        out_shape=jax.ShapeDtypeStruct((n, D_MODEL), F32),
        grid=(n // tm,),
        in_specs=[row, pl.BlockSpec((D_MODEL, D_MODEL), lambda i: (0, 0)), row, vec, vec],
        out_specs=row,
        compiler_params=_params("arbitrary"),
        name="out_ln",
    )(merged, w_out, x, ln_g, ln_b)


def _rest_col_block(tn):
    za_blocks = (COL_UP - COL_ZA) // tn
    return lambda j: jnp.where(j < za_blocks, COL_ZA // tn + j, COL_ZP // tn + (j - za_blocks))


def _layer(x2d, batch, t, pos0, w, kt_hist, v_hist, pool_hist, mk, mv, tm):
    xb = x2d.astype(BF16)
    pos = pos0 + jnp.arange(t)
    reps = max(tm // t, 1)
    rot = tuple(jnp.tile(a, (reps, 1)) for a in _rotary_tables(pos))
    tn = 1024
    seg = lambda start: (lambda j: start // tn + j)
    u = _proj(xb, w["w_in"], seg(COL_UP), 1024, F32, tm, tn, name="proj_u")
    h = _proj(xb, w["w_in"], _rest_col_block(tn), H_COLS, BF16, tm, tn, name="proj_rest")

    if kt_hist is None:
        tp = min(tm, 512)
        (qt,) = _proj_t(xb, w["w_in"], COL_Q, batch, t, tp, rot=rot, scale=A_SCALE,
                        t_out="plain", t_dtype=BF16, name="proj_qt")
        kb, k = _proj_t(xb, w["w_in"], COL_K, batch, t, tp, rot=rot, std_dtype=BF16,
                        t_out="plain", t_dtype=F32, name="proj_kt")
        v, vt = _proj_t(xb, w["w_in"], COL_V, batch, t, tp, std_dtype=F32,
                        t_out="blocked", t_dtype=BF16, name="proj_vt")
        b_a = _attn_prompt(qt, kb, vt, h, w["gain"], w["lamv"], batch, t)
    else:
        q = _proj(xb, w["w_in"], seg(COL_Q), 1024, BF16, tm, tn, rot=rot, scale=A_SCALE, name="proj_q")
        k = _proj(xb, w["w_in"], seg(COL_K), 1024, F32, tm, tn, rot=rot, name="proj_k")
        v = _proj(xb, w["w_in"], seg(COL_V), 1024, F32, tm, tn, name="proj_v")
        b_a = _attn_sample(q, kt_hist, v_hist, k, v, h, w["gain"], w["lamv"], batch, t, kt_hist.shape[-1])
    b_p = _pool(u, pool_hist, h, w["pool_w"], w["pool_scale"], batch, t, min(t, 512), pos0)
    b_m = _mem_attn(h, mk, mv, batch, t, min(t, 512))
    merged = _merge(b_a, b_p, b_m, h, w["w_branch"], min(tm, 512), 1024)
    y = _out(merged, w["w_out"], x2d, w["ln_g"], w["ln_b"], min(tm, 512))
    return y, k, v, u


def kernel(x_prompt, x_sample, cache_attn_k, cache_attn_v, cache_mem_k, cache_mem_v, state_pool, mem_prompt, w_in, w_mem_kv, lambda_q1, lambda_k1, lambda_q2, lambda_k2, subln_gain, pool_w, pool_scale, w_branch, w_out, ln_gain, ln_bias):
    bp, tp, d = x_prompt.shape
    bs, ts, _ = x_sample.shape
    past = cache_attn_k.shape[2]
    hist_rows = P_HIST + 1

    def run_layer(l, xp2d, xs2d):
        w = {
            "w_in": w_in[l].astype(BF16),
            "gain": subln_gain[l].reshape(1, A_V_DIM),
            "lamv": jnp.stack([lambda_q1[l], lambda_k1[l], lambda_q2[l], lambda_k2[l]]),
            "pool_w": pool_w[l].astype(BF16),
            "pool_scale": pool_scale[l].reshape(1, P_WIDTH),
            "w_branch": w_branch[l].astype(BF16),
            "w_out": w_out[l].astype(BF16),
            "ln_g": ln_gain[l].reshape(1, d),
            "ln_b": ln_bias[l].reshape(1, d),
        }
        memb = mem_prompt.reshape(bp * M_TOKENS, d).astype(BF16)
        wkv = w_mem_kv[l].astype(BF16)
        mk = _proj(memb, wkv, lambda j: j, M_WIDTH, F32, 512, 1024, name="proj_mk")
        mv = _proj(memb, wkv, lambda j: M_WIDTH // 1024 + j, M_WIDTH, F32, 512, 1024, name="proj_mv")

        zero_hist = jnp.zeros((bp, hist_rows, P_WIDTH), F32)
        y_p, k_p, v_p, u_p = _layer(xp2d, bp, tp, 0, w, None, None, zero_hist, mk, mv, 1024)

        hist_s = jnp.pad(state_pool[l], ((0, 0), (hist_rows - P_HIST, 0), (0, 0)))
        kt_hist = jnp.transpose(cache_attn_k[l], (0, 2, 3, 4, 1)).reshape(bs, A_HEADS, 2 * A_QK_DIM, past)
        y_s, k_s, v_s, u_s = _layer(
            xs2d, bs, ts, past, w,
            kt_hist, cache_attn_v[l].reshape(bs * past, A_WIDTH),
            hist_s, cache_mem_k[l].reshape(bs * M_TOKENS, M_WIDTH), cache_mem_v[l].reshape(bs * M_TOKENS, M_WIDTH),
            bs * ts)

        pool_p = jnp.concatenate([zero_hist, u_p.reshape(bp, tp, P_WIDTH)], axis=1)[:, -P_HIST:]
        pool_s = jnp.concatenate([hist_s, u_s.reshape(bs, ts, P_WIDTH)], axis=1)[:, -P_HIST:]
        k_p = jnp.transpose(k_p.reshape(bp, A_HEADS, 2, A_QK_DIM, tp), (0, 4, 1, 2, 3))
        outs = (k_p, v_p.reshape(bp, tp, A_HEADS, A_V_DIM),
                mk.reshape(bp, M_TOKENS, M_HEADS, M_HEAD_DIM), mv.reshape(bp, M_TOKENS, M_HEADS, M_HEAD_DIM),
                pool_p,
                k_s.reshape(bs, ts, A_HEADS, 2, A_QK_DIM), v_s.reshape(bs, ts, A_HEADS, A_V_DIM), pool_s)
        return y_p, y_s, outs

    yp = x_prompt.reshape(bp * tp, d)
    ys = x_sample.reshape(bs * ts, d)
    per_layer = []
    for l in range(w_in.shape[0]):
        yp, ys, outs = run_layer(l, yp, ys)
        per_layer.append(outs)
    stacked = [jnp.stack([o[n] for o in per_layer]) for n in range(8)]
    return (yp.reshape(bp, tp, d), ys.reshape(bs, ts, d), *stacked)
```

```python
import functools
import math

import numpy as np
import jax
import jax.numpy as jnp
from jax import lax
from jax.experimental import pallas as pl
from jax.experimental.pallas import tpu as pltpu

F32 = jnp.float32
BF16 = jnp.bfloat16

V7X_VMEM_LIMIT_BYTES = 56 * 1024 * 1024
LANES = 128

D_MODEL = 2048
CHUNK = 64
A_HEADS = 8
A_QK_DIM = 64
A_V_DIM = 128
A_WIDTH = A_HEADS * A_V_DIM
A_SCALE = A_QK_DIM ** -0.5
ROT_DIM = A_QK_DIM // 4
ROPE_THETA = 500000.0
SUBLN_EPS = 1e-5
P_WINDOWS = (2, 4, 8, 16)
P_GROUP_DIM = 256
P_WIDTH = len(P_WINDOWS) * P_GROUP_DIM
P_HIST = max(P_WINDOWS) - 1
M_TOKENS = 256
M_HEADS = 4
M_HEAD_DIM = 256
M_WIDTH = M_HEADS * M_HEAD_DIM
M_SCALE = M_HEAD_DIM ** -0.5
N_BRANCH = 3
DEPTH = 1
DEEPNORM_ALPHA = (2.0 * DEPTH) ** 0.25
LN_EPS = 1e-5
LAMBDA_INIT = 0.8 - 0.6 * math.exp(-0.3 * 0)

COL_Q, COL_K, COL_V, COL_ZA, COL_UP, COL_ZP, COL_QM, COL_ZM, COL_GM = (
    0, 1024, 2048, 3072, 4096, 5120, 6144, 7168, 8192)
IN_COLS = 14336
H_ZA, H_ZP, H_QM, H_ZM, H_GM = 0, 1024, 2048, 3072, 4096
H_COLS = 10240


def _params(*sem):
    return pltpu.CompilerParams(dimension_semantics=sem, vmem_limit_bytes=V7X_VMEM_LIMIT_BYTES)


def _sigmoid(z):
    return 1.0 / (1.0 + jnp.exp(-z))


def _silu(z):
    return z * _sigmoid(z)


def _proj_kernel(x_ref, w_ref, *rest, rotary, scale):
    acc = jnp.dot(x_ref[...], w_ref[...], preferred_element_type=F32)
    if rotary:
        cos_ref, sa_ref, sb_ref, o_ref = rest
        c, sa, sb = cos_ref[...], sa_ref[...], sb_ref[...]
        half = ROT_DIM // 2
        parts = []
        for k in range(acc.shape[1] // LANES):
            blk = acc[:, k * LANES:(k + 1) * LANES]
            lo = pltpu.roll(blk, half, 1)
            hi = pltpu.roll(blk, LANES - half, 1)
            parts.append(blk * c + lo * sa + hi * sb)
        acc = jnp.concatenate(parts, axis=1)
    else:
        (o_ref,) = rest
    if scale != 1.0:
        acc = acc * scale
    o_ref[...] = acc.astype(o_ref.dtype)


def _proj(x, w, col_block, ncols, out_dtype, tm, tn, rot=None, scale=1.0, name="proj"):
    n, kdim = x.shape
    grid = (n // tm, ncols // tn)
    in_specs = [
        pl.BlockSpec((tm, kdim), lambda i, j: (i, 0)),
        pl.BlockSpec((kdim, tn), lambda i, j: (0, col_block(j))),
    ]
    args = [x, w]
    if rot is not None:
        nrep = rot[0].shape[0] // tm
        for t in rot:
            in_specs.append(pl.BlockSpec((tm, LANES), lambda i, j: (i % nrep, 0)))
            args.append(t)
    return pl.pallas_call(
        functools.partial(_proj_kernel, rotary=rot is not None, scale=scale),
        out_shape=jax.ShapeDtypeStruct((n, ncols), out_dtype),
        grid=grid,
        in_specs=in_specs,
        out_specs=pl.BlockSpec((tm, tn), lambda i, j: (i, j)),
        compiler_params=_params("parallel", "arbitrary"),
        name=name,
    )(*args)


def _proj_t_kernel(x_ref, w_ref, *rest, rotary, scale, std_out, t_out, key_tile):
    rest = list(rest)
    acc = jnp.dot(x_ref[...], w_ref[...], preferred_element_type=F32)
    if rotary:
        c, sa, sb = rest.pop(0)[...], rest.pop(0)[...], rest.pop(0)[...]
        half = ROT_DIM // 2
        parts = []
        for k in range(acc.shape[1] // LANES):
            blk = acc[:, k * LANES:(k + 1) * LANES]
            lo = pltpu.roll(blk, half, 1)
            hi = pltpu.roll(blk, LANES - half, 1)
            parts.append(blk * c + lo * sa + hi * sb)
        acc = jnp.concatenate(parts, axis=1)
    if scale != 1.0:
        acc = acc * scale
    if std_out:
        o_ref = rest.pop(0)
        o_ref[...] = acc.astype(o_ref.dtype)
    if t_out == "plain":
        t_ref = rest.pop(0)
        t_ref[...] = acc.T.astype(t_ref.dtype)
    elif t_out == "blocked":
        t_ref = rest.pop(0)
        for c in range(acc.shape[0] // key_tile):
            t_ref[c] = acc[c * key_tile:(c + 1) * key_tile, :].T.astype(t_ref.dtype)


def _proj_t(x, w, col_start, batch, t, tm, rot=None, scale=1.0, std_dtype=None, t_out=None, t_dtype=None,
            key_tile=256, name="proj_t"):
    n, kdim = x.shape
    width = 1024
    nt = t // tm
    in_specs = [
        pl.BlockSpec((tm, kdim), lambda i: (i, 0)),
        pl.BlockSpec((kdim, width), lambda i: (0, col_start // width)),
    ]
    args = [x, w]
    if rot is not None:
        nrep = rot[0].shape[0] // tm
        for tab in rot:
            in_specs.append(pl.BlockSpec((tm, LANES), lambda i: (i % nrep, 0)))
            args.append(tab)
    out_shape, out_specs = [], []
    if std_dtype is not None:
        out_shape.append(jax.ShapeDtypeStruct((n, width), std_dtype))
        out_specs.append(pl.BlockSpec((tm, width), lambda i: (i, 0)))
    if t_out == "plain":
        out_shape.append(jax.ShapeDtypeStruct((batch, width, t), t_dtype))
        out_specs.append(pl.BlockSpec((None, width, tm), lambda i: (i // nt, 0, i % nt)))
    elif t_out == "blocked":
        out_shape.append(jax.ShapeDtypeStruct((batch, t // key_tile, width, key_tile), t_dtype))
        out_specs.append(pl.BlockSpec((None, tm // key_tile, width, key_tile), lambda i: (i // nt, i % nt, 0, 0)))
    return pl.pallas_call(
        functools.partial(_proj_t_kernel, rotary=rot is not None, scale=scale,
                          std_out=std_dtype is not None, t_out=t_out, key_tile=key_tile),
        out_shape=out_shape,
        grid=(n // tm,),
        in_specs=in_specs,
        out_specs=out_specs,
        compiler_params=_params("arbitrary"),
        name=name,
    )(*args)


def _rotary_tables(pos):
    half = ROT_DIM // 2
    inv = ROPE_THETA ** (-(jnp.arange(half, dtype=F32) * 2.0) / ROT_DIM)
    ang = pos.astype(F32)[:, None] * inv[None, :]
    cos, sin = jnp.cos(ang), jnp.sin(ang)
    t = pos.shape[0]
    ones = jnp.ones((t, A_QK_DIM - ROT_DIM), F32)
    zeros = jnp.zeros((t, A_QK_DIM - ROT_DIM), F32)
    zh = jnp.zeros((t, half), F32)
    c64 = jnp.concatenate([cos, cos, ones], axis=1)
    sa64 = jnp.concatenate([zh, sin, zeros], axis=1)
    sb64 = jnp.concatenate([-sin, zh, zeros], axis=1)
    tile = lambda a: jnp.concatenate([a, a], axis=1)
    return tile(c64), tile(sa64), tile(sb64)


def _lambda_value(lam_ref):
    lv = lam_ref[...]
    a = jnp.sum(lv[0:1] * lv[1:2], axis=1, keepdims=True)
    b = jnp.sum(lv[2:3] * lv[3:4], axis=1, keepdims=True)
    return jnp.exp(a) - jnp.exp(b) + LAMBDA_INIT


def _split_components(q):
    lane = lax.broadcasted_iota(jnp.int32, q.shape, 1)
    qf = q.astype(F32)
    return jnp.concatenate([jnp.where(lane < A_QK_DIM, qf, 0.0),
                            jnp.where(lane >= A_QK_DIM, qf, 0.0)], axis=0).astype(q.dtype)


def _attn_finish(acc, l, tq, lam, gain, za):
    o = acc[:tq] / l[:tq] - lam * (acc[tq:] / l[tq:])
    o = o * lax.rsqrt(jnp.mean(o * o, axis=1, keepdims=True) + SUBLN_EPS)
    o = o * gain * (1.0 - LAMBDA_INIT)
    return o * _silu(za)


def _attn_prompt_kernel(qt_ref, k_ref, vt_ref, za_ref, gain_ref, lam_ref, o_ref, *, tq, heads):
    i = pl.program_id(2)
    hs = [slice(n * LANES, (n + 1) * LANES) for n in range(heads)]
    qqts = []
    for sl in hs:
        qt = qt_ref[sl, :].astype(F32)
        row = lax.broadcasted_iota(jnp.int32, qt.shape, 0)
        qqts.append(jnp.concatenate([jnp.where(row < A_QK_DIM, qt, 0.0),
                                     jnp.where(row >= A_QK_DIM, qt, 0.0)], axis=1).astype(BF16))

    def step(j, carries, mask):
        start = pl.multiple_of(j * tq, tq)
        out = []
        scores = [jnp.dot(k_ref[pl.ds(start, tq), hs[n]], qqts[n], preferred_element_type=F32)
                  for n in range(heads)]
        for n, (m, l, acc) in enumerate(carries):
            s = scores[n]
            if mask is not None:
                s = jnp.where(mask, s, -jnp.inf)
            m_new = jnp.maximum(m, jnp.max(s, axis=0, keepdims=True))
            alpha = jnp.exp(m - m_new)
            p = jnp.exp(s - m_new)
            l = alpha * l + jnp.sum(p, axis=0, keepdims=True)
            acc = alpha * acc + jnp.dot(vt_ref[j, hs[n], :], p.astype(BF16), preferred_element_type=F32)
            out.append((m_new, l, acc))
        return tuple(out)

    init = tuple((jnp.full((1, 2 * tq), -jnp.inf, F32), jnp.zeros((1, 2 * tq), F32),
                  jnp.zeros((A_V_DIM, 2 * tq), F32)) for _ in range(heads))
    carries = lax.fori_loop(0, i, lambda j, c: step(j, c, None), init)
    krow = lax.broadcasted_iota(jnp.int32, (tq, 2 * tq), 0)
    qcol = lax.broadcasted_iota(jnp.int32, (tq, 2 * tq), 1) % tq
    carries = step(i, carries, (krow // CHUNK) <= (qcol // CHUNK))
    lam = _lambda_value(lam_ref)
    for n, (_, l, acc) in enumerate(carries):
        ot = acc[:, :tq] / l[:, :tq] - lam * (acc[:, tq:] / l[:, tq:])
        o = ot.T
        o = o * lax.rsqrt(jnp.mean(o * o, axis=1, keepdims=True) + SUBLN_EPS)
        o = o * gain_ref[...] * (1.0 - LAMBDA_INIT)
        o_ref[:, hs[n]] = (o * _silu(za_ref[:, hs[n]].astype(F32))).astype(o_ref.dtype)


def _attn_prompt(qt, k, vt, h, gain, lamv, batch, t, tq=256, heads=8):
    nq = t // tq
    n = batch * t
    hw = heads * LANES
    return pl.pallas_call(
        functools.partial(_attn_prompt_kernel, tq=tq, heads=heads),
        out_shape=jax.ShapeDtypeStruct((n, A_WIDTH), BF16),
        grid=(batch, A_HEADS // heads, nq),
        in_specs=[
            pl.BlockSpec((None, hw, tq), lambda b, hh, i: (b, hh, i)),
            pl.BlockSpec((t, hw), lambda b, hh, i: (b, hh)),
            pl.BlockSpec((None, nq, hw, tq), lambda b, hh, i: (b, 0, hh, 0)),
            pl.BlockSpec((tq, hw), lambda b, hh, i: (b * nq + i, H_ZA // hw + hh)),
            pl.BlockSpec((1, LANES), lambda b, hh, i: (0, 0)),
            pl.BlockSpec((4, A_QK_DIM), lambda b, hh, i: (0, 0)),
        ],
        out_specs=pl.BlockSpec((tq, hw), lambda b, hh, i: (b * nq + i, hh)),
        compiler_params=_params("arbitrary", "arbitrary", "arbitrary"),
        name="attn_prompt",
    )(qt, k, vt, h, gain, lamv)


def _chunk_mask(s, ts, q_pos0, k_pos0):
    q_pos = q_pos0 + lax.broadcasted_iota(jnp.int32, s.shape, 0) % ts
    k_pos = k_pos0 + lax.broadcasted_iota(jnp.int32, s.shape, 1)
    return jnp.where((k_pos // CHUNK) <= (q_pos // CHUNK), s, -jnp.inf)


def _rotate_lanes(blk, c, sa, sb):
    half = ROT_DIM // 2
    return blk * c + pltpu.roll(blk, half, 1) * sa + pltpu.roll(blk, LANES - half, 1) * sb


def _attn_sample_kernel(q_ref, k_ref, v_ref, za_ref, cos_ref, sa_ref, sb_ref, kct_ref, vc_ref, gain_ref, lam_ref,
                        o_ref, krot_ref, *, ts, past, mask_cache, mask_new):
    c, sa, sb = cos_ref[...], sa_ref[...], sb_ref[...]
    nt = (((1,), (1,)), ((), ()))
    hs = [slice(n * LANES, (n + 1) * LANES) for n in range(A_HEADS)]
    scores = []
    for n in range(A_HEADS):
        q = (_rotate_lanes(q_ref[:, hs[n]], c, sa, sb) * A_SCALE).astype(BF16)
        k = _rotate_lanes(k_ref[:, hs[n]], c, sa, sb)
        krot_ref[:, hs[n]] = k
        qq = _split_components(q)
        s_c = jnp.dot(qq, kct_ref[n].astype(BF16), preferred_element_type=F32)
        s_n = lax.dot_general(qq, k.astype(BF16), nt, preferred_element_type=F32)
        if mask_cache:
            s_c = _chunk_mask(s_c, ts, past, 0)
        if mask_new:
            s_n = _chunk_mask(s_n, ts, past, past)
        scores.append((s_c, s_n))
    lam = _lambda_value(lam_ref)
    for n, (s_c, s_n) in enumerate(scores):
        m = jnp.maximum(jnp.max(s_c, axis=1, keepdims=True), jnp.max(s_n, axis=1, keepdims=True))
        p_c = jnp.exp(s_c - m)
        p_n = jnp.exp(s_n - m)
        l = jnp.sum(p_c, axis=1, keepdims=True) + jnp.sum(p_n, axis=1, keepdims=True)
        vc = vc_ref[pl.ds(n, past, stride=A_HEADS), :].astype(BF16)
        acc = (jnp.dot(p_c.astype(BF16), vc, preferred_element_type=F32)
               + jnp.dot(p_n.astype(BF16), v_ref[:, hs[n]].astype(BF16), preferred_element_type=F32))
        out = _attn_finish(acc, l, ts, lam, gain_ref[...], za_ref[:, hs[n]].astype(F32))
        o_ref[:, hs[n]] = out.astype(o_ref.dtype)


def _needs_mask(q_pos, k_pos):
    return not bool(((k_pos[None, :] // CHUNK) <= (q_pos[:, None] // CHUNK)).all())


def _attn_sample(hs_act, rot, kct, vc, gain, lamv, batch, ts, past):
    q_pos = past + np.arange(ts)
    seg = lambda col: pl.BlockSpec((ts, A_WIDTH), lambda b: (b, col // A_WIDTH))
    tab = pl.BlockSpec((ts, LANES), lambda b: (0, 0))
    row = pl.BlockSpec((ts, A_WIDTH), lambda b: (b, 0))
    return pl.pallas_call(
        functools.partial(_attn_sample_kernel, ts=ts, past=past,
                          mask_cache=_needs_mask(q_pos, np.arange(past)),
                          mask_new=_needs_mask(q_pos, q_pos)),
        out_shape=[jax.ShapeDtypeStruct((batch * ts, A_WIDTH), BF16),
                   jax.ShapeDtypeStruct((batch * ts, A_WIDTH), F32)],
        grid=(batch,),
        in_specs=[
            seg(COL_Q), seg(COL_K), seg(COL_V), seg(COL_ZA), tab, tab, tab,
            pl.BlockSpec((None, A_HEADS, LANES, past), lambda b: (b, 0, 0, 0)),
            pl.BlockSpec((None, past * A_HEADS, LANES), lambda b: (b, 0, 0)),
            pl.BlockSpec((1, LANES), lambda b: (0, 0)),
            pl.BlockSpec((4, A_QK_DIM), lambda b: (0, 0)),
        ],
        out_specs=[row, row],
        compiler_params=_params("arbitrary"),
        name="attn_sample",
    )(hs_act, hs_act, hs_act, hs_act, *rot, kct, vc, gain, lamv)


def _pool_kernel(u_ref, uprev_ref, hist_ref, zp_ref, pw_ref, ps_ref, o_ref, *, tm, pos0):
    i = pl.program_id(1)
    u = u_ref[...]
    prev = jnp.where(i == 0, hist_ref[...], uprev_ref[...])
    ext = jnp.concatenate([prev, u], axis=0)
    hist_rows = prev.shape[0]
    pos = pos0 + i * tm + lax.broadcasted_iota(jnp.int32, (tm, 1), 0)
    parts = []
    for g, w in enumerate(P_WINDOWS):
        sl = slice(g * P_GROUP_DIM, (g + 1) * P_GROUP_DIM)
        tot = ext[:, sl]
        span = 1
        while span < w:
            tot = tot + pltpu.roll(tot, span, 0)
            span *= 2
        tot = tot[hist_rows:]
        cnt = jnp.minimum(pos + 1, w).astype(F32)
        pooled = tot / cnt - u[:, sl]
        parts.append(jnp.dot(pooled.astype(BF16), pw_ref[g], preferred_element_type=F32))
    mixed = jnp.concatenate(parts, axis=1) * ps_ref[...]
    o_ref[...] = (mixed * _silu(zp_ref[...].astype(F32))).astype(o_ref.dtype)


def _pool(u, u_col, hist, z, z_col, pool_w, pool_scale, batch, t, tm, pos0):
    nt = t // tm
    hr = hist.shape[1]
    per = tm // hr
    ucb, zcb = u_col // P_WIDTH, z_col // P_WIDTH
    return pl.pallas_call(
        functools.partial(_pool_kernel, tm=tm, pos0=pos0),
        out_shape=jax.ShapeDtypeStruct((batch * t, P_WIDTH), BF16),
        grid=(batch, nt),
        in_specs=[
            pl.BlockSpec((tm, P_WIDTH), lambda b, i: (b * nt + i, ucb)),
            pl.BlockSpec((hr, P_WIDTH), lambda b, i: (jnp.maximum((b * nt + i) * per - 1, 0), ucb)),
            pl.BlockSpec((None, hr, P_WIDTH), lambda b, i: (b, 0, 0)),
            pl.BlockSpec((tm, P_WIDTH), lambda b, i: (b * nt + i, zcb)),
            pl.BlockSpec((len(P_WINDOWS), P_GROUP_DIM, P_GROUP_DIM), lambda b, i: (0, 0, 0)),
            pl.BlockSpec((1, P_WIDTH), lambda b, i: (0, 0)),
        ],
        out_specs=pl.BlockSpec((tm, P_WIDTH), lambda b, i: (b * nt + i, 0)),
        compiler_params=_params("arbitrary", "arbitrary"),
        name="pool",
    )(u, u, hist, z, pool_w, pool_scale)


def _mem_attn_kernel(q_ref, mk_ref, mv_ref, zm_ref, o_ref):
    nt = (((1,), (1,)), ((), ()))
    hs = [slice(n * M_HEAD_DIM, (n + 1) * M_HEAD_DIM) for n in range(M_HEADS)]
    scores = [lax.dot_general(q_ref[:, sl].astype(BF16), mk_ref[:, sl].astype(BF16), nt,
                              preferred_element_type=F32) * M_SCALE for sl in hs]
    for sl, s in zip(hs, scores):
        m = jnp.max(s, axis=1, keepdims=True)
        p = jnp.exp(s - m)
        l = jnp.sum(p, axis=1, keepdims=True)
        o = jnp.dot(p.astype(BF16), mv_ref[:, sl].astype(BF16), preferred_element_type=F32) / l
        o_ref[:, sl] = (o * _silu(zm_ref[:, sl].astype(F32))).astype(o_ref.dtype)


def _mem_attn(act, q_col, z_col, mk, mv, batch, t, tq):
    nq = t // tq
    qcb, zcb = q_col // M_WIDTH, z_col // M_WIDTH
    return pl.pallas_call(
        _mem_attn_kernel,
        out_shape=jax.ShapeDtypeStruct((batch * t, M_WIDTH), BF16),
        grid=(batch, nq),
        in_specs=[
            pl.BlockSpec((tq, M_WIDTH), lambda b, i: (b * nq + i, qcb)),
            pl.BlockSpec((M_TOKENS, M_WIDTH), lambda b, i: (b, 0)),
            pl.BlockSpec((M_TOKENS, M_WIDTH), lambda b, i: (b, 0)),
            pl.BlockSpec((tq, M_WIDTH), lambda b, i: (b * nq + i, zcb)),
        ],
        out_specs=pl.BlockSpec((tq, M_WIDTH), lambda b, i: (b * nq + i, 0)),
        compiler_params=_params("arbitrary", "arbitrary"),
        name="mem_attn",
    )(act, mk, mv, act)


def _merge_kernel(ba_ref, bp_ref, bm_ref, g0_ref, g1_ref, g2_ref, wb_ref, o_ref):
    acc = None
    for n, (b_ref, g_ref) in enumerate(((ba_ref, g0_ref), (bp_ref, g1_ref), (bm_ref, g2_ref))):
        proj = jnp.dot(b_ref[...], wb_ref[n], preferred_element_type=F32)
        term = _sigmoid(g_ref[...].astype(F32)) * proj
        acc = term if acc is None else acc + term
    o_ref[...] = acc.astype(o_ref.dtype)


def _merge(ba, bp, bm, h, g_col, wb, tm, tn):
    n = ba.shape[0]
    width = ba.shape[1]
    gspec = lambda k: pl.BlockSpec((tm, tn), lambda j, i: (i, (g_col + k * D_MODEL) // tn + j))
    bspec = pl.BlockSpec((tm, width), lambda j, i: (i, 0))
    return pl.pallas_call(
        _merge_kernel,
        out_shape=jax.ShapeDtypeStruct((n, D_MODEL), BF16),
        grid=(D_MODEL // tn, n // tm),
        in_specs=[bspec, bspec, bspec, gspec(0), gspec(1), gspec(2),
                  pl.BlockSpec((N_BRANCH, width, tn), lambda j, i: (0, 0, j))],
        out_specs=pl.BlockSpec((tm, tn), lambda j, i: (i, j)),
        compiler_params=_params("arbitrary", "arbitrary"),
        name="merge",
    )(ba, bp, bm, h, h, h, wb)


def _out_kernel(m_ref, w_ref, x_ref, g_ref, b_ref, o_ref):
    z = DEEPNORM_ALPHA * x_ref[...] + jnp.dot(m_ref[...], w_ref[...], preferred_element_type=F32)
    mu = jnp.mean(z, axis=1, keepdims=True)
    zc = z - mu
    var = jnp.mean(zc * zc, axis=1, keepdims=True)
    o_ref[...] = zc * lax.rsqrt(var + LN_EPS) * g_ref[...] + b_ref[...]


def _out(merged, w_out, x, ln_g, ln_b, tm):
    n = merged.shape[0]
    row = pl.BlockSpec((tm, D_MODEL), lambda i: (i, 0))
    vec = pl.BlockSpec((1, D_MODEL), lambda i: (0, 0))
    return pl.pallas_call(
        _out_kernel,
        out_shape=jax.ShapeDtypeStruct((n, D_MODEL), F32),
        grid=(n // tm,),
        in_specs=[row, pl.BlockSpec((D_MODEL, D_MODEL), lambda i: (0, 0)), row, vec, vec],
        out_specs=row,
        compiler_params=_params("arbitrary"),
        name="out_ln",
    )(merged, w_out, x, ln_g, ln_b)


def _rest_col_block(tn):
    za_blocks = (COL_UP - COL_ZA) // tn
    return lambda j: jnp.where(j < za_blocks, COL_ZA // tn + j, COL_ZP // tn + (j - za_blocks))


def _layer_prompt(x2d, batch, t, w, pool_hist, mk, mv):
    xb = x2d.astype(BF16)
    rot = _rotary_tables(jnp.arange(t))
    tm, tn, tp = 1024, 1024, 512
    u = _proj(xb, w["w_in"], lambda j: COL_UP // tn + j, 1024, F32, tm, tn, name="proj_u")
    h = _proj(xb, w["w_in"], _rest_col_block(tn), H_COLS, BF16, tm, tn, name="proj_rest")
    (qt,) = _proj_t(xb, w["w_in"], COL_Q, batch, t, tp, rot=rot, scale=A_SCALE,
                    t_out="plain", t_dtype=BF16, name="proj_qt")
    kb, kt = _proj_t(xb, w["w_in"], COL_K, batch, t, tp, rot=rot, std_dtype=BF16,
                     t_out="plain", t_dtype=F32, name="proj_kt")
    v, vt = _proj_t(xb, w["w_in"], COL_V, batch, t, tp, std_dtype=F32,
                    t_out="blocked", t_dtype=BF16, name="proj_vt")
    b_a = _attn_prompt(qt, kb, vt, h, w["gain"], w["lamv"], batch, t)
    b_p = _pool(u, 0, pool_hist, h, H_ZP, w["pool_w"], w["pool_scale"], batch, t, 512, 0)
    b_m = _mem_attn(h, H_QM, H_ZM, mk, mv, batch, t, 512)
    merged = _merge(b_a, b_p, b_m, h, H_GM, w["w_branch"], 512, 1024)
    y = _out(merged, w["w_out"], x2d, w["ln_g"], w["ln_b"], 512)
    return y, kt, v, u


def _layer_sample(x2d, batch, t, past, w, kt_hist, v_hist, pool_hist, mk, mv):
    n = batch * t
    xb = x2d.astype(BF16)
    rot = _rotary_tables(past + jnp.arange(t))
    act = _proj(xb, w["w_in"], lambda j: j, IN_COLS, F32, n, 1024, name="proj_all")
    b_a, k = _attn_sample(act, rot, kt_hist, v_hist, w["gain"], w["lamv"], batch, t, past)
    b_p = _pool(act, COL_UP, pool_hist, act, COL_ZP, w["pool_w"], w["pool_scale"], batch, t, t, past)
    b_m = _mem_attn(act, COL_QM, COL_ZM, mk, mv, batch, t, t)
    merged = _merge(b_a, b_p, b_m, act, COL_GM, w["w_branch"], n, 1024)
    y = _out(merged, w["w_out"], x2d, w["ln_g"], w["ln_b"], n)
    return y, k, act


def kernel(x_prompt, x_sample, cache_attn_k, cache_attn_v, cache_mem_k, cache_mem_v, state_pool, mem_prompt, w_in, w_mem_kv, lambda_q1, lambda_k1, lambda_q2, lambda_k2, subln_gain, pool_w, pool_scale, w_branch, w_out, ln_gain, ln_bias):
    bp, tp, d = x_prompt.shape
    bs, ts, _ = x_sample.shape
    past = cache_attn_k.shape[2]
    hist_rows = P_HIST + 1

    def run_layer(l, xp2d, xs2d):
        w = {
            "w_in": w_in[l].astype(BF16),
            "gain": subln_gain[l].reshape(1, A_V_DIM),
            "lamv": jnp.stack([lambda_q1[l], lambda_k1[l], lambda_q2[l], lambda_k2[l]]),
            "pool_w": pool_w[l].astype(BF16),
            "pool_scale": pool_scale[l].reshape(1, P_WIDTH),
            "w_branch": w_branch[l].astype(BF16),
            "w_out": w_out[l].astype(BF16),
            "ln_g": ln_gain[l].reshape(1, d),
            "ln_b": ln_bias[l].reshape(1, d),
        }
        memb = mem_prompt.reshape(bp * M_TOKENS, d).astype(BF16)
        wkv = w_mem_kv[l].astype(BF16)
        mk = _proj(memb, wkv, lambda j: j, M_WIDTH, F32, 512, 1024, name="proj_mk")
        mv = _proj(memb, wkv, lambda j: M_WIDTH // 1024 + j, M_WIDTH, F32, 512, 1024, name="proj_mv")

        zero_hist = jnp.zeros((bp, hist_rows, P_WIDTH), F32)
        y_p, k_p, v_p, u_p = _layer_prompt(xp2d, bp, tp, w, zero_hist, mk, mv)

        hist_s = jnp.pad(state_pool[l], ((0, 0), (hist_rows - P_HIST, 0), (0, 0)))
        kt_hist = jnp.transpose(cache_attn_k[l], (0, 2, 3, 4, 1)).reshape(bs, A_HEADS, 2 * A_QK_DIM, past)
        y_s, k_s, act_s = _layer_sample(
            xs2d, bs, ts, past, w,
            kt_hist, cache_attn_v[l].reshape(bs, past * A_HEADS, A_V_DIM),
            hist_s, cache_mem_k[l].reshape(bs * M_TOKENS, M_WIDTH), cache_mem_v[l].reshape(bs * M_TOKENS, M_WIDTH))
        v_s = act_s[:, COL_V:COL_V + A_WIDTH]
        u_s = act_s[:, COL_UP:COL_UP + P_WIDTH]

        pool_p = jnp.concatenate([zero_hist, u_p.reshape(bp, tp, P_WIDTH)], axis=1)[:, -P_HIST:]
        pool_s = jnp.concatenate([hist_s, u_s.reshape(bs, ts, P_WIDTH)], axis=1)[:, -P_HIST:]
        k_p = jnp.transpose(k_p.reshape(bp, A_HEADS, 2, A_QK_DIM, tp), (0, 4, 1, 2, 3))
        outs = (k_p, v_p.reshape(bp, tp, A_HEADS, A_V_DIM),
                mk.reshape(bp, M_TOKENS, M_HEADS, M_HEAD_DIM), mv.reshape(bp, M_TOKENS, M_HEADS, M_HEAD_DIM),
                pool_p,
                k_s.reshape(bs, ts, A_HEADS, 2, A_QK_DIM), v_s.reshape(bs, ts, A_HEADS, A_V_DIM), pool_s)
        return y_p, y_s, outs

    yp = x_prompt.reshape(bp * tp, d)
    ys = x_sample.reshape(bs * ts, d)
    per_layer = []
    for l in range(w_in.shape[0]):
        yp, ys, outs = run_layer(l, yp, ys)
        per_layer.append(outs)
    stacked = [jnp.stack([o[n] for o in per_layer]) for n in range(8)]
    return (yp.reshape(bp, tp, d), ys.reshape(bs, ts, d), *stacked)
```

```python
import functools
import math

import numpy as np
import jax
import jax.numpy as jnp
from jax import lax
from jax.experimental import pallas as pl
from jax.experimental.pallas import tpu as pltpu

F32 = jnp.float32
BF16 = jnp.bfloat16

V7X_VMEM_LIMIT_BYTES = 56 * 1024 * 1024
LANES = 128

D_MODEL = 2048
CHUNK = 64
A_HEADS = 8
A_QK_DIM = 64
A_V_DIM = 128
A_WIDTH = A_HEADS * A_V_DIM
A_SCALE = A_QK_DIM ** -0.5
ROT_DIM = A_QK_DIM // 4
ROPE_THETA = 500000.0
SUBLN_EPS = 1e-5
P_WINDOWS = (2, 4, 8, 16)
P_GROUP_DIM = 256
P_WIDTH = len(P_WINDOWS) * P_GROUP_DIM
P_HIST = max(P_WINDOWS) - 1
M_TOKENS = 256
M_HEADS = 4
M_HEAD_DIM = 256
M_WIDTH = M_HEADS * M_HEAD_DIM
M_SCALE = M_HEAD_DIM ** -0.5
N_BRANCH = 3
DEPTH = 1
DEEPNORM_ALPHA = (2.0 * DEPTH) ** 0.25
LN_EPS = 1e-5
LAMBDA_INIT = 0.8 - 0.6 * math.exp(-0.3 * 0)
ONES_ROWS = 16

COL_Q, COL_K, COL_V, COL_ZA, COL_UP, COL_ZP, COL_QM, COL_ZM, COL_GM = (
    0, 1024, 2048, 3072, 4096, 5120, 6144, 7168, 8192)
IN_COLS = 14336
H_ZA, H_ZP, H_QM, H_ZM, H_GM = 0, 1024, 2048, 3072, 4096
H_COLS = 10240


def _params(*sem):
    return pltpu.CompilerParams(dimension_semantics=sem, vmem_limit_bytes=V7X_VMEM_LIMIT_BYTES)


def _sigmoid(z):
    return 1.0 / (1.0 + jnp.exp(-z))


def _silu(z):
    return z * _sigmoid(z)


def _proj_kernel(x_ref, w_ref, *rest, rotary, scale):
    acc = jnp.dot(x_ref[...], w_ref[...], preferred_element_type=F32)
    if rotary:
        cos_ref, sa_ref, sb_ref, o_ref = rest
        c, sa, sb = cos_ref[...], sa_ref[...], sb_ref[...]
        half = ROT_DIM // 2
        parts = []
        for k in range(acc.shape[1] // LANES):
            blk = acc[:, k * LANES:(k + 1) * LANES]
            lo = pltpu.roll(blk, half, 1)
            hi = pltpu.roll(blk, LANES - half, 1)
            parts.append(blk * c + lo * sa + hi * sb)
        acc = jnp.concatenate(parts, axis=1)
    else:
        (o_ref,) = rest
    if scale != 1.0:
        acc = acc * scale
    o_ref[...] = acc.astype(o_ref.dtype)


def _proj(x, w, col_block, ncols, out_dtype, tm, tn, rot=None, scale=1.0, name="proj"):
    n, kdim = x.shape
    grid = (n // tm, ncols // tn)
    in_specs = [
        pl.BlockSpec((tm, kdim), lambda i, j: (i, 0)),
        pl.BlockSpec((kdim, tn), lambda i, j: (0, col_block(j))),
    ]
    args = [x, w]
    if rot is not None:
        nrep = rot[0].shape[0] // tm
        for t in rot:
            in_specs.append(pl.BlockSpec((tm, LANES), lambda i, j: (i % nrep, 0)))
            args.append(t)
    return pl.pallas_call(
        functools.partial(_proj_kernel, rotary=rot is not None, scale=scale),
        out_shape=jax.ShapeDtypeStruct((n, ncols), out_dtype),
        grid=grid,
        in_specs=in_specs,
        out_specs=pl.BlockSpec((tm, tn), lambda i, j: (i, j)),
        compiler_params=_params("parallel", "arbitrary"),
        name=name,
    )(*args)


def _proj_t_kernel(x_ref, w_ref, *rest, rotary, scale, std_out, t_out, key_tile):
    rest = list(rest)
    acc = jnp.dot(x_ref[...], w_ref[...], preferred_element_type=F32)
    if rotary:
        c, sa, sb = rest.pop(0)[...], rest.pop(0)[...], rest.pop(0)[...]
        half = ROT_DIM // 2
        parts = []
        for k in range(acc.shape[1] // LANES):
            blk = acc[:, k * LANES:(k + 1) * LANES]
            lo = pltpu.roll(blk, half, 1)
            hi = pltpu.roll(blk, LANES - half, 1)
            parts.append(blk * c + lo * sa + hi * sb)
        acc = jnp.concatenate(parts, axis=1)
    if scale != 1.0:
        acc = acc * scale
    if std_out:
        o_ref = rest.pop(0)
        o_ref[...] = acc.astype(o_ref.dtype)
    if t_out == "plain":
        t_ref = rest.pop(0)
        t_ref[...] = acc.T.astype(t_ref.dtype)
    elif t_out == "blocked":
        t_ref = rest.pop(0)
        for c in range(acc.shape[0] // key_tile):
            t_ref[c] = acc[c * key_tile:(c + 1) * key_tile, :].T.astype(t_ref.dtype)


def _proj_t(x, w, col_start, batch, t, tm, rot=None, scale=1.0, std_dtype=None, t_out=None, t_dtype=None,
            key_tile=256, name="proj_t"):
    n, kdim = x.shape
    width = 1024
    nt = t // tm
    in_specs = [
        pl.BlockSpec((tm, kdim), lambda i: (i, 0)),
        pl.BlockSpec((kdim, width), lambda i: (0, col_start // width)),
    ]
    args = [x, w]
    if rot is not None:
        nrep = rot[0].shape[0] // tm
        for tab in rot:
            in_specs.append(pl.BlockSpec((tm, LANES), lambda i: (i % nrep, 0)))
            args.append(tab)
    out_shape, out_specs = [], []
    if std_dtype is not None:
        out_shape.append(jax.ShapeDtypeStruct((n, width), std_dtype))
        out_specs.append(pl.BlockSpec((tm, width), lambda i: (i, 0)))
    if t_out == "plain":
        out_shape.append(jax.ShapeDtypeStruct((batch, width, t), t_dtype))
        out_specs.append(pl.BlockSpec((None, width, tm), lambda i: (i // nt, 0, i % nt)))
    elif t_out == "blocked":
        out_shape.append(jax.ShapeDtypeStruct((batch, t // key_tile, width, key_tile), t_dtype))
        out_specs.append(pl.BlockSpec((None, tm // key_tile, width, key_tile), lambda i: (i // nt, i % nt, 0, 0)))
    return pl.pallas_call(
        functools.partial(_proj_t_kernel, rotary=rot is not None, scale=scale,
                          std_out=std_dtype is not None, t_out=t_out, key_tile=key_tile),
        out_shape=out_shape,
        grid=(n // tm,),
        in_specs=in_specs,
        out_specs=out_specs,
        compiler_params=_params("arbitrary"),
        name=name,
    )(*args)


def _rotary_tables(pos):
    half = ROT_DIM // 2
    inv = ROPE_THETA ** (-(jnp.arange(half, dtype=F32) * 2.0) / ROT_DIM)
    ang = pos.astype(F32)[:, None] * inv[None, :]
    cos, sin = jnp.cos(ang), jnp.sin(ang)
    t = pos.shape[0]
    ones = jnp.ones((t, A_QK_DIM - ROT_DIM), F32)
    zeros = jnp.zeros((t, A_QK_DIM - ROT_DIM), F32)
    zh = jnp.zeros((t, half), F32)
    c64 = jnp.concatenate([cos, cos, ones], axis=1)
    sa64 = jnp.concatenate([zh, sin, zeros], axis=1)
    sb64 = jnp.concatenate([-sin, zh, zeros], axis=1)
    tile = lambda a: jnp.concatenate([a, a], axis=1)
    return tile(c64), tile(sa64), tile(sb64)


def _lambda_value(lam_ref):
    lv = lam_ref[...]
    a = jnp.sum(lv[0:1] * lv[1:2], axis=1, keepdims=True)
    b = jnp.sum(lv[2:3] * lv[3:4], axis=1, keepdims=True)
    return jnp.exp(a) - jnp.exp(b) + LAMBDA_INIT


def _split_components(q):
    lane = lax.broadcasted_iota(jnp.int32, q.shape, 1)
    qf = q.astype(F32)
    return jnp.concatenate([jnp.where(lane < A_QK_DIM, qf, 0.0),
                            jnp.where(lane >= A_QK_DIM, qf, 0.0)], axis=0).astype(q.dtype)


def _attn_finish(acc, l, tq, lam, gain, za):
    o = acc[:tq] / l[:tq] - lam * (acc[tq:] / l[tq:])
    o = o * lax.rsqrt(jnp.mean(o * o, axis=1, keepdims=True) + SUBLN_EPS)
    o = o * gain * (1.0 - LAMBDA_INIT)
    return o * _silu(za)


def _attn_prompt_kernel(qt_ref, k_ref, vt_ref, za_ref, gain_ref, lam_ref, o_ref, m_ref, l_ref, acc_ref,
                        s0_ref, s1_ref, *,
                        tq, heads):
    i = pl.program_id(2)
    hs = [slice(n * LANES, (n + 1) * LANES) for n in range(heads)]
    qqts = []
    for sl in hs:
        qt = qt_ref[sl, :].astype(F32)
        row = lax.broadcasted_iota(jnp.int32, qt.shape, 0)
        qqts.append(jnp.concatenate([jnp.where(row < A_QK_DIM, qt, 0.0),
                                     jnp.where(row >= A_QK_DIM, qt, 0.0)], axis=1).astype(BF16))

    m_ref[...] = jnp.full(m_ref.shape, -jnp.inf, F32)
    l_ref[...] = jnp.zeros(l_ref.shape, F32)
    acc_ref[...] = jnp.zeros(acc_ref.shape, F32)

    def score(j, n):
        start = pl.multiple_of(j * tq, tq)
        return jnp.dot(k_ref[pl.ds(start, tq), hs[n]], qqts[n], preferred_element_type=F32)

    ones_rows = jnp.ones((ONES_ROWS, tq), BF16)

    def step(j, cur_ref, nxt_ref, masked):
        for n in range(heads):
            s = cur_ref[n]
            if masked:
                krow = lax.broadcasted_iota(jnp.int32, s.shape, 0)
                qcol = lax.broadcasted_iota(jnp.int32, s.shape, 1) % tq
                s = jnp.where((krow // CHUNK) <= (qcol // CHUNK), s, -jnp.inf)
            m = m_ref[n]
            m_new = jnp.maximum(m, jnp.max(s, axis=0, keepdims=True))
            alpha = jnp.exp2(m - m_new)
            p = jnp.exp2(s - m_new).astype(BF16)
            m_ref[n] = m_new
            lhs = jnp.concatenate([vt_ref[j, hs[n], :], ones_rows], axis=0)
            pv = jnp.dot(lhs, p, preferred_element_type=F32)
            acc_ref[n] = alpha * acc_ref[n] + pv[:A_V_DIM]
            l_ref[n] = alpha * l_ref[n] + pv[A_V_DIM:A_V_DIM + 1]
            if nxt_ref is not None:
                nxt_ref[n] = score(j + 1, n)

    for n in range(heads):
        s0_ref[n] = score(0, n)

    def pair(jj, carry):
        step(2 * jj, s0_ref, s1_ref, False)
        step(2 * jj + 1, s1_ref, s0_ref, False)
        return carry

    lax.fori_loop(0, i // 2, pair, 0)

    @pl.when(i % 2 == 1)
    def _():
        step(i - 1, s0_ref, s1_ref, False)
        step(i, s1_ref, None, True)

    @pl.when(i % 2 == 0)
    def _():
        step(i, s0_ref, None, True)
    lam = _lambda_value(lam_ref)
    for n in range(heads):
        l, acc = l_ref[n], acc_ref[n]
        ot = acc[:, :tq] / l[:, :tq] - lam * (acc[:, tq:] / l[:, tq:])
        o = ot.T
        o = o * lax.rsqrt(jnp.mean(o * o, axis=1, keepdims=True) + SUBLN_EPS)
        o = o * gain_ref[...] * (1.0 - LAMBDA_INIT)
        o_ref[:, hs[n]] = (o * _silu(za_ref[:, hs[n]].astype(F32))).astype(o_ref.dtype)


def _attn_prompt(qt, k, vt, h, gain, lamv, batch, t, tq=256, heads=8):
    nq = t // tq
    n = batch * t
    hw = heads * LANES
    return pl.pallas_call(
        functools.partial(_attn_prompt_kernel, tq=tq, heads=heads),
        out_shape=jax.ShapeDtypeStruct((n, A_WIDTH), BF16),
        grid=(batch, A_HEADS // heads, nq),
        in_specs=[
            pl.BlockSpec((None, hw, tq), lambda b, hh, i: (b, hh, i)),
            pl.BlockSpec((t, hw), lambda b, hh, i: (b, hh)),
            pl.BlockSpec((None, nq, hw, tq), lambda b, hh, i: (b, 0, hh, 0)),
            pl.BlockSpec((tq, hw), lambda b, hh, i: (b * nq + i, H_ZA // hw + hh)),
            pl.BlockSpec((1, LANES), lambda b, hh, i: (0, 0)),
            pl.BlockSpec((4, A_QK_DIM), lambda b, hh, i: (0, 0)),
        ],
        out_specs=pl.BlockSpec((tq, hw), lambda b, hh, i: (b * nq + i, hh)),
        scratch_shapes=[pltpu.VMEM((heads, 1, 2 * tq), F32), pltpu.VMEM((heads, 1, 2 * tq), F32),
                        pltpu.VMEM((heads, A_V_DIM, 2 * tq), F32),
                        pltpu.VMEM((heads, tq, 2 * tq), F32), pltpu.VMEM((heads, tq, 2 * tq), F32)],
        compiler_params=_params("arbitrary", "arbitrary", "arbitrary"),
        name="attn_prompt",
    )(qt, k, vt, h, gain, lamv)


def _chunk_mask(s, ts, q_pos0, k_pos0):
    q_pos = q_pos0 + lax.broadcasted_iota(jnp.int32, s.shape, 0) % ts
    k_pos = k_pos0 + lax.broadcasted_iota(jnp.int32, s.shape, 1)
    return jnp.where((k_pos // CHUNK) <= (q_pos // CHUNK), s, -jnp.inf)


def _rotate_lanes(blk, c, sa, sb):
    half = ROT_DIM // 2
    return blk * c + pltpu.roll(blk, half, 1) * sa + pltpu.roll(blk, LANES - half, 1) * sb


def _attn_sample_kernel(q_ref, k_ref, v_ref, za_ref, cos_ref, sa_ref, sb_ref, kct_ref, vc_ref, gain_ref, lam_ref,
                        o_ref, krot_ref, *, ts, past, mask_cache, mask_new):
    c, sa, sb = cos_ref[...], sa_ref[...], sb_ref[...]
    nt = (((1,), (1,)), ((), ()))
    hs = [slice(n * LANES, (n + 1) * LANES) for n in range(A_HEADS)]
    scores = []
    for n in range(A_HEADS):
        q = (_rotate_lanes(q_ref[:, hs[n]], c, sa, sb) * A_SCALE).astype(BF16)
        k = _rotate_lanes(k_ref[:, hs[n]], c, sa, sb)
        krot_ref[:, hs[n]] = k
        qq = _split_components(q)
        s_c = jnp.dot(qq, kct_ref[n].astype(BF16), preferred_element_type=F32)
        s_n = lax.dot_general(qq, k.astype(BF16), nt, preferred_element_type=F32)
        if mask_cache:
            s_c = _chunk_mask(s_c, ts, past, 0)
        if mask_new:
            s_n = _chunk_mask(s_n, ts, past, past)
        scores.append((s_c, s_n))
    lam = _lambda_value(lam_ref)
    for n, (s_c, s_n) in enumerate(scores):
        m = jnp.maximum(jnp.max(s_c, axis=1, keepdims=True), jnp.max(s_n, axis=1, keepdims=True))
        p_c = jnp.exp(s_c - m)
        p_n = jnp.exp(s_n - m)
        l = jnp.sum(p_c, axis=1, keepdims=True) + jnp.sum(p_n, axis=1, keepdims=True)
        vc = vc_ref[pl.ds(n, past, stride=A_HEADS), :].astype(BF16)
        acc = (jnp.dot(p_c.astype(BF16), vc, preferred_element_type=F32)
               + jnp.dot(p_n.astype(BF16), v_ref[:, hs[n]].astype(BF16), preferred_element_type=F32))
        out = _attn_finish(acc, l, ts, lam, gain_ref[...], za_ref[:, hs[n]].astype(F32))
        o_ref[:, hs[n]] = out.astype(o_ref.dtype)


def _needs_mask(q_pos, k_pos):
    return not bool(((k_pos[None, :] // CHUNK) <= (q_pos[:, None] // CHUNK)).all())


def _attn_sample(hs_act, rot, kct, vc, gain, lamv, batch, ts, past):
    q_pos = past + np.arange(ts)
    seg = lambda col: pl.BlockSpec((ts, A_WIDTH), lambda b: (b, col // A_WIDTH))
    tab = pl.BlockSpec((ts, LANES), lambda b: (0, 0))
    row = pl.BlockSpec((ts, A_WIDTH), lambda b: (b, 0))
    return pl.pallas_call(
        functools.partial(_attn_sample_kernel, ts=ts, past=past,
                          mask_cache=_needs_mask(q_pos, np.arange(past)),
                          mask_new=_needs_mask(q_pos, q_pos)),
        out_shape=[jax.ShapeDtypeStruct((batch * ts, A_WIDTH), BF16),
                   jax.ShapeDtypeStruct((batch * ts, A_WIDTH), F32)],
        grid=(batch,),
        in_specs=[
            seg(COL_Q), seg(COL_K), seg(COL_V), seg(COL_ZA), tab, tab, tab,
            pl.BlockSpec((None, A_HEADS, LANES, past), lambda b: (b, 0, 0, 0)),
            pl.BlockSpec((None, past * A_HEADS, LANES), lambda b: (b, 0, 0)),
            pl.BlockSpec((1, LANES), lambda b: (0, 0)),
            pl.BlockSpec((4, A_QK_DIM), lambda b: (0, 0)),
        ],
        out_specs=[row, row],
        compiler_params=_params("arbitrary"),
        name="attn_sample",
    )(hs_act, hs_act, hs_act, hs_act, *rot, kct, vc, gain, lamv)


def _pool_kernel(u_ref, uprev_ref, hist_ref, zp_ref, pw_ref, ps_ref, o_ref, *, tm, pos0):
    i = pl.program_id(1)
    u = u_ref[...]
    prev = jnp.where(i == 0, hist_ref[...], uprev_ref[...])
    ext = jnp.concatenate([prev, u], axis=0)
    hist_rows = prev.shape[0]
    pos = pos0 + i * tm + lax.broadcasted_iota(jnp.int32, (tm, 1), 0)
    parts = []
    for g, w in enumerate(P_WINDOWS):
        sl = slice(g * P_GROUP_DIM, (g + 1) * P_GROUP_DIM)
        tot = ext[:, sl]
        span = 1
        while span < w:
            tot = tot + pltpu.roll(tot, span, 0)
            span *= 2
        tot = tot[hist_rows:]
        cnt = jnp.minimum(pos + 1, w).astype(F32)
        pooled = tot / cnt - u[:, sl]
        parts.append(jnp.dot(pooled.astype(BF16), pw_ref[g], preferred_element_type=F32))
    mixed = jnp.concatenate(parts, axis=1) * ps_ref[...]
    o_ref[...] = (mixed * _silu(zp_ref[...].astype(F32))).astype(o_ref.dtype)


def _pool(u, u_col, hist, z, z_col, pool_w, pool_scale, batch, t, tm, pos0):
    nt = t // tm
    hr = hist.shape[1]
    per = tm // hr
    ucb, zcb = u_col // P_WIDTH, z_col // P_WIDTH
    return pl.pallas_call(
        functools.partial(_pool_kernel, tm=tm, pos0=pos0),
        out_shape=jax.ShapeDtypeStruct((batch * t, P_WIDTH), BF16),
        grid=(batch, nt),
        in_specs=[
            pl.BlockSpec((tm, P_WIDTH), lambda b, i: (b * nt + i, ucb)),
            pl.BlockSpec((hr, P_WIDTH), lambda b, i: (jnp.maximum((b * nt + i) * per - 1, 0), ucb)),
            pl.BlockSpec((None, hr, P_WIDTH), lambda b, i: (b, 0, 0)),
            pl.BlockSpec((tm, P_WIDTH), lambda b, i: (b * nt + i, zcb)),
            pl.BlockSpec((len(P_WINDOWS), P_GROUP_DIM, P_GROUP_DIM), lambda b, i: (0, 0, 0)),
            pl.BlockSpec((1, P_WIDTH), lambda b, i: (0, 0)),
        ],
        out_specs=pl.BlockSpec((tm, P_WIDTH), lambda b, i: (b * nt + i, 0)),
        compiler_params=_params("arbitrary", "arbitrary"),
        name="pool",
    )(u, u, hist, z, pool_w, pool_scale)


def _mem_attn_kernel(q_ref, mk_ref, mv_ref, zm_ref, o_ref):
    nt = (((1,), (1,)), ((), ()))
    hs = [slice(n * M_HEAD_DIM, (n + 1) * M_HEAD_DIM) for n in range(M_HEADS)]
    scores = [lax.dot_general(q_ref[:, sl].astype(BF16), mk_ref[:, sl].astype(BF16), nt,
                              preferred_element_type=F32) * M_SCALE for sl in hs]
    for sl, s in zip(hs, scores):
        m = jnp.max(s, axis=1, keepdims=True)
        p = jnp.exp(s - m)
        l = jnp.sum(p, axis=1, keepdims=True)
        o = jnp.dot(p.astype(BF16), mv_ref[:, sl].astype(BF16), preferred_element_type=F32) / l
        o_ref[:, sl] = (o * _silu(zm_ref[:, sl].astype(F32))).astype(o_ref.dtype)


def _mem_attn(act, q_col, z_col, mk, mv, batch, t, tq):
    nq = t // tq
    qcb, zcb = q_col // M_WIDTH, z_col // M_WIDTH
    return pl.pallas_call(
        _mem_attn_kernel,
        out_shape=jax.ShapeDtypeStruct((batch * t, M_WIDTH), BF16),
        grid=(batch, nq),
        in_specs=[
            pl.BlockSpec((tq, M_WIDTH), lambda b, i: (b * nq + i, qcb)),
            pl.BlockSpec((M_TOKENS, M_WIDTH), lambda b, i: (b, 0)),
            pl.BlockSpec((M_TOKENS, M_WIDTH), lambda b, i: (b, 0)),
            pl.BlockSpec((tq, M_WIDTH), lambda b, i: (b * nq + i, zcb)),
        ],
        out_specs=pl.BlockSpec((tq, M_WIDTH), lambda b, i: (b * nq + i, 0)),
        compiler_params=_params("arbitrary", "arbitrary"),
        name="mem_attn",
    )(act, mk, mv, act)


def _merge_kernel(ba_ref, bp_ref, bm_ref, g0_ref, g1_ref, g2_ref, wb_ref, o_ref):
    acc = None
    for n, (b_ref, g_ref) in enumerate(((ba_ref, g0_ref), (bp_ref, g1_ref), (bm_ref, g2_ref))):
        proj = jnp.dot(b_ref[...], wb_ref[n], preferred_element_type=F32)
        term = _sigmoid(g_ref[...].astype(F32)) * proj
        acc = term if acc is None else acc + term
    o_ref[...] = acc.astype(o_ref.dtype)


def _merge(ba, bp, bm, h, g_col, wb, tm, tn):
    n = ba.shape[0]
    width = ba.shape[1]
    gspec = lambda k: pl.BlockSpec((tm, tn), lambda j, i: (i, (g_col + k * D_MODEL) // tn + j))
    bspec = pl.BlockSpec((tm, width), lambda j, i: (i, 0))
    return pl.pallas_call(
        _merge_kernel,
        out_shape=jax.ShapeDtypeStruct((n, D_MODEL), BF16),
        grid=(D_MODEL // tn, n // tm),
        in_specs=[bspec, bspec, bspec, gspec(0), gspec(1), gspec(2),
                  pl.BlockSpec((N_BRANCH, width, tn), lambda j, i: (0, 0, j))],
        out_specs=pl.BlockSpec((tm, tn), lambda j, i: (i, j)),
        compiler_params=_params("arbitrary", "arbitrary"),
        name="merge",
    )(ba, bp, bm, h, h, h, wb)


def _out_kernel(m_ref, w_ref, x_ref, g_ref, b_ref, o_ref):
    z = DEEPNORM_ALPHA * x_ref[...] + jnp.dot(m_ref[...], w_ref[...], preferred_element_type=F32)
    mu = jnp.mean(z, axis=1, keepdims=True)
    zc = z - mu
    var = jnp.mean(zc * zc, axis=1, keepdims=True)
    o_ref[...] = zc * lax.rsqrt(var + LN_EPS) * g_ref[...] + b_ref[...]


def _out(merged, w_out, x, ln_g, ln_b, tm):
    n = merged.shape[0]
    row = pl.BlockSpec((tm, D_MODEL), lambda i: (i, 0))
    vec = pl.BlockSpec((1, D_MODEL), lambda i: (0, 0))
    return pl.pallas_call(
        _out_kernel,
        out_shape=jax.ShapeDtypeStruct((n, D_MODEL), F32),
        grid=(n // tm,),
        in_specs=[row, pl.BlockSpec((D_MODEL, D_MODEL), lambda i: (0, 0)), row, vec, vec],
        out_specs=row,
        compiler_params=_params("arbitrary"),
        name="out_ln",
    )(merged, w_out, x, ln_g, ln_b)


def _rest_col_block(tn):
    za_blocks = (COL_UP - COL_ZA) // tn
    return lambda j: jnp.where(j < za_blocks, COL_ZA // tn + j, COL_ZP // tn + (j - za_blocks))


def _layer_prompt(x2d, batch, t, w, pool_hist, mk, mv):
    xb = x2d.astype(BF16)
    rot = _rotary_tables(jnp.arange(t))
    tm, tn, tp = 1024, 1024, 512
    u = _proj(xb, w["w_in"], lambda j: COL_UP // tn + j, 1024, F32, tm, tn, name="proj_u")
    h = _proj(xb, w["w_in"], _rest_col_block(tn), H_COLS, BF16, tm, tn, name="proj_rest")
    (qt,) = _proj_t(xb, w["w_in"], COL_Q, batch, t, tp, rot=rot, scale=A_SCALE * math.log2(math.e),
                    t_out="plain", t_dtype=BF16, name="proj_qt")
    kb, kt = _proj_t(xb, w["w_in"], COL_K, batch, t, tp, rot=rot, std_dtype=BF16,
                     t_out="plain", t_dtype=F32, name="proj_kt")
    v, vt = _proj_t(xb, w["w_in"], COL_V, batch, t, tp, std_dtype=F32,
                    t_out="blocked", t_dtype=BF16, name="proj_vt")
    b_a = _attn_prompt(qt, kb, vt, h, w["gain"], w["lamv"], batch, t)
    b_p = _pool(u, 0, pool_hist, h, H_ZP, w["pool_w"], w["pool_scale"], batch, t, 512, 0)
    b_m = _mem_attn(h, H_QM, H_ZM, mk, mv, batch, t, 512)
    merged = _merge(b_a, b_p, b_m, h, H_GM, w["w_branch"], 512, 1024)
    y = _out(merged, w["w_out"], x2d, w["ln_g"], w["ln_b"], 512)
    return y, kt, v, u


def _layer_sample(x2d, batch, t, past, w, kt_hist, v_hist, pool_hist, mk, mv):
    n = batch * t
    xb = x2d.astype(BF16)
    rot = _rotary_tables(past + jnp.arange(t))
    act = _proj(xb, w["w_in"], lambda j: j, IN_COLS, F32, n, 1024, name="proj_all")
    b_a, k = _attn_sample(act, rot, kt_hist, v_hist, w["gain"], w["lamv"], batch, t, past)
    b_p = _pool(act, COL_UP, pool_hist, act, COL_ZP, w["pool_w"], w["pool_scale"], batch, t, t, past)
    b_m = _mem_attn(act, COL_QM, COL_ZM, mk, mv, batch, t, t)
    merged = _merge(b_a, b_p, b_m, act, COL_GM, w["w_branch"], n, 1024)
    y = _out(merged, w["w_out"], x2d, w["ln_g"], w["ln_b"], n)
    return y, k, act


def kernel(x_prompt, x_sample, cache_attn_k, cache_attn_v, cache_mem_k, cache_mem_v, state_pool, mem_prompt, w_in, w_mem_kv, lambda_q1, lambda_k1, lambda_q2, lambda_k2, subln_gain, pool_w, pool_scale, w_branch, w_out, ln_gain, ln_bias):
    bp, tp, d = x_prompt.shape
    bs, ts, _ = x_sample.shape
    past = cache_attn_k.shape[2]
    hist_rows = P_HIST + 1

    def run_layer(l, xp2d, xs2d):
        w = {
            "w_in": w_in[l].astype(BF16),
            "gain": subln_gain[l].reshape(1, A_V_DIM),
            "lamv": jnp.stack([lambda_q1[l], lambda_k1[l], lambda_q2[l], lambda_k2[l]]),
            "pool_w": pool_w[l].astype(BF16),
            "pool_scale": pool_scale[l].reshape(1, P_WIDTH),
            "w_branch": w_branch[l].astype(BF16),
            "w_out": w_out[l].astype(BF16),
            "ln_g": ln_gain[l].reshape(1, d),
            "ln_b": ln_bias[l].reshape(1, d),
        }
        memb = mem_prompt.reshape(bp * M_TOKENS, d).astype(BF16)
        wkv = w_mem_kv[l].astype(BF16)
        mk = _proj(memb, wkv, lambda j: j, M_WIDTH, F32, 512, 1024, name="proj_mk")
        mv = _proj(memb, wkv, lambda j: M_WIDTH // 1024 + j, M_WIDTH, F32, 512, 1024, name="proj_mv")

        zero_hist = jnp.zeros((bp, hist_rows, P_WIDTH), F32)
        y_p, k_p, v_p, u_p = _layer_prompt(xp2d, bp, tp, w, zero_hist, mk, mv)

        hist_s = jnp.pad(state_pool[l], ((0, 0), (hist_rows - P_HIST, 0), (0, 0)))
        kt_hist = jnp.transpose(cache_attn_k[l], (0, 2, 3, 4, 1)).reshape(bs, A_HEADS, 2 * A_QK_DIM, past)
        y_s, k_s, act_s = _layer_sample(
            xs2d, bs, ts, past, w,
            kt_hist, cache_attn_v[l].reshape(bs, past * A_HEADS, A_V_DIM),
            hist_s, cache_mem_k[l].reshape(bs * M_TOKENS, M_WIDTH), cache_mem_v[l].reshape(bs * M_TOKENS, M_WIDTH))
        v_s = act_s[:, COL_V:COL_V + A_WIDTH]
        u_s = act_s[:, COL_UP:COL_UP + P_WIDTH]

        pool_p = jnp.concatenate([zero_hist, u_p.reshape(bp, tp, P_WIDTH)], axis=1)[:, -P_HIST:]
        pool_s = jnp.concatenate([hist_s, u_s.reshape(bs, ts, P_WIDTH)], axis=1)[:, -P_HIST:]
        k_p = jnp.transpose(k_p.reshape(bp, A_HEADS, 2, A_QK_DIM, tp), (0, 4, 1, 2, 3))
        outs = (k_p, v_p.reshape(bp, tp, A_HEADS, A_V_DIM),
                mk.reshape(bp, M_TOKENS, M_HEADS, M_HEAD_DIM), mv.reshape(bp, M_TOKENS, M_HEADS, M_HEAD_DIM),
                pool_p,
                k_s.reshape(bs, ts, A_HEADS, 2, A_QK_DIM), v_s.reshape(bs, ts, A_HEADS, A_V_DIM), pool_s)
        return y_p, y_s, outs

    yp = x_prompt.reshape(bp * tp, d)
    ys = x_sample.reshape(bs * ts, d)
    per_layer = []
    for l in range(w_in.shape[0]):
        yp, ys, outs = run_layer(l, yp, ys)
        per_layer.append(outs)
    stacked = [jnp.stack([o[n] for o in per_layer]) for n in range(8)]
    return (yp.reshape(bp, tp, d), ys.reshape(bs, ts, d), *stacked)
```

```python
import functools
import math

import numpy as np
import jax
import jax.numpy as jnp
from jax import lax
from jax.experimental import pallas as pl
from jax.experimental.pallas import tpu as pltpu

F32 = jnp.float32
BF16 = jnp.bfloat16

V7X_VMEM_LIMIT_BYTES = 56 * 1024 * 1024
LANES = 128

D_MODEL = 2048
CHUNK = 64
A_HEADS = 8
A_QK_DIM = 64
A_V_DIM = 128
A_WIDTH = A_HEADS * A_V_DIM
A_SCALE = A_QK_DIM ** -0.5
ROT_DIM = A_QK_DIM // 4
ROPE_THETA = 500000.0
SUBLN_EPS = 1e-5
P_WINDOWS = (2, 4, 8, 16)
P_GROUP_DIM = 256
P_WIDTH = len(P_WINDOWS) * P_GROUP_DIM
P_HIST = max(P_WINDOWS) - 1
M_TOKENS = 256
M_HEADS = 4
M_HEAD_DIM = 256
M_WIDTH = M_HEADS * M_HEAD_DIM
M_SCALE = M_HEAD_DIM ** -0.5
N_BRANCH = 3
DEPTH = 1
DEEPNORM_ALPHA = (2.0 * DEPTH) ** 0.25
LN_EPS = 1e-5
LAMBDA_INIT = 0.8 - 0.6 * math.exp(-0.3 * 0)
ONES_ROWS = 16

COL_Q, COL_K, COL_V, COL_ZA, COL_UP, COL_ZP, COL_QM, COL_ZM, COL_GM = (
    0, 1024, 2048, 3072, 4096, 5120, 6144, 7168, 8192)
IN_COLS = 14336
H_ZA, H_ZP, H_QM, H_ZM, H_GM = 0, 1024, 2048, 3072, 4096
H_COLS = 10240


def _params(*sem):
    return pltpu.CompilerParams(dimension_semantics=sem, vmem_limit_bytes=V7X_VMEM_LIMIT_BYTES)


def _sigmoid(z):
    return 1.0 / (1.0 + jnp.exp(-z))


def _silu(z):
    return z * _sigmoid(z)


def _proj_kernel(x_ref, w_ref, *rest, rotary, scale):
    acc = jnp.dot(x_ref[...].astype(BF16), w_ref[...].astype(BF16), preferred_element_type=F32)
    if rotary:
        cos_ref, sa_ref, sb_ref, o_ref = rest
        c, sa, sb = cos_ref[...], sa_ref[...], sb_ref[...]
        half = ROT_DIM // 2
        parts = []
        for k in range(acc.shape[1] // LANES):
            blk = acc[:, k * LANES:(k + 1) * LANES]
            lo = pltpu.roll(blk, half, 1)
            hi = pltpu.roll(blk, LANES - half, 1)
            parts.append(blk * c + lo * sa + hi * sb)
        acc = jnp.concatenate(parts, axis=1)
    else:
        (o_ref,) = rest
    if scale != 1.0:
        acc = acc * scale
    o_ref[...] = acc.astype(o_ref.dtype)


def _proj(x, w, col_block, ncols, out_dtype, tm, tn, rot=None, scale=1.0, name="proj"):
    n, kdim = x.shape
    grid = (n // tm, ncols // tn)
    in_specs = [
        pl.BlockSpec((tm, kdim), lambda i, j: (i, 0)),
        pl.BlockSpec((kdim, tn), lambda i, j: (0, col_block(j))),
    ]
    args = [x, w]
    if rot is not None:
        nrep = rot[0].shape[0] // tm
        for t in rot:
            in_specs.append(pl.BlockSpec((tm, LANES), lambda i, j: (i % nrep, 0)))
            args.append(t)
    return pl.pallas_call(
        functools.partial(_proj_kernel, rotary=rot is not None, scale=scale),
        out_shape=jax.ShapeDtypeStruct((n, ncols), out_dtype),
        grid=grid,
        in_specs=in_specs,
        out_specs=pl.BlockSpec((tm, tn), lambda i, j: (i, j)),
        compiler_params=_params("parallel", "arbitrary"),
        name=name,
    )(*args)


def _proj_t_kernel(x_ref, w_ref, *rest, rotary, scale, std_out, t_out, key_tile, emit_xb):
    rest = list(rest)
    xb = x_ref[...].astype(BF16)
    acc = jnp.dot(xb, w_ref[...].astype(BF16), preferred_element_type=F32)
    if emit_xb:
        rest.pop()[...] = xb
    if rotary:
        c, sa, sb = rest.pop(0)[...], rest.pop(0)[...], rest.pop(0)[...]
        half = ROT_DIM // 2
        parts = []
        for k in range(acc.shape[1] // LANES):
            blk = acc[:, k * LANES:(k + 1) * LANES]
            lo = pltpu.roll(blk, half, 1)
            hi = pltpu.roll(blk, LANES - half, 1)
            parts.append(blk * c + lo * sa + hi * sb)
        acc = jnp.concatenate(parts, axis=1)
    if scale != 1.0:
        acc = acc * scale
    if std_out:
        o_ref = rest.pop(0)
        o_ref[...] = acc.astype(o_ref.dtype)
    if t_out == "plain":
        t_ref = rest.pop(0)
        t_ref[...] = acc.T.astype(t_ref.dtype)
    elif t_out == "blocked":
        t_ref = rest.pop(0)
        for c in range(acc.shape[0] // key_tile):
            t_ref[c] = acc[c * key_tile:(c + 1) * key_tile, :].T.astype(t_ref.dtype)


def _proj_t(x, w, col_start, batch, t, tm, rot=None, scale=1.0, std_dtype=None, t_out=None, t_dtype=None,
            key_tile=256, emit_xb=False, name="proj_t"):
    n, kdim = x.shape
    width = 1024
    nt = t // tm
    in_specs = [
        pl.BlockSpec((tm, kdim), lambda i: (i, 0)),
        pl.BlockSpec((kdim, width), lambda i: (0, col_start // width)),
    ]
    args = [x, w]
    if rot is not None:
        nrep = rot[0].shape[0] // tm
        for tab in rot:
            in_specs.append(pl.BlockSpec((tm, LANES), lambda i: (i % nrep, 0)))
            args.append(tab)
    out_shape, out_specs = [], []
    if std_dtype is not None:
        out_shape.append(jax.ShapeDtypeStruct((n, width), std_dtype))
        out_specs.append(pl.BlockSpec((tm, width), lambda i: (i, 0)))
    if t_out == "plain":
        out_shape.append(jax.ShapeDtypeStruct((batch, width, t), t_dtype))
        out_specs.append(pl.BlockSpec((None, width, tm), lambda i: (i // nt, 0, i % nt)))
    elif t_out == "blocked":
        out_shape.append(jax.ShapeDtypeStruct((batch, t // key_tile, width, key_tile), t_dtype))
        out_specs.append(pl.BlockSpec((None, tm // key_tile, width, key_tile), lambda i: (i // nt, i % nt, 0, 0)))
    if emit_xb:
        out_shape.append(jax.ShapeDtypeStruct((n, kdim), BF16))
        out_specs.append(pl.BlockSpec((tm, kdim), lambda i: (i, 0)))
    return pl.pallas_call(
        functools.partial(_proj_t_kernel, rotary=rot is not None, scale=scale,
                          std_out=std_dtype is not None, t_out=t_out, key_tile=key_tile, emit_xb=emit_xb),
        out_shape=out_shape,
        grid=(n // tm,),
        in_specs=in_specs,
        out_specs=out_specs,
        compiler_params=_params("arbitrary"),
        name=name,
    )(*args)


def _rotary_tables(pos):
    half = ROT_DIM // 2
    inv = ROPE_THETA ** (-(jnp.arange(half, dtype=F32) * 2.0) / ROT_DIM)
    ang = pos.astype(F32)[:, None] * inv[None, :]
    cos, sin = jnp.cos(ang), jnp.sin(ang)
    t = pos.shape[0]
    ones = jnp.ones((t, A_QK_DIM - ROT_DIM), F32)
    zeros = jnp.zeros((t, A_QK_DIM - ROT_DIM), F32)
    zh = jnp.zeros((t, half), F32)
    c64 = jnp.concatenate([cos, cos, ones], axis=1)
    sa64 = jnp.concatenate([zh, sin, zeros], axis=1)
    sb64 = jnp.concatenate([-sin, zh, zeros], axis=1)
    tile = lambda a: jnp.concatenate([a, a], axis=1)
    return tile(c64), tile(sa64), tile(sb64)


def _lambda_value(lam_ref):
    lv = lam_ref[...]
    a = jnp.sum(lv[0:1] * lv[1:2], axis=1, keepdims=True)
    b = jnp.sum(lv[2:3] * lv[3:4], axis=1, keepdims=True)
    return jnp.exp(a) - jnp.exp(b) + LAMBDA_INIT


def _split_components(q):
    lane = lax.broadcasted_iota(jnp.int32, q.shape, 1)
    qf = q.astype(F32)
    return jnp.concatenate([jnp.where(lane < A_QK_DIM, qf, 0.0),
                            jnp.where(lane >= A_QK_DIM, qf, 0.0)], axis=0).astype(q.dtype)


def _attn_finish(acc, l, tq, lam, gain, za):
    o = acc[:tq] / l[:tq] - lam * (acc[tq:] / l[tq:])
    o = o * lax.rsqrt(jnp.mean(o * o, axis=1, keepdims=True) + SUBLN_EPS)
    o = o * gain * (1.0 - LAMBDA_INIT)
    return o * _silu(za)


def _attn_prompt_kernel(qt_ref, k_ref, vt_ref, za_ref, gain_ref, lam_ref, o_ref, m_ref, l_ref, acc_ref,
                        s0_ref, s1_ref, *,
                        tq, heads):
    i = pl.program_id(2)
    hs = [slice(n * LANES, (n + 1) * LANES) for n in range(heads)]
    qqts = []
    for sl in hs:
        qt = qt_ref[sl, :].astype(F32)
        row = lax.broadcasted_iota(jnp.int32, qt.shape, 0)
        qqts.append(jnp.concatenate([jnp.where(row < A_QK_DIM, qt, 0.0),
                                     jnp.where(row >= A_QK_DIM, qt, 0.0)], axis=1).astype(BF16))

    m_ref[...] = jnp.full(m_ref.shape, -jnp.inf, F32)
    l_ref[...] = jnp.zeros(l_ref.shape, F32)
    acc_ref[...] = jnp.zeros(acc_ref.shape, F32)

    def score(j, n):
        start = pl.multiple_of(j * tq, tq)
        return jnp.dot(k_ref[pl.ds(start, tq), hs[n]], qqts[n], preferred_element_type=F32)

    ones_rows = jnp.ones((ONES_ROWS, tq), BF16)

    def step(j, cur_ref, nxt_ref, masked):
        for n in range(heads):
            s = cur_ref[n]
            if masked:
                krow = lax.broadcasted_iota(jnp.int32, s.shape, 0)
                qcol = lax.broadcasted_iota(jnp.int32, s.shape, 1) % tq
                s = jnp.where((krow // CHUNK) <= (qcol // CHUNK), s, -jnp.inf)
            m = m_ref[n]
            m_new = jnp.maximum(m, jnp.max(s, axis=0, keepdims=True))
            alpha = jnp.exp2(m - m_new)
            p = jnp.exp2(s - m_new).astype(BF16)
            m_ref[n] = m_new
            lhs = jnp.concatenate([vt_ref[j, hs[n], :], ones_rows], axis=0)
            pv = jnp.dot(lhs, p, preferred_element_type=F32)
            acc_ref[n] = alpha * acc_ref[n] + pv[:A_V_DIM]
            l_ref[n] = alpha * l_ref[n] + pv[A_V_DIM:A_V_DIM + 1]
            if nxt_ref is not None:
                nxt_ref[n] = score(j + 1, n)

    for n in range(heads):
        s0_ref[n] = score(0, n)

    def pair(jj, carry):
        step(2 * jj, s0_ref, s1_ref, False)
        step(2 * jj + 1, s1_ref, s0_ref, False)
        return carry

    lax.fori_loop(0, i // 2, pair, 0)

    @pl.when(i % 2 == 1)
    def _():
        step(i - 1, s0_ref, s1_ref, False)
        step(i, s1_ref, None, True)

    @pl.when(i % 2 == 0)
    def _():
        step(i, s0_ref, None, True)
    lam = _lambda_value(lam_ref)
    for n in range(heads):
        l, acc = l_ref[n], acc_ref[n]
        ot = acc[:, :tq] / l[:, :tq] - lam * (acc[:, tq:] / l[:, tq:])
        o = ot.T
        o = o * lax.rsqrt(jnp.mean(o * o, axis=1, keepdims=True) + SUBLN_EPS)
        o = o * gain_ref[...] * (1.0 - LAMBDA_INIT)
        o_ref[:, hs[n]] = (o * _silu(za_ref[:, hs[n]].astype(F32))).astype(o_ref.dtype)


def _attn_prompt(qt, k, vt, h, gain, lamv, batch, t, tq=256, heads=8):
    nq = t // tq
    n = batch * t
    hw = heads * LANES
    return pl.pallas_call(
        functools.partial(_attn_prompt_kernel, tq=tq, heads=heads),
        out_shape=jax.ShapeDtypeStruct((n, A_WIDTH), BF16),
        grid=(batch, A_HEADS // heads, nq),
        in_specs=[
            pl.BlockSpec((None, hw, tq), lambda b, hh, i: (b, hh, i)),
            pl.BlockSpec((t, hw), lambda b, hh, i: (b, hh)),
            pl.BlockSpec((None, nq, hw, tq), lambda b, hh, i: (b, 0, hh, 0)),
            pl.BlockSpec((tq, hw), lambda b, hh, i: (b * nq + i, H_ZA // hw + hh)),
            pl.BlockSpec((1, LANES), lambda b, hh, i: (0, 0)),
            pl.BlockSpec((4, A_QK_DIM), lambda b, hh, i: (0, 0)),
        ],
        out_specs=pl.BlockSpec((tq, hw), lambda b, hh, i: (b * nq + i, hh)),
        scratch_shapes=[pltpu.VMEM((heads, 1, 2 * tq), F32), pltpu.VMEM((heads, 1, 2 * tq), F32),
                        pltpu.VMEM((heads, A_V_DIM, 2 * tq), F32),
                        pltpu.VMEM((heads, tq, 2 * tq), F32), pltpu.VMEM((heads, tq, 2 * tq), F32)],
        compiler_params=_params("arbitrary", "arbitrary", "arbitrary"),
        name="attn_prompt",
    )(qt, k, vt, h, gain, lamv)


def _chunk_mask(s, ts, q_pos0, k_pos0):
    q_pos = q_pos0 + lax.broadcasted_iota(jnp.int32, s.shape, 0) % ts
    k_pos = k_pos0 + lax.broadcasted_iota(jnp.int32, s.shape, 1)
    return jnp.where((k_pos // CHUNK) <= (q_pos // CHUNK), s, -jnp.inf)


def _rotate_lanes(blk, c, sa, sb):
    half = ROT_DIM // 2
    return blk * c + pltpu.roll(blk, half, 1) * sa + pltpu.roll(blk, LANES - half, 1) * sb


def _attn_sample_kernel(q_ref, k_ref, v_ref, za_ref, cos_ref, sa_ref, sb_ref, kct_ref, vc_ref, gain_ref, lam_ref,
                        o_ref, krot_ref, *, ts, past, mask_cache, mask_new):
    c, sa, sb = cos_ref[...], sa_ref[...], sb_ref[...]
    nt = (((1,), (1,)), ((), ()))
    hs = [slice(n * LANES, (n + 1) * LANES) for n in range(A_HEADS)]
    scores = []
    for n in range(A_HEADS):
        q = (_rotate_lanes(q_ref[:, hs[n]], c, sa, sb) * A_SCALE).astype(BF16)
        k = _rotate_lanes(k_ref[:, hs[n]], c, sa, sb)
        krot_ref[:, hs[n]] = k
        qq = _split_components(q)
        s_c = jnp.dot(qq, kct_ref[n].astype(BF16), preferred_element_type=F32)
        s_n = lax.dot_general(qq, k.astype(BF16), nt, preferred_element_type=F32)
        if mask_cache:
            s_c = _chunk_mask(s_c, ts, past, 0)
        if mask_new:
            s_n = _chunk_mask(s_n, ts, past, past)
        scores.append((s_c, s_n))
    lam = _lambda_value(lam_ref)
    for n, (s_c, s_n) in enumerate(scores):
        m = jnp.maximum(jnp.max(s_c, axis=1, keepdims=True), jnp.max(s_n, axis=1, keepdims=True))
        p_c = jnp.exp(s_c - m)
        p_n = jnp.exp(s_n - m)
        l = jnp.sum(p_c, axis=1, keepdims=True) + jnp.sum(p_n, axis=1, keepdims=True)
        vc = vc_ref[pl.ds(n, past, stride=A_HEADS), :].astype(BF16)
        acc = (jnp.dot(p_c.astype(BF16), vc, preferred_element_type=F32)
               + jnp.dot(p_n.astype(BF16), v_ref[:, hs[n]].astype(BF16), preferred_element_type=F32))
        out = _attn_finish(acc, l, ts, lam, gain_ref[...], za_ref[:, hs[n]].astype(F32))
        o_ref[:, hs[n]] = out.astype(o_ref.dtype)


def _needs_mask(q_pos, k_pos):
    return not bool(((k_pos[None, :] // CHUNK) <= (q_pos[:, None] // CHUNK)).all())


def _attn_sample(hs_act, rot, kct, vc, gain, lamv, batch, ts, past):
    q_pos = past + np.arange(ts)
    seg = lambda col: pl.BlockSpec((ts, A_WIDTH), lambda b: (b, col // A_WIDTH))
    tab = pl.BlockSpec((ts, LANES), lambda b: (0, 0))
    row = pl.BlockSpec((ts, A_WIDTH), lambda b: (b, 0))
    return pl.pallas_call(
        functools.partial(_attn_sample_kernel, ts=ts, past=past,
                          mask_cache=_needs_mask(q_pos, np.arange(past)),
                          mask_new=_needs_mask(q_pos, q_pos)),
        out_shape=[jax.ShapeDtypeStruct((batch * ts, A_WIDTH), BF16),
                   jax.ShapeDtypeStruct((batch * ts, A_WIDTH), F32)],
        grid=(batch,),
        in_specs=[
            seg(COL_Q), seg(COL_K), seg(COL_V), seg(COL_ZA), tab, tab, tab,
            pl.BlockSpec((None, A_HEADS, LANES, past), lambda b: (b, 0, 0, 0)),
            pl.BlockSpec((None, past * A_HEADS, LANES), lambda b: (b, 0, 0)),
            pl.BlockSpec((1, LANES), lambda b: (0, 0)),
            pl.BlockSpec((4, A_QK_DIM), lambda b: (0, 0)),
        ],
        out_specs=[row, row],
        compiler_params=_params("arbitrary"),
        name="attn_sample",
    )(hs_act, hs_act, hs_act, hs_act, *rot, kct, vc, gain, lamv)


def _pool_kernel(u_ref, uprev_ref, hist_ref, zp_ref, pw_ref, ps_ref, o_ref, *, tm, pos0):
    i = pl.program_id(1)
    u = u_ref[...]
    prev = jnp.where(i == 0, hist_ref[...], uprev_ref[...])
    ext = jnp.concatenate([prev, u], axis=0)
    hist_rows = prev.shape[0]
    pos = pos0 + i * tm + lax.broadcasted_iota(jnp.int32, (tm, 1), 0)
    parts = []
    for g, w in enumerate(P_WINDOWS):
        sl = slice(g * P_GROUP_DIM, (g + 1) * P_GROUP_DIM)
        tot = ext[:, sl]
        span = 1
        while span < w:
            tot = tot + pltpu.roll(tot, span, 0)
            span *= 2
        tot = tot[hist_rows:]
        cnt = jnp.minimum(pos + 1, w).astype(F32)
        pooled = tot / cnt - u[:, sl]
        parts.append(jnp.dot(pooled.astype(BF16), pw_ref[g], preferred_element_type=F32))
    mixed = jnp.concatenate(parts, axis=1) * ps_ref[...]
    o_ref[...] = (mixed * _silu(zp_ref[...].astype(F32))).astype(o_ref.dtype)


def _pool(u, u_col, hist, z, z_col, pool_w, pool_scale, batch, t, tm, pos0):
    nt = t // tm
    hr = hist.shape[1]
    per = tm // hr
    ucb, zcb = u_col // P_WIDTH, z_col // P_WIDTH
    return pl.pallas_call(
        functools.partial(_pool_kernel, tm=tm, pos0=pos0),
        out_shape=jax.ShapeDtypeStruct((batch * t, P_WIDTH), BF16),
        grid=(batch, nt),
        in_specs=[
            pl.BlockSpec((tm, P_WIDTH), lambda b, i: (b * nt + i, ucb)),
            pl.BlockSpec((hr, P_WIDTH), lambda b, i: (jnp.maximum((b * nt + i) * per - 1, 0), ucb)),
            pl.BlockSpec((None, hr, P_WIDTH), lambda b, i: (b, 0, 0)),
            pl.BlockSpec((tm, P_WIDTH), lambda b, i: (b * nt + i, zcb)),
            pl.BlockSpec((len(P_WINDOWS), P_GROUP_DIM, P_GROUP_DIM), lambda b, i: (0, 0, 0)),
            pl.BlockSpec((1, P_WIDTH), lambda b, i: (0, 0)),
        ],
        out_specs=pl.BlockSpec((tm, P_WIDTH), lambda b, i: (b * nt + i, 0)),
        compiler_params=_params("arbitrary", "arbitrary"),
        name="pool",
    )(u, u, hist, z, pool_w, pool_scale)


def _mem_attn_kernel(q_ref, mk_ref, mv_ref, zm_ref, o_ref):
    nt = (((1,), (1,)), ((), ()))
    hs = [slice(n * M_HEAD_DIM, (n + 1) * M_HEAD_DIM) for n in range(M_HEADS)]
    scores = [lax.dot_general(q_ref[:, sl].astype(BF16), mk_ref[:, sl].astype(BF16), nt,
                              preferred_element_type=F32) * M_SCALE for sl in hs]
    for sl, s in zip(hs, scores):
        m = jnp.max(s, axis=1, keepdims=True)
        p = jnp.exp(s - m)
        l = jnp.sum(p, axis=1, keepdims=True)
        o = jnp.dot(p.astype(BF16), mv_ref[:, sl].astype(BF16), preferred_element_type=F32) / l
        o_ref[:, sl] = (o * _silu(zm_ref[:, sl].astype(F32))).astype(o_ref.dtype)


def _mem_attn(act, q_col, z_col, mk, mv, batch, t, tq):
    nq = t // tq
    qcb, zcb = q_col // M_WIDTH, z_col // M_WIDTH
    return pl.pallas_call(
        _mem_attn_kernel,
        out_shape=jax.ShapeDtypeStruct((batch * t, M_WIDTH), BF16),
        grid=(batch, nq),
        in_specs=[
            pl.BlockSpec((tq, M_WIDTH), lambda b, i: (b * nq + i, qcb)),
            pl.BlockSpec((M_TOKENS, M_WIDTH), lambda b, i: (b, 0)),
            pl.BlockSpec((M_TOKENS, M_WIDTH), lambda b, i: (b, 0)),
            pl.BlockSpec((tq, M_WIDTH), lambda b, i: (b * nq + i, zcb)),
        ],
        out_specs=pl.BlockSpec((tq, M_WIDTH), lambda b, i: (b * nq + i, 0)),
        compiler_params=_params("arbitrary", "arbitrary"),
        name="mem_attn",
    )(act, mk, mv, act)


def _merge_kernel(ba_ref, bp_ref, bm_ref, g0_ref, g1_ref, g2_ref, wb_ref, o_ref):
    acc = None
    for n, (b_ref, g_ref) in enumerate(((ba_ref, g0_ref), (bp_ref, g1_ref), (bm_ref, g2_ref))):
        proj = jnp.dot(b_ref[...], wb_ref[n], preferred_element_type=F32)
        term = _sigmoid(g_ref[...].astype(F32)) * proj
        acc = term if acc is None else acc + term
    o_ref[...] = acc.astype(o_ref.dtype)


def _merge(ba, bp, bm, h, g_col, wb, tm, tn):
    n = ba.shape[0]
    width = ba.shape[1]
    gspec = lambda k: pl.BlockSpec((tm, tn), lambda j, i: (i, (g_col + k * D_MODEL) // tn + j))
    bspec = pl.BlockSpec((tm, width), lambda j, i: (i, 0))
    return pl.pallas_call(
        _merge_kernel,
        out_shape=jax.ShapeDtypeStruct((n, D_MODEL), BF16),
        grid=(D_MODEL // tn, n // tm),
        in_specs=[bspec, bspec, bspec, gspec(0), gspec(1), gspec(2),
                  pl.BlockSpec((N_BRANCH, width, tn), lambda j, i: (0, 0, j))],
        out_specs=pl.BlockSpec((tm, tn), lambda j, i: (i, j)),
        compiler_params=_params("arbitrary", "arbitrary"),
        name="merge",
    )(ba, bp, bm, h, h, h, wb)


def _out_kernel(m_ref, w_ref, x_ref, g_ref, b_ref, o_ref):
    z = DEEPNORM_ALPHA * x_ref[...] + jnp.dot(m_ref[...], w_ref[...], preferred_element_type=F32)
    mu = jnp.mean(z, axis=1, keepdims=True)
    zc = z - mu
    var = jnp.mean(zc * zc, axis=1, keepdims=True)
    o_ref[...] = zc * lax.rsqrt(var + LN_EPS) * g_ref[...] + b_ref[...]


def _out(merged, w_out, x, ln_g, ln_b, tm):
    n = merged.shape[0]
    row = pl.BlockSpec((tm, D_MODEL), lambda i: (i, 0))
    vec = pl.BlockSpec((1, D_MODEL), lambda i: (0, 0))
    return pl.pallas_call(
        _out_kernel,
        out_shape=jax.ShapeDtypeStruct((n, D_MODEL), F32),
        grid=(n // tm,),
        in_specs=[row, pl.BlockSpec((D_MODEL, D_MODEL), lambda i: (0, 0)), row, vec, vec],
        out_specs=row,
        compiler_params=_params("arbitrary"),
        name="out_ln",
    )(merged, w_out, x, ln_g, ln_b)


def _rest_col_block(tn):
    za_blocks = (COL_UP - COL_ZA) // tn
    return lambda j: jnp.where(j < za_blocks, COL_ZA // tn + j, COL_ZP // tn + (j - za_blocks))


def _layer_prompt(x2d, batch, t, w, pool_hist, mk, mv):
    rot = _rotary_tables(jnp.arange(t))
    tm, tn, tp = 2048, 1024, 512
    u, xb = _proj_t(x2d, w["w_in"], COL_UP, batch, t, tp, std_dtype=F32, emit_xb=True, name="proj_u")
    h = _proj(xb, w["w_in"], _rest_col_block(tn), H_COLS, BF16, tm, tn, name="proj_rest")
    (qt,) = _proj_t(xb, w["w_in"], COL_Q, batch, t, tp, rot=rot, scale=A_SCALE * math.log2(math.e),
                    t_out="plain", t_dtype=BF16, name="proj_qt")
    kb, kt = _proj_t(xb, w["w_in"], COL_K, batch, t, tp, rot=rot, std_dtype=BF16,
                     t_out="plain", t_dtype=F32, name="proj_kt")
    v, vt = _proj_t(xb, w["w_in"], COL_V, batch, t, tp, std_dtype=F32,
                    t_out="blocked", t_dtype=BF16, name="proj_vt")
    b_a = _attn_prompt(qt, kb, vt, h, w["gain"], w["lamv"], batch, t)
    b_p = _pool(u, 0, pool_hist, h, H_ZP, w["pool_w"], w["pool_scale"], batch, t, 512, 0)
    b_m = _mem_attn(h, H_QM, H_ZM, mk, mv, batch, t, 512)
    merged = _merge(b_a, b_p, b_m, h, H_GM, w["w_branch"], 512, 1024)
    y = _out(merged, w["w_out"], x2d, w["ln_g"], w["ln_b"], 512)
    return y, kt, v, u


def _layer_sample(x2d, batch, t, past, w, kt_hist, v_hist, pool_hist, mk, mv):
    n = batch * t
    rot = _rotary_tables(past + jnp.arange(t))
    act = _proj(x2d, w["w_in"], lambda j: j, IN_COLS, F32, n, 1024, name="proj_all")
    b_a, k = _attn_sample(act, rot, kt_hist, v_hist, w["gain"], w["lamv"], batch, t, past)
    b_p = _pool(act, COL_UP, pool_hist, act, COL_ZP, w["pool_w"], w["pool_scale"], batch, t, t, past)
    b_m = _mem_attn(act, COL_QM, COL_ZM, mk, mv, batch, t, t)
    merged = _merge(b_a, b_p, b_m, act, COL_GM, w["w_branch"], n, 1024)
    y = _out(merged, w["w_out"], x2d, w["ln_g"], w["ln_b"], n)
    return y, k, act


def kernel(x_prompt, x_sample, cache_attn_k, cache_attn_v, cache_mem_k, cache_mem_v, state_pool, mem_prompt, w_in, w_mem_kv, lambda_q1, lambda_k1, lambda_q2, lambda_k2, subln_gain, pool_w, pool_scale, w_branch, w_out, ln_gain, ln_bias):
    bp, tp, d = x_prompt.shape
    bs, ts, _ = x_sample.shape
    past = cache_attn_k.shape[2]
    hist_rows = P_HIST + 1

    def run_layer(l, xp2d, xs2d):
        w = {
            "w_in": w_in[l],
            "gain": subln_gain[l].reshape(1, A_V_DIM),
            "lamv": jnp.stack([lambda_q1[l], lambda_k1[l], lambda_q2[l], lambda_k2[l]]),
            "pool_w": pool_w[l].astype(BF16),
            "pool_scale": pool_scale[l].reshape(1, P_WIDTH),
            "w_branch": w_branch[l].astype(BF16),
            "w_out": w_out[l].astype(BF16),
            "ln_g": ln_gain[l].reshape(1, d),
            "ln_b": ln_bias[l].reshape(1, d),
        }
        memb = mem_prompt.reshape(bp * M_TOKENS, d).astype(BF16)
        wkv = w_mem_kv[l].astype(BF16)
        mk = _proj(memb, wkv, lambda j: j, M_WIDTH, F32, 512, 1024, name="proj_mk")
        mv = _proj(memb, wkv, lambda j: M_WIDTH // 1024 + j, M_WIDTH, F32, 512, 1024, name="proj_mv")

        zero_hist = jnp.zeros((bp, hist_rows, P_WIDTH), F32)
        y_p, k_p, v_p, u_p = _layer_prompt(xp2d, bp, tp, w, zero_hist, mk, mv)

        hist_s = jnp.pad(state_pool[l], ((0, 0), (hist_rows - P_HIST, 0), (0, 0)))
        kt_hist = jnp.transpose(cache_attn_k[l], (0, 2, 3, 4, 1)).reshape(bs, A_HEADS, 2 * A_QK_DIM, past)
        y_s, k_s, act_s = _layer_sample(
            xs2d, bs, ts, past, w,
            kt_hist, cache_attn_v[l].reshape(bs, past * A_HEADS, A_V_DIM),
            hist_s, cache_mem_k[l].reshape(bs * M_TOKENS, M_WIDTH), cache_mem_v[l].reshape(bs * M_TOKENS, M_WIDTH))
        v_s = act_s[:, COL_V:COL_V + A_WIDTH]
        u_s = act_s[:, COL_UP:COL_UP + P_WIDTH]

        pool_p = jnp.concatenate([zero_hist, u_p.reshape(bp, tp, P_WIDTH)], axis=1)[:, -P_HIST:]
        pool_s = jnp.concatenate([hist_s, u_s.reshape(bs, ts, P_WIDTH)], axis=1)[:, -P_HIST:]
        k_p = jnp.transpose(k_p.reshape(bp, A_HEADS, 2, A_QK_DIM, tp), (0, 4, 1, 2, 3))
        outs = (k_p, v_p.reshape(bp, tp, A_HEADS, A_V_DIM),
                mk.reshape(bp, M_TOKENS, M_HEADS, M_HEAD_DIM), mv.reshape(bp, M_TOKENS, M_HEADS, M_HEAD_DIM),
                pool_p,
                k_s.reshape(bs, ts, A_HEADS, 2, A_QK_DIM), v_s.reshape(bs, ts, A_HEADS, A_V_DIM), pool_s)
        return y_p, y_s, outs

    yp = x_prompt.reshape(bp * tp, d)
    ys = x_sample.reshape(bs * ts, d)
    per_layer = []
    for l in range(w_in.shape[0]):
        yp, ys, outs = run_layer(l, yp, ys)
        per_layer.append(outs)
    stacked = [jnp.stack([o[n] for o in per_layer]) for n in range(8)]
    return (yp.reshape(bp, tp, d), ys.reshape(bs, ts, d), *stacked)
```

```python
import functools
import math

import numpy as np
import jax
import jax.numpy as jnp
from jax import lax
from jax.experimental import pallas as pl
from jax.experimental.pallas import tpu as pltpu

F32 = jnp.float32
BF16 = jnp.bfloat16

V7X_VMEM_LIMIT_BYTES = 56 * 1024 * 1024
LANES = 128

D_MODEL = 2048
CHUNK = 64
A_HEADS = 8
A_QK_DIM = 64
A_V_DIM = 128
A_WIDTH = A_HEADS * A_V_DIM
A_SCALE = A_QK_DIM ** -0.5
ROT_DIM = A_QK_DIM // 4
ROPE_THETA = 500000.0
SUBLN_EPS = 1e-5
P_WINDOWS = (2, 4, 8, 16)
P_GROUP_DIM = 256
P_WIDTH = len(P_WINDOWS) * P_GROUP_DIM
P_HIST = max(P_WINDOWS) - 1
M_TOKENS = 256
M_HEADS = 4
M_HEAD_DIM = 256
M_WIDTH = M_HEADS * M_HEAD_DIM
M_SCALE = M_HEAD_DIM ** -0.5
N_BRANCH = 3
DEPTH = 1
DEEPNORM_ALPHA = (2.0 * DEPTH) ** 0.25
LN_EPS = 1e-5
LAMBDA_INIT = 0.8 - 0.6 * math.exp(-0.3 * 0)
ONES_ROWS = 16

COL_Q, COL_K, COL_V, COL_ZA, COL_UP, COL_ZP, COL_QM, COL_ZM, COL_GM = (
    0, 1024, 2048, 3072, 4096, 5120, 6144, 7168, 8192)
IN_COLS = 14336
H_ZA, H_ZP, H_QM, H_ZM, H_GM = 0, 1024, 2048, 3072, 4096
H_COLS = 10240


def _params(*sem):
    return pltpu.CompilerParams(dimension_semantics=sem, vmem_limit_bytes=V7X_VMEM_LIMIT_BYTES)


def _sigmoid(z):
    return 1.0 / (1.0 + jnp.exp(-z))


def _silu(z):
    return z * _sigmoid(z)


def _proj_kernel(x_ref, w_ref, *rest, rotary, scale):
    acc = jnp.dot(x_ref[...].astype(BF16), w_ref[...].astype(BF16), preferred_element_type=F32)
    if rotary:
        cos_ref, sa_ref, sb_ref, o_ref = rest
        c, sa, sb = cos_ref[...], sa_ref[...], sb_ref[...]
        half = ROT_DIM // 2
        parts = []
        for k in range(acc.shape[1] // LANES):
            blk = acc[:, k * LANES:(k + 1) * LANES]
            lo = pltpu.roll(blk, half, 1)
            hi = pltpu.roll(blk, LANES - half, 1)
            parts.append(blk * c + lo * sa + hi * sb)
        acc = jnp.concatenate(parts, axis=1)
    else:
        (o_ref,) = rest
    if scale != 1.0:
        acc = acc * scale
    o_ref[...] = acc.astype(o_ref.dtype)


def _proj(x, w, col_block, ncols, out_dtype, tm, tn, rot=None, scale=1.0, name="proj"):
    n, kdim = x.shape
    grid = (n // tm, ncols // tn)
    in_specs = [
        pl.BlockSpec((tm, kdim), lambda i, j: (i, 0)),
        pl.BlockSpec((kdim, tn), lambda i, j: (0, col_block(j))),
    ]
    args = [x, w]
    if rot is not None:
        nrep = rot[0].shape[0] // tm
        for t in rot:
            in_specs.append(pl.BlockSpec((tm, LANES), lambda i, j: (i % nrep, 0)))
            args.append(t)
    return pl.pallas_call(
        functools.partial(_proj_kernel, rotary=rot is not None, scale=scale),
        out_shape=jax.ShapeDtypeStruct((n, ncols), out_dtype),
        grid=grid,
        in_specs=in_specs,
        out_specs=pl.BlockSpec((tm, tn), lambda i, j: (i, j)),
        compiler_params=_params("parallel", "arbitrary"),
        name=name,
    )(*args)


def _proj_t_kernel(x_ref, w_ref, *rest, rotary, scale, std_out, t_out, key_tile, emit_xb):
    rest = list(rest)
    xb = x_ref[...].astype(BF16)
    acc = jnp.dot(xb, w_ref[...].astype(BF16), preferred_element_type=F32)
    if emit_xb:
        rest.pop()[...] = xb
    if rotary:
        c, sa, sb = rest.pop(0)[...], rest.pop(0)[...], rest.pop(0)[...]
        half = ROT_DIM // 2
        parts = []
        for k in range(acc.shape[1] // LANES):
            blk = acc[:, k * LANES:(k + 1) * LANES]
            lo = pltpu.roll(blk, half, 1)
            hi = pltpu.roll(blk, LANES - half, 1)
            parts.append(blk * c + lo * sa + hi * sb)
        acc = jnp.concatenate(parts, axis=1)
    if scale != 1.0:
        acc = acc * scale
    if std_out:
        o_ref = rest.pop(0)
        o_ref[...] = acc.astype(o_ref.dtype)
    if t_out == "plain":
        t_ref = rest.pop(0)
        t_ref[...] = acc.T.astype(t_ref.dtype)
    elif t_out == "blocked":
        t_ref = rest.pop(0)
        for c in range(acc.shape[0] // key_tile):
            t_ref[c] = acc[c * key_tile:(c + 1) * key_tile, :].T.astype(t_ref.dtype)


def _proj_t(x, w, col_start, batch, t, tm, rot=None, scale=1.0, std_dtype=None, t_out=None, t_dtype=None,
            key_tile=256, emit_xb=False, name="proj_t"):
    n, kdim = x.shape
    width = 1024
    nt = t // tm
    in_specs = [
        pl.BlockSpec((tm, kdim), lambda i: (i, 0)),
        pl.BlockSpec((kdim, width), lambda i: (0, col_start // width)),
    ]
    args = [x, w]
    if rot is not None:
        nrep = rot[0].shape[0] // tm
        for tab in rot:
            in_specs.append(pl.BlockSpec((tm, LANES), lambda i: (i % nrep, 0)))
            args.append(tab)
    out_shape, out_specs = [], []
    if std_dtype is not None:
        out_shape.append(jax.ShapeDtypeStruct((n, width), std_dtype))
        out_specs.append(pl.BlockSpec((tm, width), lambda i: (i, 0)))
    if t_out == "plain":
        out_shape.append(jax.ShapeDtypeStruct((batch, width, t), t_dtype))
        out_specs.append(pl.BlockSpec((None, width, tm), lambda i: (i // nt, 0, i % nt)))
    elif t_out == "blocked":
        out_shape.append(jax.ShapeDtypeStruct((batch, t // key_tile, width, key_tile), t_dtype))
        out_specs.append(pl.BlockSpec((None, tm // key_tile, width, key_tile), lambda i: (i // nt, i % nt, 0, 0)))
    if emit_xb:
        out_shape.append(jax.ShapeDtypeStruct((n, kdim), BF16))
        out_specs.append(pl.BlockSpec((tm, kdim), lambda i: (i, 0)))
    return pl.pallas_call(
        functools.partial(_proj_t_kernel, rotary=rot is not None, scale=scale,
                          std_out=std_dtype is not None, t_out=t_out, key_tile=key_tile, emit_xb=emit_xb),
        out_shape=out_shape,
        grid=(n // tm,),
        in_specs=in_specs,
        out_specs=out_specs,
        compiler_params=_params("arbitrary"),
        name=name,
    )(*args)


def _proj_ft_kernel(x_ref, wt_ref, cos_ref, sin_ref, t_ref, *rest, scale, emit_std, emit_xb):
    rest = list(rest)
    xb = x_ref[...].astype(BF16)
    acc = lax.dot_general(wt_ref[...].astype(BF16), xb, (((1,), (1,)), ((), ())),
                          preferred_element_type=F32)
    c, s = cos_ref[...], sin_ref[...]
    half = ROT_DIM // 2
    pieces = []
    for g in range(acc.shape[0] // A_QK_DIM):
        b = g * A_QK_DIM
        lo, hi = acc[b:b + half], acc[b + half:b + ROT_DIM]
        pieces += [lo * c - hi * s, hi * c + lo * s, acc[b + ROT_DIM:b + A_QK_DIM]]
    acc = jnp.concatenate(pieces, axis=0)
    if scale != 1.0:
        acc = acc * scale
    t_ref[...] = acc.astype(t_ref.dtype)
    if emit_std:
        std_ref = rest.pop(0)
        std_ref[...] = acc.T.astype(std_ref.dtype)
    if emit_xb:
        rest.pop(0)[...] = xb


def _proj_ft(x, wt, cos_t, sin_t, batch, t, tm, t_dtype, scale=1.0, std_dtype=None, emit_xb=False, name="proj_ft"):
    n, kdim = x.shape
    width = wt.shape[0]
    nt = t // tm
    half = ROT_DIM // 2
    tab = pl.BlockSpec((half, tm), lambda i: (0, i % nt))
    out_shape = [jax.ShapeDtypeStruct((batch, width, t), t_dtype)]
    out_specs = [pl.BlockSpec((None, width, tm), lambda i: (i // nt, 0, i % nt))]
    if std_dtype is not None:
        out_shape.append(jax.ShapeDtypeStruct((n, width), std_dtype))
        out_specs.append(pl.BlockSpec((tm, width), lambda i: (i, 0)))
    if emit_xb:
        out_shape.append(jax.ShapeDtypeStruct((n, kdim), BF16))
        out_specs.append(pl.BlockSpec((tm, kdim), lambda i: (i, 0)))
    return pl.pallas_call(
        functools.partial(_proj_ft_kernel, scale=scale, emit_std=std_dtype is not None, emit_xb=emit_xb),
        out_shape=out_shape,
        grid=(n // tm,),
        in_specs=[pl.BlockSpec((tm, kdim), lambda i: (i, 0)),
                  pl.BlockSpec((width, kdim), lambda i: (0, 0)), tab, tab],
        out_specs=out_specs,
        compiler_params=_params("arbitrary"),
        name=name,
    )(x, wt, cos_t, sin_t)


def _rotary_angles(pos):
    half = ROT_DIM // 2
    inv = ROPE_THETA ** (-(jnp.arange(half, dtype=F32) * 2.0) / ROT_DIM)
    return pos.astype(F32)[:, None] * inv[None, :]


def _rotary_tables(pos):
    half = ROT_DIM // 2
    ang = _rotary_angles(pos)
    cos, sin = jnp.cos(ang), jnp.sin(ang)
    t = pos.shape[0]
    ones = jnp.ones((t, A_QK_DIM - ROT_DIM), F32)
    zeros = jnp.zeros((t, A_QK_DIM - ROT_DIM), F32)
    zh = jnp.zeros((t, half), F32)
    c64 = jnp.concatenate([cos, cos, ones], axis=1)
    sa64 = jnp.concatenate([zh, sin, zeros], axis=1)
    sb64 = jnp.concatenate([-sin, zh, zeros], axis=1)
    tile = lambda a: jnp.concatenate([a, a], axis=1)
    return tile(c64), tile(sa64), tile(sb64)


def _lambda_value(lam_ref):
    lv = lam_ref[...]
    a = jnp.sum(lv[0:1] * lv[1:2], axis=1, keepdims=True)
    b = jnp.sum(lv[2:3] * lv[3:4], axis=1, keepdims=True)
    return jnp.exp(a) - jnp.exp(b) + LAMBDA_INIT


def _split_components(q):
    lane = lax.broadcasted_iota(jnp.int32, q.shape, 1)
    qf = q.astype(F32)
    return jnp.concatenate([jnp.where(lane < A_QK_DIM, qf, 0.0),
                            jnp.where(lane >= A_QK_DIM, qf, 0.0)], axis=0).astype(q.dtype)


def _attn_finish(acc, l, tq, lam, gain, za):
    o = acc[:tq] / l[:tq] - lam * (acc[tq:] / l[tq:])
    o = o * lax.rsqrt(jnp.mean(o * o, axis=1, keepdims=True) + SUBLN_EPS)
    o = o * gain * (1.0 - LAMBDA_INIT)
    return o * _silu(za)


def _attn_prompt_kernel(qt_ref, k_ref, vt_ref, za_ref, gain_ref, lam_ref, o_ref, m_ref, l_ref, acc_ref,
                        s0_ref, s1_ref, *,
                        tq, heads):
    i = pl.program_id(2)
    hs = [slice(n * LANES, (n + 1) * LANES) for n in range(heads)]
    qqts = []
    for sl in hs:
        qt = qt_ref[sl, :].astype(F32)
        row = lax.broadcasted_iota(jnp.int32, qt.shape, 0)
        qqts.append(jnp.concatenate([jnp.where(row < A_QK_DIM, qt, 0.0),
                                     jnp.where(row >= A_QK_DIM, qt, 0.0)], axis=1).astype(BF16))

    m_ref[...] = jnp.full(m_ref.shape, -jnp.inf, F32)
    l_ref[...] = jnp.zeros(l_ref.shape, F32)
    acc_ref[...] = jnp.zeros(acc_ref.shape, F32)

    def score(j, n):
        start = pl.multiple_of(j * tq, tq)
        return jnp.dot(k_ref[pl.ds(start, tq), hs[n]], qqts[n], preferred_element_type=F32)

    ones_rows = jnp.ones((ONES_ROWS, tq), BF16)

    def step(j, cur_ref, nxt_ref, masked):
        for n in range(heads):
            s = cur_ref[n]
            if masked:
                krow = lax.broadcasted_iota(jnp.int32, s.shape, 0)
                qcol = lax.broadcasted_iota(jnp.int32, s.shape, 1) % tq
                s = jnp.where((krow // CHUNK) <= (qcol // CHUNK), s, -jnp.inf)
            m = m_ref[n]
            m_new = jnp.maximum(m, jnp.max(s, axis=0, keepdims=True))
            alpha = jnp.exp2(m - m_new)
            p = jnp.exp2(s - m_new).astype(BF16)
            m_ref[n] = m_new
            lhs = jnp.concatenate([vt_ref[j, hs[n], :], ones_rows], axis=0)
            pv = jnp.dot(lhs, p, preferred_element_type=F32)
            acc_ref[n] = alpha * acc_ref[n] + pv[:A_V_DIM]
            l_ref[n] = alpha * l_ref[n] + pv[A_V_DIM:A_V_DIM + 1]
            if nxt_ref is not None:
                nxt_ref[n] = score(j + 1, n)

    for n in range(heads):
        s0_ref[n] = score(0, n)

    def pair(jj, carry):
        step(2 * jj, s0_ref, s1_ref, False)
        step(2 * jj + 1, s1_ref, s0_ref, False)
        return carry

    lax.fori_loop(0, i // 2, pair, 0)

    @pl.when(i % 2 == 1)
    def _():
        step(i - 1, s0_ref, s1_ref, False)
        step(i, s1_ref, None, True)

    @pl.when(i % 2 == 0)
    def _():
        step(i, s0_ref, None, True)
    lam = _lambda_value(lam_ref)
    for n in range(heads):
        l, acc = l_ref[n], acc_ref[n]
        ot = acc[:, :tq] / l[:, :tq] - lam * (acc[:, tq:] / l[:, tq:])
        o = ot.T
        o = o * lax.rsqrt(jnp.mean(o * o, axis=1, keepdims=True) + SUBLN_EPS)
        o = o * gain_ref[...] * (1.0 - LAMBDA_INIT)
        o_ref[:, hs[n]] = (o * _silu(za_ref[:, hs[n]].astype(F32))).astype(o_ref.dtype)


def _attn_prompt(qt, k, vt, h, gain, lamv, batch, t, tq=256, heads=8):
    nq = t // tq
    n = batch * t
    hw = heads * LANES
    return pl.pallas_call(
        functools.partial(_attn_prompt_kernel, tq=tq, heads=heads),
        out_shape=jax.ShapeDtypeStruct((n, A_WIDTH), BF16),
        grid=(batch, A_HEADS // heads, nq),
        in_specs=[
            pl.BlockSpec((None, hw, tq), lambda b, hh, i: (b, hh, i)),
            pl.BlockSpec((t, hw), lambda b, hh, i: (b, hh)),
            pl.BlockSpec((None, nq, hw, tq), lambda b, hh, i: (b, 0, hh, 0)),
            pl.BlockSpec((tq, hw), lambda b, hh, i: (b * nq + i, H_ZA // hw + hh)),
            pl.BlockSpec((1, LANES), lambda b, hh, i: (0, 0)),
            pl.BlockSpec((4, A_QK_DIM), lambda b, hh, i: (0, 0)),
        ],
        out_specs=pl.BlockSpec((tq, hw), lambda b, hh, i: (b * nq + i, hh)),
        scratch_shapes=[pltpu.VMEM((heads, 1, 2 * tq), F32), pltpu.VMEM((heads, 1, 2 * tq), F32),
                        pltpu.VMEM((heads, A_V_DIM, 2 * tq), F32),
                        pltpu.VMEM((heads, tq, 2 * tq), F32), pltpu.VMEM((heads, tq, 2 * tq), F32)],
        compiler_params=_params("arbitrary", "arbitrary", "arbitrary"),
        name="attn_prompt",
    )(qt, k, vt, h, gain, lamv)


def _chunk_mask(s, ts, q_pos0, k_pos0):
    q_pos = q_pos0 + lax.broadcasted_iota(jnp.int32, s.shape, 0) % ts
    k_pos = k_pos0 + lax.broadcasted_iota(jnp.int32, s.shape, 1)
    return jnp.where((k_pos // CHUNK) <= (q_pos // CHUNK), s, -jnp.inf)


def _rotate_lanes(blk, c, sa, sb):
    half = ROT_DIM // 2
    return blk * c + pltpu.roll(blk, half, 1) * sa + pltpu.roll(blk, LANES - half, 1) * sb


def _attn_sample_kernel(q_ref, k_ref, v_ref, za_ref, cos_ref, sa_ref, sb_ref, kct_ref, vc_ref, gain_ref, lam_ref,
                        o_ref, krot_ref, *, ts, past, mask_cache, mask_new):
    c, sa, sb = cos_ref[...], sa_ref[...], sb_ref[...]
    nt = (((1,), (1,)), ((), ()))
    hs = [slice(n * LANES, (n + 1) * LANES) for n in range(A_HEADS)]
    scores = []
    for n in range(A_HEADS):
        q = (_rotate_lanes(q_ref[:, hs[n]], c, sa, sb) * A_SCALE).astype(BF16)
        k = _rotate_lanes(k_ref[:, hs[n]], c, sa, sb)
        krot_ref[:, hs[n]] = k
        qq = _split_components(q)
        s_c = jnp.dot(qq, kct_ref[n].astype(BF16), preferred_element_type=F32)
        s_n = lax.dot_general(qq, k.astype(BF16), nt, preferred_element_type=F32)
        if mask_cache:
            s_c = _chunk_mask(s_c, ts, past, 0)
        if mask_new:
            s_n = _chunk_mask(s_n, ts, past, past)
        scores.append((s_c, s_n))
    lam = _lambda_value(lam_ref)
    for n, (s_c, s_n) in enumerate(scores):
        m = jnp.maximum(jnp.max(s_c, axis=1, keepdims=True), jnp.max(s_n, axis=1, keepdims=True))
        p_c = jnp.exp(s_c - m)
        p_n = jnp.exp(s_n - m)
        l = jnp.sum(p_c, axis=1, keepdims=True) + jnp.sum(p_n, axis=1, keepdims=True)
        vc = vc_ref[pl.ds(n, past, stride=A_HEADS), :].astype(BF16)
        acc = (jnp.dot(p_c.astype(BF16), vc, preferred_element_type=F32)
               + jnp.dot(p_n.astype(BF16), v_ref[:, hs[n]].astype(BF16), preferred_element_type=F32))
        out = _attn_finish(acc, l, ts, lam, gain_ref[...], za_ref[:, hs[n]].astype(F32))
        o_ref[:, hs[n]] = out.astype(o_ref.dtype)


def _needs_mask(q_pos, k_pos):
    return not bool(((k_pos[None, :] // CHUNK) <= (q_pos[:, None] // CHUNK)).all())


def _attn_sample(hs_act, rot, kct, vc, gain, lamv, batch, ts, past):
    q_pos = past + np.arange(ts)
    seg = lambda col: pl.BlockSpec((ts, A_WIDTH), lambda b: (b, col // A_WIDTH))
    tab = pl.BlockSpec((ts, LANES), lambda b: (0, 0))
    row = pl.BlockSpec((ts, A_WIDTH), lambda b: (b, 0))
    return pl.pallas_call(
        functools.partial(_attn_sample_kernel, ts=ts, past=past,
                          mask_cache=_needs_mask(q_pos, np.arange(past)),
                          mask_new=_needs_mask(q_pos, q_pos)),
        out_shape=[jax.ShapeDtypeStruct((batch * ts, A_WIDTH), BF16),
                   jax.ShapeDtypeStruct((batch * ts, A_WIDTH), F32)],
        grid=(batch,),
        in_specs=[
            seg(COL_Q), seg(COL_K), seg(COL_V), seg(COL_ZA), tab, tab, tab,
            pl.BlockSpec((None, A_HEADS, LANES, past), lambda b: (b, 0, 0, 0)),
            pl.BlockSpec((None, past * A_HEADS, LANES), lambda b: (b, 0, 0)),
            pl.BlockSpec((1, LANES), lambda b: (0, 0)),
            pl.BlockSpec((4, A_QK_DIM), lambda b: (0, 0)),
        ],
        out_specs=[row, row],
        compiler_params=_params("arbitrary"),
        name="attn_sample",
    )(hs_act, hs_act, hs_act, hs_act, *rot, kct, vc, gain, lamv)


def _pool_kernel(u_ref, uprev_ref, hist_ref, zp_ref, pw_ref, ps_ref, o_ref, *, tm, pos0):
    i = pl.program_id(1)
    u = u_ref[...]
    prev = jnp.where(i == 0, hist_ref[...], uprev_ref[...])
    ext = jnp.concatenate([prev, u], axis=0)
    hist_rows = prev.shape[0]
    pos = pos0 + i * tm + lax.broadcasted_iota(jnp.int32, (tm, 1), 0)
    parts = []
    for g, w in enumerate(P_WINDOWS):
        sl = slice(g * P_GROUP_DIM, (g + 1) * P_GROUP_DIM)
        tot = ext[:, sl]
        span = 1
        while span < w:
            tot = tot + pltpu.roll(tot, span, 0)
            span *= 2
        tot = tot[hist_rows:]
        cnt = jnp.minimum(pos + 1, w).astype(F32)
        pooled = tot / cnt - u[:, sl]
        parts.append(jnp.dot(pooled.astype(BF16), pw_ref[g], preferred_element_type=F32))
    mixed = jnp.concatenate(parts, axis=1) * ps_ref[...]
    o_ref[...] = (mixed * _silu(zp_ref[...].astype(F32))).astype(o_ref.dtype)


def _pool(u, u_col, hist, z, z_col, pool_w, pool_scale, batch, t, tm, pos0):
    nt = t // tm
    hr = hist.shape[1]
    per = tm // hr
    ucb, zcb = u_col // P_WIDTH, z_col // P_WIDTH
    return pl.pallas_call(
        functools.partial(_pool_kernel, tm=tm, pos0=pos0),
        out_shape=jax.ShapeDtypeStruct((batch * t, P_WIDTH), BF16),
        grid=(batch, nt),
        in_specs=[
            pl.BlockSpec((tm, P_WIDTH), lambda b, i: (b * nt + i, ucb)),
            pl.BlockSpec((hr, P_WIDTH), lambda b, i: (jnp.maximum((b * nt + i) * per - 1, 0), ucb)),
            pl.BlockSpec((None, hr, P_WIDTH), lambda b, i: (b, 0, 0)),
            pl.BlockSpec((tm, P_WIDTH), lambda b, i: (b * nt + i, zcb)),
            pl.BlockSpec((len(P_WINDOWS), P_GROUP_DIM, P_GROUP_DIM), lambda b, i: (0, 0, 0)),
            pl.BlockSpec((1, P_WIDTH), lambda b, i: (0, 0)),
        ],
        out_specs=pl.BlockSpec((tm, P_WIDTH), lambda b, i: (b * nt + i, 0)),
        compiler_params=_params("arbitrary", "arbitrary"),
        name="pool",
    )(u, u, hist, z, pool_w, pool_scale)


def _mem_attn_kernel(q_ref, mk_ref, mv_ref, zm_ref, o_ref):
    nt = (((1,), (1,)), ((), ()))
    hs = [slice(n * M_HEAD_DIM, (n + 1) * M_HEAD_DIM) for n in range(M_HEADS)]
    scores = [lax.dot_general(q_ref[:, sl].astype(BF16), mk_ref[:, sl].astype(BF16), nt,
                              preferred_element_type=F32) * M_SCALE for sl in hs]
    for sl, s in zip(hs, scores):
        m = jnp.max(s, axis=1, keepdims=True)
        p = jnp.exp(s - m)
        l = jnp.sum(p, axis=1, keepdims=True)
        o = jnp.dot(p.astype(BF16), mv_ref[:, sl].astype(BF16), preferred_element_type=F32) / l
        o_ref[:, sl] = (o * _silu(zm_ref[:, sl].astype(F32))).astype(o_ref.dtype)


def _mem_attn(act, q_col, z_col, mk, mv, batch, t, tq):
    nq = t // tq
    qcb, zcb = q_col // M_WIDTH, z_col // M_WIDTH
    return pl.pallas_call(
        _mem_attn_kernel,
        out_shape=jax.ShapeDtypeStruct((batch * t, M_WIDTH), BF16),
        grid=(batch, nq),
        in_specs=[
            pl.BlockSpec((tq, M_WIDTH), lambda b, i: (b * nq + i, qcb)),
            pl.BlockSpec((M_TOKENS, M_WIDTH), lambda b, i: (b, 0)),
            pl.BlockSpec((M_TOKENS, M_WIDTH), lambda b, i: (b, 0)),
            pl.BlockSpec((tq, M_WIDTH), lambda b, i: (b * nq + i, zcb)),
        ],
        out_specs=pl.BlockSpec((tq, M_WIDTH), lambda b, i: (b * nq + i, 0)),
        compiler_params=_params("arbitrary", "arbitrary"),
        name="mem_attn",
    )(act, mk, mv, act)


def _merge_kernel(ba_ref, bp_ref, bm_ref, g0_ref, g1_ref, g2_ref, wb_ref, o_ref):
    acc = None
    for n, (b_ref, g_ref) in enumerate(((ba_ref, g0_ref), (bp_ref, g1_ref), (bm_ref, g2_ref))):
        proj = jnp.dot(b_ref[...], wb_ref[n], preferred_element_type=F32)
        term = _sigmoid(g_ref[...].astype(F32)) * proj
        acc = term if acc is None else acc + term
    o_ref[...] = acc.astype(o_ref.dtype)


def _merge(ba, bp, bm, h, g_col, wb, tm, tn):
    n = ba.shape[0]
    width = ba.shape[1]
    gspec = lambda k: pl.BlockSpec((tm, tn), lambda j, i: (i, (g_col + k * D_MODEL) // tn + j))
    bspec = pl.BlockSpec((tm, width), lambda j, i: (i, 0))
    return pl.pallas_call(
        _merge_kernel,
        out_shape=jax.ShapeDtypeStruct((n, D_MODEL), BF16),
        grid=(D_MODEL // tn, n // tm),
        in_specs=[bspec, bspec, bspec, gspec(0), gspec(1), gspec(2),
                  pl.BlockSpec((N_BRANCH, width, tn), lambda j, i: (0, 0, j))],
        out_specs=pl.BlockSpec((tm, tn), lambda j, i: (i, j)),
        compiler_params=_params("arbitrary", "arbitrary"),
        name="merge",
    )(ba, bp, bm, h, h, h, wb)


def _out_kernel(m_ref, w_ref, x_ref, g_ref, b_ref, o_ref):
    z = DEEPNORM_ALPHA * x_ref[...] + jnp.dot(m_ref[...], w_ref[...], preferred_element_type=F32)
    mu = jnp.mean(z, axis=1, keepdims=True)
    zc = z - mu
    var = jnp.mean(zc * zc, axis=1, keepdims=True)
    o_ref[...] = zc * lax.rsqrt(var + LN_EPS) * g_ref[...] + b_ref[...]


def _out(merged, w_out, x, ln_g, ln_b, tm):
    n = merged.shape[0]
    row = pl.BlockSpec((tm, D_MODEL), lambda i: (i, 0))
    vec = pl.BlockSpec((1, D_MODEL), lambda i: (0, 0))
    return pl.pallas_call(
        _out_kernel,
        out_shape=jax.ShapeDtypeStruct((n, D_MODEL), F32),
        grid=(n // tm,),
        in_specs=[row, pl.BlockSpec((D_MODEL, D_MODEL), lambda i: (0, 0)), row, vec, vec],
        out_specs=row,
        compiler_params=_params("arbitrary"),
        name="out_ln",
    )(merged, w_out, x, ln_g, ln_b)


def _rest_col_block(tn):
    za_blocks = (COL_UP - COL_ZA) // tn
    return lambda j: jnp.where(j < za_blocks, COL_ZA // tn + j, COL_ZP // tn + (j - za_blocks))


def _layer_prompt(x2d, batch, t, w, pool_hist, mk, mv):
    ang = _rotary_angles(jnp.arange(t))
    cos_t, sin_t = jnp.cos(ang).T, jnp.sin(ang).T
    wq_t = w["w_in"][:, COL_Q:COL_Q + A_WIDTH].T
    wk_t = w["w_in"][:, COL_K:COL_K + A_WIDTH].T
    tm, tn, tp = 2048, 1024, 512
    qt, xb = _proj_ft(x2d, wq_t, cos_t, sin_t, batch, t, tp, BF16, scale=A_SCALE * math.log2(math.e),
                      emit_xb=True, name="proj_qt")
    kt, kb = _proj_ft(xb, wk_t, cos_t, sin_t, batch, t, tp, F32, std_dtype=BF16, name="proj_kt")
    v, vt = _proj_t(xb, w["w_in"], COL_V, batch, t, tp, std_dtype=F32,
                    t_out="blocked", t_dtype=BF16, name="proj_vt")
    (u,) = _proj_t(xb, w["w_in"], COL_UP, batch, t, tp, std_dtype=F32, name="proj_u")
    h = _proj(xb, w["w_in"], _rest_col_block(tn), H_COLS, BF16, tm, tn, name="proj_rest")
    b_a = _attn_prompt(qt, kb, vt, h, w["gain"], w["lamv"], batch, t)
    b_p = _pool(u, 0, pool_hist, h, H_ZP, w["pool_w"], w["pool_scale"], batch, t, 512, 0)
    b_m = _mem_attn(h, H_QM, H_ZM, mk, mv, batch, t, 512)
    merged = _merge(b_a, b_p, b_m, h, H_GM, w["w_branch"], 512, 1024)
    y = _out(merged, w["w_out"], x2d, w["ln_g"], w["ln_b"], 512)
    return y, kt, v, u


def _layer_sample(x2d, batch, t, past, w, kt_hist, v_hist, pool_hist, mk, mv):
    n = batch * t
    rot = _rotary_tables(past + jnp.arange(t))
    act = _proj(x2d, w["w_in"], lambda j: j, IN_COLS, F32, n, 1024, name="proj_all")
    b_a, k = _attn_sample(act, rot, kt_hist, v_hist, w["gain"], w["lamv"], batch, t, past)
    b_p = _pool(act, COL_UP, pool_hist, act, COL_ZP, w["pool_w"], w["pool_scale"], batch, t, t, past)
    b_m = _mem_attn(act, COL_QM, COL_ZM, mk, mv, batch, t, t)
    merged = _merge(b_a, b_p, b_m, act, COL_GM, w["w_branch"], n, 1024)
    y = _out(merged, w["w_out"], x2d, w["ln_g"], w["ln_b"], n)
    return y, k, act


def kernel(x_prompt, x_sample, cache_attn_k, cache_attn_v, cache_mem_k, cache_mem_v, state_pool, mem_prompt, w_in, w_mem_kv, lambda_q1, lambda_k1, lambda_q2, lambda_k2, subln_gain, pool_w, pool_scale, w_branch, w_out, ln_gain, ln_bias):
    bp, tp, d = x_prompt.shape
    bs, ts, _ = x_sample.shape
    past = cache_attn_k.shape[2]
    hist_rows = P_HIST + 1

    def run_layer(l, xp2d, xs2d):
        w = {
            "w_in": w_in[l],
            "gain": subln_gain[l].reshape(1, A_V_DIM),
            "lamv": jnp.stack([lambda_q1[l], lambda_k1[l], lambda_q2[l], lambda_k2[l]]),
            "pool_w": pool_w[l].astype(BF16),
            "pool_scale": pool_scale[l].reshape(1, P_WIDTH),
            "w_branch": w_branch[l].astype(BF16),
            "w_out": w_out[l].astype(BF16),
            "ln_g": ln_gain[l].reshape(1, d),
            "ln_b": ln_bias[l].reshape(1, d),
        }
        memb = mem_prompt.reshape(bp * M_TOKENS, d).astype(BF16)
        wkv = w_mem_kv[l].astype(BF16)
        mk = _proj(memb, wkv, lambda j: j, M_WIDTH, F32, 512, 1024, name="proj_mk")
        mv = _proj(memb, wkv, lambda j: M_WIDTH // 1024 + j, M_WIDTH, F32, 512, 1024, name="proj_mv")

        zero_hist = jnp.zeros((bp, hist_rows, P_WIDTH), F32)
        y_p, k_p, v_p, u_p = _layer_prompt(xp2d, bp, tp, w, zero_hist, mk, mv)

        hist_s = jnp.pad(state_pool[l], ((0, 0), (hist_rows - P_HIST, 0), (0, 0)))
        kt_hist = jnp.transpose(cache_attn_k[l], (0, 2, 3, 4, 1)).reshape(bs, A_HEADS, 2 * A_QK_DIM, past)
        y_s, k_s, act_s = _layer_sample(
            xs2d, bs, ts, past, w,
            kt_hist, cache_attn_v[l].reshape(bs, past * A_HEADS, A_V_DIM),
            hist_s, cache_mem_k[l].reshape(bs * M_TOKENS, M_WIDTH), cache_mem_v[l].reshape(bs * M_TOKENS, M_WIDTH))
        v_s = act_s[:, COL_V:COL_V + A_WIDTH]
        u_s = act_s[:, COL_UP:COL_UP + P_WIDTH]

        pool_p = jnp.concatenate([zero_hist, u_p.reshape(bp, tp, P_WIDTH)], axis=1)[:, -P_HIST:]
        pool_s = jnp.concatenate([hist_s, u_s.reshape(bs, ts, P_WIDTH)], axis=1)[:, -P_HIST:]
        k_p = jnp.transpose(k_p.reshape(bp, A_HEADS, 2, A_QK_DIM, tp), (0, 4, 1, 2, 3))
        outs = (k_p, v_p.reshape(bp, tp, A_HEADS, A_V_DIM),
                mk.reshape(bp, M_TOKENS, M_HEADS, M_HEAD_DIM), mv.reshape(bp, M_TOKENS, M_HEADS, M_HEAD_DIM),
                pool_p,
                k_s.reshape(bs, ts, A_HEADS, 2, A_QK_DIM), v_s.reshape(bs, ts, A_HEADS, A_V_DIM), pool_s)
        return y_p, y_s, outs

    yp = x_prompt.reshape(bp * tp, d)
    ys = x_sample.reshape(bs * ts, d)
    per_layer = []
    for l in range(w_in.shape[0]):
        yp, ys, outs = run_layer(l, yp, ys)
        per_layer.append(outs)
    stacked = [jnp.stack([o[n] for o in per_layer]) for n in range(8)]
    return (yp.reshape(bp, tp, d), ys.reshape(bs, ts, d), *stacked)
```

```python
import functools
import math

import numpy as np
import jax
import jax.numpy as jnp
from jax import lax
from jax.experimental import pallas as pl
from jax.experimental.pallas import tpu as pltpu

F32 = jnp.float32
BF16 = jnp.bfloat16

V7X_VMEM_LIMIT_BYTES = 56 * 1024 * 1024
LANES = 128

D_MODEL = 2048
CHUNK = 64
A_HEADS = 8
A_QK_DIM = 64
A_V_DIM = 128
A_WIDTH = A_HEADS * A_V_DIM
A_SCALE = A_QK_DIM ** -0.5
ROT_DIM = A_QK_DIM // 4
ROPE_THETA = 500000.0
SUBLN_EPS = 1e-5
P_WINDOWS = (2, 4, 8, 16)
P_GROUP_DIM = 256
P_WIDTH = len(P_WINDOWS) * P_GROUP_DIM
P_HIST = max(P_WINDOWS) - 1
M_TOKENS = 256
M_HEADS = 4
M_HEAD_DIM = 256
M_WIDTH = M_HEADS * M_HEAD_DIM
M_SCALE = M_HEAD_DIM ** -0.5
N_BRANCH = 3
DEPTH = 1
DEEPNORM_ALPHA = (2.0 * DEPTH) ** 0.25
LN_EPS = 1e-5
LAMBDA_INIT = 0.8 - 0.6 * math.exp(-0.3 * 0)
ONES_ROWS = 16

COL_Q, COL_K, COL_V, COL_ZA, COL_UP, COL_ZP, COL_QM, COL_ZM, COL_GM = (
    0, 1024, 2048, 3072, 4096, 5120, 6144, 7168, 8192)
IN_COLS = 14336
H_ZA, H_ZP, H_QM, H_ZM, H_GM = 0, 1024, 2048, 3072, 4096
H_COLS = 10240


def _params(*sem):
    return pltpu.CompilerParams(dimension_semantics=sem, vmem_limit_bytes=V7X_VMEM_LIMIT_BYTES)


def _sigmoid(z):
    return 1.0 / (1.0 + jnp.exp(-z))


def _silu(z):
    return z * _sigmoid(z)


def _proj_kernel(x_ref, w_ref, *rest, rotary, scale):
    acc = jnp.dot(x_ref[...].astype(BF16), w_ref[...].astype(BF16), preferred_element_type=F32)
    if rotary:
        cos_ref, sa_ref, sb_ref, o_ref = rest
        c, sa, sb = cos_ref[...], sa_ref[...], sb_ref[...]
        half = ROT_DIM // 2
        parts = []
        for k in range(acc.shape[1] // LANES):
            blk = acc[:, k * LANES:(k + 1) * LANES]
            lo = pltpu.roll(blk, half, 1)
            hi = pltpu.roll(blk, LANES - half, 1)
            parts.append(blk * c + lo * sa + hi * sb)
        acc = jnp.concatenate(parts, axis=1)
    else:
        (o_ref,) = rest
    if scale != 1.0:
        acc = acc * scale
    o_ref[...] = acc.astype(o_ref.dtype)


def _proj(x, w, col_block, ncols, out_dtype, tm, tn, rot=None, scale=1.0, name="proj"):
    n, kdim = x.shape
    grid = (n // tm, ncols // tn)
    in_specs = [
        pl.BlockSpec((tm, kdim), lambda i, j: (i, 0)),
        pl.BlockSpec((kdim, tn), lambda i, j: (0, col_block(j))),
    ]
    args = [x, w]
    if rot is not None:
        nrep = rot[0].shape[0] // tm
        for t in rot:
            in_specs.append(pl.BlockSpec((tm, LANES), lambda i, j: (i % nrep, 0)))
            args.append(t)
    return pl.pallas_call(
        functools.partial(_proj_kernel, rotary=rot is not None, scale=scale),
        out_shape=jax.ShapeDtypeStruct((n, ncols), out_dtype),
        grid=grid,
        in_specs=in_specs,
        out_specs=pl.BlockSpec((tm, tn), lambda i, j: (i, j)),
        compiler_params=_params("parallel", "arbitrary"),
        name=name,
    )(*args)


def _proj_t_kernel(x_ref, w_ref, *rest, rotary, scale, std_out, t_out, key_tile, emit_xb):
    rest = list(rest)
    xb = x_ref[...].astype(BF16)
    acc = jnp.dot(xb, w_ref[...].astype(BF16), preferred_element_type=F32)
    if emit_xb:
        rest.pop()[...] = xb
    if rotary:
        c, sa, sb = rest.pop(0)[...], rest.pop(0)[...], rest.pop(0)[...]
        half = ROT_DIM // 2
        parts = []
        for k in range(acc.shape[1] // LANES):
            blk = acc[:, k * LANES:(k + 1) * LANES]
            lo = pltpu.roll(blk, half, 1)
            hi = pltpu.roll(blk, LANES - half, 1)
            parts.append(blk * c + lo * sa + hi * sb)
        acc = jnp.concatenate(parts, axis=1)
    if scale != 1.0:
        acc = acc * scale
    if std_out:
        o_ref = rest.pop(0)
        o_ref[...] = acc.astype(o_ref.dtype)
    if t_out == "plain":
        t_ref = rest.pop(0)
        t_ref[...] = acc.T.astype(t_ref.dtype)
    elif t_out == "blocked":
        t_ref = rest.pop(0)
        for c in range(acc.shape[0] // key_tile):
            t_ref[c] = acc[c * key_tile:(c + 1) * key_tile, :].T.astype(t_ref.dtype)


def _proj_t(x, w, col_start, batch, t, tm, rot=None, scale=1.0, std_dtype=None, t_out=None, t_dtype=None,
            key_tile=256, emit_xb=False, name="proj_t"):
    n, kdim = x.shape
    width = 1024
    nt = t // tm
    in_specs = [
        pl.BlockSpec((tm, kdim), lambda i: (i, 0)),
        pl.BlockSpec((kdim, width), lambda i: (0, col_start // width)),
    ]
    args = [x, w]
    if rot is not None:
        nrep = rot[0].shape[0] // tm
        for tab in rot:
            in_specs.append(pl.BlockSpec((tm, LANES), lambda i: (i % nrep, 0)))
            args.append(tab)
    out_shape, out_specs = [], []
    if std_dtype is not None:
        out_shape.append(jax.ShapeDtypeStruct((n, width), std_dtype))
        out_specs.append(pl.BlockSpec((tm, width), lambda i: (i, 0)))
    if t_out == "plain":
        out_shape.append(jax.ShapeDtypeStruct((batch, width, t), t_dtype))
        out_specs.append(pl.BlockSpec((None, width, tm), lambda i: (i // nt, 0, i % nt)))
    elif t_out == "blocked":
        out_shape.append(jax.ShapeDtypeStruct((batch, t // key_tile, width, key_tile), t_dtype))
        out_specs.append(pl.BlockSpec((None, tm // key_tile, width, key_tile), lambda i: (i // nt, i % nt, 0, 0)))
    if emit_xb:
        out_shape.append(jax.ShapeDtypeStruct((n, kdim), BF16))
        out_specs.append(pl.BlockSpec((tm, kdim), lambda i: (i, 0)))
    return pl.pallas_call(
        functools.partial(_proj_t_kernel, rotary=rot is not None, scale=scale,
                          std_out=std_dtype is not None, t_out=t_out, key_tile=key_tile, emit_xb=emit_xb),
        out_shape=out_shape,
        grid=(n // tm,),
        in_specs=in_specs,
        out_specs=out_specs,
        compiler_params=_params("arbitrary"),
        name=name,
    )(*args)


def _proj_ft_kernel(x_ref, w_ref, cos_ref, sin_ref, t_ref, *rest, scale, emit_std, emit_xb):
    rest = list(rest)
    wt_ref = rest.pop()

    @pl.when(pl.program_id(0) == 0)
    def _():
        chunk = 256
        for c in range(w_ref.shape[0] // chunk):
            wt_ref[:, c * chunk:(c + 1) * chunk] = w_ref[c * chunk:(c + 1) * chunk, :].T.astype(BF16)

    xb = x_ref[...].astype(BF16)
    acc = lax.dot_general(wt_ref[...], xb, (((1,), (1,)), ((), ())),
                          preferred_element_type=F32)
    c, s = cos_ref[...], sin_ref[...]
    half = ROT_DIM // 2
    pieces = []
    for g in range(acc.shape[0] // A_QK_DIM):
        b = g * A_QK_DIM
        lo, hi = acc[b:b + half], acc[b + half:b + ROT_DIM]
        pieces += [lo * c - hi * s, hi * c + lo * s, acc[b + ROT_DIM:b + A_QK_DIM]]
    acc = jnp.concatenate(pieces, axis=0)
    if scale != 1.0:
        acc = acc * scale
    t_ref[...] = acc.astype(t_ref.dtype)
    if emit_std:
        std_ref = rest.pop(0)
        std_ref[...] = acc.T.astype(std_ref.dtype)
    if emit_xb:
        rest.pop(0)[...] = xb


def _proj_ft(x, w, col_start, cos_t, sin_t, batch, t, tm, t_dtype, scale=1.0, std_dtype=None, emit_xb=False,
             name="proj_ft"):
    n, kdim = x.shape
    width = A_WIDTH
    nt = t // tm
    half = ROT_DIM // 2
    tab = pl.BlockSpec((half, tm), lambda i: (0, i % nt))
    out_shape = [jax.ShapeDtypeStruct((batch, width, t), t_dtype)]
    out_specs = [pl.BlockSpec((None, width, tm), lambda i: (i // nt, 0, i % nt))]
    if std_dtype is not None:
        out_shape.append(jax.ShapeDtypeStruct((n, width), std_dtype))
        out_specs.append(pl.BlockSpec((tm, width), lambda i: (i, 0)))
    if emit_xb:
        out_shape.append(jax.ShapeDtypeStruct((n, kdim), BF16))
        out_specs.append(pl.BlockSpec((tm, kdim), lambda i: (i, 0)))
    return pl.pallas_call(
        functools.partial(_proj_ft_kernel, scale=scale, emit_std=std_dtype is not None, emit_xb=emit_xb),
        out_shape=out_shape,
        grid=(n // tm,),
        in_specs=[pl.BlockSpec((tm, kdim), lambda i: (i, 0)),
                  pl.BlockSpec((kdim, width), lambda i: (0, col_start // width)), tab, tab],
        out_specs=out_specs,
        scratch_shapes=[pltpu.VMEM((width, kdim), BF16)],
        compiler_params=_params("arbitrary"),
        name=name,
    )(x, w, cos_t, sin_t)


def _rotary_angles(pos):
    half = ROT_DIM // 2
    inv = ROPE_THETA ** (-(jnp.arange(half, dtype=F32) * 2.0) / ROT_DIM)
    return pos.astype(F32)[:, None] * inv[None, :]


def _rotary_tables(pos):
    half = ROT_DIM // 2
    ang = _rotary_angles(pos)
    cos, sin = jnp.cos(ang), jnp.sin(ang)
    t = pos.shape[0]
    ones = jnp.ones((t, A_QK_DIM - ROT_DIM), F32)
    zeros = jnp.zeros((t, A_QK_DIM - ROT_DIM), F32)
    zh = jnp.zeros((t, half), F32)
    c64 = jnp.concatenate([cos, cos, ones], axis=1)
    sa64 = jnp.concatenate([zh, sin, zeros], axis=1)
    sb64 = jnp.concatenate([-sin, zh, zeros], axis=1)
    tile = lambda a: jnp.concatenate([a, a], axis=1)
    return tile(c64), tile(sa64), tile(sb64)


def _lambda_value(lam_ref):
    lv = lam_ref[...]
    a = jnp.sum(lv[0:1] * lv[1:2], axis=1, keepdims=True)
    b = jnp.sum(lv[2:3] * lv[3:4], axis=1, keepdims=True)
    return jnp.exp(a) - jnp.exp(b) + LAMBDA_INIT


def _split_components(q):
    lane = lax.broadcasted_iota(jnp.int32, q.shape, 1)
    qf = q.astype(F32)
    return jnp.concatenate([jnp.where(lane < A_QK_DIM, qf, 0.0),
                            jnp.where(lane >= A_QK_DIM, qf, 0.0)], axis=0).astype(q.dtype)


def _attn_finish(acc, l, tq, lam, gain, za):
    o = acc[:tq] / l[:tq] - lam * (acc[tq:] / l[tq:])
    o = o * lax.rsqrt(jnp.mean(o * o, axis=1, keepdims=True) + SUBLN_EPS)
    o = o * gain * (1.0 - LAMBDA_INIT)
    return o * _silu(za)


def _attn_prompt_kernel(qt_ref, k_ref, vt_ref, za_ref, gain_ref, lam_ref, o_ref, m_ref, l_ref, acc_ref,
                        s0_ref, s1_ref, *,
                        tq, heads):
    i = pl.program_id(2)
    hs = [slice(n * LANES, (n + 1) * LANES) for n in range(heads)]
    qqts = []
    for sl in hs:
        qt = qt_ref[sl, :].astype(F32)
        row = lax.broadcasted_iota(jnp.int32, qt.shape, 0)
        qqts.append(jnp.concatenate([jnp.where(row < A_QK_DIM, qt, 0.0),
                                     jnp.where(row >= A_QK_DIM, qt, 0.0)], axis=1).astype(BF16))

    m_ref[...] = jnp.full(m_ref.shape, -jnp.inf, F32)
    l_ref[...] = jnp.zeros(l_ref.shape, F32)
    acc_ref[...] = jnp.zeros(acc_ref.shape, F32)

    def score(j, n):
        start = pl.multiple_of(j * tq, tq)
        return jnp.dot(k_ref[pl.ds(start, tq), hs[n]], qqts[n], preferred_element_type=F32)

    ones_rows = jnp.ones((ONES_ROWS, tq), BF16)

    def step(j, cur_ref, nxt_ref, masked):
        for n in range(heads):
            s = cur_ref[n]
            if masked:
                krow = lax.broadcasted_iota(jnp.int32, s.shape, 0)
                qcol = lax.broadcasted_iota(jnp.int32, s.shape, 1) % tq
                s = jnp.where((krow // CHUNK) <= (qcol // CHUNK), s, -jnp.inf)
            m = m_ref[n]
            m_new = jnp.maximum(m, jnp.max(s, axis=0, keepdims=True))
            alpha = jnp.exp2(m - m_new)
            p = jnp.exp2(s - m_new).astype(BF16)
            m_ref[n] = m_new
            lhs = jnp.concatenate([vt_ref[j, hs[n], :], ones_rows], axis=0)
            pv = jnp.dot(lhs, p, preferred_element_type=F32)
            acc_ref[n] = alpha * acc_ref[n] + pv[:A_V_DIM]
            l_ref[n] = alpha * l_ref[n] + pv[A_V_DIM:A_V_DIM + 1]
            if nxt_ref is not None:
                nxt_ref[n] = score(j + 1, n)

    for n in range(heads):
        s0_ref[n] = score(0, n)

    def pair(jj, carry):
        step(2 * jj, s0_ref, s1_ref, False)
        step(2 * jj + 1, s1_ref, s0_ref, False)
        return carry

    lax.fori_loop(0, i // 2, pair, 0)

    @pl.when(i % 2 == 1)
    def _():
        step(i - 1, s0_ref, s1_ref, False)
        step(i, s1_ref, None, True)

    @pl.when(i % 2 == 0)
    def _():
        step(i, s0_ref, None, True)
    lam = _lambda_value(lam_ref)
    for n in range(heads):
        l, acc = l_ref[n], acc_ref[n]
        ot = acc[:, :tq] / l[:, :tq] - lam * (acc[:, tq:] / l[:, tq:])
        o = ot.T
        o = o * lax.rsqrt(jnp.mean(o * o, axis=1, keepdims=True) + SUBLN_EPS)
        o = o * gain_ref[...] * (1.0 - LAMBDA_INIT)
        o_ref[:, hs[n]] = (o * _silu(za_ref[:, hs[n]].astype(F32))).astype(o_ref.dtype)


def _attn_prompt(qt, k, vt, h, gain, lamv, batch, t, tq=256, heads=8):
    nq = t // tq
    n = batch * t
    hw = heads * LANES
    return pl.pallas_call(
        functools.partial(_attn_prompt_kernel, tq=tq, heads=heads),
        out_shape=jax.ShapeDtypeStruct((n, A_WIDTH), BF16),
        grid=(batch, A_HEADS // heads, nq),
        in_specs=[
            pl.BlockSpec((None, hw, tq), lambda b, hh, i: (b, hh, i)),
            pl.BlockSpec((t, hw), lambda b, hh, i: (b, hh)),
            pl.BlockSpec((None, nq, hw, tq), lambda b, hh, i: (b, 0, hh, 0)),
            pl.BlockSpec((tq, hw), lambda b, hh, i: (b * nq + i, H_ZA // hw + hh)),
            pl.BlockSpec((1, LANES), lambda b, hh, i: (0, 0)),
            pl.BlockSpec((4, A_QK_DIM), lambda b, hh, i: (0, 0)),
        ],
        out_specs=pl.BlockSpec((tq, hw), lambda b, hh, i: (b * nq + i, hh)),
        scratch_shapes=[pltpu.VMEM((heads, 1, 2 * tq), F32), pltpu.VMEM((heads, 1, 2 * tq), F32),
                        pltpu.VMEM((heads, A_V_DIM, 2 * tq), F32),
                        pltpu.VMEM((heads, tq, 2 * tq), F32), pltpu.VMEM((heads, tq, 2 * tq), F32)],
        compiler_params=_params("arbitrary", "arbitrary", "arbitrary"),
        name="attn_prompt",
    )(qt, k, vt, h, gain, lamv)


def _chunk_mask(s, ts, q_pos0, k_pos0):
    q_pos = q_pos0 + lax.broadcasted_iota(jnp.int32, s.shape, 0) % ts
    k_pos = k_pos0 + lax.broadcasted_iota(jnp.int32, s.shape, 1)
    return jnp.where((k_pos // CHUNK) <= (q_pos // CHUNK), s, -jnp.inf)


def _rotate_lanes(blk, c, sa, sb):
    half = ROT_DIM // 2
    return blk * c + pltpu.roll(blk, half, 1) * sa + pltpu.roll(blk, LANES - half, 1) * sb


def _attn_sample_kernel(q_ref, k_ref, v_ref, za_ref, cos_ref, sa_ref, sb_ref, kct_ref, vc_ref, gain_ref, lam_ref,
                        o_ref, krot_ref, *, ts, past, mask_cache, mask_new):
    c, sa, sb = cos_ref[...], sa_ref[...], sb_ref[...]
    nt = (((1,), (1,)), ((), ()))
    hs = [slice(n * LANES, (n + 1) * LANES) for n in range(A_HEADS)]
    scores = []
    for n in range(A_HEADS):
        q = (_rotate_lanes(q_ref[:, hs[n]], c, sa, sb) * A_SCALE).astype(BF16)
        k = _rotate_lanes(k_ref[:, hs[n]], c, sa, sb)
        krot_ref[:, hs[n]] = k
        qq = _split_components(q)
        s_c = jnp.dot(qq, kct_ref[n].astype(BF16), preferred_element_type=F32)
        s_n = lax.dot_general(qq, k.astype(BF16), nt, preferred_element_type=F32)
        if mask_cache:
            s_c = _chunk_mask(s_c, ts, past, 0)
        if mask_new:
            s_n = _chunk_mask(s_n, ts, past, past)
        scores.append((s_c, s_n))
    lam = _lambda_value(lam_ref)
    for n, (s_c, s_n) in enumerate(scores):
        m = jnp.maximum(jnp.max(s_c, axis=1, keepdims=True), jnp.max(s_n, axis=1, keepdims=True))
        p_c = jnp.exp(s_c - m)
        p_n = jnp.exp(s_n - m)
        l = jnp.sum(p_c, axis=1, keepdims=True) + jnp.sum(p_n, axis=1, keepdims=True)
        vc = vc_ref[pl.ds(n, past, stride=A_HEADS), :].astype(BF16)
        acc = (jnp.dot(p_c.astype(BF16), vc, preferred_element_type=F32)
               + jnp.dot(p_n.astype(BF16), v_ref[:, hs[n]].astype(BF16), preferred_element_type=F32))
        out = _attn_finish(acc, l, ts, lam, gain_ref[...], za_ref[:, hs[n]].astype(F32))
        o_ref[:, hs[n]] = out.astype(o_ref.dtype)


def _needs_mask(q_pos, k_pos):
    return not bool(((k_pos[None, :] // CHUNK) <= (q_pos[:, None] // CHUNK)).all())


def _attn_sample(hs_act, rot, kct, vc, gain, lamv, batch, ts, past):
    q_pos = past + np.arange(ts)
    seg = lambda col: pl.BlockSpec((ts, A_WIDTH), lambda b: (b, col // A_WIDTH))
    tab = pl.BlockSpec((ts, LANES), lambda b: (0, 0))
    row = pl.BlockSpec((ts, A_WIDTH), lambda b: (b, 0))
    return pl.pallas_call(
        functools.partial(_attn_sample_kernel, ts=ts, past=past,
                          mask_cache=_needs_mask(q_pos, np.arange(past)),
                          mask_new=_needs_mask(q_pos, q_pos)),
        out_shape=[jax.ShapeDtypeStruct((batch * ts, A_WIDTH), BF16),
                   jax.ShapeDtypeStruct((batch * ts, A_WIDTH), F32)],
        grid=(batch,),
        in_specs=[
            seg(COL_Q), seg(COL_K), seg(COL_V), seg(COL_ZA), tab, tab, tab,
            pl.BlockSpec((None, A_HEADS, LANES, past), lambda b: (b, 0, 0, 0)),
            pl.BlockSpec((None, past * A_HEADS, LANES), lambda b: (b, 0, 0)),
            pl.BlockSpec((1, LANES), lambda b: (0, 0)),
            pl.BlockSpec((4, A_QK_DIM), lambda b: (0, 0)),
        ],
        out_specs=[row, row],
        compiler_params=_params("arbitrary"),
        name="attn_sample",
    )(hs_act, hs_act, hs_act, hs_act, *rot, kct, vc, gain, lamv)


def _pool_kernel(u_ref, uprev_ref, hist_ref, zp_ref, pw_ref, ps_ref, o_ref, *, tm, pos0):
    i = pl.program_id(1)
    u = u_ref[...]
    prev = jnp.where(i == 0, hist_ref[...], uprev_ref[...])
    ext = jnp.concatenate([prev, u], axis=0)
    hist_rows = prev.shape[0]
    pos = pos0 + i * tm + lax.broadcasted_iota(jnp.int32, (tm, 1), 0)
    parts = []
    for g, w in enumerate(P_WINDOWS):
        sl = slice(g * P_GROUP_DIM, (g + 1) * P_GROUP_DIM)
        tot = ext[:, sl]
        span = 1
        while span < w:
            tot = tot + pltpu.roll(tot, span, 0)
            span *= 2
        tot = tot[hist_rows:]
        cnt = jnp.minimum(pos + 1, w).astype(F32)
        pooled = tot / cnt - u[:, sl]
        parts.append(jnp.dot(pooled.astype(BF16), pw_ref[g].astype(BF16), preferred_element_type=F32))
    mixed = jnp.concatenate(parts, axis=1) * ps_ref[...]
    o_ref[...] = (mixed * _silu(zp_ref[...].astype(F32))).astype(o_ref.dtype)


def _pool(u, u_col, hist, z, z_col, pool_w, pool_scale, batch, t, tm, pos0):
    nt = t // tm
    hr = hist.shape[1]
    per = tm // hr
    ucb, zcb = u_col // P_WIDTH, z_col // P_WIDTH
    return pl.pallas_call(
        functools.partial(_pool_kernel, tm=tm, pos0=pos0),
        out_shape=jax.ShapeDtypeStruct((batch * t, P_WIDTH), BF16),
        grid=(batch, nt),
        in_specs=[
            pl.BlockSpec((tm, P_WIDTH), lambda b, i: (b * nt + i, ucb)),
            pl.BlockSpec((hr, P_WIDTH), lambda b, i: (jnp.maximum((b * nt + i) * per - 1, 0), ucb)),
            pl.BlockSpec((None, hr, P_WIDTH), lambda b, i: (b, 0, 0)),
            pl.BlockSpec((tm, P_WIDTH), lambda b, i: (b * nt + i, zcb)),
            pl.BlockSpec((len(P_WINDOWS), P_GROUP_DIM, P_GROUP_DIM), lambda b, i: (0, 0, 0)),
            pl.BlockSpec((1, P_WIDTH), lambda b, i: (0, 0)),
        ],
        out_specs=pl.BlockSpec((tm, P_WIDTH), lambda b, i: (b * nt + i, 0)),
        compiler_params=_params("arbitrary", "arbitrary"),
        name="pool",
    )(u, u, hist, z, pool_w, pool_scale)


def _mem_attn_kernel(q_ref, mk_ref, mv_ref, zm_ref, o_ref):
    nt = (((1,), (1,)), ((), ()))
    hs = [slice(n * M_HEAD_DIM, (n + 1) * M_HEAD_DIM) for n in range(M_HEADS)]
    scores = [lax.dot_general(q_ref[:, sl].astype(BF16), mk_ref[:, sl].astype(BF16), nt,
                              preferred_element_type=F32) * M_SCALE for sl in hs]
    for sl, s in zip(hs, scores):
        m = jnp.max(s, axis=1, keepdims=True)
        p = jnp.exp(s - m)
        l = jnp.sum(p, axis=1, keepdims=True)
        o = jnp.dot(p.astype(BF16), mv_ref[:, sl].astype(BF16), preferred_element_type=F32) / l
        o_ref[:, sl] = (o * _silu(zm_ref[:, sl].astype(F32))).astype(o_ref.dtype)


def _mem_attn(act, q_col, z_col, mk, mv, batch, t, tq):
    nq = t // tq
    qcb, zcb = q_col // M_WIDTH, z_col // M_WIDTH
    return pl.pallas_call(
        _mem_attn_kernel,
        out_shape=jax.ShapeDtypeStruct((batch * t, M_WIDTH), BF16),
        grid=(batch, nq),
        in_specs=[
            pl.BlockSpec((tq, M_WIDTH), lambda b, i: (b * nq + i, qcb)),
            pl.BlockSpec((M_TOKENS, M_WIDTH), lambda b, i: (b, 0)),
            pl.BlockSpec((M_TOKENS, M_WIDTH), lambda b, i: (b, 0)),
            pl.BlockSpec((tq, M_WIDTH), lambda b, i: (b * nq + i, zcb)),
        ],
        out_specs=pl.BlockSpec((tq, M_WIDTH), lambda b, i: (b * nq + i, 0)),
        compiler_params=_params("arbitrary", "arbitrary"),
        name="mem_attn",
    )(act, mk, mv, act)


def _merge_kernel(ba_ref, bp_ref, bm_ref, g0_ref, g1_ref, g2_ref, wb_ref, o_ref, wbb_ref, *, halves):
    @pl.when(pl.program_id(1) == 0)
    def _():
        wbb_ref[...] = wb_ref[...].astype(BF16)

    branches = ((ba_ref, g0_ref), (bp_ref, g1_ref), (bm_ref, g2_ref))
    rows = o_ref.shape[0] // halves
    projs = [[jnp.dot(b_ref[r * rows:(r + 1) * rows, :], wbb_ref[n], preferred_element_type=F32)
              for n, (b_ref, _) in enumerate(branches)] for r in range(halves)]
    for r in range(halves):
        sl = slice(r * rows, (r + 1) * rows)
        acc = None
        for n, (_, g_ref) in enumerate(branches):
            term = _sigmoid(g_ref[sl, :].astype(F32)) * projs[r][n]
            acc = term if acc is None else acc + term
        o_ref[sl, :] = acc.astype(o_ref.dtype)


def _merge(ba, bp, bm, h, g_col, wb, tm, tn):
    n = ba.shape[0]
    width = ba.shape[1]
    gspec = lambda k: pl.BlockSpec((tm, tn), lambda j, i: (i, (g_col + k * D_MODEL) // tn + j))
    bspec = pl.BlockSpec((tm, width), lambda j, i: (i, 0))
    return pl.pallas_call(
        functools.partial(_merge_kernel, halves=2 if tm % 32 == 0 else 1),
        out_shape=jax.ShapeDtypeStruct((n, D_MODEL), BF16),
        grid=(D_MODEL // tn, n // tm),
        in_specs=[bspec, bspec, bspec, gspec(0), gspec(1), gspec(2),
                  pl.BlockSpec((N_BRANCH, width, tn), lambda j, i: (0, 0, j))],
        out_specs=pl.BlockSpec((tm, tn), lambda j, i: (i, j)),
        scratch_shapes=[pltpu.VMEM((N_BRANCH, width, tn), BF16)],
        compiler_params=_params("arbitrary", "arbitrary"),
        name="merge",
    )(ba, bp, bm, h, h, h, wb)


def _out_kernel(m_ref, w_ref, x_ref, g_ref, b_ref, o_ref, wb_ref, *, halves):
    @pl.when(pl.program_id(0) == 0)
    def _():
        wb_ref[...] = w_ref[...].astype(BF16)

    rows = o_ref.shape[0] // halves
    outs = [jnp.dot(m_ref[r * rows:(r + 1) * rows, :], wb_ref[...], preferred_element_type=F32)
            for r in range(halves)]
    for r in range(halves):
        sl = slice(r * rows, (r + 1) * rows)
        z = DEEPNORM_ALPHA * x_ref[sl, :] + outs[r]
        mu = jnp.mean(z, axis=1, keepdims=True)
        zc = z - mu
        var = jnp.mean(zc * zc, axis=1, keepdims=True)
        o_ref[sl, :] = zc * lax.rsqrt(var + LN_EPS) * g_ref[...] + b_ref[...]


def _out(merged, w_out, x, ln_g, ln_b, tm):
    n = merged.shape[0]
    row = pl.BlockSpec((tm, D_MODEL), lambda i: (i, 0))
    vec = pl.BlockSpec((1, D_MODEL), lambda i: (0, 0))
    wspec = pl.BlockSpec((D_MODEL, D_MODEL), lambda i: (0, 0), pipeline_mode=pl.Buffered(1))
    return pl.pallas_call(
        functools.partial(_out_kernel, halves=2 if tm % 32 == 0 else 1),
        out_shape=jax.ShapeDtypeStruct((n, D_MODEL), F32),
        grid=(n // tm,),
        in_specs=[row, wspec, row, vec, vec],
        out_specs=row,
        scratch_shapes=[pltpu.VMEM((D_MODEL, D_MODEL), BF16)],
        compiler_params=_params("arbitrary"),
        name="out_ln",
    )(merged, w_out, x, ln_g, ln_b)


def _rest_col_block(tn):
    za_blocks = (COL_UP - COL_ZA) // tn
    return lambda j: jnp.where(j < za_blocks, COL_ZA // tn + j, COL_ZP // tn + (j - za_blocks))


def _layer_prompt(x2d, batch, t, w, pool_hist, mk, mv):
    ang = _rotary_angles(jnp.arange(t))
    cos_t, sin_t = jnp.cos(ang).T, jnp.sin(ang).T
    tm, tn, tp = 2048, 1024, 512
    qt, xb = _proj_ft(x2d, w["w_in"], COL_Q, cos_t, sin_t, batch, t, tp, BF16,
                      scale=A_SCALE * math.log2(math.e), emit_xb=True, name="proj_qt")
    kt, kb = _proj_ft(xb, w["w_in"], COL_K, cos_t, sin_t, batch, t, tp, F32, std_dtype=BF16, name="proj_kt")
    v, vt = _proj_t(xb, w["w_in"], COL_V, batch, t, tp, std_dtype=F32,
                    t_out="blocked", t_dtype=BF16, name="proj_vt")
    (u,) = _proj_t(xb, w["w_in"], COL_UP, batch, t, tp, std_dtype=F32, name="proj_u")
    h = _proj(xb, w["w_in"], _rest_col_block(tn), H_COLS, BF16, tm, tn, name="proj_rest")
    b_a = _attn_prompt(qt, kb, vt, h, w["gain"], w["lamv"], batch, t)
    b_p = _pool(u, 0, pool_hist, h, H_ZP, w["pool_w"], w["pool_scale"], batch, t, 512, 0)
    b_m = _mem_attn(h, H_QM, H_ZM, mk, mv, batch, t, 512)
    merged = _merge(b_a, b_p, b_m, h, H_GM, w["w_branch"], 512, 1024)
    y = _out(merged, w["w_out"], x2d, w["ln_g"], w["ln_b"], 512)
    return y, kt, v, u


def _layer_sample(x2d, batch, t, past, w, kt_hist, v_hist, pool_hist, mk, mv):
    n = batch * t
    rot = _rotary_tables(past + jnp.arange(t))
    act = _proj(x2d, w["w_in"], lambda j: j, IN_COLS, F32, n, 1024, name="proj_all")
    b_a, k = _attn_sample(act, rot, kt_hist, v_hist, w["gain"], w["lamv"], batch, t, past)
    b_p = _pool(act, COL_UP, pool_hist, act, COL_ZP, w["pool_w"], w["pool_scale"], batch, t, t, past)
    b_m = _mem_attn(act, COL_QM, COL_ZM, mk, mv, batch, t, t)
    merged = _merge(b_a, b_p, b_m, act, COL_GM, w["w_branch"], n, 1024)
    y = _out(merged, w["w_out"], x2d, w["ln_g"], w["ln_b"], n)
    return y, k, act


def kernel(x_prompt, x_sample, cache_attn_k, cache_attn_v, cache_mem_k, cache_mem_v, state_pool, mem_prompt, w_in, w_mem_kv, lambda_q1, lambda_k1, lambda_q2, lambda_k2, subln_gain, pool_w, pool_scale, w_branch, w_out, ln_gain, ln_bias):
    bp, tp, d = x_prompt.shape
    bs, ts, _ = x_sample.shape
    past = cache_attn_k.shape[2]
    hist_rows = P_HIST + 1

    def run_layer(l, xp2d, xs2d):
        w = {
            "w_in": w_in[l],
            "gain": subln_gain[l].reshape(1, A_V_DIM),
            "lamv": jnp.stack([lambda_q1[l], lambda_k1[l], lambda_q2[l], lambda_k2[l]]),
            "pool_w": pool_w[l],
            "pool_scale": pool_scale[l].reshape(1, P_WIDTH),
            "w_branch": w_branch[l],
            "w_out": w_out[l],
            "ln_g": ln_gain[l].reshape(1, d),
            "ln_b": ln_bias[l].reshape(1, d),
        }
        memf = mem_prompt.reshape(bp * M_TOKENS, d)
        mk = _proj(memf, w_mem_kv[l], lambda j: j, M_WIDTH, F32, 512, 1024, name="proj_mk")
        mv = _proj(memf, w_mem_kv[l], lambda j: M_WIDTH // 1024 + j, M_WIDTH, F32, 512, 1024, name="proj_mv")

        zero_hist = jnp.zeros((bp, hist_rows, P_WIDTH), F32)
        y_p, k_p, v_p, u_p = _layer_prompt(xp2d, bp, tp, w, zero_hist, mk, mv)

        hist_s = jnp.pad(state_pool[l], ((0, 0), (hist_rows - P_HIST, 0), (0, 0)))
        kt_hist = jnp.transpose(cache_attn_k[l], (0, 2, 3, 4, 1)).reshape(bs, A_HEADS, 2 * A_QK_DIM, past)
        y_s, k_s, act_s = _layer_sample(
            xs2d, bs, ts, past, w,
            kt_hist, cache_attn_v[l].reshape(bs, past * A_HEADS, A_V_DIM),
            hist_s, cache_mem_k[l].reshape(bs * M_TOKENS, M_WIDTH), cache_mem_v[l].reshape(bs * M_TOKENS, M_WIDTH))
        v_s = act_s[:, COL_V:COL_V + A_WIDTH]
        u_s = act_s[:, COL_UP:COL_UP + P_WIDTH]

        pool_p = jnp.concatenate([zero_hist, u_p.reshape(bp, tp, P_WIDTH)], axis=1)[:, -P_HIST:]
        pool_s = jnp.concatenate([hist_s, u_s.reshape(bs, ts, P_WIDTH)], axis=1)[:, -P_HIST:]
        k_p = jnp.transpose(k_p.reshape(bp, A_HEADS, 2, A_QK_DIM, tp), (0, 4, 1, 2, 3))
        outs = (k_p, v_p.reshape(bp, tp, A_HEADS, A_V_DIM),
                mk.reshape(bp, M_TOKENS, M_HEADS, M_HEAD_DIM), mv.reshape(bp, M_TOKENS, M_HEADS, M_HEAD_DIM),
                pool_p,
                k_s.reshape(bs, ts, A_HEADS, 2, A_QK_DIM), v_s.reshape(bs, ts, A_HEADS, A_V_DIM), pool_s)
        return y_p, y_s, outs

    yp = x_prompt.reshape(bp * tp, d)
    ys = x_sample.reshape(bs * ts, d)
    per_layer = []
    for l in range(w_in.shape[0]):
        yp, ys, outs = run_layer(l, yp, ys)
        per_layer.append(outs)
    stacked = [jnp.stack([o[n] for o in per_layer]) for n in range(8)]
    return (yp.reshape(bp, tp, d), ys.reshape(bs, ts, d), *stacked)
```

```python
import functools
import math

import numpy as np
import jax
import jax.numpy as jnp
from jax import lax
from jax.experimental import pallas as pl
from jax.experimental.pallas import tpu as pltpu

F32 = jnp.float32
BF16 = jnp.bfloat16

V7X_VMEM_LIMIT_BYTES = 56 * 1024 * 1024
LANES = 128

D_MODEL = 2048
CHUNK = 64
A_HEADS = 8
A_QK_DIM = 64
A_V_DIM = 128
A_WIDTH = A_HEADS * A_V_DIM
A_SCALE = A_QK_DIM ** -0.5
ROT_DIM = A_QK_DIM // 4
ROPE_THETA = 500000.0
SUBLN_EPS = 1e-5
P_WINDOWS = (2, 4, 8, 16)
P_GROUP_DIM = 256
P_WIDTH = len(P_WINDOWS) * P_GROUP_DIM
P_HIST = max(P_WINDOWS) - 1
M_TOKENS = 256
M_HEADS = 4
M_HEAD_DIM = 256
M_WIDTH = M_HEADS * M_HEAD_DIM
M_SCALE = M_HEAD_DIM ** -0.5
N_BRANCH = 3
DEPTH = 1
DEEPNORM_ALPHA = (2.0 * DEPTH) ** 0.25
LN_EPS = 1e-5
LAMBDA_INIT = 0.8 - 0.6 * math.exp(-0.3 * 0)
ONES_ROWS = 16

COL_Q, COL_K, COL_V, COL_ZA, COL_UP, COL_ZP, COL_QM, COL_ZM, COL_GM = (
    0, 1024, 2048, 3072, 4096, 5120, 6144, 7168, 8192)
IN_COLS = 14336
H_ZA, H_ZP, H_QM, H_ZM, H_GM = 0, 1024, 2048, 3072, 4096
H_COLS = 10240


def _params(*sem):
    return pltpu.CompilerParams(dimension_semantics=sem, vmem_limit_bytes=V7X_VMEM_LIMIT_BYTES)


def _sigmoid(z):
    return 1.0 / (1.0 + jnp.exp(-z))


def _silu(z):
    return z * _sigmoid(z)


def _proj_kernel(x_ref, w_ref, *rest, rotary, scale):
    acc = jnp.dot(x_ref[...].astype(BF16), w_ref[...].astype(BF16), preferred_element_type=F32)
    if rotary:
        cos_ref, sa_ref, sb_ref, o_ref = rest
        c, sa, sb = cos_ref[...], sa_ref[...], sb_ref[...]
        half = ROT_DIM // 2
        parts = []
        for k in range(acc.shape[1] // LANES):
            blk = acc[:, k * LANES:(k + 1) * LANES]
            lo = pltpu.roll(blk, half, 1)
            hi = pltpu.roll(blk, LANES - half, 1)
            parts.append(blk * c + lo * sa + hi * sb)
        acc = jnp.concatenate(parts, axis=1)
    else:
        (o_ref,) = rest
    if scale != 1.0:
        acc = acc * scale
    o_ref[...] = acc.astype(o_ref.dtype)


def _proj(x, w, col_block, ncols, out_dtype, tm, tn, rot=None, scale=1.0, name="proj"):
    n, kdim = x.shape
    grid = (n // tm, ncols // tn)
    in_specs = [
        pl.BlockSpec((tm, kdim), lambda i, j: (i, 0)),
        pl.BlockSpec((kdim, tn), lambda i, j: (0, col_block(j))),
    ]
    args = [x, w]
    if rot is not None:
        nrep = rot[0].shape[0] // tm
        for t in rot:
            in_specs.append(pl.BlockSpec((tm, LANES), lambda i, j: (i % nrep, 0)))
            args.append(t)
    return pl.pallas_call(
        functools.partial(_proj_kernel, rotary=rot is not None, scale=scale),
        out_shape=jax.ShapeDtypeStruct((n, ncols), out_dtype),
        grid=grid,
        in_specs=in_specs,
        out_specs=pl.BlockSpec((tm, tn), lambda i, j: (i, j)),
        compiler_params=_params("parallel", "arbitrary"),
        name=name,
    )(*args)


def _proj_t_kernel(x_ref, w_ref, *rest, rotary, scale, std_out, t_out, key_tile, emit_xb):
    rest = list(rest)
    xb = x_ref[...].astype(BF16)
    acc = jnp.dot(xb, w_ref[...].astype(BF16), preferred_element_type=F32)
    if emit_xb:
        rest.pop()[...] = xb
    if rotary:
        c, sa, sb = rest.pop(0)[...], rest.pop(0)[...], rest.pop(0)[...]
        half = ROT_DIM // 2
        parts = []
        for k in range(acc.shape[1] // LANES):
            blk = acc[:, k * LANES:(k + 1) * LANES]
            lo = pltpu.roll(blk, half, 1)
            hi = pltpu.roll(blk, LANES - half, 1)
            parts.append(blk * c + lo * sa + hi * sb)
        acc = jnp.concatenate(parts, axis=1)
    if scale != 1.0:
        acc = acc * scale
    if std_out:
        o_ref = rest.pop(0)
        o_ref[...] = acc.astype(o_ref.dtype)
    if t_out == "plain":
        t_ref = rest.pop(0)
        t_ref[...] = acc.T.astype(t_ref.dtype)
    elif t_out == "blocked":
        t_ref = rest.pop(0)
        for c in range(acc.shape[0] // key_tile):
            t_ref[c] = acc[c * key_tile:(c + 1) * key_tile, :].T.astype(t_ref.dtype)


def _proj_t(x, w, col_start, batch, t, tm, rot=None, scale=1.0, std_dtype=None, t_out=None, t_dtype=None,
            key_tile=256, emit_xb=False, name="proj_t"):
    n, kdim = x.shape
    width = 1024
    nt = t // tm
    in_specs = [
        pl.BlockSpec((tm, kdim), lambda i: (i, 0)),
        pl.BlockSpec((kdim, width), lambda i: (0, col_start // width)),
    ]
    args = [x, w]
    if rot is not None:
        nrep = rot[0].shape[0] // tm
        for tab in rot:
            in_specs.append(pl.BlockSpec((tm, LANES), lambda i: (i % nrep, 0)))
            args.append(tab)
    out_shape, out_specs = [], []
    if std_dtype is not None:
        out_shape.append(jax.ShapeDtypeStruct((n, width), std_dtype))
        out_specs.append(pl.BlockSpec((tm, width), lambda i: (i, 0)))
    if t_out == "plain":
        out_shape.append(jax.ShapeDtypeStruct((batch, width, t), t_dtype))
        out_specs.append(pl.BlockSpec((None, width, tm), lambda i: (i // nt, 0, i % nt)))
    elif t_out == "blocked":
        out_shape.append(jax.ShapeDtypeStruct((batch, t // key_tile, width, key_tile), t_dtype))
        out_specs.append(pl.BlockSpec((None, tm // key_tile, width, key_tile), lambda i: (i // nt, i % nt, 0, 0)))
    if emit_xb:
        out_shape.append(jax.ShapeDtypeStruct((n, kdim), BF16))
        out_specs.append(pl.BlockSpec((tm, kdim), lambda i: (i, 0)))
    return pl.pallas_call(
        functools.partial(_proj_t_kernel, rotary=rot is not None, scale=scale,
                          std_out=std_dtype is not None, t_out=t_out, key_tile=key_tile, emit_xb=emit_xb),
        out_shape=out_shape,
        grid=(n // tm,),
        in_specs=in_specs,
        out_specs=out_specs,
        compiler_params=_params("arbitrary"),
        name=name,
    )(*args)


def _proj_ft_kernel(x_ref, w_ref, cos_ref, sin_ref, t_ref, *rest, scale, emit_std, emit_xb):
    rest = list(rest)
    xb = x_ref[...].astype(BF16)
    acc = jnp.dot(xb, w_ref[...].astype(BF16), preferred_element_type=F32).T
    c, s = cos_ref[...], sin_ref[...]
    half = ROT_DIM // 2
    pieces = []
    for g in range(acc.shape[0] // A_QK_DIM):
        b = g * A_QK_DIM
        lo, hi = acc[b:b + half], acc[b + half:b + ROT_DIM]
        pieces += [lo * c - hi * s, hi * c + lo * s, acc[b + ROT_DIM:b + A_QK_DIM]]
    acc = jnp.concatenate(pieces, axis=0)
    if scale != 1.0:
        acc = acc * scale
    t_ref[...] = acc.astype(t_ref.dtype)
    if emit_std:
        std_ref = rest.pop(0)
        std_ref[...] = acc.T.astype(std_ref.dtype)
    if emit_xb:
        rest.pop(0)[...] = xb


def _proj_ft(x, w, col_start, cos_t, sin_t, batch, t, tm, t_dtype, scale=1.0, std_dtype=None, emit_xb=False,
             name="proj_ft"):
    n, kdim = x.shape
    width = A_WIDTH
    nt = t // tm
    half = ROT_DIM // 2
    tab = pl.BlockSpec((half, tm), lambda i: (0, i % nt))
    out_shape = [jax.ShapeDtypeStruct((batch, width, t), t_dtype)]
    out_specs = [pl.BlockSpec((None, width, tm), lambda i: (i // nt, 0, i % nt))]
    if std_dtype is not None:
        out_shape.append(jax.ShapeDtypeStruct((n, width), std_dtype))
        out_specs.append(pl.BlockSpec((tm, width), lambda i: (i, 0)))
    if emit_xb:
        out_shape.append(jax.ShapeDtypeStruct((n, kdim), BF16))
        out_specs.append(pl.BlockSpec((tm, kdim), lambda i: (i, 0)))
    return pl.pallas_call(
        functools.partial(_proj_ft_kernel, scale=scale, emit_std=std_dtype is not None, emit_xb=emit_xb),
        out_shape=out_shape,
        grid=(n // tm,),
        in_specs=[pl.BlockSpec((tm, kdim), lambda i: (i, 0)),
                  pl.BlockSpec((kdim, width), lambda i: (0, col_start // width)), tab, tab],
        out_specs=out_specs,
        compiler_params=_params("arbitrary"),
        name=name,
    )(x, w, cos_t, sin_t)


def _rotary_angles(pos):
    half = ROT_DIM // 2
    inv = ROPE_THETA ** (-(jnp.arange(half, dtype=F32) * 2.0) / ROT_DIM)
    return pos.astype(F32)[:, None] * inv[None, :]


def _rotary_tables(pos):
    half = ROT_DIM // 2
    ang = _rotary_angles(pos)
    cos, sin = jnp.cos(ang), jnp.sin(ang)
    t = pos.shape[0]
    ones = jnp.ones((t, A_QK_DIM - ROT_DIM), F32)
    zeros = jnp.zeros((t, A_QK_DIM - ROT_DIM), F32)
    zh = jnp.zeros((t, half), F32)
    c64 = jnp.concatenate([cos, cos, ones], axis=1)
    sa64 = jnp.concatenate([zh, sin, zeros], axis=1)
    sb64 = jnp.concatenate([-sin, zh, zeros], axis=1)
    tile = lambda a: jnp.concatenate([a, a], axis=1)
    return tile(c64), tile(sa64), tile(sb64)


def _lambda_value(lam_ref):
    lv = lam_ref[...]
    a = jnp.sum(lv[0:1] * lv[1:2], axis=1, keepdims=True)
    b = jnp.sum(lv[2:3] * lv[3:4], axis=1, keepdims=True)
    return jnp.exp(a) - jnp.exp(b) + LAMBDA_INIT


def _split_components(q):
    lane = lax.broadcasted_iota(jnp.int32, q.shape, 1)
    qf = q.astype(F32)
    return jnp.concatenate([jnp.where(lane < A_QK_DIM, qf, 0.0),
                            jnp.where(lane >= A_QK_DIM, qf, 0.0)], axis=0).astype(q.dtype)


def _attn_finish(acc, l, tq, lam, gain, za):
    o = acc[:tq] / l[:tq] - lam * (acc[tq:] / l[tq:])
    o = o * lax.rsqrt(jnp.mean(o * o, axis=1, keepdims=True) + SUBLN_EPS)
    o = o * gain * (1.0 - LAMBDA_INIT)
    return o * _silu(za)


def _attn_prompt_kernel(qt_ref, k_ref, vt_ref, za_ref, gain_ref, lam_ref, o_ref, m_ref, l_ref, acc_ref,
                        s0_ref, s1_ref, *,
                        tq, heads):
    i = pl.program_id(2)
    hs = [slice(n * LANES, (n + 1) * LANES) for n in range(heads)]
    qqts = []
    for sl in hs:
        qt = qt_ref[sl, :].astype(F32)
        row = lax.broadcasted_iota(jnp.int32, qt.shape, 0)
        qqts.append(jnp.concatenate([jnp.where(row < A_QK_DIM, qt, 0.0),
                                     jnp.where(row >= A_QK_DIM, qt, 0.0)], axis=1).astype(BF16))

    m_ref[...] = jnp.full(m_ref.shape, -jnp.inf, F32)
    l_ref[...] = jnp.zeros(l_ref.shape, F32)
    acc_ref[...] = jnp.zeros(acc_ref.shape, F32)

    def score(j, n):
        start = pl.multiple_of(j * tq, tq)
        return jnp.dot(k_ref[pl.ds(start, tq), hs[n]], qqts[n], preferred_element_type=F32)

    ones_rows = jnp.ones((ONES_ROWS, tq), BF16)

    def step(j, cur_ref, nxt_ref, masked):
        for n in range(heads):
            s = cur_ref[n]
            if masked:
                krow = lax.broadcasted_iota(jnp.int32, s.shape, 0)
                qcol = lax.broadcasted_iota(jnp.int32, s.shape, 1) % tq
                s = jnp.where((krow // CHUNK) <= (qcol // CHUNK), s, -jnp.inf)
            m = m_ref[n]
            m_new = jnp.maximum(m, jnp.max(s, axis=0, keepdims=True))
            alpha = jnp.exp2(m - m_new)
            p = jnp.exp2(s - m_new).astype(BF16)
            m_ref[n] = m_new
            lhs = jnp.concatenate([vt_ref[j, hs[n], :], ones_rows], axis=0)
            pv = jnp.dot(lhs, p, preferred_element_type=F32)
            acc_ref[n] = alpha * acc_ref[n] + pv[:A_V_DIM]
            l_ref[n] = alpha * l_ref[n] + pv[A_V_DIM:A_V_DIM + 1]
            if nxt_ref is not None:
                nxt_ref[n] = score(j + 1, n)

    for n in range(heads):
        s0_ref[n] = score(0, n)

    def pair(jj, carry):
        step(2 * jj, s0_ref, s1_ref, False)
        step(2 * jj + 1, s1_ref, s0_ref, False)
        return carry

    lax.fori_loop(0, i // 2, pair, 0)

    @pl.when(i % 2 == 1)
    def _():
        step(i - 1, s0_ref, s1_ref, False)
        step(i, s1_ref, None, True)

    @pl.when(i % 2 == 0)
    def _():
        step(i, s0_ref, None, True)
    lam = _lambda_value(lam_ref)
    for n in range(heads):
        l, acc = l_ref[n], acc_ref[n]
        ot = acc[:, :tq] / l[:, :tq] - lam * (acc[:, tq:] / l[:, tq:])
        o = ot.T
        o = o * lax.rsqrt(jnp.mean(o * o, axis=1, keepdims=True) + SUBLN_EPS)
        o = o * gain_ref[...] * (1.0 - LAMBDA_INIT)
        o_ref[:, hs[n]] = (o * _silu(za_ref[:, hs[n]].astype(F32))).astype(o_ref.dtype)


def _attn_prompt(qt, k, vt, h, gain, lamv, batch, t, tq=256, heads=8):
    nq = t // tq
    n = batch * t
    hw = heads * LANES
    return pl.pallas_call(
        functools.partial(_attn_prompt_kernel, tq=tq, heads=heads),
        out_shape=jax.ShapeDtypeStruct((n, A_WIDTH), BF16),
        grid=(batch, A_HEADS // heads, nq),
        in_specs=[
            pl.BlockSpec((None, hw, tq), lambda b, hh, i: (b, hh, i)),
            pl.BlockSpec((t, hw), lambda b, hh, i: (b, hh)),
            pl.BlockSpec((None, nq, hw, tq), lambda b, hh, i: (b, 0, hh, 0)),
            pl.BlockSpec((tq, hw), lambda b, hh, i: (b * nq + i, H_ZA // hw + hh)),
            pl.BlockSpec((1, LANES), lambda b, hh, i: (0, 0)),
            pl.BlockSpec((4, A_QK_DIM), lambda b, hh, i: (0, 0)),
        ],
        out_specs=pl.BlockSpec((tq, hw), lambda b, hh, i: (b * nq + i, hh)),
        scratch_shapes=[pltpu.VMEM((heads, 1, 2 * tq), F32), pltpu.VMEM((heads, 1, 2 * tq), F32),
                        pltpu.VMEM((heads, A_V_DIM, 2 * tq), F32),
                        pltpu.VMEM((heads, tq, 2 * tq), F32), pltpu.VMEM((heads, tq, 2 * tq), F32)],
        compiler_params=_params("arbitrary", "arbitrary", "arbitrary"),
        name="attn_prompt",
    )(qt, k, vt, h, gain, lamv)


def _chunk_mask(s, ts, q_pos0, k_pos0):
    q_pos = q_pos0 + lax.broadcasted_iota(jnp.int32, s.shape, 0) % ts
    k_pos = k_pos0 + lax.broadcasted_iota(jnp.int32, s.shape, 1)
    return jnp.where((k_pos // CHUNK) <= (q_pos // CHUNK), s, -jnp.inf)


def _rotate_lanes(blk, c, sa, sb):
    half = ROT_DIM // 2
    return blk * c + pltpu.roll(blk, half, 1) * sa + pltpu.roll(blk, LANES - half, 1) * sb


def _attn_sample_kernel(q_ref, k_ref, v_ref, za_ref, cos_ref, sa_ref, sb_ref, kct_ref, vc_ref, gain_ref, lam_ref,
                        o_ref, krot_ref, *, ts, past, mask_cache, mask_new):
    c, sa, sb = cos_ref[...], sa_ref[...], sb_ref[...]
    nt = (((1,), (1,)), ((), ()))
    hs = [slice(n * LANES, (n + 1) * LANES) for n in range(A_HEADS)]
    scores = []
    for n in range(A_HEADS):
        q = (_rotate_lanes(q_ref[:, hs[n]], c, sa, sb) * A_SCALE).astype(BF16)
        k = _rotate_lanes(k_ref[:, hs[n]], c, sa, sb)
        krot_ref[:, hs[n]] = k
        qq = _split_components(q)
        s_c = jnp.dot(qq, kct_ref[n].astype(BF16), preferred_element_type=F32)
        s_n = lax.dot_general(qq, k.astype(BF16), nt, preferred_element_type=F32)
        if mask_cache:
            s_c = _chunk_mask(s_c, ts, past, 0)
        if mask_new:
            s_n = _chunk_mask(s_n, ts, past, past)
        scores.append((s_c, s_n))
    lam = _lambda_value(lam_ref)
    for n, (s_c, s_n) in enumerate(scores):
        m = jnp.maximum(jnp.max(s_c, axis=1, keepdims=True), jnp.max(s_n, axis=1, keepdims=True))
        p_c = jnp.exp(s_c - m)
        p_n = jnp.exp(s_n - m)
        l = jnp.sum(p_c, axis=1, keepdims=True) + jnp.sum(p_n, axis=1, keepdims=True)
        vc = vc_ref[pl.ds(n, past, stride=A_HEADS), :].astype(BF16)
        acc = (jnp.dot(p_c.astype(BF16), vc, preferred_element_type=F32)
               + jnp.dot(p_n.astype(BF16), v_ref[:, hs[n]].astype(BF16), preferred_element_type=F32))
        out = _attn_finish(acc, l, ts, lam, gain_ref[...], za_ref[:, hs[n]].astype(F32))
        o_ref[:, hs[n]] = out.astype(o_ref.dtype)


def _needs_mask(q_pos, k_pos):
    return not bool(((k_pos[None, :] // CHUNK) <= (q_pos[:, None] // CHUNK)).all())


def _attn_sample(hs_act, rot, kct, vc, gain, lamv, batch, ts, past):
    q_pos = past + np.arange(ts)
    seg = lambda col: pl.BlockSpec((ts, A_WIDTH), lambda b: (b, col // A_WIDTH))
    tab = pl.BlockSpec((ts, LANES), lambda b: (0, 0))
    row = pl.BlockSpec((ts, A_WIDTH), lambda b: (b, 0))
    return pl.pallas_call(
        functools.partial(_attn_sample_kernel, ts=ts, past=past,
                          mask_cache=_needs_mask(q_pos, np.arange(past)),
                          mask_new=_needs_mask(q_pos, q_pos)),
        out_shape=[jax.ShapeDtypeStruct((batch * ts, A_WIDTH), BF16),
                   jax.ShapeDtypeStruct((batch * ts, A_WIDTH), F32)],
        grid=(batch,),
        in_specs=[
            seg(COL_Q), seg(COL_K), seg(COL_V), seg(COL_ZA), tab, tab, tab,
            pl.BlockSpec((None, A_HEADS, LANES, past), lambda b: (b, 0, 0, 0)),
            pl.BlockSpec((None, past * A_HEADS, LANES), lambda b: (b, 0, 0)),
            pl.BlockSpec((1, LANES), lambda b: (0, 0)),
            pl.BlockSpec((4, A_QK_DIM), lambda b: (0, 0)),
        ],
        out_specs=[row, row],
        compiler_params=_params("arbitrary"),
        name="attn_sample",
    )(hs_act, hs_act, hs_act, hs_act, *rot, kct, vc, gain, lamv)


def _pool_kernel(u_ref, uprev_ref, hist_ref, zp_ref, pw_ref, ps_ref, o_ref, *, tm, pos0):
    i = pl.program_id(1)
    u = u_ref[...]
    prev = jnp.where(i == 0, hist_ref[...], uprev_ref[...])
    ext = jnp.concatenate([prev, u], axis=0)
    hist_rows = prev.shape[0]
    pos = pos0 + i * tm + lax.broadcasted_iota(jnp.int32, (tm, 1), 0)
    parts = []
    for g, w in enumerate(P_WINDOWS):
        sl = slice(g * P_GROUP_DIM, (g + 1) * P_GROUP_DIM)
        tot = ext[:, sl]
        span = 1
        while span < w:
            tot = tot + pltpu.roll(tot, span, 0)
            span *= 2
        tot = tot[hist_rows:]
        cnt = jnp.minimum(pos + 1, w).astype(F32)
        pooled = tot / cnt - u[:, sl]
        parts.append(jnp.dot(pooled.astype(BF16), pw_ref[g].astype(BF16), preferred_element_type=F32))
    mixed = jnp.concatenate(parts, axis=1) * ps_ref[...]
    o_ref[...] = (mixed * _silu(zp_ref[...].astype(F32))).astype(o_ref.dtype)


def _pool(u, u_col, hist, z, z_col, pool_w, pool_scale, batch, t, tm, pos0):
    nt = t // tm
    hr = hist.shape[1]
    per = tm // hr
    ucb, zcb = u_col // P_WIDTH, z_col // P_WIDTH
    return pl.pallas_call(
        functools.partial(_pool_kernel, tm=tm, pos0=pos0),
        out_shape=jax.ShapeDtypeStruct((batch * t, P_WIDTH), BF16),
        grid=(batch, nt),
        in_specs=[
            pl.BlockSpec((tm, P_WIDTH), lambda b, i: (b * nt + i, ucb)),
            pl.BlockSpec((hr, P_WIDTH), lambda b, i: (jnp.maximum((b * nt + i) * per - 1, 0), ucb)),
            pl.BlockSpec((None, hr, P_WIDTH), lambda b, i: (b, 0, 0)),
            pl.BlockSpec((tm, P_WIDTH), lambda b, i: (b * nt + i, zcb)),
            pl.BlockSpec((len(P_WINDOWS), P_GROUP_DIM, P_GROUP_DIM), lambda b, i: (0, 0, 0)),
            pl.BlockSpec((1, P_WIDTH), lambda b, i: (0, 0)),
        ],
        out_specs=pl.BlockSpec((tm, P_WIDTH), lambda b, i: (b * nt + i, 0)),
        compiler_params=_params("arbitrary", "arbitrary"),
        name="pool",
    )(u, u, hist, z, pool_w, pool_scale)


def _mem_attn_kernel(q_ref, mk_ref, mv_ref, zm_ref, o_ref):
    nt = (((1,), (1,)), ((), ()))
    hs = [slice(n * M_HEAD_DIM, (n + 1) * M_HEAD_DIM) for n in range(M_HEADS)]
    scores = [lax.dot_general(q_ref[:, sl].astype(BF16), mk_ref[:, sl].astype(BF16), nt,
                              preferred_element_type=F32) * M_SCALE for sl in hs]
    for sl, s in zip(hs, scores):
        m = jnp.max(s, axis=1, keepdims=True)
        p = jnp.exp(s - m)
        l = jnp.sum(p, axis=1, keepdims=True)
        o = jnp.dot(p.astype(BF16), mv_ref[:, sl].astype(BF16), preferred_element_type=F32) / l
        o_ref[:, sl] = (o * _silu(zm_ref[:, sl].astype(F32))).astype(o_ref.dtype)


def _mem_attn(act, q_col, z_col, mk, mv, batch, t, tq):
    nq = t // tq
    qcb, zcb = q_col // M_WIDTH, z_col // M_WIDTH
    return pl.pallas_call(
        _mem_attn_kernel,
        out_shape=jax.ShapeDtypeStruct((batch * t, M_WIDTH), BF16),
        grid=(batch, nq),
        in_specs=[
            pl.BlockSpec((tq, M_WIDTH), lambda b, i: (b * nq + i, qcb)),
            pl.BlockSpec((M_TOKENS, M_WIDTH), lambda b, i: (b, 0)),
            pl.BlockSpec((M_TOKENS, M_WIDTH), lambda b, i: (b, 0)),
            pl.BlockSpec((tq, M_WIDTH), lambda b, i: (b * nq + i, zcb)),
        ],
        out_specs=pl.BlockSpec((tq, M_WIDTH), lambda b, i: (b * nq + i, 0)),
        compiler_params=_params("arbitrary", "arbitrary"),
        name="mem_attn",
    )(act, mk, mv, act)


def _merge_kernel(ba_ref, bp_ref, bm_ref, g0_ref, g1_ref, g2_ref, wb_ref, o_ref, wbb_ref, *, halves):
    @pl.when(pl.program_id(1) == 0)
    def _():
        wbb_ref[...] = wb_ref[...].astype(BF16)

    branches = ((ba_ref, g0_ref), (bp_ref, g1_ref), (bm_ref, g2_ref))
    rows = o_ref.shape[0] // halves
    projs = [[jnp.dot(b_ref[r * rows:(r + 1) * rows, :], wbb_ref[n], preferred_element_type=F32)
              for n, (b_ref, _) in enumerate(branches)] for r in range(halves)]
    for r in range(halves):
        sl = slice(r * rows, (r + 1) * rows)
        acc = None
        for n, (_, g_ref) in enumerate(branches):
            term = _sigmoid(g_ref[sl, :].astype(F32)) * projs[r][n]
            acc = term if acc is None else acc + term
        o_ref[sl, :] = acc.astype(o_ref.dtype)


def _merge(ba, bp, bm, h, g_col, wb, tm, tn):
    n = ba.shape[0]
    width = ba.shape[1]
    gspec = lambda k: pl.BlockSpec((tm, tn), lambda j, i: (i, (g_col + k * D_MODEL) // tn + j))
    bspec = pl.BlockSpec((tm, width), lambda j, i: (i, 0))
    return pl.pallas_call(
        functools.partial(_merge_kernel, halves=2 if tm % 32 == 0 else 1),
        out_shape=jax.ShapeDtypeStruct((n, D_MODEL), BF16),
        grid=(D_MODEL // tn, n // tm),
        in_specs=[bspec, bspec, bspec, gspec(0), gspec(1), gspec(2),
                  pl.BlockSpec((N_BRANCH, width, tn), lambda j, i: (0, 0, j))],
        out_specs=pl.BlockSpec((tm, tn), lambda j, i: (i, j)),
        scratch_shapes=[pltpu.VMEM((N_BRANCH, width, tn), BF16)],
        compiler_params=_params("arbitrary", "arbitrary"),
        name="merge",
    )(ba, bp, bm, h, h, h, wb)


def _out_kernel(m_ref, w_ref, x_ref, g_ref, b_ref, o_ref, wb_ref, *, halves):
    @pl.when(pl.program_id(0) == 0)
    def _():
        wb_ref[...] = w_ref[...].astype(BF16)

    rows = o_ref.shape[0] // halves
    outs = [jnp.dot(m_ref[r * rows:(r + 1) * rows, :], wb_ref[...], preferred_element_type=F32)
            for r in range(halves)]
    for r in range(halves):
        sl = slice(r * rows, (r + 1) * rows)
        z = DEEPNORM_ALPHA * x_ref[sl, :] + outs[r]
        mu = jnp.mean(z, axis=1, keepdims=True)
        zc = z - mu
        var = jnp.mean(zc * zc, axis=1, keepdims=True)
        o_ref[sl, :] = zc * lax.rsqrt(var + LN_EPS) * g_ref[...] + b_ref[...]


def _out(merged, w_out, x, ln_g, ln_b, tm):
    n = merged.shape[0]
    row = pl.BlockSpec((tm, D_MODEL), lambda i: (i, 0))
    vec = pl.BlockSpec((1, D_MODEL), lambda i: (0, 0))
    wspec = pl.BlockSpec((D_MODEL, D_MODEL), lambda i: (0, 0), pipeline_mode=pl.Buffered(1))
    return pl.pallas_call(
        functools.partial(_out_kernel, halves=2 if tm % 32 == 0 else 1),
        out_shape=jax.ShapeDtypeStruct((n, D_MODEL), F32),
        grid=(n // tm,),
        in_specs=[row, wspec, row, vec, vec],
        out_specs=row,
        scratch_shapes=[pltpu.VMEM((D_MODEL, D_MODEL), BF16)],
        compiler_params=_params("arbitrary"),
        name="out_ln",
    )(merged, w_out, x, ln_g, ln_b)


def _rest_col_block(tn):
    za_blocks = (COL_UP - COL_ZA) // tn
    return lambda j: jnp.where(j < za_blocks, COL_ZA // tn + j, COL_ZP // tn + (j - za_blocks))


def _layer_prompt(x2d, batch, t, w, pool_hist, mk, mv):
    ang = _rotary_angles(jnp.arange(t))
    cos_t, sin_t = jnp.cos(ang).T, jnp.sin(ang).T
    tm, tn, tp = 2048, 1024, 512
    qt, xb = _proj_ft(x2d, w["w_in"], COL_Q, cos_t, sin_t, batch, t, tp, BF16,
                      scale=A_SCALE * math.log2(math.e), emit_xb=True, name="proj_qt")
    kt, kb = _proj_ft(xb, w["w_in"], COL_K, cos_t, sin_t, batch, t, tp, F32, std_dtype=BF16, name="proj_kt")
    v, vt = _proj_t(xb, w["w_in"], COL_V, batch, t, 2 * tp, std_dtype=F32,
                    t_out="blocked", t_dtype=BF16, name="proj_vt")
    (u,) = _proj_t(xb, w["w_in"], COL_UP, batch, t, 2 * tp, std_dtype=F32, name="proj_u")
    h = _proj(xb, w["w_in"], _rest_col_block(tn), H_COLS, BF16, tm, tn, name="proj_rest")
    b_a = _attn_prompt(qt, kb, vt, h, w["gain"], w["lamv"], batch, t)
    b_p = _pool(u, 0, pool_hist, h, H_ZP, w["pool_w"], w["pool_scale"], batch, t, 512, 0)
    b_m = _mem_attn(h, H_QM, H_ZM, mk, mv, batch, t, 512)
    merged = _merge(b_a, b_p, b_m, h, H_GM, w["w_branch"], 512, 1024)
    y = _out(merged, w["w_out"], x2d, w["ln_g"], w["ln_b"], 512)
    return y, kt, v, u


def _layer_sample(x2d, batch, t, past, w, kt_hist, v_hist, pool_hist, mk, mv):
    n = batch * t
    rot = _rotary_tables(past + jnp.arange(t))
    act = _proj(x2d, w["w_in"], lambda j: j, IN_COLS, F32, n, 1024, name="proj_all")
    b_a, k = _attn_sample(act, rot, kt_hist, v_hist, w["gain"], w["lamv"], batch, t, past)
    b_p = _pool(act, COL_UP, pool_hist, act, COL_ZP, w["pool_w"], w["pool_scale"], batch, t, t, past)
    b_m = _mem_attn(act, COL_QM, COL_ZM, mk, mv, batch, t, t)
    merged = _merge(b_a, b_p, b_m, act, COL_GM, w["w_branch"], n, 1024)
    y = _out(merged, w["w_out"], x2d, w["ln_g"], w["ln_b"], n)
    return y, k, act


def kernel(x_prompt, x_sample, cache_attn_k, cache_attn_v, cache_mem_k, cache_mem_v, state_pool, mem_prompt, w_in, w_mem_kv, lambda_q1, lambda_k1, lambda_q2, lambda_k2, subln_gain, pool_w, pool_scale, w_branch, w_out, ln_gain, ln_bias):
    bp, tp, d = x_prompt.shape
    bs, ts, _ = x_sample.shape
    past = cache_attn_k.shape[2]
    hist_rows = P_HIST + 1

    def run_layer(l, xp2d, xs2d):
        w = {
            "w_in": w_in[l],
            "gain": subln_gain[l].reshape(1, A_V_DIM),
            "lamv": jnp.stack([lambda_q1[l], lambda_k1[l], lambda_q2[l], lambda_k2[l]]),
            "pool_w": pool_w[l],
            "pool_scale": pool_scale[l].reshape(1, P_WIDTH),
            "w_branch": w_branch[l],
            "w_out": w_out[l],
            "ln_g": ln_gain[l].reshape(1, d),
            "ln_b": ln_bias[l].reshape(1, d),
        }
        memf = mem_prompt.reshape(bp * M_TOKENS, d)
        mk = _proj(memf, w_mem_kv[l], lambda j: j, M_WIDTH, F32, 512, 1024, name="proj_mk")
        mv = _proj(memf, w_mem_kv[l], lambda j: M_WIDTH // 1024 + j, M_WIDTH, F32, 512, 1024, name="proj_mv")

        zero_hist = jnp.zeros((bp, hist_rows, P_WIDTH), F32)
        y_p, k_p, v_p, u_p = _layer_prompt(xp2d, bp, tp, w, zero_hist, mk, mv)

        hist_s = jnp.pad(state_pool[l], ((0, 0), (hist_rows - P_HIST, 0), (0, 0)))
        kt_hist = jnp.transpose(cache_attn_k[l], (0, 2, 3, 4, 1)).reshape(bs, A_HEADS, 2 * A_QK_DIM, past)
        y_s, k_s, act_s = _layer_sample(
            xs2d, bs, ts, past, w,
            kt_hist, cache_attn_v[l].reshape(bs, past * A_HEADS, A_V_DIM),
            hist_s, cache_mem_k[l].reshape(bs * M_TOKENS, M_WIDTH), cache_mem_v[l].reshape(bs * M_TOKENS, M_WIDTH))
        v_s = act_s[:, COL_V:COL_V + A_WIDTH]
        u_s = act_s[:, COL_UP:COL_UP + P_WIDTH]

        pool_p = jnp.concatenate([zero_hist, u_p.reshape(bp, tp, P_WIDTH)], axis=1)[:, -P_HIST:]
        pool_s = jnp.concatenate([hist_s, u_s.reshape(bs, ts, P_WIDTH)], axis=1)[:, -P_HIST:]
        k_p = jnp.transpose(k_p.reshape(bp, A_HEADS, 2, A_QK_DIM, tp), (0, 4, 1, 2, 3))
        outs = (k_p, v_p.reshape(bp, tp, A_HEADS, A_V_DIM),
                mk.reshape(bp, M_TOKENS, M_HEADS, M_HEAD_DIM), mv.reshape(bp, M_TOKENS, M_HEADS, M_HEAD_DIM),
                pool_p,
                k_s.reshape(bs, ts, A_HEADS, 2, A_QK_DIM), v_s.reshape(bs, ts, A_HEADS, A_V_DIM), pool_s)
        return y_p, y_s, outs

    yp = x_prompt.reshape(bp * tp, d)
    ys = x_sample.reshape(bs * ts, d)
    per_layer = []
    for l in range(w_in.shape[0]):
        yp, ys, outs = run_layer(l, yp, ys)
        per_layer.append(outs)
    stacked = [jnp.stack([o[n] for o in per_layer]) for n in range(8)]
    return (yp.reshape(bp, tp, d), ys.reshape(bs, ts, d), *stacked)
```

```python
import functools
import math

import numpy as np
import jax
import jax.numpy as jnp
from jax import lax
from jax.experimental import pallas as pl
from jax.experimental.pallas import tpu as pltpu

F32 = jnp.float32
BF16 = jnp.bfloat16

V7X_VMEM_LIMIT_BYTES = 56 * 1024 * 1024
LANES = 128

D_MODEL = 2048
CHUNK = 64
A_HEADS = 8
A_QK_DIM = 64
A_V_DIM = 128
A_WIDTH = A_HEADS * A_V_DIM
A_SCALE = A_QK_DIM ** -0.5
ROT_DIM = A_QK_DIM // 4
ROPE_THETA = 500000.0
SUBLN_EPS = 1e-5
P_WINDOWS = (2, 4, 8, 16)
P_GROUP_DIM = 256
P_WIDTH = len(P_WINDOWS) * P_GROUP_DIM
P_HIST = max(P_WINDOWS) - 1
M_TOKENS = 256
M_HEADS = 4
M_HEAD_DIM = 256
M_WIDTH = M_HEADS * M_HEAD_DIM
M_SCALE = M_HEAD_DIM ** -0.5
N_BRANCH = 3
DEPTH = 1
DEEPNORM_ALPHA = (2.0 * DEPTH) ** 0.25
LN_EPS = 1e-5
LAMBDA_INIT = 0.8 - 0.6 * math.exp(-0.3 * 0)
ONES_ROWS = 16

COL_Q, COL_K, COL_V, COL_ZA, COL_UP, COL_ZP, COL_QM, COL_ZM, COL_GM = (
    0, 1024, 2048, 3072, 4096, 5120, 6144, 7168, 8192)
IN_COLS = 14336
H_ZA, H_ZP, H_QM, H_ZM, H_GM = 0, 1024, 2048, 3072, 4096
H_COLS = 10240


def _params(*sem):
    return pltpu.CompilerParams(dimension_semantics=sem, vmem_limit_bytes=V7X_VMEM_LIMIT_BYTES)


def _sigmoid(z):
    return 1.0 / (1.0 + jnp.exp(-z))


def _silu(z):
    return z * _sigmoid(z)


def _proj_kernel(x_ref, w_ref, *rest, rotary, scale):
    acc = jnp.dot(x_ref[...].astype(BF16), w_ref[...].astype(BF16), preferred_element_type=F32)
    if rotary:
        cos_ref, sa_ref, sb_ref, o_ref = rest
        c, sa, sb = cos_ref[...], sa_ref[...], sb_ref[...]
        half = ROT_DIM // 2
        parts = []
        for k in range(acc.shape[1] // LANES):
            blk = acc[:, k * LANES:(k + 1) * LANES]
            lo = pltpu.roll(blk, half, 1)
            hi = pltpu.roll(blk, LANES - half, 1)
            parts.append(blk * c + lo * sa + hi * sb)
        acc = jnp.concatenate(parts, axis=1)
    else:
        (o_ref,) = rest
    if scale != 1.0:
        acc = acc * scale
    o_ref[...] = acc.astype(o_ref.dtype)


def _proj(x, w, col_block, ncols, out_dtype, tm, tn, rot=None, scale=1.0, name="proj"):
    n, kdim = x.shape
    grid = (n // tm, ncols // tn)
    in_specs = [
        pl.BlockSpec((tm, kdim), lambda i, j: (i, 0)),
        pl.BlockSpec((kdim, tn), lambda i, j: (0, col_block(j))),
    ]
    args = [x, w]
    if rot is not None:
        nrep = rot[0].shape[0] // tm
        for t in rot:
            in_specs.append(pl.BlockSpec((tm, LANES), lambda i, j: (i % nrep, 0)))
            args.append(t)
    return pl.pallas_call(
        functools.partial(_proj_kernel, rotary=rot is not None, scale=scale),
        out_shape=jax.ShapeDtypeStruct((n, ncols), out_dtype),
        grid=grid,
        in_specs=in_specs,
        out_specs=pl.BlockSpec((tm, tn), lambda i, j: (i, j)),
        compiler_params=_params("parallel", "arbitrary"),
        name=name,
    )(*args)


def _rotate_sublanes(acc, c, s):
    half = ROT_DIM // 2
    pieces = []
    for g in range(acc.shape[0] // A_QK_DIM):
        b = g * A_QK_DIM
        lo, hi = acc[b:b + half], acc[b + half:b + ROT_DIM]
        pieces += [lo * c - hi * s, hi * c + lo * s, acc[b + ROT_DIM:b + A_QK_DIM]]
    return jnp.concatenate(pieces, axis=0)


def _proj_qk_kernel(x_ref, wq_ref, wk_ref, cos_ref, sin_ref, qt_ref, kt_ref, kb_ref, xb_ref, *, q_scale):
    xb = x_ref[...].astype(BF16)
    xb_ref[...] = xb
    c, s = cos_ref[...], sin_ref[...]
    q = _rotate_sublanes(jnp.dot(xb, wq_ref[...].astype(BF16), preferred_element_type=F32).T, c, s)
    qt_ref[...] = (q * q_scale).astype(qt_ref.dtype)
    k = _rotate_sublanes(jnp.dot(xb, wk_ref[...].astype(BF16), preferred_element_type=F32).T, c, s)
    kt_ref[...] = k
    kb_ref[...] = k.T.astype(kb_ref.dtype)


def _proj_qk(x, w, cos_t, sin_t, batch, t, tm, q_scale):
    n, kdim = x.shape
    width = A_WIDTH
    nt = t // tm
    tab = pl.BlockSpec((ROT_DIM // 2, tm), lambda i: (0, i % nt))
    wspec = lambda col: pl.BlockSpec((kdim, width), lambda i: (0, col // width), pipeline_mode=pl.Buffered(1))
    tspec = pl.BlockSpec((None, width, tm), lambda i: (i // nt, 0, i % nt))
    return pl.pallas_call(
        functools.partial(_proj_qk_kernel, q_scale=q_scale),
        out_shape=[jax.ShapeDtypeStruct((batch, width, t), BF16), jax.ShapeDtypeStruct((batch, width, t), F32),
                   jax.ShapeDtypeStruct((n, width), BF16), jax.ShapeDtypeStruct((n, kdim), BF16)],
        grid=(n // tm,),
        in_specs=[pl.BlockSpec((tm, kdim), lambda i: (i, 0)), wspec(COL_Q), wspec(COL_K), tab, tab],
        out_specs=[tspec, tspec, pl.BlockSpec((tm, width), lambda i: (i, 0)),
                   pl.BlockSpec((tm, kdim), lambda i: (i, 0))],
        compiler_params=_params("arbitrary"),
        name="proj_qk",
    )(x, w, w, cos_t, sin_t)


def _proj_vu_kernel(x_ref, wv_ref, wu_ref, v_ref, vt_ref, u_ref, *, key_tile):
    xb = x_ref[...]
    v = jnp.dot(xb, wv_ref[...].astype(BF16), preferred_element_type=F32)
    v_ref[...] = v
    for c in range(v.shape[0] // key_tile):
        vt_ref[c] = v[c * key_tile:(c + 1) * key_tile, :].T.astype(vt_ref.dtype)
    u_ref[...] = jnp.dot(xb, wu_ref[...].astype(BF16), preferred_element_type=F32)


def _proj_vu(xb, w, batch, t, tm, key_tile):
    n, kdim = xb.shape
    width = A_WIDTH
    nt = t // tm
    wspec = lambda col: pl.BlockSpec((kdim, width), lambda i: (0, col // width), pipeline_mode=pl.Buffered(1))
    row = pl.BlockSpec((tm, width), lambda i: (i, 0))
    return pl.pallas_call(
        functools.partial(_proj_vu_kernel, key_tile=key_tile),
        out_shape=[jax.ShapeDtypeStruct((n, width), F32),
                   jax.ShapeDtypeStruct((batch, t // key_tile, width, key_tile), BF16),
                   jax.ShapeDtypeStruct((n, width), F32)],
        grid=(n // tm,),
        in_specs=[pl.BlockSpec((tm, kdim), lambda i: (i, 0)), wspec(COL_V), wspec(COL_UP)],
        out_specs=[row, pl.BlockSpec((None, tm // key_tile, width, key_tile), lambda i: (i // nt, i % nt, 0, 0)),
                   row],
        compiler_params=_params("arbitrary"),
        name="proj_vu",
    )(xb, w, w)


def _rotary_angles(pos):
    half = ROT_DIM // 2
    inv = ROPE_THETA ** (-(jnp.arange(half, dtype=F32) * 2.0) / ROT_DIM)
    return pos.astype(F32)[:, None] * inv[None, :]


def _rotary_tables(pos):
    half = ROT_DIM // 2
    ang = _rotary_angles(pos)
    cos, sin = jnp.cos(ang), jnp.sin(ang)
    t = pos.shape[0]
    ones = jnp.ones((t, A_QK_DIM - ROT_DIM), F32)
    zeros = jnp.zeros((t, A_QK_DIM - ROT_DIM), F32)
    zh = jnp.zeros((t, half), F32)
    c64 = jnp.concatenate([cos, cos, ones], axis=1)
    sa64 = jnp.concatenate([zh, sin, zeros], axis=1)
    sb64 = jnp.concatenate([-sin, zh, zeros], axis=1)
    tile = lambda a: jnp.concatenate([a, a], axis=1)
    return tile(c64), tile(sa64), tile(sb64)


def _lambda_value(lam_ref):
    lv = lam_ref[...]
    a = jnp.sum(lv[0:1] * lv[1:2], axis=1, keepdims=True)
    b = jnp.sum(lv[2:3] * lv[3:4], axis=1, keepdims=True)
    return jnp.exp(a) - jnp.exp(b) + LAMBDA_INIT


def _split_components(q):
    lane = lax.broadcasted_iota(jnp.int32, q.shape, 1)
    qf = q.astype(F32)
    return jnp.concatenate([jnp.where(lane < A_QK_DIM, qf, 0.0),
                            jnp.where(lane >= A_QK_DIM, qf, 0.0)], axis=0).astype(q.dtype)


def _attn_finish(acc, l, tq, lam, gain, za):
    o = acc[:tq] / l[:tq] - lam * (acc[tq:] / l[tq:])
    o = o * lax.rsqrt(jnp.mean(o * o, axis=1, keepdims=True) + SUBLN_EPS)
    o = o * gain * (1.0 - LAMBDA_INIT)
    return o * _silu(za)


def _attn_prompt_kernel(qt_ref, k_ref, vt_ref, za_ref, gain_ref, lam_ref, o_ref, m_ref, l_ref, acc_ref,
                        s0_ref, s1_ref, *,
                        tq, heads):
    i = pl.program_id(2)
    hs = [slice(n * LANES, (n + 1) * LANES) for n in range(heads)]
    qqts = []
    for sl in hs:
        qt = qt_ref[sl, :].astype(F32)
        row = lax.broadcasted_iota(jnp.int32, qt.shape, 0)
        qqts.append(jnp.concatenate([jnp.where(row < A_QK_DIM, qt, 0.0),
                                     jnp.where(row >= A_QK_DIM, qt, 0.0)], axis=1).astype(BF16))

    m_ref[...] = jnp.full(m_ref.shape, -jnp.inf, F32)
    l_ref[...] = jnp.zeros(l_ref.shape, F32)
    acc_ref[...] = jnp.zeros(acc_ref.shape, F32)

    def score(j, n):
        start = pl.multiple_of(j * tq, tq)
        return jnp.dot(k_ref[pl.ds(start, tq), hs[n]], qqts[n], preferred_element_type=F32)

    ones_rows = jnp.ones((ONES_ROWS, tq), BF16)

    def step(j, cur_ref, nxt_ref, masked):
        for n in range(heads):
            s = cur_ref[n]
            if masked:
                krow = lax.broadcasted_iota(jnp.int32, s.shape, 0)
                qcol = lax.broadcasted_iota(jnp.int32, s.shape, 1) % tq
                s = jnp.where((krow // CHUNK) <= (qcol // CHUNK), s, -jnp.inf)
            m = m_ref[n]
            m_new = jnp.maximum(m, jnp.max(s, axis=0, keepdims=True))
            alpha = jnp.exp2(m - m_new)
            p = jnp.exp2(s - m_new).astype(BF16)
            m_ref[n] = m_new
            lhs = jnp.concatenate([vt_ref[j, hs[n], :], ones_rows], axis=0)
            pv = jnp.dot(lhs, p, preferred_element_type=F32)
            acc_ref[n] = alpha * acc_ref[n] + pv[:A_V_DIM]
            l_ref[n] = alpha * l_ref[n] + pv[A_V_DIM:A_V_DIM + 1]
            if nxt_ref is not None:
                nxt_ref[n] = score(j + 1, n)

    for n in range(heads):
        s0_ref[n] = score(0, n)

    def pair(jj, carry):
        step(2 * jj, s0_ref, s1_ref, False)
        step(2 * jj + 1, s1_ref, s0_ref, False)
        return carry

    lax.fori_loop(0, i // 2, pair, 0)

    @pl.when(i % 2 == 1)
    def _():
        step(i - 1, s0_ref, s1_ref, False)
        step(i, s1_ref, None, True)

    @pl.when(i % 2 == 0)
    def _():
        step(i, s0_ref, None, True)
    lam = _lambda_value(lam_ref)
    for n in range(heads):
        l, acc = l_ref[n], acc_ref[n]
        ot = acc[:, :tq] / l[:, :tq] - lam * (acc[:, tq:] / l[:, tq:])
        o = ot.T
        o = o * lax.rsqrt(jnp.mean(o * o, axis=1, keepdims=True) + SUBLN_EPS)
        o = o * gain_ref[...] * (1.0 - LAMBDA_INIT)
        o_ref[:, hs[n]] = (o * _silu(za_ref[:, hs[n]].astype(F32))).astype(o_ref.dtype)


def _attn_prompt(qt, k, vt, h, gain, lamv, batch, t, tq=256, heads=8):
    nq = t // tq
    n = batch * t
    hw = heads * LANES
    return pl.pallas_call(
        functools.partial(_attn_prompt_kernel, tq=tq, heads=heads),
        out_shape=jax.ShapeDtypeStruct((n, A_WIDTH), BF16),
        grid=(batch, A_HEADS // heads, nq),
        in_specs=[
            pl.BlockSpec((None, hw, tq), lambda b, hh, i: (b, hh, i)),
            pl.BlockSpec((t, hw), lambda b, hh, i: (b, hh)),
            pl.BlockSpec((None, nq, hw, tq), lambda b, hh, i: (b, 0, hh, 0)),
            pl.BlockSpec((tq, hw), lambda b, hh, i: (b * nq + i, H_ZA // hw + hh)),
            pl.BlockSpec((1, LANES), lambda b, hh, i: (0, 0)),
            pl.BlockSpec((4, A_QK_DIM), lambda b, hh, i: (0, 0)),
        ],
        out_specs=pl.BlockSpec((tq, hw), lambda b, hh, i: (b * nq + i, hh)),
        scratch_shapes=[pltpu.VMEM((heads, 1, 2 * tq), F32), pltpu.VMEM((heads, 1, 2 * tq), F32),
                        pltpu.VMEM((heads, A_V_DIM, 2 * tq), F32),
                        pltpu.VMEM((heads, tq, 2 * tq), F32), pltpu.VMEM((heads, tq, 2 * tq), F32)],
        compiler_params=_params("arbitrary", "arbitrary", "arbitrary"),
        name="attn_prompt",
    )(qt, k, vt, h, gain, lamv)


def _chunk_mask(s, ts, q_pos0, k_pos0):
    q_pos = q_pos0 + lax.broadcasted_iota(jnp.int32, s.shape, 0) % ts
    k_pos = k_pos0 + lax.broadcasted_iota(jnp.int32, s.shape, 1)
    return jnp.where((k_pos // CHUNK) <= (q_pos // CHUNK), s, -jnp.inf)


def _rotate_lanes(blk, c, sa, sb):
    half = ROT_DIM // 2
    return blk * c + pltpu.roll(blk, half, 1) * sa + pltpu.roll(blk, LANES - half, 1) * sb


def _attn_sample_kernel(q_ref, k_ref, v_ref, za_ref, cos_ref, sa_ref, sb_ref, kct_ref, vc_ref, gain_ref, lam_ref,
                        o_ref, krot_ref, *, ts, past, mask_cache, mask_new):
    c, sa, sb = cos_ref[...], sa_ref[...], sb_ref[...]
    nt = (((1,), (1,)), ((), ()))
    hs = [slice(n * LANES, (n + 1) * LANES) for n in range(A_HEADS)]
    scores = []
    for n in range(A_HEADS):
        q = (_rotate_lanes(q_ref[:, hs[n]], c, sa, sb) * A_SCALE).astype(BF16)
        k = _rotate_lanes(k_ref[:, hs[n]], c, sa, sb)
        krot_ref[:, hs[n]] = k
        qq = _split_components(q)
        s_c = jnp.dot(qq, kct_ref[n].astype(BF16), preferred_element_type=F32)
        s_n = lax.dot_general(qq, k.astype(BF16), nt, preferred_element_type=F32)
        if mask_cache:
            s_c = _chunk_mask(s_c, ts, past, 0)
        if mask_new:
            s_n = _chunk_mask(s_n, ts, past, past)
        scores.append((s_c, s_n))
    lam = _lambda_value(lam_ref)
    for n, (s_c, s_n) in enumerate(scores):
        m = jnp.maximum(jnp.max(s_c, axis=1, keepdims=True), jnp.max(s_n, axis=1, keepdims=True))
        p_c = jnp.exp(s_c - m)
        p_n = jnp.exp(s_n - m)
        l = jnp.sum(p_c, axis=1, keepdims=True) + jnp.sum(p_n, axis=1, keepdims=True)
        vc = vc_ref[pl.ds(n, past, stride=A_HEADS), :].astype(BF16)
        acc = (jnp.dot(p_c.astype(BF16), vc, preferred_element_type=F32)
               + jnp.dot(p_n.astype(BF16), v_ref[:, hs[n]].astype(BF16), preferred_element_type=F32))
        out = _attn_finish(acc, l, ts, lam, gain_ref[...], za_ref[:, hs[n]].astype(F32))
        o_ref[:, hs[n]] = out.astype(o_ref.dtype)


def _needs_mask(q_pos, k_pos):
    return not bool(((k_pos[None, :] // CHUNK) <= (q_pos[:, None] // CHUNK)).all())


def _attn_sample(hs_act, rot, kct, vc, gain, lamv, batch, ts, past):
    q_pos = past + np.arange(ts)
    seg = lambda col: pl.BlockSpec((ts, A_WIDTH), lambda b: (b, col // A_WIDTH))
    tab = pl.BlockSpec((ts, LANES), lambda b: (0, 0))
    row = pl.BlockSpec((ts, A_WIDTH), lambda b: (b, 0))
    return pl.pallas_call(
        functools.partial(_attn_sample_kernel, ts=ts, past=past,
                          mask_cache=_needs_mask(q_pos, np.arange(past)),
                          mask_new=_needs_mask(q_pos, q_pos)),
        out_shape=[jax.ShapeDtypeStruct((batch * ts, A_WIDTH), BF16),
                   jax.ShapeDtypeStruct((batch * ts, A_WIDTH), F32)],
        grid=(batch,),
        in_specs=[
            seg(COL_Q), seg(COL_K), seg(COL_V), seg(COL_ZA), tab, tab, tab,
            pl.BlockSpec((None, A_HEADS, LANES, past), lambda b: (b, 0, 0, 0)),
            pl.BlockSpec((None, past * A_HEADS, LANES), lambda b: (b, 0, 0)),
            pl.BlockSpec((1, LANES), lambda b: (0, 0)),
            pl.BlockSpec((4, A_QK_DIM), lambda b: (0, 0)),
        ],
        out_specs=[row, row],
        compiler_params=_params("arbitrary"),
        name="attn_sample",
    )(hs_act, hs_act, hs_act, hs_act, *rot, kct, vc, gain, lamv)


def _pool_kernel(u_ref, uprev_ref, hist_ref, zp_ref, pw_ref, ps_ref, o_ref, *, tm, pos0):
    i = pl.program_id(1)
    u = u_ref[...]
    prev = jnp.where(i == 0, hist_ref[...], uprev_ref[...])
    ext = jnp.concatenate([prev, u], axis=0)
    hist_rows = prev.shape[0]
    pos = pos0 + i * tm + lax.broadcasted_iota(jnp.int32, (tm, 1), 0)
    parts = []
    for g, w in enumerate(P_WINDOWS):
        sl = slice(g * P_GROUP_DIM, (g + 1) * P_GROUP_DIM)
        tot = ext[:, sl]
        span = 1
        while span < w:
            tot = tot + pltpu.roll(tot, span, 0)
            span *= 2
        tot = tot[hist_rows:]
        cnt = jnp.minimum(pos + 1, w).astype(F32)
        pooled = tot / cnt - u[:, sl]
        parts.append(jnp.dot(pooled.astype(BF16), pw_ref[g].astype(BF16), preferred_element_type=F32))
    mixed = jnp.concatenate(parts, axis=1) * ps_ref[...]
    o_ref[...] = (mixed * _silu(zp_ref[...].astype(F32))).astype(o_ref.dtype)


def _pool(u, u_col, hist, z, z_col, pool_w, pool_scale, batch, t, tm, pos0):
    nt = t // tm
    hr = hist.shape[1]
    per = tm // hr
    ucb, zcb = u_col // P_WIDTH, z_col // P_WIDTH
    return pl.pallas_call(
        functools.partial(_pool_kernel, tm=tm, pos0=pos0),
        out_shape=jax.ShapeDtypeStruct((batch * t, P_WIDTH), BF16),
        grid=(batch, nt),
        in_specs=[
            pl.BlockSpec((tm, P_WIDTH), lambda b, i: (b * nt + i, ucb)),
            pl.BlockSpec((hr, P_WIDTH), lambda b, i: (jnp.maximum((b * nt + i) * per - 1, 0), ucb)),
            pl.BlockSpec((None, hr, P_WIDTH), lambda b, i: (b, 0, 0)),
            pl.BlockSpec((tm, P_WIDTH), lambda b, i: (b * nt + i, zcb)),
            pl.BlockSpec((len(P_WINDOWS), P_GROUP_DIM, P_GROUP_DIM), lambda b, i: (0, 0, 0)),
            pl.BlockSpec((1, P_WIDTH), lambda b, i: (0, 0)),
        ],
        out_specs=pl.BlockSpec((tm, P_WIDTH), lambda b, i: (b * nt + i, 0)),
        compiler_params=_params("arbitrary", "arbitrary"),
        name="pool",
    )(u, u, hist, z, pool_w, pool_scale)


def _mem_attn_kernel(q_ref, mk_ref, mv_ref, zm_ref, o_ref):
    nt = (((1,), (1,)), ((), ()))
    hs = [slice(n * M_HEAD_DIM, (n + 1) * M_HEAD_DIM) for n in range(M_HEADS)]
    scores = [lax.dot_general(q_ref[:, sl].astype(BF16), mk_ref[:, sl].astype(BF16), nt,
                              preferred_element_type=F32) * M_SCALE for sl in hs]
    for sl, s in zip(hs, scores):
        m = jnp.max(s, axis=1, keepdims=True)
        p = jnp.exp(s - m)
        l = jnp.sum(p, axis=1, keepdims=True)
        o = jnp.dot(p.astype(BF16), mv_ref[:, sl].astype(BF16), preferred_element_type=F32) / l
        o_ref[:, sl] = (o * _silu(zm_ref[:, sl].astype(F32))).astype(o_ref.dtype)


def _mem_attn(act, q_col, z_col, mk, mv, batch, t, tq):
    nq = t // tq
    qcb, zcb = q_col // M_WIDTH, z_col // M_WIDTH
    return pl.pallas_call(
        _mem_attn_kernel,
        out_shape=jax.ShapeDtypeStruct((batch * t, M_WIDTH), BF16),
        grid=(batch, nq),
        in_specs=[
            pl.BlockSpec((tq, M_WIDTH), lambda b, i: (b * nq + i, qcb)),
            pl.BlockSpec((M_TOKENS, M_WIDTH), lambda b, i: (b, 0)),
            pl.BlockSpec((M_TOKENS, M_WIDTH), lambda b, i: (b, 0)),
            pl.BlockSpec((tq, M_WIDTH), lambda b, i: (b * nq + i, zcb)),
        ],
        out_specs=pl.BlockSpec((tq, M_WIDTH), lambda b, i: (b * nq + i, 0)),
        compiler_params=_params("arbitrary", "arbitrary"),
        name="mem_attn",
    )(act, mk, mv, act)


def _merge_kernel(ba_ref, bp_ref, bm_ref, g0_ref, g1_ref, g2_ref, wb_ref, o_ref, wbb_ref, *, halves):
    @pl.when(pl.program_id(1) == 0)
    def _():
        wbb_ref[...] = wb_ref[...].astype(BF16)

    branches = ((ba_ref, g0_ref), (bp_ref, g1_ref), (bm_ref, g2_ref))
    rows = o_ref.shape[0] // halves
    projs = [[jnp.dot(b_ref[r * rows:(r + 1) * rows, :], wbb_ref[n], preferred_element_type=F32)
              for n, (b_ref, _) in enumerate(branches)] for r in range(halves)]
    for r in range(halves):
        sl = slice(r * rows, (r + 1) * rows)
        acc = None
        for n, (_, g_ref) in enumerate(branches):
            term = _sigmoid(g_ref[sl, :].astype(F32)) * projs[r][n]
            acc = term if acc is None else acc + term
        o_ref[sl, :] = acc.astype(o_ref.dtype)


def _merge(ba, bp, bm, h, g_col, wb, tm, tn):
    n = ba.shape[0]
    width = ba.shape[1]
    gspec = lambda k: pl.BlockSpec((tm, tn), lambda j, i: (i, (g_col + k * D_MODEL) // tn + j))
    bspec = pl.BlockSpec((tm, width), lambda j, i: (i, 0))
    return pl.pallas_call(
        functools.partial(_merge_kernel, halves=2 if tm % 32 == 0 else 1),
        out_shape=jax.ShapeDtypeStruct((n, D_MODEL), BF16),
        grid=(D_MODEL // tn, n // tm),
        in_specs=[bspec, bspec, bspec, gspec(0), gspec(1), gspec(2),
                  pl.BlockSpec((N_BRANCH, width, tn), lambda j, i: (0, 0, j))],
        out_specs=pl.BlockSpec((tm, tn), lambda j, i: (i, j)),
        scratch_shapes=[pltpu.VMEM((N_BRANCH, width, tn), BF16)],
        compiler_params=_params("arbitrary", "arbitrary"),
        name="merge",
    )(ba, bp, bm, h, h, h, wb)


def _out_kernel(m_ref, w_ref, x_ref, g_ref, b_ref, o_ref, wb_ref, *, halves):
    @pl.when(pl.program_id(0) == 0)
    def _():
        wb_ref[...] = w_ref[...].astype(BF16)

    rows = o_ref.shape[0] // halves
    outs = [jnp.dot(m_ref[r * rows:(r + 1) * rows, :], wb_ref[...], preferred_element_type=F32)
            for r in range(halves)]
    for r in range(halves):
        sl = slice(r * rows, (r + 1) * rows)
        z = DEEPNORM_ALPHA * x_ref[sl, :] + outs[r]
        mu = jnp.mean(z, axis=1, keepdims=True)
        zc = z - mu
        var = jnp.mean(zc * zc, axis=1, keepdims=True)
        o_ref[sl, :] = zc * lax.rsqrt(var + LN_EPS) * g_ref[...] + b_ref[...]


def _out(merged, w_out, x, ln_g, ln_b, tm):
    n = merged.shape[0]
    row = pl.BlockSpec((tm, D_MODEL), lambda i: (i, 0))
    vec = pl.BlockSpec((1, D_MODEL), lambda i: (0, 0))
    wspec = pl.BlockSpec((D_MODEL, D_MODEL), lambda i: (0, 0), pipeline_mode=pl.Buffered(1))
    return pl.pallas_call(
        functools.partial(_out_kernel, halves=2 if tm % 32 == 0 else 1),
        out_shape=jax.ShapeDtypeStruct((n, D_MODEL), F32),
        grid=(n // tm,),
        in_specs=[row, wspec, row, vec, vec],
        out_specs=row,
        scratch_shapes=[pltpu.VMEM((D_MODEL, D_MODEL), BF16)],
        compiler_params=_params("arbitrary"),
        name="out_ln",
    )(merged, w_out, x, ln_g, ln_b)


def _rest_col_block(tn):
    za_blocks = (COL_UP - COL_ZA) // tn
    return lambda j: jnp.where(j < za_blocks, COL_ZA // tn + j, COL_ZP // tn + (j - za_blocks))


def _layer_prompt(x2d, batch, t, w, pool_hist, mk, mv):
    ang = _rotary_angles(jnp.arange(t))
    cos_t, sin_t = jnp.cos(ang).T, jnp.sin(ang).T
    tm, tn, tp = 2048, 1024, 512
    qt, kt, kb, xb = _proj_qk(x2d, w["w_in"], cos_t, sin_t, batch, t, tp, A_SCALE * math.log2(math.e))
    v, vt, u = _proj_vu(xb, w["w_in"], batch, t, tp, 256)
    h = _proj(xb, w["w_in"], _rest_col_block(tn), H_COLS, BF16, tm, tn, name="proj_rest")
    b_a = _attn_prompt(qt, kb, vt, h, w["gain"], w["lamv"], batch, t)
    b_p = _pool(u, 0, pool_hist, h, H_ZP, w["pool_w"], w["pool_scale"], batch, t, 512, 0)
    b_m = _mem_attn(h, H_QM, H_ZM, mk, mv, batch, t, 512)
    merged = _merge(b_a, b_p, b_m, h, H_GM, w["w_branch"], 512, 1024)
    y = _out(merged, w["w_out"], x2d, w["ln_g"], w["ln_b"], 512)
    return y, kt, v, u


def _layer_sample(x2d, batch, t, past, w, kt_hist, v_hist, pool_hist, mk, mv):
    n = batch * t
    rot = _rotary_tables(past + jnp.arange(t))
    act = _proj(x2d, w["w_in"], lambda j: j, IN_COLS, F32, n, 1024, name="proj_all")
    b_a, k = _attn_sample(act, rot, kt_hist, v_hist, w["gain"], w["lamv"], batch, t, past)
    b_p = _pool(act, COL_UP, pool_hist, act, COL_ZP, w["pool_w"], w["pool_scale"], batch, t, t, past)
    b_m = _mem_attn(act, COL_QM, COL_ZM, mk, mv, batch, t, t)
    merged = _merge(b_a, b_p, b_m, act, COL_GM, w["w_branch"], n, 1024)
    y = _out(merged, w["w_out"], x2d, w["ln_g"], w["ln_b"], n)
    return y, k, act


def kernel(x_prompt, x_sample, cache_attn_k, cache_attn_v, cache_mem_k, cache_mem_v, state_pool, mem_prompt, w_in, w_mem_kv, lambda_q1, lambda_k1, lambda_q2, lambda_k2, subln_gain, pool_w, pool_scale, w_branch, w_out, ln_gain, ln_bias):
    bp, tp, d = x_prompt.shape
    bs, ts, _ = x_sample.shape
    past = cache_attn_k.shape[2]
    hist_rows = P_HIST + 1

    def run_layer(l, xp2d, xs2d):
        w = {
            "w_in": w_in[l],
            "gain": subln_gain[l].reshape(1, A_V_DIM),
            "lamv": jnp.stack([lambda_q1[l], lambda_k1[l], lambda_q2[l], lambda_k2[l]]),
            "pool_w": pool_w[l],
            "pool_scale": pool_scale[l].reshape(1, P_WIDTH),
            "w_branch": w_branch[l],
            "w_out": w_out[l],
            "ln_g": ln_gain[l].reshape(1, d),
            "ln_b": ln_bias[l].reshape(1, d),
        }
        memf = mem_prompt.reshape(bp * M_TOKENS, d)
        mk = _proj(memf, w_mem_kv[l], lambda j: j, M_WIDTH, F32, 512, 1024, name="proj_mk")
        mv = _proj(memf, w_mem_kv[l], lambda j: M_WIDTH // 1024 + j, M_WIDTH, F32, 512, 1024, name="proj_mv")

        zero_hist = jnp.zeros((bp, hist_rows, P_WIDTH), F32)
        y_p, k_p, v_p, u_p = _layer_prompt(xp2d, bp, tp, w, zero_hist, mk, mv)

        hist_s = jnp.pad(state_pool[l], ((0, 0), (hist_rows - P_HIST, 0), (0, 0)))
        kt_hist = jnp.transpose(cache_attn_k[l], (0, 2, 3, 4, 1)).reshape(bs, A_HEADS, 2 * A_QK_DIM, past)
        y_s, k_s, act_s = _layer_sample(
            xs2d, bs, ts, past, w,
            kt_hist, cache_attn_v[l].reshape(bs, past * A_HEADS, A_V_DIM),
            hist_s, cache_mem_k[l].reshape(bs * M_TOKENS, M_WIDTH), cache_mem_v[l].reshape(bs * M_TOKENS, M_WIDTH))
        v_s = act_s[:, COL_V:COL_V + A_WIDTH]
        u_s = act_s[:, COL_UP:COL_UP + P_WIDTH]

        pool_p = jnp.concatenate([zero_hist, u_p.reshape(bp, tp, P_WIDTH)], axis=1)[:, -P_HIST:]
        pool_s = jnp.concatenate([hist_s, u_s.reshape(bs, ts, P_WIDTH)], axis=1)[:, -P_HIST:]
        k_p = jnp.transpose(k_p.reshape(bp, A_HEADS, 2, A_QK_DIM, tp), (0, 4, 1, 2, 3))
        outs = (k_p, v_p.reshape(bp, tp, A_HEADS, A_V_DIM),
                mk.reshape(bp, M_TOKENS, M_HEADS, M_HEAD_DIM), mv.reshape(bp, M_TOKENS, M_HEADS, M_HEAD_DIM),
                pool_p,
                k_s.reshape(bs, ts, A_HEADS, 2, A_QK_DIM), v_s.reshape(bs, ts, A_HEADS, A_V_DIM), pool_s)
        return y_p, y_s, outs

    yp = x_prompt.reshape(bp * tp, d)
    ys = x_sample.reshape(bs * ts, d)
    per_layer = []
    for l in range(w_in.shape[0]):
        yp, ys, outs = run_layer(l, yp, ys)
        per_layer.append(outs)
    stacked = [jnp.stack([o[n] for o in per_layer]) for n in range(8)]
    return (yp.reshape(bp, tp, d), ys.reshape(bs, ts, d), *stacked)
```

```python
import functools
import math

import numpy as np
import jax
import jax.numpy as jnp
from jax import lax
from jax.experimental import pallas as pl
from jax.experimental.pallas import tpu as pltpu

F32 = jnp.float32
BF16 = jnp.bfloat16

V7X_VMEM_LIMIT_BYTES = 56 * 1024 * 1024
LANES = 128

D_MODEL = 2048
CHUNK = 64
A_HEADS = 8
A_QK_DIM = 64
A_V_DIM = 128
A_WIDTH = A_HEADS * A_V_DIM
A_SCALE = A_QK_DIM ** -0.5
ROT_DIM = A_QK_DIM // 4
ROPE_THETA = 500000.0
SUBLN_EPS = 1e-5
P_WINDOWS = (2, 4, 8, 16)
P_GROUP_DIM = 256
P_WIDTH = len(P_WINDOWS) * P_GROUP_DIM
P_HIST = max(P_WINDOWS) - 1
M_TOKENS = 256
M_HEADS = 4
M_HEAD_DIM = 256
M_WIDTH = M_HEADS * M_HEAD_DIM
M_SCALE = M_HEAD_DIM ** -0.5
N_BRANCH = 3
DEPTH = 1
DEEPNORM_ALPHA = (2.0 * DEPTH) ** 0.25
LN_EPS = 1e-5
LAMBDA_INIT = 0.8 - 0.6 * math.exp(-0.3 * 0)
ONES_ROWS = 16

COL_Q, COL_K, COL_V, COL_ZA, COL_UP, COL_ZP, COL_QM, COL_ZM, COL_GM = (
    0, 1024, 2048, 3072, 4096, 5120, 6144, 7168, 8192)
IN_COLS = 14336
H_ZA, H_GM = 0, 1024
H_COLS = 7168


def _params(*sem):
    return pltpu.CompilerParams(dimension_semantics=sem, vmem_limit_bytes=V7X_VMEM_LIMIT_BYTES)


def _sigmoid(z):
    return 1.0 / (1.0 + jnp.exp(-z))


def _silu(z):
    return z * _sigmoid(z)


def _proj_kernel(x_ref, w_ref, *rest, rotary, scale):
    acc = jnp.dot(x_ref[...].astype(BF16), w_ref[...].astype(BF16), preferred_element_type=F32)
    if rotary:
        cos_ref, sa_ref, sb_ref, o_ref = rest
        c, sa, sb = cos_ref[...], sa_ref[...], sb_ref[...]
        half = ROT_DIM // 2
        parts = []
        for k in range(acc.shape[1] // LANES):
            blk = acc[:, k * LANES:(k + 1) * LANES]
            lo = pltpu.roll(blk, half, 1)
            hi = pltpu.roll(blk, LANES - half, 1)
            parts.append(blk * c + lo * sa + hi * sb)
        acc = jnp.concatenate(parts, axis=1)
    else:
        (o_ref,) = rest
    if scale != 1.0:
        acc = acc * scale
    o_ref[...] = acc.astype(o_ref.dtype)


def _proj(x, w, col_block, ncols, out_dtype, tm, tn, rot=None, scale=1.0, name="proj"):
    n, kdim = x.shape
    grid = (n // tm, ncols // tn)
    in_specs = [
        pl.BlockSpec((tm, kdim), lambda i, j: (i, 0)),
        pl.BlockSpec((kdim, tn), lambda i, j: (0, col_block(j))),
    ]
    args = [x, w]
    if rot is not None:
        nrep = rot[0].shape[0] // tm
        for t in rot:
            in_specs.append(pl.BlockSpec((tm, LANES), lambda i, j: (i % nrep, 0)))
            args.append(t)
    return pl.pallas_call(
        functools.partial(_proj_kernel, rotary=rot is not None, scale=scale),
        out_shape=jax.ShapeDtypeStruct((n, ncols), out_dtype),
        grid=grid,
        in_specs=in_specs,
        out_specs=pl.BlockSpec((tm, tn), lambda i, j: (i, j)),
        compiler_params=_params("parallel", "arbitrary"),
        name=name,
    )(*args)


def _rotate_sublanes(acc, c, s):
    half = ROT_DIM // 2
    pieces = []
    for g in range(acc.shape[0] // A_QK_DIM):
        b = g * A_QK_DIM
        lo, hi = acc[b:b + half], acc[b + half:b + ROT_DIM]
        pieces += [lo * c - hi * s, hi * c + lo * s, acc[b + ROT_DIM:b + A_QK_DIM]]
    return jnp.concatenate(pieces, axis=0)


def _proj_qk_kernel(x_ref, wq_ref, wk_ref, cos_ref, sin_ref, qt_ref, kt_ref, kb_ref, xb_ref, *, q_scale):
    xb = x_ref[...].astype(BF16)
    xb_ref[...] = xb
    c, s = cos_ref[...], sin_ref[...]
    q = _rotate_sublanes(jnp.dot(xb, wq_ref[...].astype(BF16), preferred_element_type=F32).T, c, s)
    qt_ref[...] = (q * q_scale).astype(qt_ref.dtype)
    k = _rotate_sublanes(jnp.dot(xb, wk_ref[...].astype(BF16), preferred_element_type=F32).T, c, s)
    kt_ref[...] = k
    kb_ref[...] = k.T.astype(kb_ref.dtype)


def _proj_qk(x, w, cos_t, sin_t, batch, t, tm, q_scale):
    n, kdim = x.shape
    width = A_WIDTH
    nt = t // tm
    tab = pl.BlockSpec((ROT_DIM // 2, tm), lambda i: (0, i % nt))
    wspec = lambda col: pl.BlockSpec((kdim, width), lambda i: (0, col // width), pipeline_mode=pl.Buffered(1))
    tspec = pl.BlockSpec((None, width, tm), lambda i: (i // nt, 0, i % nt))
    return pl.pallas_call(
        functools.partial(_proj_qk_kernel, q_scale=q_scale),
        out_shape=[jax.ShapeDtypeStruct((batch, width, t), BF16), jax.ShapeDtypeStruct((batch, width, t), F32),
                   jax.ShapeDtypeStruct((n, width), BF16), jax.ShapeDtypeStruct((n, kdim), BF16)],
        grid=(n // tm,),
        in_specs=[pl.BlockSpec((tm, kdim), lambda i: (i, 0)), wspec(COL_Q), wspec(COL_K), tab, tab],
        out_specs=[tspec, tspec, pl.BlockSpec((tm, width), lambda i: (i, 0)),
                   pl.BlockSpec((tm, kdim), lambda i: (i, 0))],
        compiler_params=_params("arbitrary"),
        name="proj_qk",
    )(x, w, w, cos_t, sin_t)


def _proj_vup_kernel(x_ref, wv_ref, wu_ref, wz_ref, hist_ref, pw_ref, ps_ref, v_ref, vt_ref, u_ref, bp_ref,
                     prev_ref, *, key_tile, nt):
    i = pl.program_id(0) % nt
    xb = x_ref[...]
    tm = xb.shape[0]
    v = jnp.dot(xb, wv_ref[...].astype(BF16), preferred_element_type=F32)
    v_ref[...] = v
    for c in range(tm // key_tile):
        vt_ref[c] = v[c * key_tile:(c + 1) * key_tile, :].T.astype(vt_ref.dtype)
    u = jnp.dot(xb, wu_ref[...].astype(BF16), preferred_element_type=F32)
    u_ref[...] = u
    zp = jnp.dot(xb, wz_ref[...].astype(BF16), preferred_element_type=F32)

    @pl.when(i == 0)
    def _():
        prev_ref[...] = hist_ref[...]

    pos = i * tm + lax.broadcasted_iota(jnp.int32, (tm, 1), 0)
    bp_ref[...] = _pool_mix(u, prev_ref[...], zp, pw_ref, ps_ref, pos).astype(bp_ref.dtype)
    prev_ref[...] = u[tm - prev_ref.shape[0]:, :]


def _proj_vup(xb, w, hist, pool_w, pool_scale, batch, t, tm, key_tile):
    n, kdim = xb.shape
    width = A_WIDTH
    nt = t // tm
    hr = hist.shape[1]
    wspec = lambda col: pl.BlockSpec((kdim, width), lambda i: (0, col // width), pipeline_mode=pl.Buffered(1))
    row = pl.BlockSpec((tm, width), lambda i: (i, 0))
    return pl.pallas_call(
        functools.partial(_proj_vup_kernel, key_tile=key_tile, nt=nt),
        out_shape=[jax.ShapeDtypeStruct((n, width), F32),
                   jax.ShapeDtypeStruct((batch, t // key_tile, width, key_tile), BF16),
                   jax.ShapeDtypeStruct((n, width), F32),
                   jax.ShapeDtypeStruct((n, width), BF16)],
        grid=(n // tm,),
        in_specs=[pl.BlockSpec((tm, kdim), lambda i: (i, 0)), wspec(COL_V), wspec(COL_UP), wspec(COL_ZP),
                  pl.BlockSpec((None, hr, P_WIDTH), lambda i: (i // nt, 0, 0)),
                  pl.BlockSpec((len(P_WINDOWS), P_GROUP_DIM, P_GROUP_DIM), lambda i: (0, 0, 0)),
                  pl.BlockSpec((1, P_WIDTH), lambda i: (0, 0))],
        out_specs=[row, pl.BlockSpec((None, tm // key_tile, width, key_tile), lambda i: (i // nt, i % nt, 0, 0)),
                   row, row],
        scratch_shapes=[pltpu.VMEM((hr, P_WIDTH), F32)],
        compiler_params=_params("arbitrary"),
        name="proj_vup",
    )(xb, w, w, w, hist, pool_w, pool_scale)


def _rotary_angles(pos):
    half = ROT_DIM // 2
    inv = ROPE_THETA ** (-(jnp.arange(half, dtype=F32) * 2.0) / ROT_DIM)
    return pos.astype(F32)[:, None] * inv[None, :]


def _rotary_tables(pos):
    half = ROT_DIM // 2
    ang = _rotary_angles(pos)
    cos, sin = jnp.cos(ang), jnp.sin(ang)
    t = pos.shape[0]
    ones = jnp.ones((t, A_QK_DIM - ROT_DIM), F32)
    zeros = jnp.zeros((t, A_QK_DIM - ROT_DIM), F32)
    zh = jnp.zeros((t, half), F32)
    c64 = jnp.concatenate([cos, cos, ones], axis=1)
    sa64 = jnp.concatenate([zh, sin, zeros], axis=1)
    sb64 = jnp.concatenate([-sin, zh, zeros], axis=1)
    tile = lambda a: jnp.concatenate([a, a], axis=1)
    return tile(c64), tile(sa64), tile(sb64)


def _lambda_value(lam_ref):
    lv = lam_ref[...]
    a = jnp.sum(lv[0:1] * lv[1:2], axis=1, keepdims=True)
    b = jnp.sum(lv[2:3] * lv[3:4], axis=1, keepdims=True)
    return jnp.exp(a) - jnp.exp(b) + LAMBDA_INIT


def _split_components(q):
    lane = lax.broadcasted_iota(jnp.int32, q.shape, 1)
    qf = q.astype(F32)
    return jnp.concatenate([jnp.where(lane < A_QK_DIM, qf, 0.0),
                            jnp.where(lane >= A_QK_DIM, qf, 0.0)], axis=0).astype(q.dtype)


def _attn_finish(acc, l, tq, lam, gain, za):
    o = acc[:tq] / l[:tq] - lam * (acc[tq:] / l[tq:])
    o = o * lax.rsqrt(jnp.mean(o * o, axis=1, keepdims=True) + SUBLN_EPS)
    o = o * gain * (1.0 - LAMBDA_INIT)
    return o * _silu(za)


def _attn_prompt_kernel(qt_ref, k_ref, vt_ref, za_ref, gain_ref, lam_ref, o_ref, m_ref, l_ref, acc_ref,
                        s0_ref, s1_ref, *,
                        tq, heads):
    i = pl.program_id(2)
    hs = [slice(n * LANES, (n + 1) * LANES) for n in range(heads)]
    qqts = []
    for sl in hs:
        qt = qt_ref[sl, :].astype(F32)
        row = lax.broadcasted_iota(jnp.int32, qt.shape, 0)
        qqts.append(jnp.concatenate([jnp.where(row < A_QK_DIM, qt, 0.0),
                                     jnp.where(row >= A_QK_DIM, qt, 0.0)], axis=1).astype(BF16))

    m_ref[...] = jnp.full(m_ref.shape, -jnp.inf, F32)
    l_ref[...] = jnp.zeros(l_ref.shape, F32)
    acc_ref[...] = jnp.zeros(acc_ref.shape, F32)

    def score(j, n):
        start = pl.multiple_of(j * tq, tq)
        return jnp.dot(k_ref[pl.ds(start, tq), hs[n]], qqts[n], preferred_element_type=F32)

    ones_rows = jnp.ones((ONES_ROWS, tq), BF16)

    def step(j, cur_ref, nxt_ref, masked):
        for n in range(heads):
            s = cur_ref[n]
            if masked:
                krow = lax.broadcasted_iota(jnp.int32, s.shape, 0)
                qcol = lax.broadcasted_iota(jnp.int32, s.shape, 1) % tq
                s = jnp.where((krow // CHUNK) <= (qcol // CHUNK), s, -jnp.inf)
            m = m_ref[n]
            m_new = jnp.maximum(m, jnp.max(s, axis=0, keepdims=True))
            alpha = jnp.exp2(m - m_new)
            p = jnp.exp2(s - m_new).astype(BF16)
            m_ref[n] = m_new
            lhs = jnp.concatenate([vt_ref[j, hs[n], :], ones_rows], axis=0)
            pv = jnp.dot(lhs, p, preferred_element_type=F32)
            acc_ref[n] = alpha * acc_ref[n] + pv[:A_V_DIM]
            l_ref[n] = alpha * l_ref[n] + pv[A_V_DIM:A_V_DIM + 1]
            if nxt_ref is not None:
                nxt_ref[n] = score(j + 1, n)

    for n in range(heads):
        s0_ref[n] = score(0, n)

    def pair(jj, carry):
        step(2 * jj, s0_ref, s1_ref, False)
        step(2 * jj + 1, s1_ref, s0_ref, False)
        return carry

    lax.fori_loop(0, i // 2, pair, 0)

    @pl.when(i % 2 == 1)
    def _():
        step(i - 1, s0_ref, s1_ref, False)
        step(i, s1_ref, None, True)

    @pl.when(i % 2 == 0)
    def _():
        step(i, s0_ref, None, True)
    lam = _lambda_value(lam_ref)
    for n in range(heads):
        l, acc = l_ref[n], acc_ref[n]
        ot = acc[:, :tq] / l[:, :tq] - lam * (acc[:, tq:] / l[:, tq:])
        o = ot.T
        o = o * lax.rsqrt(jnp.mean(o * o, axis=1, keepdims=True) + SUBLN_EPS)
        o = o * gain_ref[...] * (1.0 - LAMBDA_INIT)
        o_ref[:, hs[n]] = (o * _silu(za_ref[:, hs[n]].astype(F32))).astype(o_ref.dtype)


def _attn_prompt(qt, k, vt, h, gain, lamv, batch, t, tq=256, heads=8):
    nq = t // tq
    n = batch * t
    hw = heads * LANES
    return pl.pallas_call(
        functools.partial(_attn_prompt_kernel, tq=tq, heads=heads),
        out_shape=jax.ShapeDtypeStruct((n, A_WIDTH), BF16),
        grid=(batch, A_HEADS // heads, nq),
        in_specs=[
            pl.BlockSpec((None, hw, tq), lambda b, hh, i: (b, hh, i)),
            pl.BlockSpec((t, hw), lambda b, hh, i: (b, hh)),
            pl.BlockSpec((None, nq, hw, tq), lambda b, hh, i: (b, 0, hh, 0)),
            pl.BlockSpec((tq, hw), lambda b, hh, i: (b * nq + i, H_ZA // hw + hh)),
            pl.BlockSpec((1, LANES), lambda b, hh, i: (0, 0)),
            pl.BlockSpec((4, A_QK_DIM), lambda b, hh, i: (0, 0)),
        ],
        out_specs=pl.BlockSpec((tq, hw), lambda b, hh, i: (b * nq + i, hh)),
        scratch_shapes=[pltpu.VMEM((heads, 1, 2 * tq), F32), pltpu.VMEM((heads, 1, 2 * tq), F32),
                        pltpu.VMEM((heads, A_V_DIM, 2 * tq), F32),
                        pltpu.VMEM((heads, tq, 2 * tq), F32), pltpu.VMEM((heads, tq, 2 * tq), F32)],
        compiler_params=_params("arbitrary", "arbitrary", "arbitrary"),
        name="attn_prompt",
    )(qt, k, vt, h, gain, lamv)


def _chunk_mask(s, ts, q_pos0, k_pos0):
    q_pos = q_pos0 + lax.broadcasted_iota(jnp.int32, s.shape, 0) % ts
    k_pos = k_pos0 + lax.broadcasted_iota(jnp.int32, s.shape, 1)
    return jnp.where((k_pos // CHUNK) <= (q_pos // CHUNK), s, -jnp.inf)


def _rotate_lanes(blk, c, sa, sb):
    half = ROT_DIM // 2
    return blk * c + pltpu.roll(blk, half, 1) * sa + pltpu.roll(blk, LANES - half, 1) * sb


def _attn_sample_kernel(q_ref, k_ref, v_ref, za_ref, cos_ref, sa_ref, sb_ref, kct_ref, vc_ref, gain_ref, lam_ref,
                        o_ref, krot_ref, *, ts, past, mask_cache, mask_new):
    c, sa, sb = cos_ref[...], sa_ref[...], sb_ref[...]
    nt = (((1,), (1,)), ((), ()))
    hs = [slice(n * LANES, (n + 1) * LANES) for n in range(A_HEADS)]
    scores = []
    for n in range(A_HEADS):
        q = (_rotate_lanes(q_ref[:, hs[n]], c, sa, sb) * A_SCALE).astype(BF16)
        k = _rotate_lanes(k_ref[:, hs[n]], c, sa, sb)
        krot_ref[:, hs[n]] = k
        qq = _split_components(q)
        s_c = jnp.dot(qq, kct_ref[n].astype(BF16), preferred_element_type=F32)
        s_n = lax.dot_general(qq, k.astype(BF16), nt, preferred_element_type=F32)
        if mask_cache:
            s_c = _chunk_mask(s_c, ts, past, 0)
        if mask_new:
            s_n = _chunk_mask(s_n, ts, past, past)
        scores.append((s_c, s_n))
    lam = _lambda_value(lam_ref)
    for n, (s_c, s_n) in enumerate(scores):
        m = jnp.maximum(jnp.max(s_c, axis=1, keepdims=True), jnp.max(s_n, axis=1, keepdims=True))
        p_c = jnp.exp(s_c - m)
        p_n = jnp.exp(s_n - m)
        l = jnp.sum(p_c, axis=1, keepdims=True) + jnp.sum(p_n, axis=1, keepdims=True)
        vc = vc_ref[pl.ds(n, past, stride=A_HEADS), :].astype(BF16)
        acc = (jnp.dot(p_c.astype(BF16), vc, preferred_element_type=F32)
               + jnp.dot(p_n.astype(BF16), v_ref[:, hs[n]].astype(BF16), preferred_element_type=F32))
        out = _attn_finish(acc, l, ts, lam, gain_ref[...], za_ref[:, hs[n]].astype(F32))
        o_ref[:, hs[n]] = out.astype(o_ref.dtype)


def _needs_mask(q_pos, k_pos):
    return not bool(((k_pos[None, :] // CHUNK) <= (q_pos[:, None] // CHUNK)).all())


def _attn_sample(hs_act, rot, kct, vc, gain, lamv, batch, ts, past):
    q_pos = past + np.arange(ts)
    seg = lambda col: pl.BlockSpec((ts, A_WIDTH), lambda b: (b, col // A_WIDTH))
    tab = pl.BlockSpec((ts, LANES), lambda b: (0, 0))
    row = pl.BlockSpec((ts, A_WIDTH), lambda b: (b, 0))
    return pl.pallas_call(
        functools.partial(_attn_sample_kernel, ts=ts, past=past,
                          mask_cache=_needs_mask(q_pos, np.arange(past)),
                          mask_new=_needs_mask(q_pos, q_pos)),
        out_shape=[jax.ShapeDtypeStruct((batch * ts, A_WIDTH), BF16),
                   jax.ShapeDtypeStruct((batch * ts, A_WIDTH), F32)],
        grid=(batch,),
        in_specs=[
            seg(COL_Q), seg(COL_K), seg(COL_V), seg(COL_ZA), tab, tab, tab,
            pl.BlockSpec((None, A_HEADS, LANES, past), lambda b: (b, 0, 0, 0)),
            pl.BlockSpec((None, past * A_HEADS, LANES), lambda b: (b, 0, 0)),
            pl.BlockSpec((1, LANES), lambda b: (0, 0)),
            pl.BlockSpec((4, A_QK_DIM), lambda b: (0, 0)),
        ],
        out_specs=[row, row],
        compiler_params=_params("arbitrary"),
        name="attn_sample",
    )(hs_act, hs_act, hs_act, hs_act, *rot, kct, vc, gain, lamv)


def _pool_mix(u, prev, zp, pw_ref, ps_ref, pos):
    tm = u.shape[0]
    ext = jnp.concatenate([prev, u], axis=0)
    hist_rows = prev.shape[0]
    parts = []
    for g, w in enumerate(P_WINDOWS):
        sl = slice(g * P_GROUP_DIM, (g + 1) * P_GROUP_DIM)
        tot = ext[:, sl]
        span = 1
        while span < w:
            tot = tot + pltpu.roll(tot, span, 0)
            span *= 2
        tot = tot[hist_rows:]
        cnt = jnp.minimum(pos + 1, w).astype(F32)
        pooled = tot / cnt - u[:, sl]
        parts.append(jnp.dot(pooled.astype(BF16), pw_ref[g].astype(BF16), preferred_element_type=F32))
    mixed = jnp.concatenate(parts, axis=1) * ps_ref[...]
    return mixed * _silu(zp)


def _pool_kernel(u_ref, uprev_ref, hist_ref, zp_ref, pw_ref, ps_ref, o_ref, *, tm, pos0):
    i = pl.program_id(1)
    prev = jnp.where(i == 0, hist_ref[...], uprev_ref[...])
    pos = pos0 + i * tm + lax.broadcasted_iota(jnp.int32, (tm, 1), 0)
    o_ref[...] = _pool_mix(u_ref[...], prev, zp_ref[...].astype(F32), pw_ref, ps_ref, pos).astype(o_ref.dtype)


def _pool(u, u_col, hist, z, z_col, pool_w, pool_scale, batch, t, tm, pos0):
    nt = t // tm
    hr = hist.shape[1]
    per = tm // hr
    ucb, zcb = u_col // P_WIDTH, z_col // P_WIDTH
    return pl.pallas_call(
        functools.partial(_pool_kernel, tm=tm, pos0=pos0),
        out_shape=jax.ShapeDtypeStruct((batch * t, P_WIDTH), BF16),
        grid=(batch, nt),
        in_specs=[
            pl.BlockSpec((tm, P_WIDTH), lambda b, i: (b * nt + i, ucb)),
            pl.BlockSpec((hr, P_WIDTH), lambda b, i: (jnp.maximum((b * nt + i) * per - 1, 0), ucb)),
            pl.BlockSpec((None, hr, P_WIDTH), lambda b, i: (b, 0, 0)),
            pl.BlockSpec((tm, P_WIDTH), lambda b, i: (b * nt + i, zcb)),
            pl.BlockSpec((len(P_WINDOWS), P_GROUP_DIM, P_GROUP_DIM), lambda b, i: (0, 0, 0)),
            pl.BlockSpec((1, P_WIDTH), lambda b, i: (0, 0)),
        ],
        out_specs=pl.BlockSpec((tm, P_WIDTH), lambda b, i: (b * nt + i, 0)),
        compiler_params=_params("arbitrary", "arbitrary"),
        name="pool",
    )(u, u, hist, z, pool_w, pool_scale)


def _mem_attend(q, zm, mk_ref, mv_ref, o_ref):
    nt = (((1,), (1,)), ((), ()))
    hs = [slice(n * M_HEAD_DIM, (n + 1) * M_HEAD_DIM) for n in range(M_HEADS)]
    scores = [lax.dot_general(q[:, sl], mk_ref[:, sl].astype(BF16), nt,
                              preferred_element_type=F32) * M_SCALE for sl in hs]
    for sl, s in zip(hs, scores):
        m = jnp.max(s, axis=1, keepdims=True)
        p = jnp.exp(s - m)
        l = jnp.sum(p, axis=1, keepdims=True)
        o = jnp.dot(p.astype(BF16), mv_ref[:, sl].astype(BF16), preferred_element_type=F32) / l
        o_ref[:, sl] = (o * _silu(zm[:, sl])).astype(o_ref.dtype)


def _mem_attn_kernel(q_ref, mk_ref, mv_ref, zm_ref, o_ref):
    _mem_attend(q_ref[...].astype(BF16), zm_ref[...].astype(F32), mk_ref, mv_ref, o_ref)


def _proj_mem_kernel(x_ref, wq_ref, wz_ref, mk_ref, mv_ref, o_ref):
    xb = x_ref[...]
    q = jnp.dot(xb, wq_ref[...].astype(BF16), preferred_element_type=F32).astype(BF16)
    zm = jnp.dot(xb, wz_ref[...].astype(BF16), preferred_element_type=F32)
    _mem_attend(q, zm, mk_ref, mv_ref, o_ref)


def _proj_mem(xb, w, mk, mv, batch, t, tm):
    n, kdim = xb.shape
    nt = t // tm
    wspec = lambda col: pl.BlockSpec((kdim, M_WIDTH), lambda b, i: (0, col // M_WIDTH),
                                     pipeline_mode=pl.Buffered(1))
    mem = pl.BlockSpec((M_TOKENS, M_WIDTH), lambda b, i: (b, 0))
    return pl.pallas_call(
        _proj_mem_kernel,
        out_shape=jax.ShapeDtypeStruct((n, M_WIDTH), BF16),
        grid=(batch, nt),
        in_specs=[pl.BlockSpec((tm, kdim), lambda b, i: (b * nt + i, 0)), wspec(COL_QM), wspec(COL_ZM), mem, mem],
        out_specs=pl.BlockSpec((tm, M_WIDTH), lambda b, i: (b * nt + i, 0)),
        compiler_params=_params("arbitrary", "arbitrary"),
        name="proj_mem",
    )(xb, w, w, mk, mv)


def _mem_attn(act, q_col, z_col, mk, mv, batch, t, tq):
    nq = t // tq
    qcb, zcb = q_col // M_WIDTH, z_col // M_WIDTH
    return pl.pallas_call(
        _mem_attn_kernel,
        out_shape=jax.ShapeDtypeStruct((batch * t, M_WIDTH), BF16),
        grid=(batch, nq),
        in_specs=[
            pl.BlockSpec((tq, M_WIDTH), lambda b, i: (b * nq + i, qcb)),
            pl.BlockSpec((M_TOKENS, M_WIDTH), lambda b, i: (b, 0)),
            pl.BlockSpec((M_TOKENS, M_WIDTH), lambda b, i: (b, 0)),
            pl.BlockSpec((tq, M_WIDTH), lambda b, i: (b * nq + i, zcb)),
        ],
        out_specs=pl.BlockSpec((tq, M_WIDTH), lambda b, i: (b * nq + i, 0)),
        compiler_params=_params("arbitrary", "arbitrary"),
        name="mem_attn",
    )(act, mk, mv, act)


def _merge_kernel(ba_ref, bp_ref, bm_ref, g0_ref, g1_ref, g2_ref, wb_ref, o_ref, wbb_ref, *, halves):
    @pl.when(pl.program_id(1) == 0)
    def _():
        wbb_ref[...] = wb_ref[...].astype(BF16)

    branches = ((ba_ref, g0_ref), (bp_ref, g1_ref), (bm_ref, g2_ref))
    rows = o_ref.shape[0] // halves
    projs = [[jnp.dot(b_ref[r * rows:(r + 1) * rows, :], wbb_ref[n], preferred_element_type=F32)
              for n, (b_ref, _) in enumerate(branches)] for r in range(halves)]
    for r in range(halves):
        sl = slice(r * rows, (r + 1) * rows)
        acc = None
        for n, (_, g_ref) in enumerate(branches):
            term = _sigmoid(g_ref[sl, :].astype(F32)) * projs[r][n]
            acc = term if acc is None else acc + term
        o_ref[sl, :] = acc.astype(o_ref.dtype)


def _merge(ba, bp, bm, h, g_col, wb, tm, tn):
    n = ba.shape[0]
    width = ba.shape[1]
    gspec = lambda k: pl.BlockSpec((tm, tn), lambda j, i: (i, (g_col + k * D_MODEL) // tn + j))
    bspec = pl.BlockSpec((tm, width), lambda j, i: (i, 0))
    return pl.pallas_call(
        functools.partial(_merge_kernel, halves=2 if tm % 32 == 0 else 1),
        out_shape=jax.ShapeDtypeStruct((n, D_MODEL), BF16),
        grid=(D_MODEL // tn, n // tm),
        in_specs=[bspec, bspec, bspec, gspec(0), gspec(1), gspec(2),
                  pl.BlockSpec((N_BRANCH, width, tn), lambda j, i: (0, 0, j))],
        out_specs=pl.BlockSpec((tm, tn), lambda j, i: (i, j)),
        scratch_shapes=[pltpu.VMEM((N_BRANCH, width, tn), BF16)],
        compiler_params=_params("arbitrary", "arbitrary"),
        name="merge",
    )(ba, bp, bm, h, h, h, wb)


def _out_kernel(m_ref, w_ref, x_ref, g_ref, b_ref, o_ref, wb_ref, *, halves):
    @pl.when(pl.program_id(0) == 0)
    def _():
        wb_ref[...] = w_ref[...].astype(BF16)

    rows = o_ref.shape[0] // halves
    outs = [jnp.dot(m_ref[r * rows:(r + 1) * rows, :], wb_ref[...], preferred_element_type=F32)
            for r in range(halves)]
    for r in range(halves):
        sl = slice(r * rows, (r + 1) * rows)
        z = DEEPNORM_ALPHA * x_ref[sl, :] + outs[r]
        mu = jnp.mean(z, axis=1, keepdims=True)
        zc = z - mu
        var = jnp.mean(zc * zc, axis=1, keepdims=True)
        o_ref[sl, :] = zc * lax.rsqrt(var + LN_EPS) * g_ref[...] + b_ref[...]


def _out(merged, w_out, x, ln_g, ln_b, tm):
    n = merged.shape[0]
    row = pl.BlockSpec((tm, D_MODEL), lambda i: (i, 0))
    vec = pl.BlockSpec((1, D_MODEL), lambda i: (0, 0))
    wspec = pl.BlockSpec((D_MODEL, D_MODEL), lambda i: (0, 0), pipeline_mode=pl.Buffered(1))
    return pl.pallas_call(
        functools.partial(_out_kernel, halves=4 if tm % 64 == 0 else 1),
        out_shape=jax.ShapeDtypeStruct((n, D_MODEL), F32),
        grid=(n // tm,),
        in_specs=[row, wspec, row, vec, vec],
        out_specs=row,
        scratch_shapes=[pltpu.VMEM((D_MODEL, D_MODEL), BF16)],
        compiler_params=_params("arbitrary"),
        name="out_ln",
    )(merged, w_out, x, ln_g, ln_b)


def _rest_col_block(tn):
    za_blocks = (COL_UP - COL_ZA) // tn
    return lambda j: jnp.where(j < za_blocks, COL_ZA // tn + j, COL_GM // tn + (j - za_blocks))


def _layer_prompt(x2d, batch, t, w, pool_hist, mk, mv):
    ang = _rotary_angles(jnp.arange(t))
    cos_t, sin_t = jnp.cos(ang).T, jnp.sin(ang).T
    tm, tn, tp = 2048, 1024, 512
    qt, kt, kb, xb = _proj_qk(x2d, w["w_in"], cos_t, sin_t, batch, t, tp, A_SCALE * math.log2(math.e))
    v, vt, u, b_p = _proj_vup(xb, w["w_in"], pool_hist, w["pool_w"], w["pool_scale"], batch, t, tp, 256)
    b_m = _proj_mem(xb, w["w_in"], mk, mv, batch, t, tp)
    h = _proj(xb, w["w_in"], _rest_col_block(tn), H_COLS, BF16, tm, tn, name="proj_rest")
    b_a = _attn_prompt(qt, kb, vt, h, w["gain"], w["lamv"], batch, t)
    merged = _merge(b_a, b_p, b_m, h, H_GM, w["w_branch"], 512, 1024)
    y = _out(merged, w["w_out"], x2d, w["ln_g"], w["ln_b"], 512)
    return y, kt, v, u


def _layer_sample(x2d, batch, t, past, w, kt_hist, v_hist, pool_hist, mk, mv):
    n = batch * t
    rot = _rotary_tables(past + jnp.arange(t))
    act = _proj(x2d, w["w_in"], lambda j: j, IN_COLS, F32, n, 1024, name="proj_all")
    b_a, k = _attn_sample(act, rot, kt_hist, v_hist, w["gain"], w["lamv"], batch, t, past)
    b_p = _pool(act, COL_UP, pool_hist, act, COL_ZP, w["pool_w"], w["pool_scale"], batch, t, t, past)
    b_m = _mem_attn(act, COL_QM, COL_ZM, mk, mv, batch, t, t)
    merged = _merge(b_a, b_p, b_m, act, COL_GM, w["w_branch"], n, 1024)
    y = _out(merged, w["w_out"], x2d, w["ln_g"], w["ln_b"], n)
    return y, k, act


def kernel(x_prompt, x_sample, cache_attn_k, cache_attn_v, cache_mem_k, cache_mem_v, state_pool, mem_prompt, w_in, w_mem_kv, lambda_q1, lambda_k1, lambda_q2, lambda_k2, subln_gain, pool_w, pool_scale, w_branch, w_out, ln_gain, ln_bias):
    bp, tp, d = x_prompt.shape
    bs, ts, _ = x_sample.shape
    past = cache_attn_k.shape[2]
    hist_rows = P_HIST + 1

    def run_layer(l, xp2d, xs2d):
        w = {
            "w_in": w_in[l],
            "gain": subln_gain[l].reshape(1, A_V_DIM),
            "lamv": jnp.stack([lambda_q1[l], lambda_k1[l], lambda_q2[l], lambda_k2[l]]),
            "pool_w": pool_w[l],
            "pool_scale": pool_scale[l].reshape(1, P_WIDTH),
            "w_branch": w_branch[l],
            "w_out": w_out[l],
            "ln_g": ln_gain[l].reshape(1, d),
            "ln_b": ln_bias[l].reshape(1, d),
        }
        memf = mem_prompt.reshape(bp * M_TOKENS, d)
        mk = _proj(memf, w_mem_kv[l], lambda j: j, M_WIDTH, F32, 512, 1024, name="proj_mk")
        mv = _proj(memf, w_mem_kv[l], lambda j: M_WIDTH // 1024 + j, M_WIDTH, F32, 512, 1024, name="proj_mv")

        zero_hist = jnp.zeros((bp, hist_rows, P_WIDTH), F32)
        y_p, k_p, v_p, u_p = _layer_prompt(xp2d, bp, tp, w, zero_hist, mk, mv)

        hist_s = jnp.pad(state_pool[l], ((0, 0), (hist_rows - P_HIST, 0), (0, 0)))
        kt_hist = jnp.transpose(cache_attn_k[l], (0, 2, 3, 4, 1)).reshape(bs, A_HEADS, 2 * A_QK_DIM, past)
        y_s, k_s, act_s = _layer_sample(
            xs2d, bs, ts, past, w,
            kt_hist, cache_attn_v[l].reshape(bs, past * A_HEADS, A_V_DIM),
            hist_s, cache_mem_k[l].reshape(bs * M_TOKENS, M_WIDTH), cache_mem_v[l].reshape(bs * M_TOKENS, M_WIDTH))
        v_s = act_s[:, COL_V:COL_V + A_WIDTH]
        u_s = act_s[:, COL_UP:COL_UP + P_WIDTH]

        pool_p = jnp.concatenate([zero_hist, u_p.reshape(bp, tp, P_WIDTH)], axis=1)[:, -P_HIST:]
        pool_s = jnp.concatenate([hist_s, u_s.reshape(bs, ts, P_WIDTH)], axis=1)[:, -P_HIST:]
        k_p = jnp.transpose(k_p.reshape(bp, A_HEADS, 2, A_QK_DIM, tp), (0, 4, 1, 2, 3))
        outs = (k_p, v_p.reshape(bp, tp, A_HEADS, A_V_DIM),
                mk.reshape(bp, M_TOKENS, M_HEADS, M_HEAD_DIM), mv.reshape(bp, M_TOKENS, M_HEADS, M_HEAD_DIM),
                pool_p,
                k_s.reshape(bs, ts, A_HEADS, 2, A_QK_DIM), v_s.reshape(bs, ts, A_HEADS, A_V_DIM), pool_s)
        return y_p, y_s, outs

    yp = x_prompt.reshape(bp * tp, d)
    ys = x_sample.reshape(bs * ts, d)
    per_layer = []
    for l in range(w_in.shape[0]):
        yp, ys, outs = run_layer(l, yp, ys)
        per_layer.append(outs)
    stacked = [jnp.stack([o[n] for o in per_layer]) for n in range(8)]
    return (yp.reshape(bp, tp, d), ys.reshape(bs, ts, d), *stacked)
```

```python
import functools
import math

import numpy as np
import jax
import jax.numpy as jnp
from jax import lax
from jax.experimental import pallas as pl
from jax.experimental.pallas import tpu as pltpu

F32 = jnp.float32
BF16 = jnp.bfloat16

V7X_VMEM_LIMIT_BYTES = 56 * 1024 * 1024
LANES = 128

D_MODEL = 2048
CHUNK = 64
A_HEADS = 8
A_QK_DIM = 64
A_V_DIM = 128
A_WIDTH = A_HEADS * A_V_DIM
A_SCALE = A_QK_DIM ** -0.5
ROT_DIM = A_QK_DIM // 4
ROPE_THETA = 500000.0
SUBLN_EPS = 1e-5
P_WINDOWS = (2, 4, 8, 16)
P_GROUP_DIM = 256
P_WIDTH = len(P_WINDOWS) * P_GROUP_DIM
P_HIST = max(P_WINDOWS) - 1
M_TOKENS = 256
M_HEADS = 4
M_HEAD_DIM = 256
M_WIDTH = M_HEADS * M_HEAD_DIM
M_SCALE = M_HEAD_DIM ** -0.5
N_BRANCH = 3
DEPTH = 1
DEEPNORM_ALPHA = (2.0 * DEPTH) ** 0.25
LN_EPS = 1e-5
LAMBDA_INIT = 0.8 - 0.6 * math.exp(-0.3 * 0)
ONES_ROWS = 16

COL_Q, COL_K, COL_V, COL_ZA, COL_UP, COL_ZP, COL_QM, COL_ZM, COL_GM = (
    0, 1024, 2048, 3072, 4096, 5120, 6144, 7168, 8192)
IN_COLS = 14336
H_ZA, H_GM = 0, 1024
H_COLS = 7168


def _params(*sem):
    return pltpu.CompilerParams(dimension_semantics=sem, vmem_limit_bytes=V7X_VMEM_LIMIT_BYTES)


def _sigmoid(z):
    return 1.0 / (1.0 + jnp.exp(-z))


def _silu(z):
    return z * _sigmoid(z)


def _proj_kernel(x_ref, w_ref, *rest, rotary, scale):
    acc = jnp.dot(x_ref[...].astype(BF16), w_ref[...].astype(BF16), preferred_element_type=F32)
    if rotary:
        cos_ref, sa_ref, sb_ref, o_ref = rest
        c, sa, sb = cos_ref[...], sa_ref[...], sb_ref[...]
        half = ROT_DIM // 2
        parts = []
        for k in range(acc.shape[1] // LANES):
            blk = acc[:, k * LANES:(k + 1) * LANES]
            lo = pltpu.roll(blk, half, 1)
            hi = pltpu.roll(blk, LANES - half, 1)
            parts.append(blk * c + lo * sa + hi * sb)
        acc = jnp.concatenate(parts, axis=1)
    else:
        (o_ref,) = rest
    if scale != 1.0:
        acc = acc * scale
    o_ref[...] = acc.astype(o_ref.dtype)


def _proj(x, w, col_block, ncols, out_dtype, tm, tn, rot=None, scale=1.0, name="proj"):
    n, kdim = x.shape
    grid = (n // tm, ncols // tn)
    in_specs = [
        pl.BlockSpec((tm, kdim), lambda i, j: (i, 0)),
        pl.BlockSpec((kdim, tn), lambda i, j: (0, col_block(j))),
    ]
    args = [x, w]
    if rot is not None:
        nrep = rot[0].shape[0] // tm
        for t in rot:
            in_specs.append(pl.BlockSpec((tm, LANES), lambda i, j: (i % nrep, 0)))
            args.append(t)
    return pl.pallas_call(
        functools.partial(_proj_kernel, rotary=rot is not None, scale=scale),
        out_shape=jax.ShapeDtypeStruct((n, ncols), out_dtype),
        grid=grid,
        in_specs=in_specs,
        out_specs=pl.BlockSpec((tm, tn), lambda i, j: (i, j)),
        compiler_params=_params("parallel", "arbitrary"),
        name=name,
    )(*args)


def _proj_memkv_kernel(x_ref, w_ref, mk_ref, mv_ref):
    kv = jnp.dot(x_ref[...].astype(BF16), w_ref[...].astype(BF16), preferred_element_type=F32)
    mk_ref[...] = kv[:, :M_WIDTH]
    mv_ref[...] = kv[:, M_WIDTH:]


def _proj_memkv(mem, w, tm):
    n, kdim = mem.shape
    row = pl.BlockSpec((tm, M_WIDTH), lambda i: (i, 0))
    return pl.pallas_call(
        _proj_memkv_kernel,
        out_shape=[jax.ShapeDtypeStruct((n, M_WIDTH), F32)] * 2,
        grid=(n // tm,),
        in_specs=[pl.BlockSpec((tm, kdim), lambda i: (i, 0)),
                  pl.BlockSpec((kdim, 2 * M_WIDTH), lambda i: (0, 0), pipeline_mode=pl.Buffered(1))],
        out_specs=[row, row],
        compiler_params=_params("arbitrary"),
        name="proj_memkv",
    )(mem, w)


def _rotate_sublanes(acc, c, s):
    half = ROT_DIM // 2
    pieces = []
    for g in range(acc.shape[0] // A_QK_DIM):
        b = g * A_QK_DIM
        lo, hi = acc[b:b + half], acc[b + half:b + ROT_DIM]
        pieces += [lo * c - hi * s, hi * c + lo * s, acc[b + ROT_DIM:b + A_QK_DIM]]
    return jnp.concatenate(pieces, axis=0)


def _proj_qk_kernel(x_ref, wq_ref, wk_ref, cos_ref, sin_ref, qt_ref, kt_ref, kb_ref, xb_ref, *, q_scale):
    xb = x_ref[...].astype(BF16)
    xb_ref[...] = xb
    c, s = cos_ref[...], sin_ref[...]
    q = _rotate_sublanes(jnp.dot(xb, wq_ref[...].astype(BF16), preferred_element_type=F32).T, c, s)
    qt_ref[...] = (q * q_scale).astype(qt_ref.dtype)
    k = _rotate_sublanes(jnp.dot(xb, wk_ref[...].astype(BF16), preferred_element_type=F32).T, c, s)
    kt_ref[...] = k
    kb_ref[...] = k.T.astype(kb_ref.dtype)


def _proj_qk(x, w, cos_t, sin_t, batch, t, tm, q_scale):
    n, kdim = x.shape
    width = A_WIDTH
    nt = t // tm
    tab = pl.BlockSpec((ROT_DIM // 2, tm), lambda i: (0, i % nt))
    wspec = lambda col: pl.BlockSpec((kdim, width), lambda i: (0, col // width), pipeline_mode=pl.Buffered(1))
    tspec = pl.BlockSpec((None, width, tm), lambda i: (i // nt, 0, i % nt))
    return pl.pallas_call(
        functools.partial(_proj_qk_kernel, q_scale=q_scale),
        out_shape=[jax.ShapeDtypeStruct((batch, width, t), BF16), jax.ShapeDtypeStruct((batch, width, t), F32),
                   jax.ShapeDtypeStruct((n, width), BF16), jax.ShapeDtypeStruct((n, kdim), BF16)],
        grid=(n // tm,),
        in_specs=[pl.BlockSpec((tm, kdim), lambda i: (i, 0)), wspec(COL_Q), wspec(COL_K), tab, tab],
        out_specs=[tspec, tspec, pl.BlockSpec((tm, width), lambda i: (i, 0)),
                   pl.BlockSpec((tm, kdim), lambda i: (i, 0))],
        compiler_params=_params("arbitrary"),
        name="proj_qk",
    )(x, w, w, cos_t, sin_t)


def _proj_vup_kernel(x_ref, wv_ref, wu_ref, wz_ref, hist_ref, pw_ref, ps_ref, v_ref, vt_ref, u_ref, bp_ref,
                     prev_ref, *, key_tile, nt):
    i = pl.program_id(0) % nt
    xb = x_ref[...]
    tm = xb.shape[0]
    v = jnp.dot(xb, wv_ref[...].astype(BF16), preferred_element_type=F32)
    v_ref[...] = v
    for c in range(tm // key_tile):
        vt_ref[c] = v[c * key_tile:(c + 1) * key_tile, :].T.astype(vt_ref.dtype)
    u = jnp.dot(xb, wu_ref[...].astype(BF16), preferred_element_type=F32)
    u_ref[...] = u
    zp = jnp.dot(xb, wz_ref[...].astype(BF16), preferred_element_type=F32)

    @pl.when(i == 0)
    def _():
        prev_ref[...] = hist_ref[...]

    pos = i * tm + lax.broadcasted_iota(jnp.int32, (tm, 1), 0)
    bp_ref[...] = _pool_mix(u, prev_ref[...], zp, pw_ref, ps_ref, pos).astype(bp_ref.dtype)
    prev_ref[...] = u[tm - prev_ref.shape[0]:, :]


def _proj_vup(xb, w, hist, pool_w, pool_scale, batch, t, tm, key_tile):
    n, kdim = xb.shape
    width = A_WIDTH
    nt = t // tm
    hr = hist.shape[1]
    wspec = lambda col: pl.BlockSpec((kdim, width), lambda i: (0, col // width), pipeline_mode=pl.Buffered(1))
    row = pl.BlockSpec((tm, width), lambda i: (i, 0))
    return pl.pallas_call(
        functools.partial(_proj_vup_kernel, key_tile=key_tile, nt=nt),
        out_shape=[jax.ShapeDtypeStruct((n, width), F32),
                   jax.ShapeDtypeStruct((batch, t // key_tile, width, key_tile), BF16),
                   jax.ShapeDtypeStruct((n, width), F32),
                   jax.ShapeDtypeStruct((n, width), BF16)],
        grid=(n // tm,),
        in_specs=[pl.BlockSpec((tm, kdim), lambda i: (i, 0)), wspec(COL_V), wspec(COL_UP), wspec(COL_ZP),
                  pl.BlockSpec((None, hr, P_WIDTH), lambda i: (i // nt, 0, 0)),
                  pl.BlockSpec((len(P_WINDOWS), P_GROUP_DIM, P_GROUP_DIM), lambda i: (0, 0, 0)),
                  pl.BlockSpec((1, P_WIDTH), lambda i: (0, 0))],
        out_specs=[row, pl.BlockSpec((None, tm // key_tile, width, key_tile), lambda i: (i // nt, i % nt, 0, 0)),
                   row, row],
        scratch_shapes=[pltpu.VMEM((hr, P_WIDTH), F32)],
        compiler_params=_params("arbitrary"),
        name="proj_vup",
    )(xb, w, w, w, hist, pool_w, pool_scale)


def _rotary_angles(pos):
    half = ROT_DIM // 2
    inv = ROPE_THETA ** (-(jnp.arange(half, dtype=F32) * 2.0) / ROT_DIM)
    return pos.astype(F32)[:, None] * inv[None, :]


def _rotary_tables(pos):
    half = ROT_DIM // 2
    ang = _rotary_angles(pos)
    cos, sin = jnp.cos(ang), jnp.sin(ang)
    t = pos.shape[0]
    ones = jnp.ones((t, A_QK_DIM - ROT_DIM), F32)
    zeros = jnp.zeros((t, A_QK_DIM - ROT_DIM), F32)
    zh = jnp.zeros((t, half), F32)
    c64 = jnp.concatenate([cos, cos, ones], axis=1)
    sa64 = jnp.concatenate([zh, sin, zeros], axis=1)
    sb64 = jnp.concatenate([-sin, zh, zeros], axis=1)
    tile = lambda a: jnp.concatenate([a, a], axis=1)
    return tile(c64), tile(sa64), tile(sb64)


def _lambda_value(lam_ref):
    lv = lam_ref[...]
    a = jnp.sum(lv[0:1] * lv[1:2], axis=1, keepdims=True)
    b = jnp.sum(lv[2:3] * lv[3:4], axis=1, keepdims=True)
    return jnp.exp(a) - jnp.exp(b) + LAMBDA_INIT


def _split_components(q):
    lane = lax.broadcasted_iota(jnp.int32, q.shape, 1)
    qf = q.astype(F32)
    return jnp.concatenate([jnp.where(lane < A_QK_DIM, qf, 0.0),
                            jnp.where(lane >= A_QK_DIM, qf, 0.0)], axis=0).astype(q.dtype)


def _attn_finish(acc, l, tq, lam, gain, za):
    o = acc[:tq] / l[:tq] - lam * (acc[tq:] / l[tq:])
    o = o * lax.rsqrt(jnp.mean(o * o, axis=1, keepdims=True) + SUBLN_EPS)
    o = o * gain * (1.0 - LAMBDA_INIT)
    return o * _silu(za)


def _attn_prompt_kernel(qt_ref, k_ref, vt_ref, za_ref, gain_ref, lam_ref, o_ref, m_ref, l_ref, acc_ref,
                        s0_ref, s1_ref, *,
                        tq, heads):
    i = pl.program_id(2)
    hs = [slice(n * LANES, (n + 1) * LANES) for n in range(heads)]
    qqts = []
    for sl in hs:
        qt = qt_ref[sl, :].astype(F32)
        row = lax.broadcasted_iota(jnp.int32, qt.shape, 0)
        qqts.append(jnp.concatenate([jnp.where(row < A_QK_DIM, qt, 0.0),
                                     jnp.where(row >= A_QK_DIM, qt, 0.0)], axis=1).astype(BF16))

    m_ref[...] = jnp.full(m_ref.shape, -jnp.inf, F32)
    l_ref[...] = jnp.zeros(l_ref.shape, F32)
    acc_ref[...] = jnp.zeros(acc_ref.shape, F32)

    def score(j, n):
        start = pl.multiple_of(j * tq, tq)
        return jnp.dot(k_ref[pl.ds(start, tq), hs[n]], qqts[n], preferred_element_type=F32)

    ones_rows = jnp.ones((ONES_ROWS, tq), BF16)

    def step(j, cur_ref, nxt_ref, masked):
        for n in range(heads):
            s = cur_ref[n]
            if masked:
                krow = lax.broadcasted_iota(jnp.int32, s.shape, 0)
                qcol = lax.broadcasted_iota(jnp.int32, s.shape, 1) % tq
                s = jnp.where((krow // CHUNK) <= (qcol // CHUNK), s, -jnp.inf)
            m = m_ref[n]
            m_new = jnp.maximum(m, jnp.max(s, axis=0, keepdims=True))
            alpha = jnp.exp2(m - m_new)
            p = jnp.exp2(s - m_new).astype(BF16)
            m_ref[n] = m_new
            lhs = jnp.concatenate([vt_ref[j, hs[n], :], ones_rows], axis=0)
            pv = jnp.dot(lhs, p, preferred_element_type=F32)
            acc_ref[n] = alpha * acc_ref[n] + pv[:A_V_DIM]
            l_ref[n] = alpha * l_ref[n] + pv[A_V_DIM:A_V_DIM + 1]
            if nxt_ref is not None:
                nxt_ref[n] = score(j + 1, n)

    for n in range(heads):
        s0_ref[n] = score(0, n)

    def pair(jj, carry):
        step(2 * jj, s0_ref, s1_ref, False)
        step(2 * jj + 1, s1_ref, s0_ref, False)
        return carry

    lax.fori_loop(0, i // 2, pair, 0)

    @pl.when(i % 2 == 1)
    def _():
        step(i - 1, s0_ref, s1_ref, False)
        step(i, s1_ref, None, True)

    @pl.when(i % 2 == 0)
    def _():
        step(i, s0_ref, None, True)
    lam = _lambda_value(lam_ref)
    gain = gain_ref[...] * (1.0 - LAMBDA_INIT)
    for n in range(heads):
        inv, acc = 1.0 / l_ref[n], acc_ref[n]
        ot = acc[:, :tq] * inv[:, :tq] - lam * (acc[:, tq:] * inv[:, tq:])
        o = ot.T
        o = o * lax.rsqrt(jnp.mean(o * o, axis=1, keepdims=True) + SUBLN_EPS) * gain
        o_ref[:, hs[n]] = (o * _silu(za_ref[:, hs[n]].astype(F32))).astype(o_ref.dtype)


def _attn_prompt(qt, k, vt, h, gain, lamv, batch, t, tq=256, heads=8):
    nq = t // tq
    n = batch * t
    hw = heads * LANES
    return pl.pallas_call(
        functools.partial(_attn_prompt_kernel, tq=tq, heads=heads),
        out_shape=jax.ShapeDtypeStruct((n, A_WIDTH), BF16),
        grid=(batch, A_HEADS // heads, nq),
        in_specs=[
            pl.BlockSpec((None, hw, tq), lambda b, hh, i: (b, hh, i)),
            pl.BlockSpec((t, hw), lambda b, hh, i: (b, hh)),
            pl.BlockSpec((None, nq, hw, tq), lambda b, hh, i: (b, 0, hh, 0)),
            pl.BlockSpec((tq, hw), lambda b, hh, i: (b * nq + i, H_ZA // hw + hh)),
            pl.BlockSpec((1, LANES), lambda b, hh, i: (0, 0)),
            pl.BlockSpec((4, A_QK_DIM), lambda b, hh, i: (0, 0)),
        ],
        out_specs=pl.BlockSpec((tq, hw), lambda b, hh, i: (b * nq + i, hh)),
        scratch_shapes=[pltpu.VMEM((heads, 1, 2 * tq), F32), pltpu.VMEM((heads, 1, 2 * tq), F32),
                        pltpu.VMEM((heads, A_V_DIM, 2 * tq), F32),
                        pltpu.VMEM((heads, tq, 2 * tq), F32), pltpu.VMEM((heads, tq, 2 * tq), F32)],
        compiler_params=_params("arbitrary", "arbitrary", "arbitrary"),
        name="attn_prompt",
    )(qt, k, vt, h, gain, lamv)


def _chunk_mask(s, ts, q_pos0, k_pos0):
    q_pos = q_pos0 + lax.broadcasted_iota(jnp.int32, s.shape, 0) % ts
    k_pos = k_pos0 + lax.broadcasted_iota(jnp.int32, s.shape, 1)
    return jnp.where((k_pos // CHUNK) <= (q_pos // CHUNK), s, -jnp.inf)


def _rotate_lanes(blk, c, sa, sb):
    half = ROT_DIM // 2
    return blk * c + pltpu.roll(blk, half, 1) * sa + pltpu.roll(blk, LANES - half, 1) * sb


def _mix_sample_kernel(q_ref, k_ref, v_ref, za_ref, u_ref, zp_ref, qm_ref, zm_ref, cos_ref, sa_ref, sb_ref,
                       kct_ref, vc_ref, hist_ref, mk_ref, mv_ref, gain_ref, lam_ref, pw_ref, ps_ref,
                       o_ref, krot_ref, bp_ref, bm_ref, *, ts, past, mask_cache, mask_new):
    pos = past + lax.broadcasted_iota(jnp.int32, (ts, 1), 0)
    bp_ref[...] = _pool_mix(u_ref[...], hist_ref[...], zp_ref[...], pw_ref, ps_ref, pos).astype(bp_ref.dtype)
    _mem_attend(qm_ref[...].astype(BF16), zm_ref[...], mk_ref, mv_ref, bm_ref)

    c, sa, sb = cos_ref[...], sa_ref[...], sb_ref[...]
    nt = (((1,), (1,)), ((), ()))
    hs = [slice(n * LANES, (n + 1) * LANES) for n in range(A_HEADS)]
    scores = []
    for n in range(A_HEADS):
        q = (_rotate_lanes(q_ref[:, hs[n]], c, sa, sb) * A_SCALE).astype(BF16)
        k = _rotate_lanes(k_ref[:, hs[n]], c, sa, sb)
        krot_ref[:, hs[n]] = k
        qq = _split_components(q)
        s_c = jnp.dot(qq, kct_ref[n].astype(BF16), preferred_element_type=F32)
        s_n = lax.dot_general(qq, k.astype(BF16), nt, preferred_element_type=F32)
        if mask_cache:
            s_c = _chunk_mask(s_c, ts, past, 0)
        if mask_new:
            s_n = _chunk_mask(s_n, ts, past, past)
        scores.append((s_c, s_n))
    lam = _lambda_value(lam_ref)
    for n, (s_c, s_n) in enumerate(scores):
        m = jnp.maximum(jnp.max(s_c, axis=1, keepdims=True), jnp.max(s_n, axis=1, keepdims=True))
        p_c = jnp.exp(s_c - m)
        p_n = jnp.exp(s_n - m)
        l = jnp.sum(p_c, axis=1, keepdims=True) + jnp.sum(p_n, axis=1, keepdims=True)
        vc = vc_ref[pl.ds(n, past, stride=A_HEADS), :].astype(BF16)
        acc = (jnp.dot(p_c.astype(BF16), vc, preferred_element_type=F32)
               + jnp.dot(p_n.astype(BF16), v_ref[:, hs[n]].astype(BF16), preferred_element_type=F32))
        out = _attn_finish(acc, l, ts, lam, gain_ref[...], za_ref[:, hs[n]].astype(F32))
        o_ref[:, hs[n]] = out.astype(o_ref.dtype)


def _needs_mask(q_pos, k_pos):
    return not bool(((k_pos[None, :] // CHUNK) <= (q_pos[:, None] // CHUNK)).all())


def _mix_sample(act, rot, kct, vc, hist, mk, mv, gain, lamv, pool_w, pool_scale, batch, ts, past):
    q_pos = past + np.arange(ts)
    seg = lambda col: pl.BlockSpec((ts, A_WIDTH), lambda b: (b, col // A_WIDTH))
    tab = pl.BlockSpec((ts, LANES), lambda b: (0, 0))
    row = pl.BlockSpec((ts, A_WIDTH), lambda b: (b, 0))
    mem = pl.BlockSpec((M_TOKENS, M_WIDTH), lambda b: (b, 0))
    bshape = jax.ShapeDtypeStruct((batch * ts, A_WIDTH), BF16)
    return pl.pallas_call(
        functools.partial(_mix_sample_kernel, ts=ts, past=past,
                          mask_cache=_needs_mask(q_pos, np.arange(past)),
                          mask_new=_needs_mask(q_pos, q_pos)),
        out_shape=[bshape, jax.ShapeDtypeStruct((batch * ts, A_WIDTH), F32), bshape, bshape],
        grid=(batch,),
        in_specs=[
            seg(COL_Q), seg(COL_K), seg(COL_V), seg(COL_ZA), seg(COL_UP), seg(COL_ZP), seg(COL_QM), seg(COL_ZM),
            tab, tab, tab,
            pl.BlockSpec((None, A_HEADS, LANES, past), lambda b: (b, 0, 0, 0)),
            pl.BlockSpec((None, past * A_HEADS, LANES), lambda b: (b, 0, 0)),
            pl.BlockSpec((None, hist.shape[1], P_WIDTH), lambda b: (b, 0, 0)),
            mem, mem,
            pl.BlockSpec((1, LANES), lambda b: (0, 0)),
            pl.BlockSpec((4, A_QK_DIM), lambda b: (0, 0)),
            pl.BlockSpec((len(P_WINDOWS), P_GROUP_DIM, P_GROUP_DIM), lambda b: (0, 0, 0)),
            pl.BlockSpec((1, P_WIDTH), lambda b: (0, 0)),
        ],
        out_specs=[row, row, row, row],
        compiler_params=_params("arbitrary"),
        name="mix_sample",
    )(*([act] * 8), *rot, kct, vc, hist, mk, mv, gain, lamv, pool_w, pool_scale)


def _pool_mix(u, prev, zp, pw_ref, ps_ref, pos):
    tm = u.shape[0]
    ext = jnp.concatenate([prev, u], axis=0)
    hist_rows = prev.shape[0]
    parts = []
    for g, w in enumerate(P_WINDOWS):
        sl = slice(g * P_GROUP_DIM, (g + 1) * P_GROUP_DIM)
        tot = ext[:, sl]
        span = 1
        while span < w:
            tot = tot + pltpu.roll(tot, span, 0)
            span *= 2
        tot = tot[hist_rows:]
        cnt = jnp.minimum(pos + 1, w).astype(F32)
        pooled = tot / cnt - u[:, sl]
        parts.append(jnp.dot(pooled.astype(BF16), pw_ref[g].astype(BF16), preferred_element_type=F32))
    mixed = jnp.concatenate(parts, axis=1) * ps_ref[...]
    return mixed * _silu(zp)


def _mem_attend(q, zm, mk_ref, mv_ref, o_ref):
    nt = (((1,), (1,)), ((), ()))
    hs = [slice(n * M_HEAD_DIM, (n + 1) * M_HEAD_DIM) for n in range(M_HEADS)]
    scores = [lax.dot_general(q[:, sl], mk_ref[:, sl].astype(BF16), nt,
                              preferred_element_type=F32) * M_SCALE for sl in hs]
    for sl, s in zip(hs, scores):
        m = jnp.max(s, axis=1, keepdims=True)
        p = jnp.exp(s - m)
        l = jnp.sum(p, axis=1, keepdims=True)
        o = jnp.dot(p.astype(BF16), mv_ref[:, sl].astype(BF16), preferred_element_type=F32) / l
        o_ref[:, sl] = (o * _silu(zm[:, sl])).astype(o_ref.dtype)


def _proj_mem_kernel(x_ref, wq_ref, wz_ref, mk_ref, mv_ref, o_ref):
    xb = x_ref[...]
    q = jnp.dot(xb, wq_ref[...].astype(BF16), preferred_element_type=F32).astype(BF16)
    zm = jnp.dot(xb, wz_ref[...].astype(BF16), preferred_element_type=F32)
    _mem_attend(q, zm, mk_ref, mv_ref, o_ref)


def _proj_mem(xb, w, mk, mv, batch, t, tm):
    n, kdim = xb.shape
    nt = t // tm
    wspec = lambda col: pl.BlockSpec((kdim, M_WIDTH), lambda b, i: (0, col // M_WIDTH),
                                     pipeline_mode=pl.Buffered(1))
    mem = pl.BlockSpec((M_TOKENS, M_WIDTH), lambda b, i: (b, 0))
    return pl.pallas_call(
        _proj_mem_kernel,
        out_shape=jax.ShapeDtypeStruct((n, M_WIDTH), BF16),
        grid=(batch, nt),
        in_specs=[pl.BlockSpec((tm, kdim), lambda b, i: (b * nt + i, 0)), wspec(COL_QM), wspec(COL_ZM), mem, mem],
        out_specs=pl.BlockSpec((tm, M_WIDTH), lambda b, i: (b * nt + i, 0)),
        compiler_params=_params("arbitrary", "arbitrary"),
        name="proj_mem",
    )(xb, w, w, mk, mv)


def _merge_kernel(ba_ref, bp_ref, bm_ref, g0_ref, g1_ref, g2_ref, wb_ref, o_ref, wbb_ref, *, halves):
    @pl.when(pl.program_id(1) == 0)
    def _():
        wbb_ref[...] = wb_ref[...].astype(BF16)

    branches = ((ba_ref, g0_ref), (bp_ref, g1_ref), (bm_ref, g2_ref))
    rows = o_ref.shape[0] // halves
    projs = [[jnp.dot(b_ref[r * rows:(r + 1) * rows, :], wbb_ref[n], preferred_element_type=F32)
              for n, (b_ref, _) in enumerate(branches)] for r in range(halves)]
    for r in range(halves):
        sl = slice(r * rows, (r + 1) * rows)
        acc = None
        for n, (_, g_ref) in enumerate(branches):
            term = _sigmoid(g_ref[sl, :].astype(F32)) * projs[r][n]
            acc = term if acc is None else acc + term
        o_ref[sl, :] = acc.astype(o_ref.dtype)


def _merge(ba, bp, bm, h, g_col, wb, tm, tn):
    n = ba.shape[0]
    width = ba.shape[1]
    gspec = lambda k: pl.BlockSpec((tm, tn), lambda j, i: (i, (g_col + k * D_MODEL) // tn + j))
    bspec = pl.BlockSpec((tm, width), lambda j, i: (i, 0))
    return pl.pallas_call(
        functools.partial(_merge_kernel, halves=2 if tm % 32 == 0 else 1),
        out_shape=jax.ShapeDtypeStruct((n, D_MODEL), BF16),
        grid=(D_MODEL // tn, n // tm),
        in_specs=[bspec, bspec, bspec, gspec(0), gspec(1), gspec(2),
                  pl.BlockSpec((N_BRANCH, width, tn), lambda j, i: (0, 0, j))],
        out_specs=pl.BlockSpec((tm, tn), lambda j, i: (i, j)),
        scratch_shapes=[pltpu.VMEM((N_BRANCH, width, tn), BF16)],
        compiler_params=_params("arbitrary", "arbitrary"),
        name="merge",
    )(ba, bp, bm, h, h, h, wb)


def _out_kernel(m_ref, w_ref, x_ref, g_ref, b_ref, o_ref, wb_ref, *, halves):
    @pl.when(pl.program_id(0) == 0)
    def _():
        wb_ref[...] = w_ref[...].astype(BF16)

    rows = o_ref.shape[0] // halves
    outs = [jnp.dot(m_ref[r * rows:(r + 1) * rows, :], wb_ref[...], preferred_element_type=F32)
            for r in range(halves)]
    for r in range(halves):
        sl = slice(r * rows, (r + 1) * rows)
        z = DEEPNORM_ALPHA * x_ref[sl, :] + outs[r]
        mu = jnp.mean(z, axis=1, keepdims=True)
        zc = z - mu
        var = jnp.mean(zc * zc, axis=1, keepdims=True)
        o_ref[sl, :] = zc * lax.rsqrt(var + LN_EPS) * g_ref[...] + b_ref[...]


def _out(merged, w_out, x, ln_g, ln_b, tm):
    n = merged.shape[0]
    row = pl.BlockSpec((tm, D_MODEL), lambda i: (i, 0))
    vec = pl.BlockSpec((1, D_MODEL), lambda i: (0, 0))
    wspec = pl.BlockSpec((D_MODEL, D_MODEL), lambda i: (0, 0), pipeline_mode=pl.Buffered(1))
    return pl.pallas_call(
        functools.partial(_out_kernel, halves=4 if tm % 64 == 0 else 1),
        out_shape=jax.ShapeDtypeStruct((n, D_MODEL), F32),
        grid=(n // tm,),
        in_specs=[row, wspec, row, vec, vec],
        out_specs=row,
        scratch_shapes=[pltpu.VMEM((D_MODEL, D_MODEL), BF16)],
        compiler_params=_params("arbitrary"),
        name="out_ln",
    )(merged, w_out, x, ln_g, ln_b)


def _rest_col_block(tn):
    za_blocks = (COL_UP - COL_ZA) // tn
    return lambda j: jnp.where(j < za_blocks, COL_ZA // tn + j, COL_GM // tn + (j - za_blocks))


def _layer_prompt(x2d, batch, t, w, pool_hist, mk, mv):
    ang = _rotary_angles(jnp.arange(t))
    cos_t, sin_t = jnp.cos(ang).T, jnp.sin(ang).T
    tm, tn, tp = 2048, 1024, 512
    qt, kt, kb, xb = _proj_qk(x2d, w["w_in"], cos_t, sin_t, batch, t, tp, A_SCALE * math.log2(math.e))
    v, vt, u, b_p = _proj_vup(xb, w["w_in"], pool_hist, w["pool_w"], w["pool_scale"], batch, t, tp, 256)
    b_m = _proj_mem(xb, w["w_in"], mk, mv, batch, t, tp)
    h = _proj(xb, w["w_in"], _rest_col_block(tn), H_COLS, BF16, tm, tn, name="proj_rest")
    b_a = _attn_prompt(qt, kb, vt, h, w["gain"], w["lamv"], batch, t)
    merged = _merge(b_a, b_p, b_m, h, H_GM, w["w_branch"], 512, 1024)
    y = _out(merged, w["w_out"], x2d, w["ln_g"], w["ln_b"], 512)
    return y, kt, v, u


def _layer_sample(x2d, batch, t, past, w, kt_hist, v_hist, pool_hist, mk, mv):
    n = batch * t
    rot = _rotary_tables(past + jnp.arange(t))
    act = _proj(x2d, w["w_in"], lambda j: j, IN_COLS, F32, n, 1024, name="proj_all")
    b_a, k, b_p, b_m = _mix_sample(act, rot, kt_hist, v_hist, pool_hist, mk, mv, w["gain"], w["lamv"],
                                   w["pool_w"], w["pool_scale"], batch, t, past)
    merged = _merge(b_a, b_p, b_m, act, COL_GM, w["w_branch"], n, 1024)
    y = _out(merged, w["w_out"], x2d, w["ln_g"], w["ln_b"], n)
    return y, k, act


def kernel(x_prompt, x_sample, cache_attn_k, cache_attn_v, cache_mem_k, cache_mem_v, state_pool, mem_prompt, w_in, w_mem_kv, lambda_q1, lambda_k1, lambda_q2, lambda_k2, subln_gain, pool_w, pool_scale, w_branch, w_out, ln_gain, ln_bias):
    bp, tp, d = x_prompt.shape
    bs, ts, _ = x_sample.shape
    past = cache_attn_k.shape[2]
    hist_rows = P_HIST + 1

    def run_layer(l, xp2d, xs2d):
        w = {
            "w_in": w_in[l],
            "gain": subln_gain[l].reshape(1, A_V_DIM),
            "lamv": jnp.stack([lambda_q1[l], lambda_k1[l], lambda_q2[l], lambda_k2[l]]),
            "pool_w": pool_w[l],
            "pool_scale": pool_scale[l].reshape(1, P_WIDTH),
            "w_branch": w_branch[l],
            "w_out": w_out[l],
            "ln_g": ln_gain[l].reshape(1, d),
            "ln_b": ln_bias[l].reshape(1, d),
        }
        memf = mem_prompt.reshape(bp * M_TOKENS, d)
        mk, mv = _proj_memkv(memf, w_mem_kv[l], 512)

        zero_hist = jnp.zeros((bp, hist_rows, P_WIDTH), F32)
        y_p, k_p, v_p, u_p = _layer_prompt(xp2d, bp, tp, w, zero_hist, mk, mv)

        hist_s = jnp.pad(state_pool[l], ((0, 0), (hist_rows - P_HIST, 0), (0, 0)))
        kt_hist = jnp.transpose(cache_attn_k[l], (0, 2, 3, 4, 1)).reshape(bs, A_HEADS, 2 * A_QK_DIM, past)
        y_s, k_s, act_s = _layer_sample(
            xs2d, bs, ts, past, w,
            kt_hist, cache_attn_v[l].reshape(bs, past * A_HEADS, A_V_DIM),
            hist_s, cache_mem_k[l].reshape(bs * M_TOKENS, M_WIDTH), cache_mem_v[l].reshape(bs * M_TOKENS, M_WIDTH))
        v_s = act_s[:, COL_V:COL_V + A_WIDTH]
        u_s = act_s[:, COL_UP:COL_UP + P_WIDTH]

        pool_p = jnp.concatenate([zero_hist, u_p.reshape(bp, tp, P_WIDTH)], axis=1)[:, -P_HIST:]
        pool_s = jnp.concatenate([hist_s, u_s.reshape(bs, ts, P_WIDTH)], axis=1)[:, -P_HIST:]
        k_p = jnp.transpose(k_p.reshape(bp, A_HEADS, 2, A_QK_DIM, tp), (0, 4, 1, 2, 3))
        outs = (k_p, v_p.reshape(bp, tp, A_HEADS, A_V_DIM),
                mk.reshape(bp, M_TOKENS, M_HEADS, M_HEAD_DIM), mv.reshape(bp, M_TOKENS, M_HEADS, M_HEAD_DIM),
                pool_p,
                k_s.reshape(bs, ts, A_HEADS, 2, A_QK_DIM), v_s.reshape(bs, ts, A_HEADS, A_V_DIM), pool_s)
        return y_p, y_s, outs

    yp = x_prompt.reshape(bp * tp, d)
    ys = x_sample.reshape(bs * ts, d)
    per_layer = []
    for l in range(w_in.shape[0]):
        yp, ys, outs = run_layer(l, yp, ys)
        per_layer.append(outs)
    stacked = [jnp.stack([o[n] for o in per_layer]) for n in range(8)]
    return (yp.reshape(bp, tp, d), ys.reshape(bs, ts, d), *stacked)
```

```python
import functools
import math

import numpy as np
import jax
import jax.numpy as jnp
from jax import lax
from jax.experimental import pallas as pl
from jax.experimental.pallas import tpu as pltpu

F32 = jnp.float32
BF16 = jnp.bfloat16

V7X_VMEM_LIMIT_BYTES = 56 * 1024 * 1024
LANES = 128

D_MODEL = 2048
CHUNK = 64
A_HEADS = 8
A_QK_DIM = 64
A_V_DIM = 128
A_WIDTH = A_HEADS * A_V_DIM
A_SCALE = A_QK_DIM ** -0.5
ROT_DIM = A_QK_DIM // 4
ROPE_THETA = 500000.0
SUBLN_EPS = 1e-5
P_WINDOWS = (2, 4, 8, 16)
P_GROUP_DIM = 256
P_WIDTH = len(P_WINDOWS) * P_GROUP_DIM
P_HIST = max(P_WINDOWS) - 1
M_TOKENS = 256
M_HEADS = 4
M_HEAD_DIM = 256
M_WIDTH = M_HEADS * M_HEAD_DIM
M_SCALE = M_HEAD_DIM ** -0.5
N_BRANCH = 3
DEPTH = 1
DEEPNORM_ALPHA = (2.0 * DEPTH) ** 0.25
LN_EPS = 1e-5
LAMBDA_INIT = 0.8 - 0.6 * math.exp(-0.3 * 0)
ONES_ROWS = 16

COL_Q, COL_K, COL_V, COL_ZA, COL_UP, COL_ZP, COL_QM, COL_ZM, COL_GM = (
    0, 1024, 2048, 3072, 4096, 5120, 6144, 7168, 8192)
IN_COLS = 14336
H_ZA, H_GM = 0, 1024
H_COLS = 7168


def _params(*sem):
    return pltpu.CompilerParams(dimension_semantics=sem, vmem_limit_bytes=V7X_VMEM_LIMIT_BYTES)


def _sigmoid(z):
    return 1.0 / (1.0 + jnp.exp(-z))


def _silu(z):
    return z * _sigmoid(z)


def _proj_kernel(x_ref, w_ref, o_ref):
    acc = jnp.dot(x_ref[...].astype(BF16), w_ref[...].astype(BF16), preferred_element_type=F32)
    o_ref[...] = acc.astype(o_ref.dtype)


def _proj(x, w, col_block, ncols, out_dtype, tm, tn, name="proj"):
    n, kdim = x.shape
    return pl.pallas_call(
        _proj_kernel,
        out_shape=jax.ShapeDtypeStruct((n, ncols), out_dtype),
        grid=(n // tm, ncols // tn),
        in_specs=[pl.BlockSpec((tm, kdim), lambda i, j: (i, 0)),
                  pl.BlockSpec((kdim, tn), lambda i, j: (0, col_block(j)))],
        out_specs=pl.BlockSpec((tm, tn), lambda i, j: (i, j)),
        compiler_params=_params("arbitrary", "arbitrary"),
        name=name,
    )(x, w)


def _proj_memkv_kernel(x_ref, w_ref, mk_ref, mv_ref):
    kv = jnp.dot(x_ref[...].astype(BF16), w_ref[...].astype(BF16), preferred_element_type=F32)

    @pl.when(pl.program_id(1) == 0)
    def _():
        mk_ref[...] = kv

    @pl.when(pl.program_id(1) == 1)
    def _():
        mv_ref[...] = kv


def _proj_memkv(mem, w, tm):
    n, kdim = mem.shape
    row = pl.BlockSpec((tm, M_WIDTH), lambda i, j: (i, 0))
    return pl.pallas_call(
        _proj_memkv_kernel,
        out_shape=[jax.ShapeDtypeStruct((n, M_WIDTH), F32)] * 2,
        grid=(n // tm, 2),
        in_specs=[pl.BlockSpec((tm, kdim), lambda i, j: (i, 0)),
                  pl.BlockSpec((kdim, M_WIDTH), lambda i, j: (0, j))],
        out_specs=[row, row],
        compiler_params=_params("arbitrary", "arbitrary"),
        name="proj_memkv",
    )(mem, w)


def _rotate_sublanes(acc, c, s):
    half = ROT_DIM // 2
    pieces = []
    for g in range(acc.shape[0] // A_QK_DIM):
        b = g * A_QK_DIM
        lo, hi = acc[b:b + half], acc[b + half:b + ROT_DIM]
        pieces += [lo * c - hi * s, hi * c + lo * s, acc[b + ROT_DIM:b + A_QK_DIM]]
    return jnp.concatenate(pieces, axis=0)


def _proj_qk_kernel(x_ref, wq_ref, wk_ref, cos_ref, sin_ref, qt_ref, kt_ref, kb_ref, xb_ref, *, q_scale):
    xb = x_ref[...].astype(BF16)
    xb_ref[...] = xb
    c, s = cos_ref[...], sin_ref[...]
    q = _rotate_sublanes(jnp.dot(xb, wq_ref[...].astype(BF16), preferred_element_type=F32).T, c, s)
    q = (q * q_scale).astype(qt_ref.dtype)
    tile = qt_ref.shape[-1]
    for n in range(qt_ref.shape[0]):
        qt_ref[n] = q[:, n * tile:(n + 1) * tile]
    k = _rotate_sublanes(jnp.dot(xb, wk_ref[...].astype(BF16), preferred_element_type=F32).T, c, s)
    kt_ref[...] = k
    kb_ref[...] = k.T.astype(kb_ref.dtype)


def _proj_qk(x, w, cos_t, sin_t, batch, t, tm, q_tile, q_scale):
    n, kdim = x.shape
    width = A_WIDTH
    nt = t // tm
    tab = pl.BlockSpec((ROT_DIM // 2, tm), lambda i: (0, i % nt))
    wspec = lambda col: pl.BlockSpec((kdim, width), lambda i: (0, col // width), pipeline_mode=pl.Buffered(1))
    tspec = pl.BlockSpec((None, width, tm), lambda i: (i // nt, 0, i % nt))
    qspec = pl.BlockSpec((None, tm // q_tile, width, q_tile), lambda i: (i // nt, i % nt, 0, 0))
    return pl.pallas_call(
        functools.partial(_proj_qk_kernel, q_scale=q_scale),
        out_shape=[jax.ShapeDtypeStruct((batch, t // q_tile, width, q_tile), BF16),
                   jax.ShapeDtypeStruct((batch, width, t), F32),
                   jax.ShapeDtypeStruct((n, width), BF16), jax.ShapeDtypeStruct((n, kdim), BF16)],
        grid=(n // tm,),
        in_specs=[pl.BlockSpec((tm, kdim), lambda i: (i, 0)), wspec(COL_Q), wspec(COL_K), tab, tab],
        out_specs=[qspec, tspec, pl.BlockSpec((tm, width), lambda i: (i, 0)),
                   pl.BlockSpec((tm, kdim), lambda i: (i, 0))],
        compiler_params=_params("arbitrary"),
        name="proj_qk",
    )(x, w, w, cos_t, sin_t)


def _proj_vup_kernel(x_ref, wv_ref, wu_ref, wz_ref, hist_ref, pw_ref, ps_ref, v_ref, vt_ref, u_ref, bp_ref,
                     prev_ref, *, key_tile, nt):
    i = pl.program_id(0) % nt
    xb = x_ref[...]
    tm = xb.shape[0]
    v = jnp.dot(xb, wv_ref[...].astype(BF16), preferred_element_type=F32)
    v_ref[...] = v
    for c in range(tm // key_tile):
        vt_ref[c] = v[c * key_tile:(c + 1) * key_tile, :].T.astype(vt_ref.dtype)
    u = jnp.dot(xb, wu_ref[...].astype(BF16), preferred_element_type=F32)
    u_ref[...] = u
    zp = jnp.dot(xb, wz_ref[...].astype(BF16), preferred_element_type=F32)

    @pl.when(i == 0)
    def _():
        prev_ref[...] = hist_ref[...]

    pos = i * tm + lax.broadcasted_iota(jnp.int32, (tm, 1), 0)
    bp_ref[...] = _pool_mix(u, prev_ref[...], zp, pw_ref, ps_ref, pos).astype(bp_ref.dtype)
    prev_ref[...] = u[tm - prev_ref.shape[0]:, :]


def _proj_vup(xb, w, hist, pool_w, pool_scale, batch, t, tm, key_tile):
    n, kdim = xb.shape
    width = A_WIDTH
    nt = t // tm
    hr = hist.shape[1]
    wspec = lambda col: pl.BlockSpec((kdim, width), lambda i: (0, col // width), pipeline_mode=pl.Buffered(1))
    row = pl.BlockSpec((tm, width), lambda i: (i, 0))
    return pl.pallas_call(
        functools.partial(_proj_vup_kernel, key_tile=key_tile, nt=nt),
        out_shape=[jax.ShapeDtypeStruct((n, width), F32),
                   jax.ShapeDtypeStruct((batch, t // key_tile, width, key_tile), BF16),
                   jax.ShapeDtypeStruct((n, width), F32),
                   jax.ShapeDtypeStruct((n, width), BF16)],
        grid=(n // tm,),
        in_specs=[pl.BlockSpec((tm, kdim), lambda i: (i, 0)), wspec(COL_V), wspec(COL_UP), wspec(COL_ZP),
                  pl.BlockSpec((None, hr, P_WIDTH), lambda i: (i // nt, 0, 0)),
                  pl.BlockSpec((len(P_WINDOWS), P_GROUP_DIM, P_GROUP_DIM), lambda i: (0, 0, 0)),
                  pl.BlockSpec((1, P_WIDTH), lambda i: (0, 0))],
        out_specs=[row, pl.BlockSpec((None, tm // key_tile, width, key_tile), lambda i: (i // nt, i % nt, 0, 0)),
                   row, row],
        scratch_shapes=[pltpu.VMEM((hr, P_WIDTH), F32)],
        compiler_params=_params("arbitrary"),
        name="proj_vup",
    )(xb, w, w, w, hist, pool_w, pool_scale)


def _rotary_angles(pos):
    half = ROT_DIM // 2
    inv = ROPE_THETA ** (-(jnp.arange(half, dtype=F32) * 2.0) / ROT_DIM)
    return pos.astype(F32)[:, None] * inv[None, :]


def _rotary_tables(pos):
    half = ROT_DIM // 2
    ang = _rotary_angles(pos)
    cos, sin = jnp.cos(ang), jnp.sin(ang)
    t = pos.shape[0]
    ones = jnp.ones((t, A_QK_DIM - ROT_DIM), F32)
    zeros = jnp.zeros((t, A_QK_DIM - ROT_DIM), F32)
    zh = jnp.zeros((t, half), F32)
    c64 = jnp.concatenate([cos, cos, ones], axis=1)
    sa64 = jnp.concatenate([zh, sin, zeros], axis=1)
    sb64 = jnp.concatenate([-sin, zh, zeros], axis=1)
    tile = lambda a: jnp.concatenate([a, a], axis=1)
    return tile(c64), tile(sa64), tile(sb64)


def _lambda_value(lam_ref):
    lv = lam_ref[...]
    a = jnp.sum(lv[0:1] * lv[1:2], axis=1, keepdims=True)
    b = jnp.sum(lv[2:3] * lv[3:4], axis=1, keepdims=True)
    return jnp.exp(a) - jnp.exp(b) + LAMBDA_INIT


def _split_components(q):
    lane = lax.broadcasted_iota(jnp.int32, q.shape, 1)
    qf = q.astype(F32)
    return jnp.concatenate([jnp.where(lane < A_QK_DIM, qf, 0.0),
                            jnp.where(lane >= A_QK_DIM, qf, 0.0)], axis=0).astype(q.dtype)


def _attn_finish(acc, l, tq, lam, gain, za):
    o = acc[:tq] / l[:tq] - lam * (acc[tq:] / l[tq:])
    o = o * lax.rsqrt(jnp.mean(o * o, axis=1, keepdims=True) + SUBLN_EPS)
    o = o * gain * (1.0 - LAMBDA_INIT)
    return o * _silu(za)


def _attn_prompt_kernel(qt_ref, k_ref, vt_ref, za_ref, gain_ref, lam_ref, o_ref, m_ref, l_ref, acc_ref,
                        s0_ref, s1_ref, *,
                        tq, heads):
    i = pl.program_id(2)
    hs = [slice(n * LANES, (n + 1) * LANES) for n in range(heads)]
    qqts = []
    for sl in hs:
        qt = qt_ref[sl, :].astype(F32)
        row = lax.broadcasted_iota(jnp.int32, qt.shape, 0)
        qqts.append(jnp.concatenate([jnp.where(row < A_QK_DIM, qt, 0.0),
                                     jnp.where(row >= A_QK_DIM, qt, 0.0)], axis=1).astype(BF16))

    m_ref[...] = jnp.full(m_ref.shape, -jnp.inf, F32)
    l_ref[...] = jnp.zeros(l_ref.shape, F32)
    acc_ref[...] = jnp.zeros(acc_ref.shape, F32)

    def score(j, n):
        start = pl.multiple_of(j * tq, tq)
        return jnp.dot(k_ref[pl.ds(start, tq), hs[n]], qqts[n], preferred_element_type=F32)

    ones_rows = jnp.ones((ONES_ROWS, tq), BF16)

    def step(j, cur_ref, nxt_ref, masked):
        for n in range(heads):
            s = cur_ref[n]
            if masked:
                krow = lax.broadcasted_iota(jnp.int32, s.shape, 0)
                qcol = lax.broadcasted_iota(jnp.int32, s.shape, 1) % tq
                s = jnp.where((krow // CHUNK) <= (qcol // CHUNK), s, -jnp.inf)
            m = m_ref[n]
            m_new = jnp.maximum(m, jnp.max(s, axis=0, keepdims=True))
            alpha = jnp.exp2(m - m_new)
            p = jnp.exp2(s - m_new).astype(BF16)
            m_ref[n] = m_new
            lhs = jnp.concatenate([vt_ref[j, hs[n], :], ones_rows], axis=0)
            pv = jnp.dot(lhs, p, preferred_element_type=F32)
            acc_ref[n] = alpha * acc_ref[n] + pv[:A_V_DIM]
            l_ref[n] = alpha * l_ref[n] + pv[A_V_DIM:A_V_DIM + 1]
            if nxt_ref is not None:
                nxt_ref[n] = score(j + 1, n)

    for n in range(heads):
        s0_ref[n] = score(0, n)

    def pair(jj, carry):
        step(2 * jj, s0_ref, s1_ref, False)
        step(2 * jj + 1, s1_ref, s0_ref, False)
        return carry

    lax.fori_loop(0, i // 2, pair, 0)

    @pl.when(i % 2 == 1)
    def _():
        step(i - 1, s0_ref, s1_ref, False)
        step(i, s1_ref, None, True)

    @pl.when(i % 2 == 0)
    def _():
        step(i, s0_ref, None, True)
    lam = _lambda_value(lam_ref)
    gain = gain_ref[...] * (1.0 - LAMBDA_INIT)
    for n in range(heads):
        inv, acc = 1.0 / l_ref[n], acc_ref[n]
        ot = acc[:, :tq] * inv[:, :tq] - lam * (acc[:, tq:] * inv[:, tq:])
        o = ot.T
        o = o * lax.rsqrt(jnp.mean(o * o, axis=1, keepdims=True) + SUBLN_EPS) * gain
        o_ref[:, hs[n]] = (o * _silu(za_ref[:, hs[n]].astype(F32))).astype(o_ref.dtype)


def _attn_prompt(qt, k, vt, h, gain, lamv, batch, t, tq=256, heads=8):
    nq = t // tq
    n = batch * t
    hw = heads * LANES
    return pl.pallas_call(
        functools.partial(_attn_prompt_kernel, tq=tq, heads=heads),
        out_shape=jax.ShapeDtypeStruct((n, A_WIDTH), BF16),
        grid=(batch, A_HEADS // heads, nq),
        in_specs=[
            pl.BlockSpec((None, None, hw, tq), lambda b, hh, i: (b, i, hh, 0)),
            pl.BlockSpec((t, hw), lambda b, hh, i: (b, hh)),
            pl.BlockSpec((None, nq, hw, tq), lambda b, hh, i: (b, 0, hh, 0)),
            pl.BlockSpec((tq, hw), lambda b, hh, i: (b * nq + i, H_ZA // hw + hh)),
            pl.BlockSpec((1, LANES), lambda b, hh, i: (0, 0)),
            pl.BlockSpec((4, A_QK_DIM), lambda b, hh, i: (0, 0)),
        ],
        out_specs=pl.BlockSpec((tq, hw), lambda b, hh, i: (b * nq + i, hh)),
        scratch_shapes=[pltpu.VMEM((heads, 1, 2 * tq), F32), pltpu.VMEM((heads, 1, 2 * tq), F32),
                        pltpu.VMEM((heads, A_V_DIM, 2 * tq), F32),
                        pltpu.VMEM((heads, tq, 2 * tq), F32), pltpu.VMEM((heads, tq, 2 * tq), F32)],
        compiler_params=_params("arbitrary", "arbitrary", "arbitrary"),
        name="attn_prompt",
    )(qt, k, vt, h, gain, lamv)


def _chunk_mask(s, ts, q_pos0, k_pos0):
    q_pos = q_pos0 + lax.broadcasted_iota(jnp.int32, s.shape, 0) % ts
    k_pos = k_pos0 + lax.broadcasted_iota(jnp.int32, s.shape, 1)
    return jnp.where((k_pos // CHUNK) <= (q_pos // CHUNK), s, -jnp.inf)


def _rotate_lanes(blk, c, sa, sb):
    half = ROT_DIM // 2
    return blk * c + pltpu.roll(blk, half, 1) * sa + pltpu.roll(blk, LANES - half, 1) * sb


def _mix_sample_kernel(q_ref, k_ref, v_ref, za_ref, u_ref, zp_ref, qm_ref, zm_ref, cos_ref, sa_ref, sb_ref,
                       kct_ref, vc_ref, hist_ref, mk_ref, mv_ref, gain_ref, lam_ref, pw_ref, ps_ref,
                       o_ref, krot_ref, bp_ref, bm_ref, *, ts, past, mask_cache, mask_new):
    pos = past + lax.broadcasted_iota(jnp.int32, (ts, 1), 0)
    bp_ref[...] = _pool_mix(u_ref[...], hist_ref[...], zp_ref[...], pw_ref, ps_ref, pos).astype(bp_ref.dtype)
    _mem_attend(qm_ref[...].astype(BF16), zm_ref[...], mk_ref, mv_ref, bm_ref)

    c, sa, sb = cos_ref[...], sa_ref[...], sb_ref[...]
    nt = (((1,), (1,)), ((), ()))
    hs = [slice(n * LANES, (n + 1) * LANES) for n in range(A_HEADS)]
    scores = []
    for n in range(A_HEADS):
        q = (_rotate_lanes(q_ref[:, hs[n]], c, sa, sb) * A_SCALE).astype(BF16)
        k = _rotate_lanes(k_ref[:, hs[n]], c, sa, sb)
        krot_ref[:, hs[n]] = k
        qq = _split_components(q)
        s_c = jnp.dot(qq, kct_ref[n].astype(BF16), preferred_element_type=F32)
        s_n = lax.dot_general(qq, k.astype(BF16), nt, preferred_element_type=F32)
        if mask_cache:
            s_c = _chunk_mask(s_c, ts, past, 0)
        if mask_new:
            s_n = _chunk_mask(s_n, ts, past, past)
        scores.append((s_c, s_n))
    lam = _lambda_value(lam_ref)
    for n, (s_c, s_n) in enumerate(scores):
        m = jnp.maximum(jnp.max(s_c, axis=1, keepdims=True), jnp.max(s_n, axis=1, keepdims=True))
        p_c = jnp.exp(s_c - m)
        p_n = jnp.exp(s_n - m)
        l = jnp.sum(p_c, axis=1, keepdims=True) + jnp.sum(p_n, axis=1, keepdims=True)
        vc = vc_ref[pl.ds(n, past, stride=A_HEADS), :].astype(BF16)
        acc = (jnp.dot(p_c.astype(BF16), vc, preferred_element_type=F32)
               + jnp.dot(p_n.astype(BF16), v_ref[:, hs[n]].astype(BF16), preferred_element_type=F32))
        out = _attn_finish(acc, l, ts, lam, gain_ref[...], za_ref[:, hs[n]].astype(F32))
        o_ref[:, hs[n]] = out.astype(o_ref.dtype)


def _needs_mask(q_pos, k_pos):
    return not bool(((k_pos[None, :] // CHUNK) <= (q_pos[:, None] // CHUNK)).all())


def _mix_sample(act, rot, kct, vc, hist, mk, mv, gain, lamv, pool_w, pool_scale, batch, ts, past):
    q_pos = past + np.arange(ts)
    seg = lambda col: pl.BlockSpec((ts, A_WIDTH), lambda b: (b, col // A_WIDTH))
    tab = pl.BlockSpec((ts, LANES), lambda b: (0, 0))
    row = pl.BlockSpec((ts, A_WIDTH), lambda b: (b, 0))
    mem = pl.BlockSpec((M_TOKENS, M_WIDTH), lambda b: (b, 0))
    bshape = jax.ShapeDtypeStruct((batch * ts, A_WIDTH), BF16)
    return pl.pallas_call(
        functools.partial(_mix_sample_kernel, ts=ts, past=past,
                          mask_cache=_needs_mask(q_pos, np.arange(past)),
                          mask_new=_needs_mask(q_pos, q_pos)),
        out_shape=[bshape, jax.ShapeDtypeStruct((batch * ts, A_WIDTH), F32), bshape, bshape],
        grid=(batch,),
        in_specs=[
            seg(COL_Q), seg(COL_K), seg(COL_V), seg(COL_ZA), seg(COL_UP), seg(COL_ZP), seg(COL_QM), seg(COL_ZM),
            tab, tab, tab,
            pl.BlockSpec((None, A_HEADS, LANES, past), lambda b: (b, 0, 0, 0)),
            pl.BlockSpec((None, past * A_HEADS, LANES), lambda b: (b, 0, 0)),
            pl.BlockSpec((None, hist.shape[1], P_WIDTH), lambda b: (b, 0, 0)),
            mem, mem,
            pl.BlockSpec((1, LANES), lambda b: (0, 0)),
            pl.BlockSpec((4, A_QK_DIM), lambda b: (0, 0)),
            pl.BlockSpec((len(P_WINDOWS), P_GROUP_DIM, P_GROUP_DIM), lambda b: (0, 0, 0)),
            pl.BlockSpec((1, P_WIDTH), lambda b: (0, 0)),
        ],
        out_specs=[row, row, row, row],
        compiler_params=_params("arbitrary"),
        name="mix_sample",
    )(*([act] * 8), *rot, kct, vc, hist, mk, mv, gain, lamv, pool_w, pool_scale)


def _pool_mix(u, prev, zp, pw_ref, ps_ref, pos):
    tm = u.shape[0]
    ext = jnp.concatenate([prev, u], axis=0)
    hist_rows = prev.shape[0]
    parts = []
    for g, w in enumerate(P_WINDOWS):
        sl = slice(g * P_GROUP_DIM, (g + 1) * P_GROUP_DIM)
        tot = ext[:, sl]
        span = 1
        while span < w:
            tot = tot + pltpu.roll(tot, span, 0)
            span *= 2
        tot = tot[hist_rows:]
        cnt = jnp.minimum(pos + 1, w).astype(F32)
        pooled = tot / cnt - u[:, sl]
        parts.append(jnp.dot(pooled.astype(BF16), pw_ref[g].astype(BF16), preferred_element_type=F32))
    mixed = jnp.concatenate(parts, axis=1) * ps_ref[...]
    return mixed * _silu(zp)


def _mem_attend(q, zm, mk_ref, mv_ref, o_ref):
    nt = (((1,), (1,)), ((), ()))
    hs = [slice(n * M_HEAD_DIM, (n + 1) * M_HEAD_DIM) for n in range(M_HEADS)]
    scores = [lax.dot_general(q[:, sl], mk_ref[:, sl].astype(BF16), nt,
                              preferred_element_type=F32) * M_SCALE for sl in hs]
    for sl, s in zip(hs, scores):
        m = jnp.max(s, axis=1, keepdims=True)
        p = jnp.exp(s - m)
        l = jnp.sum(p, axis=1, keepdims=True)
        o = jnp.dot(p.astype(BF16), mv_ref[:, sl].astype(BF16), preferred_element_type=F32) / l
        o_ref[:, sl] = (o * _silu(zm[:, sl])).astype(o_ref.dtype)


def _proj_mem_kernel(x_ref, wq_ref, wz_ref, mk_ref, mv_ref, o_ref):
    xb = x_ref[...]
    q = jnp.dot(xb, wq_ref[...].astype(BF16), preferred_element_type=F32).astype(BF16)
    zm = jnp.dot(xb, wz_ref[...].astype(BF16), preferred_element_type=F32)
    _mem_attend(q, zm, mk_ref, mv_ref, o_ref)


def _proj_mem(xb, w, mk, mv, batch, t, tm):
    n, kdim = xb.shape
    nt = t // tm
    wspec = lambda col: pl.BlockSpec((kdim, M_WIDTH), lambda b, i: (0, col // M_WIDTH),
                                     pipeline_mode=pl.Buffered(1))
    mem = pl.BlockSpec((M_TOKENS, M_WIDTH), lambda b, i: (b, 0))
    return pl.pallas_call(
        _proj_mem_kernel,
        out_shape=jax.ShapeDtypeStruct((n, M_WIDTH), BF16),
        grid=(batch, nt),
        in_specs=[pl.BlockSpec((tm, kdim), lambda b, i: (b * nt + i, 0)), wspec(COL_QM), wspec(COL_ZM), mem, mem],
        out_specs=pl.BlockSpec((tm, M_WIDTH), lambda b, i: (b * nt + i, 0)),
        compiler_params=_params("arbitrary", "arbitrary"),
        name="proj_mem",
    )(xb, w, w, mk, mv)


def _merge_kernel(ba_ref, bp_ref, bm_ref, g0_ref, g1_ref, g2_ref, wb_ref, o_ref, wbb_ref, *, halves):
    @pl.when(pl.program_id(1) == 0)
    def _():
        wbb_ref[...] = wb_ref[...].astype(BF16)

    branches = ((ba_ref, g0_ref), (bp_ref, g1_ref), (bm_ref, g2_ref))
    rows = o_ref.shape[0] // halves
    projs = [[jnp.dot(b_ref[r * rows:(r + 1) * rows, :], wbb_ref[n], preferred_element_type=F32)
              for n, (b_ref, _) in enumerate(branches)] for r in range(halves)]
    for r in range(halves):
        sl = slice(r * rows, (r + 1) * rows)
        acc = None
        for n, (_, g_ref) in enumerate(branches):
            term = _sigmoid(g_ref[sl, :].astype(F32)) * projs[r][n]
            acc = term if acc is None else acc + term
        o_ref[sl, :] = acc.astype(o_ref.dtype)


def _merge(ba, bp, bm, h, g_col, wb, tm, tn):
    n = ba.shape[0]
    width = ba.shape[1]
    gspec = lambda k: pl.BlockSpec((tm, tn), lambda j, i: (i, (g_col + k * D_MODEL) // tn + j))
    bspec = pl.BlockSpec((tm, width), lambda j, i: (i, 0))
    return pl.pallas_call(
        functools.partial(_merge_kernel, halves=2 if tm % 32 == 0 else 1),
        out_shape=jax.ShapeDtypeStruct((n, D_MODEL), BF16),
        grid=(D_MODEL // tn, n // tm),
        in_specs=[bspec, bspec, bspec, gspec(0), gspec(1), gspec(2),
                  pl.BlockSpec((N_BRANCH, width, tn), lambda j, i: (0, 0, j))],
        out_specs=pl.BlockSpec((tm, tn), lambda j, i: (i, j)),
        scratch_shapes=[pltpu.VMEM((N_BRANCH, width, tn), BF16)],
        compiler_params=_params("arbitrary", "arbitrary"),
        name="merge",
    )(ba, bp, bm, h, h, h, wb)


def _out_kernel(m_ref, w_ref, x_ref, g_ref, b_ref, o_ref, wb_ref, *, halves):
    @pl.when(pl.program_id(0) == 0)
    def _():
        wb_ref[...] = w_ref[...].astype(BF16)

    rows = o_ref.shape[0] // halves
    outs = [jnp.dot(m_ref[r * rows:(r + 1) * rows, :], wb_ref[...], preferred_element_type=F32)
            for r in range(halves)]
    for r in range(halves):
        sl = slice(r * rows, (r + 1) * rows)
        z = DEEPNORM_ALPHA * x_ref[sl, :] + outs[r]
        mu = jnp.mean(z, axis=1, keepdims=True)
        zc = z - mu
        var = jnp.mean(zc * zc, axis=1, keepdims=True)
        o_ref[sl, :] = zc * lax.rsqrt(var + LN_EPS) * g_ref[...] + b_ref[...]


def _out(merged, w_out, x, ln_g, ln_b, tm):
    n = merged.shape[0]
    row = pl.BlockSpec((tm, D_MODEL), lambda i: (i, 0))
    vec = pl.BlockSpec((1, D_MODEL), lambda i: (0, 0))
    wspec = pl.BlockSpec((D_MODEL, D_MODEL), lambda i: (0, 0), pipeline_mode=pl.Buffered(1))
    return pl.pallas_call(
        functools.partial(_out_kernel, halves=4 if tm % 64 == 0 else 1),
        out_shape=jax.ShapeDtypeStruct((n, D_MODEL), F32),
        grid=(n // tm,),
        in_specs=[row, wspec, row, vec, vec],
        out_specs=row,
        scratch_shapes=[pltpu.VMEM((D_MODEL, D_MODEL), BF16)],
        compiler_params=_params("arbitrary"),
        name="out_ln",
    )(merged, w_out, x, ln_g, ln_b)


def _rest_col_block(tn):
    za_blocks = (COL_UP - COL_ZA) // tn
    return lambda j: jnp.where(j < za_blocks, COL_ZA // tn + j, COL_GM // tn + (j - za_blocks))


def _layer_prompt(x2d, batch, t, w, pool_hist, mk, mv):
    ang = _rotary_angles(jnp.arange(t))
    cos_t, sin_t = jnp.cos(ang).T, jnp.sin(ang).T
    tm, tn, tp = 2048, 1024, 512
    qt, kt, kb, xb = _proj_qk(x2d, w["w_in"], cos_t, sin_t, batch, t, tp, 256, A_SCALE * math.log2(math.e))
    v, vt, u, b_p = _proj_vup(xb, w["w_in"], pool_hist, w["pool_w"], w["pool_scale"], batch, t, tp, 256)
    b_m = _proj_mem(xb, w["w_in"], mk, mv, batch, t, tp)
    h = _proj(xb, w["w_in"], _rest_col_block(tn), H_COLS, BF16, tm, tn, name="proj_rest")
    b_a = _attn_prompt(qt, kb, vt, h, w["gain"], w["lamv"], batch, t)
    merged = _merge(b_a, b_p, b_m, h, H_GM, w["w_branch"], 512, 1024)
    y = _out(merged, w["w_out"], x2d, w["ln_g"], w["ln_b"], 512)
    return y, kt, v, u


def _layer_sample(x2d, batch, t, past, w, kt_hist, v_hist, pool_hist, mk, mv):
    n = batch * t
    rot = _rotary_tables(past + jnp.arange(t))
    act = _proj(x2d, w["w_in"], lambda j: j, IN_COLS, F32, n, 1024, name="proj_all")
    b_a, k, b_p, b_m = _mix_sample(act, rot, kt_hist, v_hist, pool_hist, mk, mv, w["gain"], w["lamv"],
                                   w["pool_w"], w["pool_scale"], batch, t, past)
    merged = _merge(b_a, b_p, b_m, act, COL_GM, w["w_branch"], n, 1024)
    y = _out(merged, w["w_out"], x2d, w["ln_g"], w["ln_b"], n)
    return y, k, act


def kernel(x_prompt, x_sample, cache_attn_k, cache_attn_v, cache_mem_k, cache_mem_v, state_pool, mem_prompt, w_in, w_mem_kv, lambda_q1, lambda_k1, lambda_q2, lambda_k2, subln_gain, pool_w, pool_scale, w_branch, w_out, ln_gain, ln_bias):
    bp, tp, d = x_prompt.shape
    bs, ts, _ = x_sample.shape
    past = cache_attn_k.shape[2]
    hist_rows = P_HIST + 1

    def run_layer(l, xp2d, xs2d):
        w = {
            "w_in": w_in[l],
            "gain": subln_gain[l].reshape(1, A_V_DIM),
            "lamv": jnp.stack([lambda_q1[l], lambda_k1[l], lambda_q2[l], lambda_k2[l]]),
            "pool_w": pool_w[l],
            "pool_scale": pool_scale[l].reshape(1, P_WIDTH),
            "w_branch": w_branch[l],
            "w_out": w_out[l],
            "ln_g": ln_gain[l].reshape(1, d),
            "ln_b": ln_bias[l].reshape(1, d),
        }
        memf = mem_prompt.reshape(bp * M_TOKENS, d)
        mk, mv = _proj_memkv(memf, w_mem_kv[l], 512)

        zero_hist = jnp.zeros((bp, hist_rows, P_WIDTH), F32)
        y_p, k_p, v_p, u_p = _layer_prompt(xp2d, bp, tp, w, zero_hist, mk, mv)

        hist_s = jnp.pad(state_pool[l], ((0, 0), (hist_rows - P_HIST, 0), (0, 0)))
        kt_hist = jnp.transpose(cache_attn_k[l], (0, 2, 3, 4, 1)).reshape(bs, A_HEADS, 2 * A_QK_DIM, past)
        y_s, k_s, act_s = _layer_sample(
            xs2d, bs, ts, past, w,
            kt_hist, cache_attn_v[l].reshape(bs, past * A_HEADS, A_V_DIM),
            hist_s, cache_mem_k[l].reshape(bs * M_TOKENS, M_WIDTH), cache_mem_v[l].reshape(bs * M_TOKENS, M_WIDTH))
        v_s = act_s[:, COL_V:COL_V + A_WIDTH]
        u_s = act_s[:, COL_UP:COL_UP + P_WIDTH]

        pool_p = jnp.concatenate([zero_hist, u_p.reshape(bp, tp, P_WIDTH)], axis=1)[:, -P_HIST:]
        pool_s = jnp.concatenate([hist_s, u_s.reshape(bs, ts, P_WIDTH)], axis=1)[:, -P_HIST:]
        k_p = jnp.transpose(k_p.reshape(bp, A_HEADS, 2, A_QK_DIM, tp), (0, 4, 1, 2, 3))
        outs = (k_p, v_p.reshape(bp, tp, A_HEADS, A_V_DIM),
                mk.reshape(bp, M_TOKENS, M_HEADS, M_HEAD_DIM), mv.reshape(bp, M_TOKENS, M_HEADS, M_HEAD_DIM),
                pool_p,
                k_s.reshape(bs, ts, A_HEADS, 2, A_QK_DIM), v_s.reshape(bs, ts, A_HEADS, A_V_DIM), pool_s)
        return y_p, y_s, outs

    yp = x_prompt.reshape(bp * tp, d)
    ys = x_sample.reshape(bs * ts, d)
    per_layer = []
    for l in range(w_in.shape[0]):
        yp, ys, outs = run_layer(l, yp, ys)
        per_layer.append(outs)
    stacked = [jnp.stack([o[n] for o in per_layer]) for n in range(8)]
    return (yp.reshape(bp, tp, d), ys.reshape(bs, ts, d), *stacked)
```

```python
import functools
import math

import numpy as np
import jax
import jax.numpy as jnp
from jax import lax
from jax.experimental import pallas as pl
from jax.experimental.pallas import tpu as pltpu

F32 = jnp.float32
BF16 = jnp.bfloat16

V7X_VMEM_LIMIT_BYTES = 56 * 1024 * 1024
LANES = 128

D_MODEL = 2048
CHUNK = 64
A_HEADS = 8
A_QK_DIM = 64
A_V_DIM = 128
A_WIDTH = A_HEADS * A_V_DIM
A_SCALE = A_QK_DIM ** -0.5
ROT_DIM = A_QK_DIM // 4
ROPE_THETA = 500000.0
SUBLN_EPS = 1e-5
P_WINDOWS = (2, 4, 8, 16)
P_GROUP_DIM = 256
P_WIDTH = len(P_WINDOWS) * P_GROUP_DIM
P_HIST = max(P_WINDOWS) - 1
M_TOKENS = 256
M_HEADS = 4
M_HEAD_DIM = 256
M_WIDTH = M_HEADS * M_HEAD_DIM
M_SCALE = M_HEAD_DIM ** -0.5
N_BRANCH = 3
DEPTH = 1
DEEPNORM_ALPHA = (2.0 * DEPTH) ** 0.25
LN_EPS = 1e-5
LAMBDA_INIT = 0.8 - 0.6 * math.exp(-0.3 * 0)
ONES_ROWS = 16

COL_Q, COL_K, COL_V, COL_ZA, COL_UP, COL_ZP, COL_QM, COL_ZM, COL_GM = (
    0, 1024, 2048, 3072, 4096, 5120, 6144, 7168, 8192)
IN_COLS = 14336
H_ZA, H_GM = 0, 1024
H_COLS = 7168


def _params(*sem):
    return pltpu.CompilerParams(dimension_semantics=sem, vmem_limit_bytes=V7X_VMEM_LIMIT_BYTES)


def _sigmoid(z):
    return 1.0 / (1.0 + jnp.exp(-z))


def _silu(z):
    return z * _sigmoid(z)


def _proj_kernel(x_ref, w_ref, o_ref):
    acc = jnp.dot(x_ref[...].astype(BF16), w_ref[...].astype(BF16), preferred_element_type=F32)
    o_ref[...] = acc.astype(o_ref.dtype)


def _proj(x, w, col_block, ncols, out_dtype, tm, tn, name="proj"):
    n, kdim = x.shape
    return pl.pallas_call(
        _proj_kernel,
        out_shape=jax.ShapeDtypeStruct((n, ncols), out_dtype),
        grid=(n // tm, ncols // tn),
        in_specs=[pl.BlockSpec((tm, kdim), lambda i, j: (i, 0)),
                  pl.BlockSpec((kdim, tn), lambda i, j: (0, col_block(j)))],
        out_specs=pl.BlockSpec((tm, tn), lambda i, j: (i, j)),
        compiler_params=_params("arbitrary", "arbitrary"),
        name=name,
    )(x, w)


def _proj_memkv_kernel(x_ref, w_ref, mk_ref, mv_ref):
    kv = jnp.dot(x_ref[...].astype(BF16), w_ref[...].astype(BF16), preferred_element_type=F32)

    @pl.when(pl.program_id(1) == 0)
    def _():
        mk_ref[...] = kv

    @pl.when(pl.program_id(1) == 1)
    def _():
        mv_ref[...] = kv


def _proj_memkv(mem, w, tm):
    n, kdim = mem.shape
    row = pl.BlockSpec((tm, M_WIDTH), lambda i, j: (i, 0))
    return pl.pallas_call(
        _proj_memkv_kernel,
        out_shape=[jax.ShapeDtypeStruct((n, M_WIDTH), F32)] * 2,
        grid=(n // tm, 2),
        in_specs=[pl.BlockSpec((tm, kdim), lambda i, j: (i, 0)),
                  pl.BlockSpec((kdim, M_WIDTH), lambda i, j: (0, j))],
        out_specs=[row, row],
        compiler_params=_params("arbitrary", "arbitrary"),
        name="proj_memkv",
    )(mem, w)


def _rotate_sublanes(acc, c, s):
    half = ROT_DIM // 2
    pieces = []
    for g in range(acc.shape[0] // A_QK_DIM):
        b = g * A_QK_DIM
        lo, hi = acc[b:b + half], acc[b + half:b + ROT_DIM]
        pieces += [lo * c - hi * s, hi * c + lo * s, acc[b + ROT_DIM:b + A_QK_DIM]]
    return jnp.concatenate(pieces, axis=0)


def _proj_qk_kernel(x_ref, wq_ref, wk_ref, cos_ref, sin_ref, qt_ref, kt_ref, kb_ref, xb_ref, *, q_scale):
    xb = x_ref[...].astype(BF16)
    xb_ref[...] = xb
    c, s = cos_ref[...], sin_ref[...]
    q = _rotate_sublanes(jnp.dot(xb, wq_ref[...].astype(BF16), preferred_element_type=F32).T, c, s)
    q = (q * q_scale).astype(qt_ref.dtype)
    tile = qt_ref.shape[-1]
    for n in range(qt_ref.shape[0]):
        qt_ref[n] = q[:, n * tile:(n + 1) * tile]
    k = _rotate_sublanes(jnp.dot(xb, wk_ref[...].astype(BF16), preferred_element_type=F32).T, c, s)
    kt_ref[...] = k
    kb_ref[...] = k.T.astype(kb_ref.dtype)


def _proj_qk(x, w, cos_t, sin_t, batch, t, tm, q_tile, q_scale):
    n, kdim = x.shape
    width = A_WIDTH
    nt = t // tm
    tab = pl.BlockSpec((ROT_DIM // 2, tm), lambda i: (0, i % nt))
    wspec = lambda col: pl.BlockSpec((kdim, width), lambda i: (0, col // width), pipeline_mode=pl.Buffered(1))
    tspec = pl.BlockSpec((None, width, tm), lambda i: (i // nt, 0, i % nt))
    qspec = pl.BlockSpec((None, tm // q_tile, width, q_tile), lambda i: (i // nt, i % nt, 0, 0))
    return pl.pallas_call(
        functools.partial(_proj_qk_kernel, q_scale=q_scale),
        out_shape=[jax.ShapeDtypeStruct((batch, t // q_tile, width, q_tile), BF16),
                   jax.ShapeDtypeStruct((batch, width, t), F32),
                   jax.ShapeDtypeStruct((n, width), BF16), jax.ShapeDtypeStruct((n, kdim), BF16)],
        grid=(n // tm,),
        in_specs=[pl.BlockSpec((tm, kdim), lambda i: (i, 0)), wspec(COL_Q), wspec(COL_K), tab, tab],
        out_specs=[qspec, tspec, pl.BlockSpec((tm, width), lambda i: (i, 0)),
                   pl.BlockSpec((tm, kdim), lambda i: (i, 0))],
        compiler_params=_params("arbitrary"),
        name="proj_qk",
    )(x, w, w, cos_t, sin_t)


def _proj_vup_kernel(x_ref, wv_ref, wu_ref, wz_ref, hist_ref, pw_ref, ps_ref, v_ref, vt_ref, u_ref, bp_ref,
                     prev_ref, *, key_tile, nt):
    i = pl.program_id(0) % nt

    @pl.when(i == 0)
    def _():
        prev_ref[...] = hist_ref[...]

    xb = x_ref[...]
    tm = xb.shape[0]
    u = jnp.dot(xb, wu_ref[...].astype(BF16), preferred_element_type=F32)
    u_ref[...] = u
    zp = jnp.dot(xb, wz_ref[...].astype(BF16), preferred_element_type=F32)
    v = jnp.dot(xb, wv_ref[...].astype(BF16), preferred_element_type=F32)
    for h in range(A_HEADS):
        v_ref[pl.ds(h, tm, stride=A_HEADS), :] = v[:, h * A_V_DIM:(h + 1) * A_V_DIM]
    for c in range(tm // key_tile):
        vt_ref[c] = v[c * key_tile:(c + 1) * key_tile, :].T.astype(vt_ref.dtype)
    pos = i * tm + lax.broadcasted_iota(jnp.int32, (tm, 1), 0)
    bp_ref[...] = _pool_mix(u, prev_ref[...], zp, pw_ref, ps_ref, pos).astype(bp_ref.dtype)
    prev_ref[...] = u[tm - prev_ref.shape[0]:, :]


def _proj_vup(xb, w, hist, pool_w, pool_scale, batch, t, tm, key_tile):
    n, kdim = xb.shape
    width = A_WIDTH
    nt = t // tm
    hr = hist.shape[1]
    wspec = lambda col: pl.BlockSpec((kdim, width), lambda i: (0, col // width), pipeline_mode=pl.Buffered(1))
    row = pl.BlockSpec((tm, width), lambda i: (i, 0))
    return pl.pallas_call(
        functools.partial(_proj_vup_kernel, key_tile=key_tile, nt=nt),
        out_shape=[jax.ShapeDtypeStruct((n * A_HEADS, A_V_DIM), F32),
                   jax.ShapeDtypeStruct((batch, t // key_tile, width, key_tile), BF16),
                   jax.ShapeDtypeStruct((n, width), F32),
                   jax.ShapeDtypeStruct((n, width), BF16)],
        grid=(n // tm,),
        in_specs=[pl.BlockSpec((tm, kdim), lambda i: (i, 0)), wspec(COL_V), wspec(COL_UP), wspec(COL_ZP),
                  pl.BlockSpec((None, hr, P_WIDTH), lambda i: (i // nt, 0, 0)),
                  pl.BlockSpec((len(P_WINDOWS), P_GROUP_DIM, P_GROUP_DIM), lambda i: (0, 0, 0)),
                  pl.BlockSpec((1, P_WIDTH), lambda i: (0, 0))],
        out_specs=[pl.BlockSpec((tm * A_HEADS, A_V_DIM), lambda i: (i, 0)),
                   pl.BlockSpec((None, tm // key_tile, width, key_tile), lambda i: (i // nt, i % nt, 0, 0)),
                   row, row],
        scratch_shapes=[pltpu.VMEM((hr, P_WIDTH), F32)],
        compiler_params=_params("arbitrary"),
        name="proj_vup",
    )(xb, w, w, w, hist, pool_w, pool_scale)


def _rotary_angles(pos):
    half = ROT_DIM // 2
    inv = ROPE_THETA ** (-(jnp.arange(half, dtype=F32) * 2.0) / ROT_DIM)
    return pos.astype(F32)[:, None] * inv[None, :]


def _rotary_tables(pos):
    half = ROT_DIM // 2
    ang = _rotary_angles(pos)
    cos, sin = jnp.cos(ang), jnp.sin(ang)
    t = pos.shape[0]
    ones = jnp.ones((t, A_QK_DIM - ROT_DIM), F32)
    zeros = jnp.zeros((t, A_QK_DIM - ROT_DIM), F32)
    zh = jnp.zeros((t, half), F32)
    c64 = jnp.concatenate([cos, cos, ones], axis=1)
    sa64 = jnp.concatenate([zh, sin, zeros], axis=1)
    sb64 = jnp.concatenate([-sin, zh, zeros], axis=1)
    tile = lambda a: jnp.concatenate([a, a], axis=1)
    return tile(c64), tile(sa64), tile(sb64)


def _lambda_value(lam_ref):
    lv = lam_ref[...]
    a = jnp.sum(lv[0:1] * lv[1:2], axis=1, keepdims=True)
    b = jnp.sum(lv[2:3] * lv[3:4], axis=1, keepdims=True)
    return jnp.exp(a) - jnp.exp(b) + LAMBDA_INIT


def _split_components(q):
    lane = lax.broadcasted_iota(jnp.int32, q.shape, 1)
    qf = q.astype(F32)
    return jnp.concatenate([jnp.where(lane < A_QK_DIM, qf, 0.0),
                            jnp.where(lane >= A_QK_DIM, qf, 0.0)], axis=0).astype(q.dtype)


def _attn_finish(acc, l, tq, lam, gain, za):
    o = acc[:tq] / l[:tq] - lam * (acc[tq:] / l[tq:])
    o = o * lax.rsqrt(jnp.mean(o * o, axis=1, keepdims=True) + SUBLN_EPS)
    o = o * gain * (1.0 - LAMBDA_INIT)
    return o * _silu(za)


def _attn_prompt_kernel(qt_ref, k_ref, vt_ref, za_ref, gain_ref, lam_ref, o_ref, m_ref, l_ref, acc_ref,
                        s0_ref, s1_ref, *,
                        tq, heads):
    i = pl.program_id(2)
    hs = [slice(n * LANES, (n + 1) * LANES) for n in range(heads)]
    qqts = []
    for sl in hs:
        qt = qt_ref[sl, :].astype(F32)
        row = lax.broadcasted_iota(jnp.int32, qt.shape, 0)
        qqts.append(jnp.concatenate([jnp.where(row < A_QK_DIM, qt, 0.0),
                                     jnp.where(row >= A_QK_DIM, qt, 0.0)], axis=1).astype(BF16))

    m_ref[...] = jnp.full(m_ref.shape, -jnp.inf, F32)
    l_ref[...] = jnp.zeros(l_ref.shape, F32)
    acc_ref[...] = jnp.zeros(acc_ref.shape, F32)

    def score(j, n):
        start = pl.multiple_of(j * tq, tq)
        return jnp.dot(k_ref[pl.ds(start, tq), hs[n]], qqts[n], preferred_element_type=F32)

    ones_rows = jnp.ones((ONES_ROWS, tq), BF16)

    def step(j, cur_ref, nxt_ref, masked):
        for n in range(heads):
            s = cur_ref[n]
            if masked:
                krow = lax.broadcasted_iota(jnp.int32, s.shape, 0)
                qcol = lax.broadcasted_iota(jnp.int32, s.shape, 1) % tq
                s = jnp.where((krow // CHUNK) <= (qcol // CHUNK), s, -jnp.inf)
            m = m_ref[n]
            m_new = jnp.maximum(m, jnp.max(s, axis=0, keepdims=True))
            alpha = jnp.exp2(m - m_new)
            p = jnp.exp2(s - m_new).astype(BF16)
            m_ref[n] = m_new
            lhs = jnp.concatenate([vt_ref[j, hs[n], :], ones_rows], axis=0)
            pv = jnp.dot(lhs, p, preferred_element_type=F32)
            acc_ref[n] = alpha * acc_ref[n] + pv[:A_V_DIM]
            l_ref[n] = alpha * l_ref[n] + pv[A_V_DIM:A_V_DIM + 1]
            if nxt_ref is not None:
                nxt_ref[n] = score(j + 1, n)

    for n in range(heads):
        s0_ref[n] = score(0, n)

    def pair(jj, carry):
        step(2 * jj, s0_ref, s1_ref, False)
        step(2 * jj + 1, s1_ref, s0_ref, False)
        return carry

    lax.fori_loop(0, i // 2, pair, 0)

    @pl.when(i % 2 == 1)
    def _():
        step(i - 1, s0_ref, s1_ref, False)
        step(i, s1_ref, None, True)

    @pl.when(i % 2 == 0)
    def _():
        step(i, s0_ref, None, True)
    lam = _lambda_value(lam_ref)
    gain = gain_ref[...] * (1.0 - LAMBDA_INIT)
    for n in range(heads):
        inv, acc = 1.0 / l_ref[n], acc_ref[n]
        ot = acc[:, :tq] * inv[:, :tq] - lam * (acc[:, tq:] * inv[:, tq:])
        o = ot.T
        o = o * lax.rsqrt(jnp.mean(o * o, axis=1, keepdims=True) + SUBLN_EPS) * gain
        o_ref[:, hs[n]] = (o * _silu(za_ref[:, hs[n]].astype(F32))).astype(o_ref.dtype)


def _attn_prompt(qt, k, vt, h, gain, lamv, batch, t, tq=256, heads=8):
    nq = t // tq
    n = batch * t
    hw = heads * LANES
    return pl.pallas_call(
        functools.partial(_attn_prompt_kernel, tq=tq, heads=heads),
        out_shape=jax.ShapeDtypeStruct((n, A_WIDTH), BF16),
        grid=(batch, A_HEADS // heads, nq),
        in_specs=[
            pl.BlockSpec((None, None, hw, tq), lambda b, hh, i: (b, i, hh, 0)),
            pl.BlockSpec((t, hw), lambda b, hh, i: (b, hh)),
            pl.BlockSpec((None, nq, hw, tq), lambda b, hh, i: (b, 0, hh, 0)),
            pl.BlockSpec((tq, hw), lambda b, hh, i: (b * nq + i, H_ZA // hw + hh)),
            pl.BlockSpec((1, LANES), lambda b, hh, i: (0, 0)),
            pl.BlockSpec((4, A_QK_DIM), lambda b, hh, i: (0, 0)),
        ],
        out_specs=pl.BlockSpec((tq, hw), lambda b, hh, i: (b * nq + i, hh)),
        scratch_shapes=[pltpu.VMEM((heads, 1, 2 * tq), F32), pltpu.VMEM((heads, 1, 2 * tq), F32),
                        pltpu.VMEM((heads, A_V_DIM, 2 * tq), F32),
                        pltpu.VMEM((heads, tq, 2 * tq), F32), pltpu.VMEM((heads, tq, 2 * tq), F32)],
        compiler_params=_params("arbitrary", "arbitrary", "arbitrary"),
        name="attn_prompt",
    )(qt, k, vt, h, gain, lamv)


def _chunk_mask(s, ts, q_pos0, k_pos0):
    q_pos = q_pos0 + lax.broadcasted_iota(jnp.int32, s.shape, 0) % ts
    k_pos = k_pos0 + lax.broadcasted_iota(jnp.int32, s.shape, 1)
    return jnp.where((k_pos // CHUNK) <= (q_pos // CHUNK), s, -jnp.inf)


def _rotate_lanes(blk, c, sa, sb):
    half = ROT_DIM // 2
    return blk * c + pltpu.roll(blk, half, 1) * sa + pltpu.roll(blk, LANES - half, 1) * sb


def _mix_sample_kernel(q_ref, k_ref, v_ref, za_ref, u_ref, zp_ref, qm_ref, zm_ref, cos_ref, sa_ref, sb_ref,
                       kct_ref, vc_ref, hist_ref, mk_ref, mv_ref, gain_ref, lam_ref, pw_ref, ps_ref,
                       o_ref, krot_ref, bp_ref, bm_ref, *, ts, past, mask_cache, mask_new):
    pos = past + lax.broadcasted_iota(jnp.int32, (ts, 1), 0)
    bp_ref[...] = _pool_mix(u_ref[...], hist_ref[...], zp_ref[...], pw_ref, ps_ref, pos).astype(bp_ref.dtype)
    _mem_attend(qm_ref[...].astype(BF16), zm_ref[...], mk_ref, mv_ref, bm_ref)

    c, sa, sb = cos_ref[...], sa_ref[...], sb_ref[...]
    nt = (((1,), (1,)), ((), ()))
    hs = [slice(n * LANES, (n + 1) * LANES) for n in range(A_HEADS)]
    scores = []
    for n in range(A_HEADS):
        q = (_rotate_lanes(q_ref[:, hs[n]], c, sa, sb) * A_SCALE).astype(BF16)
        k = _rotate_lanes(k_ref[:, hs[n]], c, sa, sb)
        krot_ref[:, hs[n]] = k
        qq = _split_components(q)
        s_c = jnp.dot(qq, kct_ref[n].astype(BF16), preferred_element_type=F32)
        s_n = lax.dot_general(qq, k.astype(BF16), nt, preferred_element_type=F32)
        if mask_cache:
            s_c = _chunk_mask(s_c, ts, past, 0)
        if mask_new:
            s_n = _chunk_mask(s_n, ts, past, past)
        scores.append((s_c, s_n))
    lam = _lambda_value(lam_ref)
    for n, (s_c, s_n) in enumerate(scores):
        m = jnp.maximum(jnp.max(s_c, axis=1, keepdims=True), jnp.max(s_n, axis=1, keepdims=True))
        p_c = jnp.exp(s_c - m)
        p_n = jnp.exp(s_n - m)
        l = jnp.sum(p_c, axis=1, keepdims=True) + jnp.sum(p_n, axis=1, keepdims=True)
        vc = vc_ref[pl.ds(n, past, stride=A_HEADS), :].astype(BF16)
        acc = (jnp.dot(p_c.astype(BF16), vc, preferred_element_type=F32)
               + jnp.dot(p_n.astype(BF16), v_ref[:, hs[n]].astype(BF16), preferred_element_type=F32))
        out = _attn_finish(acc, l, ts, lam, gain_ref[...], za_ref[:, hs[n]].astype(F32))
        o_ref[:, hs[n]] = out.astype(o_ref.dtype)


def _needs_mask(q_pos, k_pos):
    return not bool(((k_pos[None, :] // CHUNK) <= (q_pos[:, None] // CHUNK)).all())


def _mix_sample(act, rot, kct, vc, hist, mk, mv, gain, lamv, pool_w, pool_scale, batch, ts, past):
    q_pos = past + np.arange(ts)
    seg = lambda col: pl.BlockSpec((ts, A_WIDTH), lambda b: (b, col // A_WIDTH))
    tab = pl.BlockSpec((ts, LANES), lambda b: (0, 0))
    row = pl.BlockSpec((ts, A_WIDTH), lambda b: (b, 0))
    mem = pl.BlockSpec((M_TOKENS, M_WIDTH), lambda b: (b, 0))
    bshape = jax.ShapeDtypeStruct((batch * ts, A_WIDTH), BF16)
    return pl.pallas_call(
        functools.partial(_mix_sample_kernel, ts=ts, past=past,
                          mask_cache=_needs_mask(q_pos, np.arange(past)),
                          mask_new=_needs_mask(q_pos, q_pos)),
        out_shape=[bshape, jax.ShapeDtypeStruct((batch * ts, A_WIDTH), F32), bshape, bshape],
        grid=(batch,),
        in_specs=[
            seg(COL_Q), seg(COL_K), seg(COL_V), seg(COL_ZA), seg(COL_UP), seg(COL_ZP), seg(COL_QM), seg(COL_ZM),
            tab, tab, tab,
            pl.BlockSpec((None, A_HEADS, LANES, past), lambda b: (b, 0, 0, 0)),
            pl.BlockSpec((None, past * A_HEADS, LANES), lambda b: (b, 0, 0)),
            pl.BlockSpec((None, hist.shape[1], P_WIDTH), lambda b: (b, 0, 0)),
            mem, mem,
            pl.BlockSpec((1, LANES), lambda b: (0, 0)),
            pl.BlockSpec((4, A_QK_DIM), lambda b: (0, 0)),
            pl.BlockSpec((len(P_WINDOWS), P_GROUP_DIM, P_GROUP_DIM), lambda b: (0, 0, 0)),
            pl.BlockSpec((1, P_WIDTH), lambda b: (0, 0)),
        ],
        out_specs=[row, row, row, row],
        compiler_params=_params("arbitrary"),
        name="mix_sample",
    )(*([act] * 8), *rot, kct, vc, hist, mk, mv, gain, lamv, pool_w, pool_scale)


def _pool_mix(u, prev, zp, pw_ref, ps_ref, pos):
    tm = u.shape[0]
    ext = jnp.concatenate([prev, u], axis=0)
    hist_rows = prev.shape[0]
    parts = []
    for g, w in enumerate(P_WINDOWS):
        sl = slice(g * P_GROUP_DIM, (g + 1) * P_GROUP_DIM)
        tot = ext[:, sl]
        span = 1
        while span < w:
            tot = tot + pltpu.roll(tot, span, 0)
            span *= 2
        tot = tot[hist_rows:]
        cnt = jnp.minimum(pos + 1, w).astype(F32)
        pooled = tot / cnt - u[:, sl]
        parts.append(jnp.dot(pooled.astype(BF16), pw_ref[g].astype(BF16), preferred_element_type=F32))
    mixed = jnp.concatenate(parts, axis=1) * ps_ref[...]
    return mixed * _silu(zp)


def _mem_attend(q, zm, mk_ref, mv_ref, o_ref):
    nt = (((1,), (1,)), ((), ()))
    hs = [slice(n * M_HEAD_DIM, (n + 1) * M_HEAD_DIM) for n in range(M_HEADS)]
    scores = [lax.dot_general(q[:, sl], mk_ref[:, sl].astype(BF16), nt,
                              preferred_element_type=F32) * M_SCALE for sl in hs]
    for sl, s in zip(hs, scores):
        m = jnp.max(s, axis=1, keepdims=True)
        p = jnp.exp(s - m)
        l = jnp.sum(p, axis=1, keepdims=True)
        o = jnp.dot(p.astype(BF16), mv_ref[:, sl].astype(BF16), preferred_element_type=F32) / l
        o_ref[:, sl] = (o * _silu(zm[:, sl])).astype(o_ref.dtype)


def _proj_mem_kernel(x_ref, wq_ref, wz_ref, mk_ref, mv_ref, o_ref):
    xb = x_ref[...]
    q = jnp.dot(xb, wq_ref[...].astype(BF16), preferred_element_type=F32).astype(BF16)
    zm = jnp.dot(xb, wz_ref[...].astype(BF16), preferred_element_type=F32)
    _mem_attend(q, zm, mk_ref, mv_ref, o_ref)


def _proj_mem(xb, w, mk, mv, batch, t, tm):
    n, kdim = xb.shape
    nt = t // tm
    wspec = lambda col: pl.BlockSpec((kdim, M_WIDTH), lambda b, i: (0, col // M_WIDTH),
                                     pipeline_mode=pl.Buffered(1))
    mem = pl.BlockSpec((M_TOKENS, M_WIDTH), lambda b, i: (b, 0))
    return pl.pallas_call(
        _proj_mem_kernel,
        out_shape=jax.ShapeDtypeStruct((n, M_WIDTH), BF16),
        grid=(batch, nt),
        in_specs=[pl.BlockSpec((tm, kdim), lambda b, i: (b * nt + i, 0)), wspec(COL_QM), wspec(COL_ZM), mem, mem],
        out_specs=pl.BlockSpec((tm, M_WIDTH), lambda b, i: (b * nt + i, 0)),
        compiler_params=_params("arbitrary", "arbitrary"),
        name="proj_mem",
    )(xb, w, w, mk, mv)


def _merge_kernel(ba_ref, bp_ref, bm_ref, g0_ref, g1_ref, g2_ref, wb_ref, o_ref, wbb_ref, *, halves):
    @pl.when(pl.program_id(1) == 0)
    def _():
        wbb_ref[...] = wb_ref[...].astype(BF16)

    branches = ((ba_ref, g0_ref), (bp_ref, g1_ref), (bm_ref, g2_ref))
    rows = o_ref.shape[0] // halves
    projs = [[jnp.dot(b_ref[r * rows:(r + 1) * rows, :], wbb_ref[n], preferred_element_type=F32)
              for n, (b_ref, _) in enumerate(branches)] for r in range(halves)]
    for r in range(halves):
        sl = slice(r * rows, (r + 1) * rows)
        acc = None
        for n, (_, g_ref) in enumerate(branches):
            term = _sigmoid(g_ref[sl, :].astype(F32)) * projs[r][n]
            acc = term if acc is None else acc + term
        o_ref[sl, :] = acc.astype(o_ref.dtype)


def _merge(ba, bp, bm, h, g_col, wb, tm, tn):
    n = ba.shape[0]
    width = ba.shape[1]
    gspec = lambda k: pl.BlockSpec((tm, tn), lambda j, i: (i, (g_col + k * D_MODEL) // tn + j))
    bspec = pl.BlockSpec((tm, width), lambda j, i: (i, 0))
    return pl.pallas_call(
        functools.partial(_merge_kernel, halves=2 if tm % 32 == 0 else 1),
        out_shape=jax.ShapeDtypeStruct((n, D_MODEL), BF16),
        grid=(D_MODEL // tn, n // tm),
        in_specs=[bspec, bspec, bspec, gspec(0), gspec(1), gspec(2),
                  pl.BlockSpec((N_BRANCH, width, tn), lambda j, i: (0, 0, j))],
        out_specs=pl.BlockSpec((tm, tn), lambda j, i: (i, j)),
        scratch_shapes=[pltpu.VMEM((N_BRANCH, width, tn), BF16)],
        compiler_params=_params("arbitrary", "arbitrary"),
        name="merge",
    )(ba, bp, bm, h, h, h, wb)


def _out_kernel(m_ref, w_ref, x_ref, g_ref, b_ref, o_ref, wb_ref, *, halves):
    @pl.when(pl.program_id(0) == 0)
    def _():
        wb_ref[...] = w_ref[...].astype(BF16)

    rows = o_ref.shape[0] // halves
    outs = [jnp.dot(m_ref[r * rows:(r + 1) * rows, :], wb_ref[...], preferred_element_type=F32)
            for r in range(halves)]
    for r in range(halves):
        sl = slice(r * rows, (r + 1) * rows)
        z = DEEPNORM_ALPHA * x_ref[sl, :] + outs[r]
        mu = jnp.mean(z, axis=1, keepdims=True)
        zc = z - mu
        var = jnp.mean(zc * zc, axis=1, keepdims=True)
        o_ref[sl, :] = zc * lax.rsqrt(var + LN_EPS) * g_ref[...] + b_ref[...]


def _out(merged, w_out, x, ln_g, ln_b, tm):
    n = merged.shape[0]
    row = pl.BlockSpec((tm, D_MODEL), lambda i: (i, 0))
    vec = pl.BlockSpec((1, D_MODEL), lambda i: (0, 0))
    wspec = pl.BlockSpec((D_MODEL, D_MODEL), lambda i: (0, 0), pipeline_mode=pl.Buffered(1))
    return pl.pallas_call(
        functools.partial(_out_kernel, halves=4 if tm % 64 == 0 else 1),
        out_shape=jax.ShapeDtypeStruct((n, D_MODEL), F32),
        grid=(n // tm,),
        in_specs=[row, wspec, row, vec, vec],
        out_specs=row,
        scratch_shapes=[pltpu.VMEM((D_MODEL, D_MODEL), BF16)],
        compiler_params=_params("arbitrary"),
        name="out_ln",
    )(merged, w_out, x, ln_g, ln_b)


def _rest_col_block(tn):
    za_blocks = (COL_UP - COL_ZA) // tn
    return lambda j: jnp.where(j < za_blocks, COL_ZA // tn + j, COL_GM // tn + (j - za_blocks))


def _layer_prompt(x2d, batch, t, w, pool_hist, mk, mv):
    ang = _rotary_angles(jnp.arange(t))
    cos_t, sin_t = jnp.cos(ang).T, jnp.sin(ang).T
    tm, tn, tp = 2048, 1024, 512
    qt, kt, kb, xb = _proj_qk(x2d, w["w_in"], cos_t, sin_t, batch, t, tp, 256, A_SCALE * math.log2(math.e))
    v, vt, u, b_p = _proj_vup(xb, w["w_in"], pool_hist, w["pool_w"], w["pool_scale"], batch, t, tp, 256)
    b_m = _proj_mem(xb, w["w_in"], mk, mv, batch, t, tp)
    h = _proj(xb, w["w_in"], _rest_col_block(tn), H_COLS, BF16, tm, tn, name="proj_rest")
    b_a = _attn_prompt(qt, kb, vt, h, w["gain"], w["lamv"], batch, t)
    merged = _merge(b_a, b_p, b_m, h, H_GM, w["w_branch"], 512, 1024)
    y = _out(merged, w["w_out"], x2d, w["ln_g"], w["ln_b"], 512)
    return y, kt, v, u


def _layer_sample(x2d, batch, t, past, w, kt_hist, v_hist, pool_hist, mk, mv):
    n = batch * t
    rot = _rotary_tables(past + jnp.arange(t))
    act = _proj(x2d, w["w_in"], lambda j: j, IN_COLS, F32, n, 1024, name="proj_all")
    b_a, k, b_p, b_m = _mix_sample(act, rot, kt_hist, v_hist, pool_hist, mk, mv, w["gain"], w["lamv"],
                                   w["pool_w"], w["pool_scale"], batch, t, past)
    merged = _merge(b_a, b_p, b_m, act, COL_GM, w["w_branch"], n, 1024)
    y = _out(merged, w["w_out"], x2d, w["ln_g"], w["ln_b"], n)
    return y, k, act


def kernel(x_prompt, x_sample, cache_attn_k, cache_attn_v, cache_mem_k, cache_mem_v, state_pool, mem_prompt, w_in, w_mem_kv, lambda_q1, lambda_k1, lambda_q2, lambda_k2, subln_gain, pool_w, pool_scale, w_branch, w_out, ln_gain, ln_bias):
    bp, tp, d = x_prompt.shape
    bs, ts, _ = x_sample.shape
    past = cache_attn_k.shape[2]
    hist_rows = P_HIST + 1

    def run_layer(l, xp2d, xs2d):
        w = {
            "w_in": w_in[l],
            "gain": subln_gain[l].reshape(1, A_V_DIM),
            "lamv": jnp.stack([lambda_q1[l], lambda_k1[l], lambda_q2[l], lambda_k2[l]]),
            "pool_w": pool_w[l],
            "pool_scale": pool_scale[l].reshape(1, P_WIDTH),
            "w_branch": w_branch[l],
            "w_out": w_out[l],
            "ln_g": ln_gain[l].reshape(1, d),
            "ln_b": ln_bias[l].reshape(1, d),
        }
        memf = mem_prompt.reshape(bp * M_TOKENS, d)
        mk, mv = _proj_memkv(memf, w_mem_kv[l], 512)

        zero_hist = jnp.zeros((bp, hist_rows, P_WIDTH), F32)
        y_p, k_p, v_p, u_p = _layer_prompt(xp2d, bp, tp, w, zero_hist, mk, mv)

        hist_s = jnp.pad(state_pool[l], ((0, 0), (hist_rows - P_HIST, 0), (0, 0)))
        kt_hist = jnp.transpose(cache_attn_k[l], (0, 2, 3, 4, 1)).reshape(bs, A_HEADS, 2 * A_QK_DIM, past)
        y_s, k_s, act_s = _layer_sample(
            xs2d, bs, ts, past, w,
            kt_hist, cache_attn_v[l].reshape(bs, past * A_HEADS, A_V_DIM),
            hist_s, cache_mem_k[l].reshape(bs * M_TOKENS, M_WIDTH), cache_mem_v[l].reshape(bs * M_TOKENS, M_WIDTH))
        v_s = act_s[:, COL_V:COL_V + A_WIDTH]
        u_s = act_s[:, COL_UP:COL_UP + P_WIDTH]

        pool_p = jnp.concatenate([zero_hist, u_p.reshape(bp, tp, P_WIDTH)], axis=1)[:, -P_HIST:]
        pool_s = jnp.concatenate([hist_s, u_s.reshape(bs, ts, P_WIDTH)], axis=1)[:, -P_HIST:]
        k_p = jnp.transpose(k_p.reshape(bp, A_HEADS, 2, A_QK_DIM, tp), (0, 4, 1, 2, 3))
        outs = (k_p, v_p.reshape(bp, tp, A_HEADS, A_V_DIM),
                mk.reshape(bp, M_TOKENS, M_HEADS, M_HEAD_DIM), mv.reshape(bp, M_TOKENS, M_HEADS, M_HEAD_DIM),
                pool_p,
                k_s.reshape(bs, ts, A_HEADS, 2, A_QK_DIM), v_s.reshape(bs, ts, A_HEADS, A_V_DIM), pool_s)
        return y_p, y_s, outs

    yp = x_prompt.reshape(bp * tp, d)
    ys = x_sample.reshape(bs * ts, d)
    per_layer = []
    for l in range(w_in.shape[0]):
        yp, ys, outs = run_layer(l, yp, ys)
        per_layer.append(outs)
    stacked = [jnp.stack([o[n] for o in per_layer]) for n in range(8)]
    return (yp.reshape(bp, tp, d), ys.reshape(bs, ts, d), *stacked)
```

```python
import functools
import math

import numpy as np
import jax
import jax.numpy as jnp
from jax import lax
from jax.experimental import pallas as pl
from jax.experimental.pallas import tpu as pltpu

F32 = jnp.float32
BF16 = jnp.bfloat16

V7X_VMEM_LIMIT_BYTES = 56 * 1024 * 1024
LANES = 128

D_MODEL = 2048
CHUNK = 64
A_HEADS = 8
A_QK_DIM = 64
A_V_DIM = 128
A_WIDTH = A_HEADS * A_V_DIM
A_SCALE = A_QK_DIM ** -0.5
ROT_DIM = A_QK_DIM // 4
ROPE_THETA = 500000.0
SUBLN_EPS = 1e-5
P_WINDOWS = (2, 4, 8, 16)
P_GROUP_DIM = 256
P_WIDTH = len(P_WINDOWS) * P_GROUP_DIM
P_HIST = max(P_WINDOWS) - 1
M_TOKENS = 256
M_HEADS = 4
M_HEAD_DIM = 256
M_WIDTH = M_HEADS * M_HEAD_DIM
M_SCALE = M_HEAD_DIM ** -0.5
N_BRANCH = 3
LN_EPS = 1e-5
ONES_ROWS = 16

COL_Q, COL_K, COL_V, COL_ZA, COL_UP, COL_ZP, COL_QM, COL_ZM, COL_GM = (
    0, 1024, 2048, 3072, 4096, 5120, 6144, 7168, 8192)
IN_COLS = 14336
H_ZA, H_GM = 0, 1024
H_COLS = 7168


def _params(*sem):
    return pltpu.CompilerParams(dimension_semantics=sem, vmem_limit_bytes=V7X_VMEM_LIMIT_BYTES)


def _sigmoid(z):
    return 1.0 / (1.0 + jnp.exp(-z))


def _silu(z):
    return z * _sigmoid(z)


def _proj_kernel(x_ref, w_ref, o_ref):
    acc = jnp.dot(x_ref[...].astype(BF16), w_ref[...].astype(BF16), preferred_element_type=F32)
    o_ref[...] = acc.astype(o_ref.dtype)


def _proj(x, w, col_block, ncols, out_dtype, tm, tn, name="proj"):
    n, kdim = x.shape
    return pl.pallas_call(
        _proj_kernel,
        out_shape=jax.ShapeDtypeStruct((n, ncols), out_dtype),
        grid=(n // tm, ncols // tn),
        in_specs=[pl.BlockSpec((tm, kdim), lambda i, j: (i, 0)),
                  pl.BlockSpec((kdim, tn), lambda i, j: (0, col_block(j)))],
        out_specs=pl.BlockSpec((tm, tn), lambda i, j: (i, j)),
        compiler_params=_params("arbitrary", "arbitrary"),
        name=name,
    )(x, w)


def _proj_memkv_kernel(x_ref, w_ref, mk_ref, mv_ref):
    kv = jnp.dot(x_ref[...].astype(BF16), w_ref[...].astype(BF16), preferred_element_type=F32)

    @pl.when(pl.program_id(1) == 0)
    def _():
        mk_ref[...] = kv

    @pl.when(pl.program_id(1) == 1)
    def _():
        mv_ref[...] = kv


def _proj_memkv(mem, w, tm):
    n, kdim = mem.shape
    row = pl.BlockSpec((tm, M_WIDTH), lambda i, j: (i, 0))
    return pl.pallas_call(
        _proj_memkv_kernel,
        out_shape=[jax.ShapeDtypeStruct((n, M_WIDTH), F32)] * 2,
        grid=(n // tm, 2),
        in_specs=[pl.BlockSpec((tm, kdim), lambda i, j: (i, 0)),
                  pl.BlockSpec((kdim, M_WIDTH), lambda i, j: (0, j))],
        out_specs=[row, row],
        compiler_params=_params("arbitrary", "arbitrary"),
        name="proj_memkv",
    )(mem, w)


def _rotate_sublanes(acc, c, s):
    half = ROT_DIM // 2
    pieces = []
    for g in range(acc.shape[0] // A_QK_DIM):
        b = g * A_QK_DIM
        lo, hi = acc[b:b + half], acc[b + half:b + ROT_DIM]
        pieces += [lo * c - hi * s, hi * c + lo * s, acc[b + ROT_DIM:b + A_QK_DIM]]
    return jnp.concatenate(pieces, axis=0)


def _proj_qk_kernel(x_ref, wq_ref, wk_ref, cos_ref, sin_ref, qt_ref, kt_ref, kb_ref, xb_ref, wqb_ref, wkb_ref, *,
                    q_scale):
    @pl.when(pl.program_id(0) == 0)
    def _():
        wqb_ref[...] = wq_ref[...].astype(BF16)
        wkb_ref[...] = wk_ref[...].astype(BF16)

    xb = x_ref[...].astype(BF16)
    xb_ref[...] = xb
    c, s = cos_ref[...], sin_ref[...]
    q = _rotate_sublanes(jnp.dot(xb, wqb_ref[...], preferred_element_type=F32).T, c, s)
    q = (q * q_scale).astype(qt_ref.dtype)
    tile = qt_ref.shape[-1]
    for n in range(qt_ref.shape[0]):
        qt_ref[n] = q[:, n * tile:(n + 1) * tile]
    k = _rotate_sublanes(jnp.dot(xb, wkb_ref[...], preferred_element_type=F32).T, c, s)
    kt_ref[...] = k
    kb_ref[...] = k.T.astype(kb_ref.dtype)


def _proj_qk(x, w, cos_t, sin_t, batch, t, tm, q_tile, q_scale):
    n, kdim = x.shape
    width = A_WIDTH
    nt = t // tm
    tab = pl.BlockSpec((ROT_DIM // 2, tm), lambda i: (0, i % nt))
    wspec = lambda col: pl.BlockSpec((kdim, width), lambda i: (0, col // width), pipeline_mode=pl.Buffered(1))
    tspec = pl.BlockSpec((None, width, tm), lambda i: (i // nt, 0, i % nt))
    qspec = pl.BlockSpec((None, tm // q_tile, width, q_tile), lambda i: (i // nt, i % nt, 0, 0))
    return pl.pallas_call(
        functools.partial(_proj_qk_kernel, q_scale=q_scale),
        out_shape=[jax.ShapeDtypeStruct((batch, t // q_tile, width, q_tile), BF16),
                   jax.ShapeDtypeStruct((batch, width, t), F32),
                   jax.ShapeDtypeStruct((n, width), BF16), jax.ShapeDtypeStruct((n, kdim), BF16)],
        grid=(n // tm,),
        in_specs=[pl.BlockSpec((tm, kdim), lambda i: (i, 0)), wspec(COL_Q), wspec(COL_K), tab, tab],
        out_specs=[qspec, tspec, pl.BlockSpec((tm, width), lambda i: (i, 0)),
                   pl.BlockSpec((tm, kdim), lambda i: (i, 0))],
        scratch_shapes=[pltpu.VMEM((kdim, width), BF16), pltpu.VMEM((kdim, width), BF16)],
        compiler_params=_params("arbitrary"),
        name="proj_qk",
    )(x, w, w, cos_t, sin_t)


def _proj_vup_kernel(x_ref, wv_ref, wu_ref, wz_ref, hist_ref, pw_ref, ps_ref, v_ref, vt_ref, u_ref, bp_ref,
                     prev_ref, *, key_tile, nt):
    i = pl.program_id(0) % nt

    @pl.when(i == 0)
    def _():
        prev_ref[...] = hist_ref[...]

    xb = x_ref[...]
    tm = xb.shape[0]
    u = jnp.dot(xb, wu_ref[...].astype(BF16), preferred_element_type=F32)
    u_ref[...] = u
    zp = jnp.dot(xb, wz_ref[...].astype(BF16), preferred_element_type=F32)
    v = jnp.dot(xb, wv_ref[...].astype(BF16), preferred_element_type=F32)
    for h in range(A_HEADS):
        v_ref[pl.ds(h, tm, stride=A_HEADS), :] = v[:, h * A_V_DIM:(h + 1) * A_V_DIM]
    for c in range(tm // key_tile):
        vt_ref[c] = v[c * key_tile:(c + 1) * key_tile, :].T.astype(vt_ref.dtype)
    pos = i * tm + lax.broadcasted_iota(jnp.int32, (tm, 1), 0)
    bp_ref[...] = _pool_mix(u, prev_ref[...], zp, pw_ref, ps_ref, pos).astype(bp_ref.dtype)
    prev_ref[...] = u[tm - prev_ref.shape[0]:, :]


def _proj_vup(xb, w, hist, pool_w, pool_scale, batch, t, tm, key_tile):
    n, kdim = xb.shape
    width = A_WIDTH
    nt = t // tm
    hr = hist.shape[1]
    wspec = lambda col: pl.BlockSpec((kdim, width), lambda i: (0, col // width), pipeline_mode=pl.Buffered(1))
    row = pl.BlockSpec((tm, width), lambda i: (i, 0))
    return pl.pallas_call(
        functools.partial(_proj_vup_kernel, key_tile=key_tile, nt=nt),
        out_shape=[jax.ShapeDtypeStruct((n * A_HEADS, A_V_DIM), F32),
                   jax.ShapeDtypeStruct((batch, t // key_tile, width, key_tile), BF16),
                   jax.ShapeDtypeStruct((n, width), F32),
                   jax.ShapeDtypeStruct((n, width), BF16)],
        grid=(n // tm,),
        in_specs=[pl.BlockSpec((tm, kdim), lambda i: (i, 0)), wspec(COL_V), wspec(COL_UP), wspec(COL_ZP),
                  pl.BlockSpec((None, hr, P_WIDTH), lambda i: (i // nt, 0, 0)),
                  pl.BlockSpec((len(P_WINDOWS), P_GROUP_DIM, P_GROUP_DIM), lambda i: (0, 0, 0)),
                  pl.BlockSpec((1, P_WIDTH), lambda i: (0, 0))],
        out_specs=[pl.BlockSpec((tm * A_HEADS, A_V_DIM), lambda i: (i, 0)),
                   pl.BlockSpec((None, tm // key_tile, width, key_tile), lambda i: (i // nt, i % nt, 0, 0)),
                   row, row],
        scratch_shapes=[pltpu.VMEM((hr, P_WIDTH), F32)],
        compiler_params=_params("arbitrary"),
        name="proj_vup",
    )(xb, w, w, w, hist, pool_w, pool_scale)


def _rotary_angles(pos):
    half = ROT_DIM // 2
    inv = ROPE_THETA ** (-(jnp.arange(half, dtype=F32) * 2.0) / ROT_DIM)
    return pos.astype(F32)[:, None] * inv[None, :]


def _rotary_tables(pos):
    half = ROT_DIM // 2
    ang = _rotary_angles(pos)
    cos, sin = jnp.cos(ang), jnp.sin(ang)
    t = pos.shape[0]
    ones = jnp.ones((t, A_QK_DIM - ROT_DIM), F32)
    zeros = jnp.zeros((t, A_QK_DIM - ROT_DIM), F32)
    zh = jnp.zeros((t, half), F32)
    c64 = jnp.concatenate([cos, cos, ones], axis=1)
    sa64 = jnp.concatenate([zh, sin, zeros], axis=1)
    sb64 = jnp.concatenate([-sin, zh, zeros], axis=1)
    tile = lambda a: jnp.concatenate([a, a], axis=1)
    return tile(c64), tile(sa64), tile(sb64)


def _lambda_values(lam_ref):
    lv = lam_ref[...]
    a = jnp.sum(lv[0:1] * lv[1:2], axis=1, keepdims=True)
    b = jnp.sum(lv[2:3] * lv[3:4], axis=1, keepdims=True)
    lam_init = lv[4:5, 0:1]
    return jnp.exp(a) - jnp.exp(b) + lam_init, lam_init


def _split_components(q):
    lane = lax.broadcasted_iota(jnp.int32, q.shape, 1)
    qf = q.astype(F32)
    return jnp.concatenate([jnp.where(lane < A_QK_DIM, qf, 0.0),
                            jnp.where(lane >= A_QK_DIM, qf, 0.0)], axis=0).astype(q.dtype)


def _attn_finish(acc, l, tq, lam, gain, za):
    o = acc[:tq] / l[:tq] - lam * (acc[tq:] / l[tq:])
    o = o * lax.rsqrt(jnp.mean(o * o, axis=1, keepdims=True) + SUBLN_EPS) * gain
    return o * _silu(za)


def _attn_prompt_kernel(qt_ref, k_ref, vt_ref, za_ref, gain_ref, lam_ref, o_ref, m_ref, l_ref, acc_ref,
                        s0_ref, s1_ref, *,
                        tq, heads):
    i = pl.program_id(2)
    hs = [slice(n * LANES, (n + 1) * LANES) for n in range(heads)]
    qqts = []
    for sl in hs:
        qt = qt_ref[sl, :].astype(F32)
        row = lax.broadcasted_iota(jnp.int32, qt.shape, 0)
        qqts.append(jnp.concatenate([jnp.where(row < A_QK_DIM, qt, 0.0),
                                     jnp.where(row >= A_QK_DIM, qt, 0.0)], axis=1).astype(BF16))

    m_ref[...] = jnp.full(m_ref.shape, -jnp.inf, F32)
    l_ref[...] = jnp.zeros(l_ref.shape, F32)
    acc_ref[...] = jnp.zeros(acc_ref.shape, F32)

    def score(j, n):
        start = pl.multiple_of(j * tq, tq)
        return jnp.dot(k_ref[pl.ds(start, tq), hs[n]], qqts[n], preferred_element_type=F32)

    ones_rows = jnp.ones((ONES_ROWS, tq), BF16)

    def step(j, cur_ref, nxt_ref, masked):
        for n in range(heads):
            s = cur_ref[n]
            if masked:
                krow = lax.broadcasted_iota(jnp.int32, s.shape, 0)
                qcol = lax.broadcasted_iota(jnp.int32, s.shape, 1) % tq
                s = jnp.where((krow // CHUNK) <= (qcol // CHUNK), s, -jnp.inf)
            m = m_ref[n]
            m_new = jnp.maximum(m, jnp.max(s, axis=0, keepdims=True))
            alpha = jnp.exp2(m - m_new)
            p = jnp.exp2(s - m_new).astype(BF16)
            m_ref[n] = m_new
            lhs = jnp.concatenate([vt_ref[j, hs[n], :], ones_rows], axis=0)
            pv = jnp.dot(lhs, p, preferred_element_type=F32)
            acc_ref[n] = alpha * acc_ref[n] + pv[:A_V_DIM]
            l_ref[n] = alpha * l_ref[n] + pv[A_V_DIM:A_V_DIM + 1]
            if nxt_ref is not None:
                nxt_ref[n] = score(j + 1, n)

    for n in range(heads):
        s0_ref[n] = score(0, n)

    def pair(jj, carry):
        step(2 * jj, s0_ref, s1_ref, False)
        step(2 * jj + 1, s1_ref, s0_ref, False)
        return carry

    lax.fori_loop(0, i // 2, pair, 0)

    @pl.when(i % 2 == 1)
    def _():
        step(i - 1, s0_ref, s1_ref, False)
        step(i, s1_ref, None, True)

    @pl.when(i % 2 == 0)
    def _():
        step(i, s0_ref, None, True)
    lam, lam_init = _lambda_values(lam_ref)
    gain = gain_ref[...] * (1.0 - lam_init)
    for n in range(heads):
        inv, acc = 1.0 / l_ref[n], acc_ref[n]
        ot = acc[:, :tq] * inv[:, :tq] - lam * (acc[:, tq:] * inv[:, tq:])
        o = ot.T
        o = o * lax.rsqrt(jnp.mean(o * o, axis=1, keepdims=True) + SUBLN_EPS) * gain
        o_ref[:, hs[n]] = (o * _silu(za_ref[:, hs[n]].astype(F32))).astype(o_ref.dtype)


def _attn_prompt(qt, k, vt, h, gain, lamv, batch, t, tq=256, heads=8):
    nq = t // tq
    n = batch * t
    hw = heads * LANES
    return pl.pallas_call(
        functools.partial(_attn_prompt_kernel, tq=tq, heads=heads),
        out_shape=jax.ShapeDtypeStruct((n, A_WIDTH), BF16),
        grid=(batch, A_HEADS // heads, nq),
        in_specs=[
            pl.BlockSpec((None, None, hw, tq), lambda b, hh, i: (b, i, hh, 0)),
            pl.BlockSpec((t, hw), lambda b, hh, i: (b, hh)),
            pl.BlockSpec((None, nq, hw, tq), lambda b, hh, i: (b, 0, hh, 0)),
            pl.BlockSpec((tq, hw), lambda b, hh, i: (b * nq + i, H_ZA // hw + hh)),
            pl.BlockSpec((1, LANES), lambda b, hh, i: (0, 0)),
            pl.BlockSpec((5, A_QK_DIM), lambda b, hh, i: (0, 0)),
        ],
        out_specs=pl.BlockSpec((tq, hw), lambda b, hh, i: (b * nq + i, hh)),
        scratch_shapes=[pltpu.VMEM((heads, 1, 2 * tq), F32), pltpu.VMEM((heads, 1, 2 * tq), F32),
                        pltpu.VMEM((heads, A_V_DIM, 2 * tq), F32),
                        pltpu.VMEM((heads, tq, 2 * tq), F32), pltpu.VMEM((heads, tq, 2 * tq), F32)],
        compiler_params=_params("arbitrary", "arbitrary", "arbitrary"),
        name="attn_prompt",
    )(qt, k, vt, h, gain, lamv)


def _chunk_mask(s, ts, q_pos0, k_pos0):
    q_pos = q_pos0 + lax.broadcasted_iota(jnp.int32, s.shape, 0) % ts
    k_pos = k_pos0 + lax.broadcasted_iota(jnp.int32, s.shape, 1)
    return jnp.where((k_pos // CHUNK) <= (q_pos // CHUNK), s, -jnp.inf)


def _rotate_lanes(blk, c, sa, sb):
    half = ROT_DIM // 2
    return blk * c + pltpu.roll(blk, half, 1) * sa + pltpu.roll(blk, LANES - half, 1) * sb


def _mix_sample_kernel(q_ref, k_ref, v_ref, za_ref, u_ref, zp_ref, qm_ref, zm_ref, cos_ref, sa_ref, sb_ref,
                       kct_ref, vc_ref, hist_ref, mk_ref, mv_ref, gain_ref, lam_ref, pw_ref, ps_ref,
                       o_ref, krot_ref, bp_ref, bm_ref, *, ts, past, mask_cache, mask_new):
    pos = past + lax.broadcasted_iota(jnp.int32, (ts, 1), 0)
    bp_ref[...] = _pool_mix(u_ref[...], hist_ref[...], zp_ref[...], pw_ref, ps_ref, pos).astype(bp_ref.dtype)
    _mem_attend(qm_ref[...].astype(BF16), zm_ref[...], mk_ref, mv_ref, bm_ref)

    c, sa, sb = cos_ref[...], sa_ref[...], sb_ref[...]
    nt = (((1,), (1,)), ((), ()))
    hs = [slice(n * LANES, (n + 1) * LANES) for n in range(A_HEADS)]
    scores = []
    for n in range(A_HEADS):
        q = (_rotate_lanes(q_ref[:, hs[n]], c, sa, sb) * A_SCALE).astype(BF16)
        k = _rotate_lanes(k_ref[:, hs[n]], c, sa, sb)
        krot_ref[:, hs[n]] = k
        qq = _split_components(q)
        s_c = jnp.dot(qq, kct_ref[n].astype(BF16), preferred_element_type=F32)
        s_n = lax.dot_general(qq, k.astype(BF16), nt, preferred_element_type=F32)
        if mask_cache:
            s_c = _chunk_mask(s_c, ts, past, 0)
        if mask_new:
            s_n = _chunk_mask(s_n, ts, past, past)
        scores.append((s_c, s_n))
    lam, lam_init = _lambda_values(lam_ref)
    gain = gain_ref[...] * (1.0 - lam_init)
    for n, (s_c, s_n) in enumerate(scores):
        m = jnp.maximum(jnp.max(s_c, axis=1, keepdims=True), jnp.max(s_n, axis=1, keepdims=True))
        p_c = jnp.exp(s_c - m)
        p_n = jnp.exp(s_n - m)
        l = jnp.sum(p_c, axis=1, keepdims=True) + jnp.sum(p_n, axis=1, keepdims=True)
        vc = vc_ref[pl.ds(n, past, stride=A_HEADS), :].astype(BF16)
        acc = (jnp.dot(p_c.astype(BF16), vc, preferred_element_type=F32)
               + jnp.dot(p_n.astype(BF16), v_ref[:, hs[n]].astype(BF16), preferred_element_type=F32))
        out = _attn_finish(acc, l, ts, lam, gain, za_ref[:, hs[n]].astype(F32))
        o_ref[:, hs[n]] = out.astype(o_ref.dtype)


def _needs_mask(q_pos, k_pos):
    return not bool(((k_pos[None, :] // CHUNK) <= (q_pos[:, None] // CHUNK)).all())


def _mix_sample(act, rot, kct, vc, hist, mk, mv, gain, lamv, pool_w, pool_scale, batch, ts, past):
    q_pos = past + np.arange(ts)
    seg = lambda col: pl.BlockSpec((ts, A_WIDTH), lambda b: (b, col // A_WIDTH))
    tab = pl.BlockSpec((ts, LANES), lambda b: (0, 0))
    row = pl.BlockSpec((ts, A_WIDTH), lambda b: (b, 0))
    mem = pl.BlockSpec((M_TOKENS, M_WIDTH), lambda b: (b, 0))
    bshape = jax.ShapeDtypeStruct((batch * ts, A_WIDTH), BF16)
    return pl.pallas_call(
        functools.partial(_mix_sample_kernel, ts=ts, past=past,
                          mask_cache=_needs_mask(q_pos, np.arange(past)),
                          mask_new=_needs_mask(q_pos, q_pos)),
        out_shape=[bshape, jax.ShapeDtypeStruct((batch * ts, A_WIDTH), F32), bshape, bshape],
        grid=(batch,),
        in_specs=[
            seg(COL_Q), seg(COL_K), seg(COL_V), seg(COL_ZA), seg(COL_UP), seg(COL_ZP), seg(COL_QM), seg(COL_ZM),
            tab, tab, tab,
            pl.BlockSpec((None, A_HEADS, LANES, past), lambda b: (b, 0, 0, 0)),
            pl.BlockSpec((None, past * A_HEADS, LANES), lambda b: (b, 0, 0)),
            pl.BlockSpec((None, hist.shape[1], P_WIDTH), lambda b: (b, 0, 0)),
            mem, mem,
            pl.BlockSpec((1, LANES), lambda b: (0, 0)),
            pl.BlockSpec((5, A_QK_DIM), lambda b: (0, 0)),
            pl.BlockSpec((len(P_WINDOWS), P_GROUP_DIM, P_GROUP_DIM), lambda b: (0, 0, 0)),
            pl.BlockSpec((1, P_WIDTH), lambda b: (0, 0)),
        ],
        out_specs=[row, row, row, row],
        compiler_params=_params("arbitrary"),
        name="mix_sample",
    )(*([act] * 8), *rot, kct, vc, hist, mk, mv, gain, lamv, pool_w, pool_scale)


def _pool_mix(u, prev, zp, pw_ref, ps_ref, pos):
    tm = u.shape[0]
    ext = jnp.concatenate([prev, u], axis=0)
    hist_rows = prev.shape[0]
    parts = []
    for g, w in enumerate(P_WINDOWS):
        sl = slice(g * P_GROUP_DIM, (g + 1) * P_GROUP_DIM)
        tot = ext[:, sl]
        span = 1
        while span < w:
            tot = tot + pltpu.roll(tot, span, 0)
            span *= 2
        tot = tot[hist_rows:]
        cnt = jnp.minimum(pos + 1, w).astype(F32)
        pooled = tot / cnt - u[:, sl]
        parts.append(jnp.dot(pooled.astype(BF16), pw_ref[g].astype(BF16), preferred_element_type=F32))
    mixed = jnp.concatenate(parts, axis=1) * ps_ref[...]
    return mixed * _silu(zp)


def _mem_attend(q, zm, mk_ref, mv_ref, o_ref):
    nt = (((1,), (1,)), ((), ()))
    hs = [slice(n * M_HEAD_DIM, (n + 1) * M_HEAD_DIM) for n in range(M_HEADS)]
    scores = [lax.dot_general(q[:, sl], mk_ref[:, sl].astype(BF16), nt,
                              preferred_element_type=F32) * M_SCALE for sl in hs]
    for sl, s in zip(hs, scores):
        m = jnp.max(s, axis=1, keepdims=True)
        p = jnp.exp(s - m)
        l = jnp.sum(p, axis=1, keepdims=True)
        o = jnp.dot(p.astype(BF16), mv_ref[:, sl].astype(BF16), preferred_element_type=F32) / l
        o_ref[:, sl] = (o * _silu(zm[:, sl])).astype(o_ref.dtype)


def _proj_mem_kernel(x_ref, wq_ref, wz_ref, mk_ref, mv_ref, o_ref, wqb_ref, wzb_ref):
    @pl.when((pl.program_id(0) == 0) & (pl.program_id(1) == 0))
    def _():
        wqb_ref[...] = wq_ref[...].astype(BF16)
        wzb_ref[...] = wz_ref[...].astype(BF16)

    xb = x_ref[...]
    q = jnp.dot(xb, wqb_ref[...], preferred_element_type=F32).astype(BF16)
    zm = jnp.dot(xb, wzb_ref[...], preferred_element_type=F32)
    _mem_attend(q, zm, mk_ref, mv_ref, o_ref)


def _proj_mem(xb, w, mk, mv, batch, t, tm):
    n, kdim = xb.shape
    nt = t // tm
    wspec = lambda col: pl.BlockSpec((kdim, M_WIDTH), lambda b, i: (0, col // M_WIDTH),
                                     pipeline_mode=pl.Buffered(1))
    mem = pl.BlockSpec((M_TOKENS, M_WIDTH), lambda b, i: (b, 0))
    return pl.pallas_call(
        _proj_mem_kernel,
        out_shape=jax.ShapeDtypeStruct((n, M_WIDTH), BF16),
        grid=(batch, nt),
        in_specs=[pl.BlockSpec((tm, kdim), lambda b, i: (b * nt + i, 0)), wspec(COL_QM), wspec(COL_ZM), mem, mem],
        out_specs=pl.BlockSpec((tm, M_WIDTH), lambda b, i: (b * nt + i, 0)),
        scratch_shapes=[pltpu.VMEM((kdim, M_WIDTH), BF16), pltpu.VMEM((kdim, M_WIDTH), BF16)],
        compiler_params=_params("arbitrary", "arbitrary"),
        name="proj_mem",
    )(xb, w, w, mk, mv)


def _merge_kernel(ba_ref, bp_ref, bm_ref, g0_ref, g1_ref, g2_ref, wb_ref, o_ref, wbb_ref, *, halves):
    @pl.when(pl.program_id(1) == 0)
    def _():
        wbb_ref[...] = wb_ref[...].astype(BF16)

    branches = ((ba_ref, g0_ref), (bp_ref, g1_ref), (bm_ref, g2_ref))
    rows = o_ref.shape[0] // halves
    projs = [[jnp.dot(b_ref[r * rows:(r + 1) * rows, :], wbb_ref[n], preferred_element_type=F32)
              for n, (b_ref, _) in enumerate(branches)] for r in range(halves)]
    for r in range(halves):
        sl = slice(r * rows, (r + 1) * rows)
        acc = None
        for n, (_, g_ref) in enumerate(branches):
            term = _sigmoid(g_ref[sl, :].astype(F32)) * projs[r][n]
            acc = term if acc is None else acc + term
        o_ref[sl, :] = acc.astype(o_ref.dtype)


def _merge(ba, bp, bm, h, g_col, wb, tm, tn):
    n = ba.shape[0]
    width = ba.shape[1]
    gspec = lambda k: pl.BlockSpec((tm, tn), lambda j, i: (i, (g_col + k * D_MODEL) // tn + j))
    bspec = pl.BlockSpec((tm, width), lambda j, i: (i, 0))
    return pl.pallas_call(
        functools.partial(_merge_kernel, halves=2 if tm % 32 == 0 else 1),
        out_shape=jax.ShapeDtypeStruct((n, D_MODEL), BF16),
        grid=(D_MODEL // tn, n // tm),
        in_specs=[bspec, bspec, bspec, gspec(0), gspec(1), gspec(2),
                  pl.BlockSpec((N_BRANCH, width, tn), lambda j, i: (0, 0, j))],
        out_specs=pl.BlockSpec((tm, tn), lambda j, i: (i, j)),
        scratch_shapes=[pltpu.VMEM((N_BRANCH, width, tn), BF16)],
        compiler_params=_params("arbitrary", "arbitrary"),
        name="merge",
    )(ba, bp, bm, h, h, h, wb)


def _out_kernel(m_ref, w_ref, x_ref, g_ref, b_ref, o_ref, wb_ref, *, halves, alpha):
    @pl.when(pl.program_id(0) == 0)
    def _():
        wb_ref[...] = w_ref[...].astype(BF16)

    rows = o_ref.shape[0] // halves
    outs = [jnp.dot(m_ref[r * rows:(r + 1) * rows, :], wb_ref[...], preferred_element_type=F32)
            for r in range(halves)]
    for r in range(halves):
        sl = slice(r * rows, (r + 1) * rows)
        z = alpha * x_ref[sl, :] + outs[r]
        mu = jnp.mean(z, axis=1, keepdims=True)
        zc = z - mu
        var = jnp.mean(zc * zc, axis=1, keepdims=True)
        o_ref[sl, :] = zc * lax.rsqrt(var + LN_EPS) * g_ref[...] + b_ref[...]


def _out(merged, w_out, x, ln_g, ln_b, alpha, tm):
    n = merged.shape[0]
    row = pl.BlockSpec((tm, D_MODEL), lambda i: (i, 0))
    vec = pl.BlockSpec((1, D_MODEL), lambda i: (0, 0))
    wspec = pl.BlockSpec((D_MODEL, D_MODEL), lambda i: (0, 0), pipeline_mode=pl.Buffered(1))
    return pl.pallas_call(
        functools.partial(_out_kernel, halves=4 if tm % 64 == 0 else 1, alpha=alpha),
        out_shape=jax.ShapeDtypeStruct((n, D_MODEL), F32),
        grid=(n // tm,),
        in_specs=[row, wspec, row, vec, vec],
        out_specs=row,
        scratch_shapes=[pltpu.VMEM((D_MODEL, D_MODEL), BF16)],
        compiler_params=_params("arbitrary"),
        name="out_ln",
    )(merged, w_out, x, ln_g, ln_b)


def _rest_col_block(tn):
    za_blocks = (COL_UP - COL_ZA) // tn
    return lambda j: jnp.where(j < za_blocks, COL_ZA // tn + j, COL_GM // tn + (j - za_blocks))


def _layer_prompt(x2d, batch, t, w, pool_hist, mk, mv):
    ang = _rotary_angles(jnp.arange(t))
    cos_t, sin_t = jnp.cos(ang).T, jnp.sin(ang).T
    tm, tn, tp = 2048, 1024, 512
    qt, kt, kb, xb = _proj_qk(x2d, w["w_in"], cos_t, sin_t, batch, t, tp, 256, A_SCALE * math.log2(math.e))
    v, vt, u, b_p = _proj_vup(xb, w["w_in"], pool_hist, w["pool_w"], w["pool_scale"], batch, t, tp, 256)
    b_m = _proj_mem(xb, w["w_in"], mk, mv, batch, t, tp)
    h = _proj(xb, w["w_in"], _rest_col_block(tn), H_COLS, BF16, tm, tn, name="proj_rest")
    b_a = _attn_prompt(qt, kb, vt, h, w["gain"], w["lamv"], batch, t)
    merged = _merge(b_a, b_p, b_m, h, H_GM, w["w_branch"], 512, 1024)
    y = _out(merged, w["w_out"], x2d, w["ln_g"], w["ln_b"], w["alpha"], 512)
    return y, kt, v, u


def _layer_sample(x2d, batch, t, past, w, kt_hist, v_hist, pool_hist, mk, mv):
    n = batch * t
    rot = _rotary_tables(past + jnp.arange(t))
    act = _proj(x2d, w["w_in"], lambda j: j, IN_COLS, F32, n, 1024, name="proj_all")
    b_a, k, b_p, b_m = _mix_sample(act, rot, kt_hist, v_hist, pool_hist, mk, mv, w["gain"], w["lamv"],
                                   w["pool_w"], w["pool_scale"], batch, t, past)
    merged = _merge(b_a, b_p, b_m, act, COL_GM, w["w_branch"], n, 1024)
    y = _out(merged, w["w_out"], x2d, w["ln_g"], w["ln_b"], w["alpha"], n)
    return y, k, act


def kernel(x_prompt, x_sample, cache_attn_k, cache_attn_v, cache_mem_k, cache_mem_v, state_pool, mem_prompt, w_in, w_mem_kv, lambda_q1, lambda_k1, lambda_q2, lambda_k2, subln_gain, pool_w, pool_scale, w_branch, w_out, ln_gain, ln_bias):
    bp, tp, d = x_prompt.shape
    bs, ts, _ = x_sample.shape
    past = cache_attn_k.shape[2]
    hist_rows = P_HIST + 1

    def run_layer(l, xp2d, xs2d):
        w = {
            "w_in": w_in[l],
            "gain": subln_gain[l].reshape(1, A_V_DIM),
            "lamv": jnp.stack([lambda_q1[l], lambda_k1[l], lambda_q2[l], lambda_k2[l],
                               jnp.full((A_QK_DIM,), 0.8 - 0.6 * math.exp(-0.3 * l), F32)]),
            "pool_w": pool_w[l],
            "pool_scale": pool_scale[l].reshape(1, P_WIDTH),
            "w_branch": w_branch[l],
            "w_out": w_out[l],
            "ln_g": ln_gain[l].reshape(1, d),
            "ln_b": ln_bias[l].reshape(1, d),
            "alpha": (2.0 * w_in.shape[0]) ** 0.25,
        }
        memf = mem_prompt.reshape(bp * M_TOKENS, d)
        mk, mv = _proj_memkv(memf, w_mem_kv[l], 512)

        zero_hist = jnp.zeros((bp, hist_rows, P_WIDTH), F32)
        y_p, k_p, v_p, u_p = _layer_prompt(xp2d, bp, tp, w, zero_hist, mk, mv)

        hist_s = jnp.pad(state_pool[l], ((0, 0), (hist_rows - P_HIST, 0), (0, 0)))
        kt_hist = jnp.transpose(cache_attn_k[l], (0, 2, 3, 4, 1)).reshape(bs, A_HEADS, 2 * A_QK_DIM, past)
        y_s, k_s, act_s = _layer_sample(
            xs2d, bs, ts, past, w,
            kt_hist, cache_attn_v[l].reshape(bs, past * A_HEADS, A_V_DIM),
            hist_s, cache_mem_k[l].reshape(bs * M_TOKENS, M_WIDTH), cache_mem_v[l].reshape(bs * M_TOKENS, M_WIDTH))
        v_s = act_s[:, COL_V:COL_V + A_WIDTH]
        u_s = act_s[:, COL_UP:COL_UP + P_WIDTH]

        pool_p = jnp.concatenate([zero_hist, u_p.reshape(bp, tp, P_WIDTH)], axis=1)[:, -P_HIST:]
        pool_s = jnp.concatenate([hist_s, u_s.reshape(bs, ts, P_WIDTH)], axis=1)[:, -P_HIST:]
        k_p = jnp.transpose(k_p.reshape(bp, A_HEADS, 2, A_QK_DIM, tp), (0, 4, 1, 2, 3))
        outs = (k_p, v_p.reshape(bp, tp, A_HEADS, A_V_DIM),
                mk.reshape(bp, M_TOKENS, M_HEADS, M_HEAD_DIM), mv.reshape(bp, M_TOKENS, M_HEADS, M_HEAD_DIM),
                pool_p,
                k_s.reshape(bs, ts, A_HEADS, 2, A_QK_DIM), v_s.reshape(bs, ts, A_HEADS, A_V_DIM), pool_s)
        return y_p, y_s, outs

    yp = x_prompt.reshape(bp * tp, d)
    ys = x_sample.reshape(bs * ts, d)
    per_layer = []
    for l in range(w_in.shape[0]):
        yp, ys, outs = run_layer(l, yp, ys)
        per_layer.append(outs)
    stacked = [jnp.stack([o[n] for o in per_layer]) for n in range(8)]
    return (yp.reshape(bp, tp, d), ys.reshape(bs, ts, d), *stacked)
```

```python
import functools
import math

import numpy as np
import jax
import jax.numpy as jnp
from jax import lax
from jax.experimental import pallas as pl
from jax.experimental.pallas import tpu as pltpu

F32 = jnp.float32
BF16 = jnp.bfloat16

V7X_VMEM_LIMIT_BYTES = 56 * 1024 * 1024
LANES = 128

D_MODEL = 2048
CHUNK = 64
A_HEADS = 8
A_QK_DIM = 64
A_V_DIM = 128
A_WIDTH = A_HEADS * A_V_DIM
A_SCALE = A_QK_DIM ** -0.5
ROT_DIM = A_QK_DIM // 4
ROPE_THETA = 500000.0
SUBLN_EPS = 1e-5
P_WINDOWS = (2, 4, 8, 16)
P_GROUP_DIM = 256
P_WIDTH = len(P_WINDOWS) * P_GROUP_DIM
P_HIST = max(P_WINDOWS) - 1
M_TOKENS = 256
M_HEADS = 4
M_HEAD_DIM = 256
M_WIDTH = M_HEADS * M_HEAD_DIM
M_SCALE = M_HEAD_DIM ** -0.5
MEM_ROWS = M_TOKENS * M_HEADS * (M_HEAD_DIM // LANES)
N_BRANCH = 3
LN_EPS = 1e-5
ONES_ROWS = 16

COL_Q, COL_K, COL_V, COL_ZA, COL_UP, COL_ZP, COL_QM, COL_ZM, COL_GM = (
    0, 1024, 2048, 3072, 4096, 5120, 6144, 7168, 8192)
IN_COLS = 14336
H_ZA, H_GM = 0, 1024
H_COLS = 7168


def _params(*sem):
    return pltpu.CompilerParams(dimension_semantics=sem, vmem_limit_bytes=V7X_VMEM_LIMIT_BYTES)


def _sigmoid(z):
    return 1.0 / (1.0 + jnp.exp(-z))


def _silu(z):
    return z * _sigmoid(z)


def _proj_kernel(x_ref, w_ref, o_ref):
    acc = jnp.dot(x_ref[...].astype(BF16), w_ref[...].astype(BF16), preferred_element_type=F32)
    o_ref[...] = acc.astype(o_ref.dtype)


def _proj(x, w, col_block, ncols, out_dtype, tm, tn, name="proj"):
    n, kdim = x.shape
    return pl.pallas_call(
        _proj_kernel,
        out_shape=jax.ShapeDtypeStruct((n, ncols), out_dtype),
        grid=(n // tm, ncols // tn),
        in_specs=[pl.BlockSpec((tm, kdim), lambda i, j: (i, 0)),
                  pl.BlockSpec((kdim, tn), lambda i, j: (0, col_block(j)))],
        out_specs=pl.BlockSpec((tm, tn), lambda i, j: (i, j)),
        compiler_params=_params("arbitrary", "arbitrary"),
        name=name,
    )(x, w)


def _proj_memkv_kernel(x_ref, w_ref, mk_ref, mv_ref):
    kv = jnp.dot(x_ref[...].astype(BF16), w_ref[...].astype(BF16), preferred_element_type=F32)
    tm = kv.shape[0]
    halves = M_HEAD_DIM // LANES
    group = halves * M_HEADS

    def store(o_ref):
        for h in range(M_HEADS):
            for c in range(halves):
                col = h * M_HEAD_DIM + c * LANES
                o_ref[pl.ds(c * M_HEADS + h, tm, stride=group), :] = kv[:, col:col + LANES]

    @pl.when(pl.program_id(1) == 0)
    def _():
        store(mk_ref)

    @pl.when(pl.program_id(1) == 1)
    def _():
        store(mv_ref)


def _proj_memkv(mem, w, tm):
    n, kdim = mem.shape
    group = M_WIDTH // LANES
    row = pl.BlockSpec((tm * group, LANES), lambda i, j: (i, 0))
    return pl.pallas_call(
        _proj_memkv_kernel,
        out_shape=[jax.ShapeDtypeStruct((n * group, LANES), F32)] * 2,
        grid=(n // tm, 2),
        in_specs=[pl.BlockSpec((tm, kdim), lambda i, j: (i, 0)),
                  pl.BlockSpec((kdim, M_WIDTH), lambda i, j: (0, j))],
        out_specs=[row, row],
        compiler_params=_params("arbitrary", "arbitrary"),
        name="proj_memkv",
    )(mem, w)


def _rotate_sublanes(acc, c, s):
    half = ROT_DIM // 2
    pieces = []
    for g in range(acc.shape[0] // A_QK_DIM):
        b = g * A_QK_DIM
        lo, hi = acc[b:b + half], acc[b + half:b + ROT_DIM]
        pieces += [lo * c - hi * s, hi * c + lo * s, acc[b + ROT_DIM:b + A_QK_DIM]]
    return jnp.concatenate(pieces, axis=0)


def _proj_qk_kernel(x_ref, wq_ref, wk_ref, cos_ref, sin_ref, qt_ref, kt_ref, kb_ref, xb_ref, *, q_scale):
    xb = x_ref[...].astype(BF16)
    xb_ref[...] = xb
    c, s = cos_ref[...], sin_ref[...]
    q = _rotate_sublanes(jnp.dot(xb, wq_ref[...].astype(BF16), preferred_element_type=F32).T, c, s)
    q = (q * q_scale).astype(qt_ref.dtype)
    tile = qt_ref.shape[-1]
    for n in range(qt_ref.shape[0]):
        qt_ref[n] = q[:, n * tile:(n + 1) * tile]
    k = _rotate_sublanes(jnp.dot(xb, wk_ref[...].astype(BF16), preferred_element_type=F32).T, c, s)
    kt_ref[...] = k
    kb_ref[...] = k.T.astype(kb_ref.dtype)


def _proj_qk(x, w, cos_t, sin_t, batch, t, tm, q_tile, q_scale):
    n, kdim = x.shape
    width = A_WIDTH
    nt = t // tm
    tab = pl.BlockSpec((ROT_DIM // 2, tm), lambda i: (0, i % nt))
    wspec = lambda col: pl.BlockSpec((kdim, width), lambda i: (0, col // width), pipeline_mode=pl.Buffered(1))
    tspec = pl.BlockSpec((None, width, tm), lambda i: (i // nt, 0, i % nt))
    qspec = pl.BlockSpec((None, tm // q_tile, width, q_tile), lambda i: (i // nt, i % nt, 0, 0))
    return pl.pallas_call(
        functools.partial(_proj_qk_kernel, q_scale=q_scale),
        out_shape=[jax.ShapeDtypeStruct((batch, t // q_tile, width, q_tile), BF16),
                   jax.ShapeDtypeStruct((batch, width, t), F32),
                   jax.ShapeDtypeStruct((n, width), BF16), jax.ShapeDtypeStruct((n, kdim), BF16)],
        grid=(n // tm,),
        in_specs=[pl.BlockSpec((tm, kdim), lambda i: (i, 0)), wspec(COL_Q), wspec(COL_K), tab, tab],
        out_specs=[qspec, tspec, pl.BlockSpec((tm, width), lambda i: (i, 0)),
                   pl.BlockSpec((tm, kdim), lambda i: (i, 0))],
        compiler_params=_params("arbitrary"),
        name="proj_qk",
    )(x, w, w, cos_t, sin_t)


def _proj_vup_kernel(x_ref, wv_ref, wu_ref, wz_ref, hist_ref, pw_ref, ps_ref, v_ref, vt_ref, u_ref, bp_ref,
                     prev_ref, *, key_tile, nt):
    i = pl.program_id(0) % nt

    @pl.when(i == 0)
    def _():
        prev_ref[...] = hist_ref[...]

    xb = x_ref[...]
    tm = xb.shape[0]
    u = jnp.dot(xb, wu_ref[...].astype(BF16), preferred_element_type=F32)
    u_ref[...] = u
    zp = jnp.dot(xb, wz_ref[...].astype(BF16), preferred_element_type=F32)
    v = jnp.dot(xb, wv_ref[...].astype(BF16), preferred_element_type=F32)
    for h in range(A_HEADS):
        v_ref[pl.ds(h, tm, stride=A_HEADS), :] = v[:, h * A_V_DIM:(h + 1) * A_V_DIM]
    for c in range(tm // key_tile):
        vt_ref[c] = v[c * key_tile:(c + 1) * key_tile, :].T.astype(vt_ref.dtype)
    pos = i * tm + lax.broadcasted_iota(jnp.int32, (tm, 1), 0)
    bp_ref[...] = _pool_mix(u, prev_ref[...], zp, pw_ref, ps_ref, pos).astype(bp_ref.dtype)
    prev_ref[...] = u[tm - prev_ref.shape[0]:, :]


def _proj_vup(xb, w, hist, pool_w, pool_scale, batch, t, tm, key_tile):
    n, kdim = xb.shape
    width = A_WIDTH
    nt = t // tm
    hr = hist.shape[1]
    wspec = lambda col: pl.BlockSpec((kdim, width), lambda i: (0, col // width), pipeline_mode=pl.Buffered(1))
    row = pl.BlockSpec((tm, width), lambda i: (i, 0))
    return pl.pallas_call(
        functools.partial(_proj_vup_kernel, key_tile=key_tile, nt=nt),
        out_shape=[jax.ShapeDtypeStruct((n * A_HEADS, A_V_DIM), F32),
                   jax.ShapeDtypeStruct((batch, t // key_tile, width, key_tile), BF16),
                   jax.ShapeDtypeStruct((n, width), F32),
                   jax.ShapeDtypeStruct((n, width), BF16)],
        grid=(n // tm,),
        in_specs=[pl.BlockSpec((tm, kdim), lambda i: (i, 0)), wspec(COL_V), wspec(COL_UP), wspec(COL_ZP),
                  pl.BlockSpec((None, hr, P_WIDTH), lambda i: (i // nt, 0, 0)),
                  pl.BlockSpec((len(P_WINDOWS), P_GROUP_DIM, P_GROUP_DIM), lambda i: (0, 0, 0)),
                  pl.BlockSpec((1, P_WIDTH), lambda i: (0, 0))],
        out_specs=[pl.BlockSpec((tm * A_HEADS, A_V_DIM), lambda i: (i, 0)),
                   pl.BlockSpec((None, tm // key_tile, width, key_tile), lambda i: (i // nt, i % nt, 0, 0)),
                   row, row],
        scratch_shapes=[pltpu.VMEM((hr, P_WIDTH), F32)],
        compiler_params=_params("arbitrary"),
        name="proj_vup",
    )(xb, w, w, w, hist, pool_w, pool_scale)


def _rotary_angles(pos):
    half = ROT_DIM // 2
    inv = ROPE_THETA ** (-(jnp.arange(half, dtype=F32) * 2.0) / ROT_DIM)
    return pos.astype(F32)[:, None] * inv[None, :]


def _rotary_tables(pos):
    half = ROT_DIM // 2
    ang = _rotary_angles(pos)
    cos, sin = jnp.cos(ang), jnp.sin(ang)
    t = pos.shape[0]
    ones = jnp.ones((t, A_QK_DIM - ROT_DIM), F32)
    zeros = jnp.zeros((t, A_QK_DIM - ROT_DIM), F32)
    zh = jnp.zeros((t, half), F32)
    c64 = jnp.concatenate([cos, cos, ones], axis=1)
    sa64 = jnp.concatenate([zh, sin, zeros], axis=1)
    sb64 = jnp.concatenate([-sin, zh, zeros], axis=1)
    tile = lambda a: jnp.concatenate([a, a], axis=1)
    return tile(c64), tile(sa64), tile(sb64)


def _lambda_values(lam_ref):
    lv = lam_ref[...]
    a = jnp.sum(lv[0:1] * lv[1:2], axis=1, keepdims=True)
    b = jnp.sum(lv[2:3] * lv[3:4], axis=1, keepdims=True)
    lam_init = lv[4:5, 0:1]
    return jnp.exp(a) - jnp.exp(b) + lam_init, lam_init


def _split_components(q):
    lane = lax.broadcasted_iota(jnp.int32, q.shape, 1)
    qf = q.astype(F32)
    return jnp.concatenate([jnp.where(lane < A_QK_DIM, qf, 0.0),
                            jnp.where(lane >= A_QK_DIM, qf, 0.0)], axis=0).astype(q.dtype)


def _attn_finish(acc, l, tq, lam, gain, za):
    o = acc[:tq] / l[:tq] - lam * (acc[tq:] / l[tq:])
    o = o * lax.rsqrt(jnp.mean(o * o, axis=1, keepdims=True) + SUBLN_EPS) * gain
    return o * _silu(za)


def _attn_prompt_kernel(qt_ref, k_ref, vt_ref, za_ref, gain_ref, lam_ref, o_ref, m_ref, l_ref, acc_ref,
                        s0_ref, s1_ref, *,
                        tq, heads):
    i = pl.program_id(2)
    hs = [slice(n * LANES, (n + 1) * LANES) for n in range(heads)]
    qqts = []
    for sl in hs:
        qt = qt_ref[sl, :].astype(F32)
        row = lax.broadcasted_iota(jnp.int32, qt.shape, 0)
        qqts.append(jnp.concatenate([jnp.where(row < A_QK_DIM, qt, 0.0),
                                     jnp.where(row >= A_QK_DIM, qt, 0.0)], axis=1).astype(BF16))

    m_ref[...] = jnp.full(m_ref.shape, -jnp.inf, F32)
    l_ref[...] = jnp.zeros(l_ref.shape, F32)
    acc_ref[...] = jnp.zeros(acc_ref.shape, F32)

    def score(j, n):
        start = pl.multiple_of(j * tq, tq)
        return jnp.dot(k_ref[pl.ds(start, tq), hs[n]], qqts[n], preferred_element_type=F32)

    ones_rows = jnp.ones((ONES_ROWS, tq), BF16)

    def step(j, cur_ref, nxt_ref, masked):
        for n in range(heads):
            s = cur_ref[n]
            if masked:
                krow = lax.broadcasted_iota(jnp.int32, s.shape, 0)
                qcol = lax.broadcasted_iota(jnp.int32, s.shape, 1) % tq
                s = jnp.where((krow // CHUNK) <= (qcol // CHUNK), s, -jnp.inf)
            m = m_ref[n]
            m_new = jnp.maximum(m, jnp.max(s, axis=0, keepdims=True))
            alpha = jnp.exp2(m - m_new)
            p = jnp.exp2(s - m_new).astype(BF16)
            m_ref[n] = m_new
            lhs = jnp.concatenate([vt_ref[j, hs[n], :], ones_rows], axis=0)
            pv = jnp.dot(lhs, p, preferred_element_type=F32)
            acc_ref[n] = alpha * acc_ref[n] + pv[:A_V_DIM]
            l_ref[n] = alpha * l_ref[n] + pv[A_V_DIM:A_V_DIM + 1]
            if nxt_ref is not None:
                nxt_ref[n] = score(j + 1, n)

    for n in range(heads):
        s0_ref[n] = score(0, n)

    def pair(jj, carry):
        step(2 * jj, s0_ref, s1_ref, False)
        step(2 * jj + 1, s1_ref, s0_ref, False)
        return carry

    lax.fori_loop(0, i // 2, pair, 0)

    @pl.when(i % 2 == 1)
    def _():
        step(i - 1, s0_ref, s1_ref, False)
        step(i, s1_ref, None, True)

    @pl.when(i % 2 == 0)
    def _():
        step(i, s0_ref, None, True)
    lam, lam_init = _lambda_values(lam_ref)
    gain = gain_ref[...] * (1.0 - lam_init)
    for n in range(heads):
        inv, acc = 1.0 / l_ref[n], acc_ref[n]
        ot = acc[:, :tq] * inv[:, :tq] - lam * (acc[:, tq:] * inv[:, tq:])
        o = ot.T
        o = o * lax.rsqrt(jnp.mean(o * o, axis=1, keepdims=True) + SUBLN_EPS) * gain
        o_ref[:, hs[n]] = (o * _silu(za_ref[:, hs[n]].astype(F32))).astype(o_ref.dtype)


def _attn_prompt(qt, k, vt, h, gain, lamv, batch, t, tq=256, heads=8):
    nq = t // tq
    n = batch * t
    hw = heads * LANES
    return pl.pallas_call(
        functools.partial(_attn_prompt_kernel, tq=tq, heads=heads),
        out_shape=jax.ShapeDtypeStruct((n, A_WIDTH), BF16),
        grid=(batch, A_HEADS // heads, nq),
        in_specs=[
            pl.BlockSpec((None, None, hw, tq), lambda b, hh, i: (b, i, hh, 0)),
            pl.BlockSpec((t, hw), lambda b, hh, i: (b, hh)),
            pl.BlockSpec((None, nq, hw, tq), lambda b, hh, i: (b, 0, hh, 0)),
            pl.BlockSpec((tq, hw), lambda b, hh, i: (b * nq + i, H_ZA // hw + hh)),
            pl.BlockSpec((1, LANES), lambda b, hh, i: (0, 0)),
            pl.BlockSpec((5, A_QK_DIM), lambda b, hh, i: (0, 0)),
        ],
        out_specs=pl.BlockSpec((tq, hw), lambda b, hh, i: (b * nq + i, hh)),
        scratch_shapes=[pltpu.VMEM((heads, 1, 2 * tq), F32), pltpu.VMEM((heads, 1, 2 * tq), F32),
                        pltpu.VMEM((heads, A_V_DIM, 2 * tq), F32),
                        pltpu.VMEM((heads, tq, 2 * tq), F32), pltpu.VMEM((heads, tq, 2 * tq), F32)],
        compiler_params=_params("arbitrary", "arbitrary", "arbitrary"),
        name="attn_prompt",
    )(qt, k, vt, h, gain, lamv)


def _chunk_mask(s, ts, q_pos0, k_pos0):
    q_pos = q_pos0 + lax.broadcasted_iota(jnp.int32, s.shape, 0) % ts
    k_pos = k_pos0 + lax.broadcasted_iota(jnp.int32, s.shape, 1)
    return jnp.where((k_pos // CHUNK) <= (q_pos // CHUNK), s, -jnp.inf)


def _rotate_lanes(blk, c, sa, sb):
    half = ROT_DIM // 2
    return blk * c + pltpu.roll(blk, half, 1) * sa + pltpu.roll(blk, LANES - half, 1) * sb


def _mix_sample_kernel(q_ref, k_ref, v_ref, za_ref, u_ref, zp_ref, qm_ref, zm_ref, cos_ref, sa_ref, sb_ref,
                       kct_ref, vc_ref, hist_ref, mk_ref, mv_ref, gain_ref, lam_ref, pw_ref, ps_ref,
                       o_ref, krot_ref, bp_ref, bm_ref, *, ts, past, mask_cache, mask_new):
    pos = past + lax.broadcasted_iota(jnp.int32, (ts, 1), 0)
    bp_ref[...] = _pool_mix(u_ref[...], hist_ref[...], zp_ref[...], pw_ref, ps_ref, pos).astype(bp_ref.dtype)
    _mem_attend(qm_ref[...].astype(BF16), zm_ref[...], mk_ref, mv_ref, bm_ref)

    c, sa, sb = cos_ref[...], sa_ref[...], sb_ref[...]
    nt = (((1,), (1,)), ((), ()))
    hs = [slice(n * LANES, (n + 1) * LANES) for n in range(A_HEADS)]
    scores = []
    for n in range(A_HEADS):
        q = (_rotate_lanes(q_ref[:, hs[n]], c, sa, sb) * A_SCALE).astype(BF16)
        k = _rotate_lanes(k_ref[:, hs[n]], c, sa, sb)
        krot_ref[:, hs[n]] = k
        qq = _split_components(q)
        s_c = jnp.dot(qq, kct_ref[n].astype(BF16), preferred_element_type=F32)
        s_n = lax.dot_general(qq, k.astype(BF16), nt, preferred_element_type=F32)
        if mask_cache:
            s_c = _chunk_mask(s_c, ts, past, 0)
        if mask_new:
            s_n = _chunk_mask(s_n, ts, past, past)
        scores.append((s_c, s_n))
    lam, lam_init = _lambda_values(lam_ref)
    gain = gain_ref[...] * (1.0 - lam_init)
    for n, (s_c, s_n) in enumerate(scores):
        m = jnp.maximum(jnp.max(s_c, axis=1, keepdims=True), jnp.max(s_n, axis=1, keepdims=True))
        p_c = jnp.exp(s_c - m)
        p_n = jnp.exp(s_n - m)
        l = jnp.sum(p_c, axis=1, keepdims=True) + jnp.sum(p_n, axis=1, keepdims=True)
        vc = vc_ref[pl.ds(n, past, stride=A_HEADS), :].astype(BF16)
        acc = (jnp.dot(p_c.astype(BF16), vc, preferred_element_type=F32)
               + jnp.dot(p_n.astype(BF16), v_ref[:, hs[n]].astype(BF16), preferred_element_type=F32))
        out = _attn_finish(acc, l, ts, lam, gain, za_ref[:, hs[n]].astype(F32))
        o_ref[:, hs[n]] = out.astype(o_ref.dtype)


def _needs_mask(q_pos, k_pos):
    return not bool(((k_pos[None, :] // CHUNK) <= (q_pos[:, None] // CHUNK)).all())


def _mix_sample(act, rot, kct, vc, hist, mk, mv, gain, lamv, pool_w, pool_scale, batch, ts, past):
    q_pos = past + np.arange(ts)
    seg = lambda col: pl.BlockSpec((ts, A_WIDTH), lambda b: (b, col // A_WIDTH))
    tab = pl.BlockSpec((ts, LANES), lambda b: (0, 0))
    row = pl.BlockSpec((ts, A_WIDTH), lambda b: (b, 0))
    mem = pl.BlockSpec((MEM_ROWS, LANES), lambda b: (b, 0))
    bshape = jax.ShapeDtypeStruct((batch * ts, A_WIDTH), BF16)
    return pl.pallas_call(
        functools.partial(_mix_sample_kernel, ts=ts, past=past,
                          mask_cache=_needs_mask(q_pos, np.arange(past)),
                          mask_new=_needs_mask(q_pos, q_pos)),
        out_shape=[bshape, jax.ShapeDtypeStruct((batch * ts, A_WIDTH), F32), bshape, bshape],
        grid=(batch,),
        in_specs=[
            seg(COL_Q), seg(COL_K), seg(COL_V), seg(COL_ZA), seg(COL_UP), seg(COL_ZP), seg(COL_QM), seg(COL_ZM),
            tab, tab, tab,
            pl.BlockSpec((None, A_HEADS, LANES, past), lambda b: (b, 0, 0, 0)),
            pl.BlockSpec((None, past * A_HEADS, LANES), lambda b: (b, 0, 0)),
            pl.BlockSpec((None, hist.shape[1], P_WIDTH), lambda b: (b, 0, 0)),
            mem, mem,
            pl.BlockSpec((1, LANES), lambda b: (0, 0)),
            pl.BlockSpec((5, A_QK_DIM), lambda b: (0, 0)),
            pl.BlockSpec((len(P_WINDOWS), P_GROUP_DIM, P_GROUP_DIM), lambda b: (0, 0, 0)),
            pl.BlockSpec((1, P_WIDTH), lambda b: (0, 0)),
        ],
        out_specs=[row, row, row, row],
        compiler_params=_params("arbitrary"),
        name="mix_sample",
    )(*([act] * 8), *rot, kct, vc, hist, mk, mv, gain, lamv, pool_w, pool_scale)


def _pool_mix(u, prev, zp, pw_ref, ps_ref, pos):
    tm = u.shape[0]
    ext = jnp.concatenate([prev, u], axis=0)
    hist_rows = prev.shape[0]
    parts = []
    for g, w in enumerate(P_WINDOWS):
        sl = slice(g * P_GROUP_DIM, (g + 1) * P_GROUP_DIM)
        tot = ext[:, sl]
        span = 1
        while span < w:
            tot = tot + pltpu.roll(tot, span, 0)
            span *= 2
        tot = tot[hist_rows:]
        cnt = jnp.minimum(pos + 1, w).astype(F32)
        pooled = tot / cnt - u[:, sl]
        parts.append(jnp.dot(pooled.astype(BF16), pw_ref[g].astype(BF16), preferred_element_type=F32))
    mixed = jnp.concatenate(parts, axis=1) * ps_ref[...]
    return mixed * _silu(zp)


def _mem_head(ref, h):
    halves = M_HEAD_DIM // LANES
    parts = [ref[pl.ds(c * M_HEADS + h, M_TOKENS, stride=halves * M_HEADS), :] for c in range(halves)]
    return jnp.concatenate(parts, axis=1)


def _mem_attend(q, zm, mk_ref, mv_ref, o_ref):
    nt = (((1,), (1,)), ((), ()))
    hs = [slice(n * M_HEAD_DIM, (n + 1) * M_HEAD_DIM) for n in range(M_HEADS)]
    scores = [lax.dot_general(q[:, sl], _mem_head(mk_ref, n).astype(BF16), nt,
                              preferred_element_type=F32) * M_SCALE for n, sl in enumerate(hs)]
    for n, (sl, s) in enumerate(zip(hs, scores)):
        m = jnp.max(s, axis=1, keepdims=True)
        p = jnp.exp(s - m)
        l = jnp.sum(p, axis=1, keepdims=True)
        o = jnp.dot(p.astype(BF16), _mem_head(mv_ref, n).astype(BF16), preferred_element_type=F32) / l
        o_ref[:, sl] = (o * _silu(zm[:, sl])).astype(o_ref.dtype)


def _proj_mem_kernel(x_ref, wq_ref, wz_ref, mk_ref, mv_ref, o_ref):
    xb = x_ref[...]
    q = jnp.dot(xb, wq_ref[...].astype(BF16), preferred_element_type=F32).astype(BF16)
    zm = jnp.dot(xb, wz_ref[...].astype(BF16), preferred_element_type=F32)
    _mem_attend(q, zm, mk_ref, mv_ref, o_ref)


def _proj_mem(xb, w, mk, mv, batch, t, tm):
    n, kdim = xb.shape
    nt = t // tm
    wspec = lambda col: pl.BlockSpec((kdim, M_WIDTH), lambda b, i: (0, col // M_WIDTH),
                                     pipeline_mode=pl.Buffered(1))
    mem = pl.BlockSpec((MEM_ROWS, LANES), lambda b, i: (b, 0))
    return pl.pallas_call(
        _proj_mem_kernel,
        out_shape=jax.ShapeDtypeStruct((n, M_WIDTH), BF16),
        grid=(batch, nt),
        in_specs=[pl.BlockSpec((tm, kdim), lambda b, i: (b * nt + i, 0)), wspec(COL_QM), wspec(COL_ZM), mem, mem],
        out_specs=pl.BlockSpec((tm, M_WIDTH), lambda b, i: (b * nt + i, 0)),
        compiler_params=_params("arbitrary", "arbitrary"),
        name="proj_mem",
    )(xb, w, w, mk, mv)


def _merge_kernel(ba_ref, bp_ref, bm_ref, g0_ref, g1_ref, g2_ref, wb_ref, o_ref, wbb_ref, *, halves):
    @pl.when(pl.program_id(1) == 0)
    def _():
        wbb_ref[...] = wb_ref[...].astype(BF16)

    branches = ((ba_ref, g0_ref), (bp_ref, g1_ref), (bm_ref, g2_ref))
    rows = o_ref.shape[0] // halves
    projs = [[jnp.dot(b_ref[r * rows:(r + 1) * rows, :], wbb_ref[n], preferred_element_type=F32)
              for n, (b_ref, _) in enumerate(branches)] for r in range(halves)]
    for r in range(halves):
        sl = slice(r * rows, (r + 1) * rows)
        acc = None
        for n, (_, g_ref) in enumerate(branches):
            term = _sigmoid(g_ref[sl, :].astype(F32)) * projs[r][n]
            acc = term if acc is None else acc + term
        o_ref[sl, :] = acc.astype(o_ref.dtype)


def _merge(ba, bp, bm, h, g_col, wb, tm, tn):
    n = ba.shape[0]
    width = ba.shape[1]
    gspec = lambda k: pl.BlockSpec((tm, tn), lambda j, i: (i, (g_col + k * D_MODEL) // tn + j))
    bspec = pl.BlockSpec((tm, width), lambda j, i: (i, 0))
    return pl.pallas_call(
        functools.partial(_merge_kernel, halves=2 if tm % 32 == 0 else 1),
        out_shape=jax.ShapeDtypeStruct((n, D_MODEL), BF16),
        grid=(D_MODEL // tn, n // tm),
        in_specs=[bspec, bspec, bspec, gspec(0), gspec(1), gspec(2),
                  pl.BlockSpec((N_BRANCH, width, tn), lambda j, i: (0, 0, j))],
        out_specs=pl.BlockSpec((tm, tn), lambda j, i: (i, j)),
        scratch_shapes=[pltpu.VMEM((N_BRANCH, width, tn), BF16)],
        compiler_params=_params("arbitrary", "arbitrary"),
        name="merge",
    )(ba, bp, bm, h, h, h, wb)


def _out_kernel(m_ref, w_ref, x_ref, g_ref, b_ref, o_ref, wb_ref, *, halves, alpha):
    @pl.when(pl.program_id(0) == 0)
    def _():
        wb_ref[...] = w_ref[...].astype(BF16)

    rows = o_ref.shape[0] // halves
    outs = [jnp.dot(m_ref[r * rows:(r + 1) * rows, :], wb_ref[...], preferred_element_type=F32)
            for r in range(halves)]
    for r in range(halves):
        sl = slice(r * rows, (r + 1) * rows)
        z = alpha * x_ref[sl, :] + outs[r]
        mu = jnp.mean(z, axis=1, keepdims=True)
        zc = z - mu
        var = jnp.mean(zc * zc, axis=1, keepdims=True)
        o_ref[sl, :] = zc * lax.rsqrt(var + LN_EPS) * g_ref[...] + b_ref[...]


def _out(merged, w_out, x, ln_g, ln_b, alpha, tm):
    n = merged.shape[0]
    row = pl.BlockSpec((tm, D_MODEL), lambda i: (i, 0))
    vec = pl.BlockSpec((1, D_MODEL), lambda i: (0, 0))
    wspec = pl.BlockSpec((D_MODEL, D_MODEL), lambda i: (0, 0), pipeline_mode=pl.Buffered(1))
    return pl.pallas_call(
        functools.partial(_out_kernel, halves=4 if tm % 64 == 0 else 1, alpha=alpha),
        out_shape=jax.ShapeDtypeStruct((n, D_MODEL), F32),
        grid=(n // tm,),
        in_specs=[row, wspec, row, vec, vec],
        out_specs=row,
        scratch_shapes=[pltpu.VMEM((D_MODEL, D_MODEL), BF16)],
        compiler_params=_params("arbitrary"),
        name="out_ln",
    )(merged, w_out, x, ln_g, ln_b)


def _rest_col_block(tn):
    za_blocks = (COL_UP - COL_ZA) // tn
    return lambda j: jnp.where(j < za_blocks, COL_ZA // tn + j, COL_GM // tn + (j - za_blocks))


def _layer_prompt(x2d, batch, t, w, pool_hist, mk, mv):
    ang = _rotary_angles(jnp.arange(t))
    cos_t, sin_t = jnp.cos(ang).T, jnp.sin(ang).T
    tm, tn, tp = 2048, 1024, 512
    qt, kt, kb, xb = _proj_qk(x2d, w["w_in"], cos_t, sin_t, batch, t, tp, 256, A_SCALE * math.log2(math.e))
    v, vt, u, b_p = _proj_vup(xb, w["w_in"], pool_hist, w["pool_w"], w["pool_scale"], batch, t, tp, 256)
    b_m = _proj_mem(xb, w["w_in"], mk, mv, batch, t, tp)
    h = _proj(xb, w["w_in"], _rest_col_block(tn), H_COLS, BF16, tm, tn, name="proj_rest")
    b_a = _attn_prompt(qt, kb, vt, h, w["gain"], w["lamv"], batch, t)
    merged = _merge(b_a, b_p, b_m, h, H_GM, w["w_branch"], 512, 1024)
    y = _out(merged, w["w_out"], x2d, w["ln_g"], w["ln_b"], w["alpha"], 512)
    return y, kt, v, u


def _layer_sample(x2d, batch, t, past, w, kt_hist, v_hist, pool_hist, mk, mv):
    n = batch * t
    rot = _rotary_tables(past + jnp.arange(t))
    act = _proj(x2d, w["w_in"], lambda j: j, IN_COLS, F32, n, 1024, name="proj_all")
    b_a, k, b_p, b_m = _mix_sample(act, rot, kt_hist, v_hist, pool_hist, mk, mv, w["gain"], w["lamv"],
                                   w["pool_w"], w["pool_scale"], batch, t, past)
    merged = _merge(b_a, b_p, b_m, act, COL_GM, w["w_branch"], n, 1024)
    y = _out(merged, w["w_out"], x2d, w["ln_g"], w["ln_b"], w["alpha"], n)
    return y, k, act


def kernel(x_prompt, x_sample, cache_attn_k, cache_attn_v, cache_mem_k, cache_mem_v, state_pool, mem_prompt, w_in, w_mem_kv, lambda_q1, lambda_k1, lambda_q2, lambda_k2, subln_gain, pool_w, pool_scale, w_branch, w_out, ln_gain, ln_bias):
    bp, tp, d = x_prompt.shape
    bs, ts, _ = x_sample.shape
    past = cache_attn_k.shape[2]
    hist_rows = P_HIST + 1
    halves = M_HEAD_DIM // LANES

    def to_mem_layout(a):
        a = a.reshape(a.shape[0], M_TOKENS, M_HEADS, halves, LANES)
        return jnp.transpose(a, (0, 1, 3, 2, 4)).reshape(a.shape[0] * MEM_ROWS, LANES)

    def from_mem_layout(a, batch):
        a = a.reshape(batch, M_TOKENS, halves, M_HEADS, LANES)
        return jnp.transpose(a, (0, 1, 3, 2, 4)).reshape(batch, M_TOKENS, M_HEADS, M_HEAD_DIM)

    def run_layer(l, xp2d, xs2d):
        w = {
            "w_in": w_in[l],
            "gain": subln_gain[l].reshape(1, A_V_DIM),
            "lamv": jnp.stack([lambda_q1[l], lambda_k1[l], lambda_q2[l], lambda_k2[l],
                               jnp.full((A_QK_DIM,), 0.8 - 0.6 * math.exp(-0.3 * l), F32)]),
            "pool_w": pool_w[l],
            "pool_scale": pool_scale[l].reshape(1, P_WIDTH),
            "w_branch": w_branch[l],
            "w_out": w_out[l],
            "ln_g": ln_gain[l].reshape(1, d),
            "ln_b": ln_bias[l].reshape(1, d),
            "alpha": (2.0 * w_in.shape[0]) ** 0.25,
        }
        memf = mem_prompt.reshape(bp * M_TOKENS, d)
        mk, mv = _proj_memkv(memf, w_mem_kv[l], 512)

        zero_hist = jnp.zeros((bp, hist_rows, P_WIDTH), F32)
        y_p, k_p, v_p, u_p = _layer_prompt(xp2d, bp, tp, w, zero_hist, mk, mv)

        hist_s = jnp.pad(state_pool[l], ((0, 0), (hist_rows - P_HIST, 0), (0, 0)))
        kt_hist = jnp.transpose(cache_attn_k[l], (0, 2, 3, 4, 1)).reshape(bs, A_HEADS, 2 * A_QK_DIM, past)
        y_s, k_s, act_s = _layer_sample(
            xs2d, bs, ts, past, w,
            kt_hist, cache_attn_v[l].reshape(bs, past * A_HEADS, A_V_DIM),
            hist_s, to_mem_layout(cache_mem_k[l]), to_mem_layout(cache_mem_v[l]))
        v_s = act_s[:, COL_V:COL_V + A_WIDTH]
        u_s = act_s[:, COL_UP:COL_UP + P_WIDTH]

        pool_p = jnp.concatenate([zero_hist, u_p.reshape(bp, tp, P_WIDTH)], axis=1)[:, -P_HIST:]
        pool_s = jnp.concatenate([hist_s, u_s.reshape(bs, ts, P_WIDTH)], axis=1)[:, -P_HIST:]
        k_p = jnp.transpose(k_p.reshape(bp, A_HEADS, 2, A_QK_DIM, tp), (0, 4, 1, 2, 3))
        outs = (k_p, v_p.reshape(bp, tp, A_HEADS, A_V_DIM),
                from_mem_layout(mk, bp), from_mem_layout(mv, bp),
                pool_p,
                k_s.reshape(bs, ts, A_HEADS, 2, A_QK_DIM), v_s.reshape(bs, ts, A_HEADS, A_V_DIM), pool_s)
        return y_p, y_s, outs

    yp = x_prompt.reshape(bp * tp, d)
    ys = x_sample.reshape(bs * ts, d)
    per_layer = []
    for l in range(w_in.shape[0]):
        yp, ys, outs = run_layer(l, yp, ys)
        per_layer.append(outs)
    stacked = [jnp.stack([o[n] for o in per_layer]) for n in range(8)]
    return (yp.reshape(bp, tp, d), ys.reshape(bs, ts, d), *stacked)
```

```python
import functools
import math

import numpy as np
import jax
import jax.numpy as jnp
from jax import lax
from jax.experimental import pallas as pl
from jax.experimental.pallas import tpu as pltpu

F32 = jnp.float32
BF16 = jnp.bfloat16

V7X_VMEM_LIMIT_BYTES = 56 * 1024 * 1024
LANES = 128

D_MODEL = 2048
CHUNK = 64
A_HEADS = 8
A_QK_DIM = 64
A_V_DIM = 128
A_WIDTH = A_HEADS * A_V_DIM
A_SCALE = A_QK_DIM ** -0.5
ROT_DIM = A_QK_DIM // 4
ROPE_THETA = 500000.0
SUBLN_EPS = 1e-5
P_WINDOWS = (2, 4, 8, 16)
P_GROUP_DIM = 256
P_WIDTH = len(P_WINDOWS) * P_GROUP_DIM
P_HIST = max(P_WINDOWS) - 1
M_TOKENS = 256
M_HEADS = 4
M_HEAD_DIM = 256
M_WIDTH = M_HEADS * M_HEAD_DIM
M_SCALE = M_HEAD_DIM ** -0.5
MEM_ROWS = M_TOKENS * M_HEADS * (M_HEAD_DIM // LANES)
N_BRANCH = 3
LN_EPS = 1e-5
ONES_ROWS = 16

COL_Q = 0
COL_K = COL_Q + A_WIDTH
COL_V = COL_K + A_WIDTH
COL_ZA = COL_V + A_WIDTH
COL_UP = COL_ZA + A_WIDTH
COL_ZP = COL_UP + P_WIDTH
COL_QM = COL_ZP + P_WIDTH
COL_ZM = COL_QM + M_WIDTH
COL_GM = COL_ZM + M_WIDTH
IN_COLS = COL_GM + N_BRANCH * D_MODEL
H_ZA = 0
H_GM = H_ZA + A_WIDTH
H_COLS = H_GM + N_BRANCH * D_MODEL

KEY_TILE = 256
ROW_TILE = 512
COL_TILE = 1024
GATE_ROW_TILE = 2048
MEM_ROW_TILE = 512


def _params(*sem):
    return pltpu.CompilerParams(dimension_semantics=sem, vmem_limit_bytes=V7X_VMEM_LIMIT_BYTES)


def _sigmoid(z):
    return 1.0 / (1.0 + jnp.exp(-z))


def _silu(z):
    return z * _sigmoid(z)


def _proj_kernel(x_ref, w_ref, o_ref):
    acc = jnp.dot(x_ref[...].astype(BF16), w_ref[...].astype(BF16), preferred_element_type=F32)
    o_ref[...] = acc.astype(o_ref.dtype)


def _proj(x, w, col_block, ncols, out_dtype, tm, tn, name="proj"):
    n, kdim = x.shape
    return pl.pallas_call(
        _proj_kernel,
        out_shape=jax.ShapeDtypeStruct((n, ncols), out_dtype),
        grid=(n // tm, ncols // tn),
        in_specs=[pl.BlockSpec((tm, kdim), lambda i, j: (i, 0)),
                  pl.BlockSpec((kdim, tn), lambda i, j: (0, col_block(j)))],
        out_specs=pl.BlockSpec((tm, tn), lambda i, j: (i, j)),
        compiler_params=_params("arbitrary", "arbitrary"),
        name=name,
    )(x, w)


def _proj_memkv_kernel(x_ref, w_ref, mk_ref, mv_ref):
    kv = jnp.dot(x_ref[...].astype(BF16), w_ref[...].astype(BF16), preferred_element_type=F32)
    tm = kv.shape[0]
    halves = M_HEAD_DIM // LANES
    group = halves * M_HEADS

    def store(o_ref):
        for h in range(M_HEADS):
            for c in range(halves):
                col = h * M_HEAD_DIM + c * LANES
                o_ref[pl.ds(c * M_HEADS + h, tm, stride=group), :] = kv[:, col:col + LANES]

    @pl.when(pl.program_id(1) == 0)
    def _():
        store(mk_ref)

    @pl.when(pl.program_id(1) == 1)
    def _():
        store(mv_ref)


def _proj_memkv(mem, w, tm):
    n, kdim = mem.shape
    group = M_WIDTH // LANES
    row = pl.BlockSpec((tm * group, LANES), lambda i, j: (i, 0))
    return pl.pallas_call(
        _proj_memkv_kernel,
        out_shape=[jax.ShapeDtypeStruct((n * group, LANES), F32)] * 2,
        grid=(n // tm, 2),
        in_specs=[pl.BlockSpec((tm, kdim), lambda i, j: (i, 0)),
                  pl.BlockSpec((kdim, M_WIDTH), lambda i, j: (0, j))],
        out_specs=[row, row],
        compiler_params=_params("arbitrary", "arbitrary"),
        name="proj_memkv",
    )(mem, w)


def _rotate_sublanes(acc, c, s):
    half = ROT_DIM // 2
    pieces = []
    for g in range(acc.shape[0] // A_QK_DIM):
        b = g * A_QK_DIM
        lo, hi = acc[b:b + half], acc[b + half:b + ROT_DIM]
        pieces += [lo * c - hi * s, hi * c + lo * s, acc[b + ROT_DIM:b + A_QK_DIM]]
    return jnp.concatenate(pieces, axis=0)


def _proj_qk_kernel(x_ref, wq_ref, wk_ref, cos_ref, sin_ref, qt_ref, kt_ref, kb_ref, xb_ref, *, q_scale):
    xb = x_ref[...].astype(BF16)
    xb_ref[...] = xb
    c, s = cos_ref[...], sin_ref[...]
    q = _rotate_sublanes(jnp.dot(xb, wq_ref[...].astype(BF16), preferred_element_type=F32).T, c, s)
    q = (q * q_scale).astype(qt_ref.dtype)
    tile = qt_ref.shape[-1]
    for n in range(qt_ref.shape[0]):
        qt_ref[n] = q[:, n * tile:(n + 1) * tile]
    k = _rotate_sublanes(jnp.dot(xb, wk_ref[...].astype(BF16), preferred_element_type=F32).T, c, s)
    kt_ref[...] = k
    kb_ref[...] = k.T.astype(kb_ref.dtype)


def _proj_qk(x, w, cos_t, sin_t, batch, t, tm, q_tile, q_scale):
    n, kdim = x.shape
    width = A_WIDTH
    nt = t // tm
    tab = pl.BlockSpec((ROT_DIM // 2, tm), lambda i: (0, i % nt))
    wspec = lambda col: pl.BlockSpec((kdim, width), lambda i: (0, col // width), pipeline_mode=pl.Buffered(1))
    tspec = pl.BlockSpec((None, width, tm), lambda i: (i // nt, 0, i % nt))
    qspec = pl.BlockSpec((None, tm // q_tile, width, q_tile), lambda i: (i // nt, i % nt, 0, 0))
    return pl.pallas_call(
        functools.partial(_proj_qk_kernel, q_scale=q_scale),
        out_shape=[jax.ShapeDtypeStruct((batch, t // q_tile, width, q_tile), BF16),
                   jax.ShapeDtypeStruct((batch, width, t), F32),
                   jax.ShapeDtypeStruct((n, width), BF16), jax.ShapeDtypeStruct((n, kdim), BF16)],
        grid=(n // tm,),
        in_specs=[pl.BlockSpec((tm, kdim), lambda i: (i, 0)), wspec(COL_Q), wspec(COL_K), tab, tab],
        out_specs=[qspec, tspec, pl.BlockSpec((tm, width), lambda i: (i, 0)),
                   pl.BlockSpec((tm, kdim), lambda i: (i, 0))],
        compiler_params=_params("arbitrary"),
        name="proj_qk",
    )(x, w, w, cos_t, sin_t)


def _proj_vup_kernel(x_ref, wv_ref, wu_ref, wz_ref, hist_ref, pw_ref, ps_ref, v_ref, vt_ref, u_ref, bp_ref,
                     prev_ref, *, key_tile, nt):
    i = pl.program_id(0) % nt

    @pl.when(i == 0)
    def _():
        prev_ref[...] = hist_ref[...]

    xb = x_ref[...]
    tm = xb.shape[0]
    u = jnp.dot(xb, wu_ref[...].astype(BF16), preferred_element_type=F32)
    u_ref[...] = u
    zp = jnp.dot(xb, wz_ref[...].astype(BF16), preferred_element_type=F32)
    v = jnp.dot(xb, wv_ref[...].astype(BF16), preferred_element_type=F32)
    for h in range(A_HEADS):
        v_ref[pl.ds(h, tm, stride=A_HEADS), :] = v[:, h * A_V_DIM:(h + 1) * A_V_DIM]
    for c in range(tm // key_tile):
        vt_ref[c] = v[c * key_tile:(c + 1) * key_tile, :].T.astype(vt_ref.dtype)
    pos = i * tm + lax.broadcasted_iota(jnp.int32, (tm, 1), 0)
    bp_ref[...] = _pool_mix(u, prev_ref[...], zp, pw_ref, ps_ref, pos).astype(bp_ref.dtype)
    prev_ref[...] = u[tm - prev_ref.shape[0]:, :]


def _proj_vup(xb, w, hist, pool_w, pool_scale, batch, t, tm, key_tile):
    n, kdim = xb.shape
    width = A_WIDTH
    nt = t // tm
    hr = hist.shape[1]
    wspec = lambda col: pl.BlockSpec((kdim, width), lambda i: (0, col // width), pipeline_mode=pl.Buffered(1))
    row = pl.BlockSpec((tm, width), lambda i: (i, 0))
    return pl.pallas_call(
        functools.partial(_proj_vup_kernel, key_tile=key_tile, nt=nt),
        out_shape=[jax.ShapeDtypeStruct((n * A_HEADS, A_V_DIM), F32),
                   jax.ShapeDtypeStruct((batch, t // key_tile, width, key_tile), BF16),
                   jax.ShapeDtypeStruct((n, width), F32),
                   jax.ShapeDtypeStruct((n, width), BF16)],
        grid=(n // tm,),
        in_specs=[pl.BlockSpec((tm, kdim), lambda i: (i, 0)), wspec(COL_V), wspec(COL_UP), wspec(COL_ZP),
                  pl.BlockSpec((None, hr, P_WIDTH), lambda i: (i // nt, 0, 0)),
                  pl.BlockSpec((len(P_WINDOWS), P_GROUP_DIM, P_GROUP_DIM), lambda i: (0, 0, 0)),
                  pl.BlockSpec((1, P_WIDTH), lambda i: (0, 0))],
        out_specs=[pl.BlockSpec((tm * A_HEADS, A_V_DIM), lambda i: (i, 0)),
                   pl.BlockSpec((None, tm // key_tile, width, key_tile), lambda i: (i // nt, i % nt, 0, 0)),
                   row, row],
        scratch_shapes=[pltpu.VMEM((hr, P_WIDTH), F32)],
        compiler_params=_params("arbitrary"),
        name="proj_vup",
    )(xb, w, w, w, hist, pool_w, pool_scale)


def _rotary_angles(pos):
    half = ROT_DIM // 2
    inv = ROPE_THETA ** (-(jnp.arange(half, dtype=F32) * 2.0) / ROT_DIM)
    return pos.astype(F32)[:, None] * inv[None, :]


def _rotary_tables(pos):
    half = ROT_DIM // 2
    ang = _rotary_angles(pos)
    cos, sin = jnp.cos(ang), jnp.sin(ang)
    t = pos.shape[0]
    ones = jnp.ones((t, A_QK_DIM - ROT_DIM), F32)
    zeros = jnp.zeros((t, A_QK_DIM - ROT_DIM), F32)
    zh = jnp.zeros((t, half), F32)
    c64 = jnp.concatenate([cos, cos, ones], axis=1)
    sa64 = jnp.concatenate([zh, sin, zeros], axis=1)
    sb64 = jnp.concatenate([-sin, zh, zeros], axis=1)
    tile = lambda a: jnp.concatenate([a, a], axis=1)
    return tile(c64), tile(sa64), tile(sb64)


def _lambda_values(lam_ref):
    lv = lam_ref[...]
    a = jnp.sum(lv[0:1] * lv[1:2], axis=1, keepdims=True)
    b = jnp.sum(lv[2:3] * lv[3:4], axis=1, keepdims=True)
    lam_init = lv[4:5, 0:1]
    return jnp.exp(a) - jnp.exp(b) + lam_init, lam_init


def _split_components(q):
    lane = lax.broadcasted_iota(jnp.int32, q.shape, 1)
    qf = q.astype(F32)
    return jnp.concatenate([jnp.where(lane < A_QK_DIM, qf, 0.0),
                            jnp.where(lane >= A_QK_DIM, qf, 0.0)], axis=0).astype(q.dtype)


def _attn_finish(acc, l, tq, lam, gain, za):
    o = acc[:tq] / l[:tq] - lam * (acc[tq:] / l[tq:])
    o = o * lax.rsqrt(jnp.mean(o * o, axis=1, keepdims=True) + SUBLN_EPS) * gain
    return o * _silu(za)


def _attn_prompt_kernel(qt_ref, k_ref, vt_ref, za_ref, gain_ref, lam_ref, o_ref, m_ref, l_ref, acc_ref,
                        s0_ref, s1_ref, *, tq, heads, nq):
    i = pl.program_id(2)
    hs = [slice(n * LANES, (n + 1) * LANES) for n in range(heads)]

    def stacked_queries(qi):
        out = []
        for sl in hs:
            qt = qt_ref[qi, sl, :].astype(F32)
            row = lax.broadcasted_iota(jnp.int32, qt.shape, 0)
            out.append(jnp.concatenate([jnp.where(row < A_QK_DIM, qt, 0.0),
                                        jnp.where(row >= A_QK_DIM, qt, 0.0)], axis=1).astype(BF16))
        return out

    def score(j, qq):
        start = pl.multiple_of(j * tq, tq)
        sl, stacked = qq
        return jnp.dot(k_ref[pl.ds(start, tq), sl], stacked, preferred_element_type=F32)

    qqts = stacked_queries(i)

    @pl.when(i == 0)
    def _():
        for n in range(heads):
            s0_ref[n] = score(0, (hs[n], qqts[n]))

    m_ref[...] = jnp.full(m_ref.shape, -jnp.inf, F32)
    l_ref[...] = jnp.zeros(l_ref.shape, F32)
    acc_ref[...] = jnp.zeros(acc_ref.shape, F32)
    ones_rows = jnp.ones((ONES_ROWS, tq), BF16)

    def step(j, cur_ref, nxt_ref, masked):
        for n in range(heads):
            s = cur_ref[n]
            if masked:
                krow = lax.broadcasted_iota(jnp.int32, s.shape, 0)
                qcol = lax.broadcasted_iota(jnp.int32, s.shape, 1) % tq
                s = jnp.where((krow // CHUNK) <= (qcol // CHUNK), s, -jnp.inf)
            m = m_ref[n]
            m_new = jnp.maximum(m, jnp.max(s, axis=0, keepdims=True))
            alpha = jnp.exp2(m - m_new)
            p = jnp.exp2(s - m_new).astype(BF16)
            m_ref[n] = m_new
            lhs = jnp.concatenate([vt_ref[j, hs[n], :], ones_rows], axis=0)
            pv = jnp.dot(lhs, p, preferred_element_type=F32)
            acc_ref[n] = alpha * acc_ref[n] + pv[:A_V_DIM]
            l_ref[n] = alpha * l_ref[n] + pv[A_V_DIM:A_V_DIM + 1]
            if nxt_ref is not None:
                nxt_ref[n] = score(j + 1, (hs[n], qqts[n]))

    def pair(jj, carry):
        step(2 * jj, s0_ref, s1_ref, False)
        step(2 * jj + 1, s1_ref, s0_ref, False)
        return carry

    lax.fori_loop(0, i // 2, pair, 0)

    @pl.when(i % 2 == 1)
    def _():
        step(i - 1, s0_ref, s1_ref, False)
        step(i, s1_ref, None, True)

    @pl.when(i % 2 == 0)
    def _():
        step(i, s0_ref, None, True)

    nxt = stacked_queries(jnp.minimum(i + 1, nq - 1))
    lam, lam_init = _lambda_values(lam_ref)
    gain = gain_ref[...] * (1.0 - lam_init)
    for n in range(heads):
        s0_ref[n] = score(0, (hs[n], nxt[n]))
        inv, acc = 1.0 / l_ref[n], acc_ref[n]
        ot = acc[:, :tq] * inv[:, :tq] - lam * (acc[:, tq:] * inv[:, tq:])
        o = ot.T
        o = o * lax.rsqrt(jnp.mean(o * o, axis=1, keepdims=True) + SUBLN_EPS) * gain
        o_ref[:, hs[n]] = (o * _silu(za_ref[:, hs[n]].astype(F32))).astype(o_ref.dtype)


def _attn_prompt(qt, k, vt, h, gain, lamv, batch, t, tq, heads=A_HEADS):
    nq = t // tq
    n = batch * t
    hw = heads * LANES
    tiles = pl.BlockSpec((None, nq, hw, tq), lambda b, hh, i: (b, 0, hh, 0))
    return pl.pallas_call(
        functools.partial(_attn_prompt_kernel, tq=tq, heads=heads, nq=nq),
        out_shape=jax.ShapeDtypeStruct((n, A_WIDTH), BF16),
        grid=(batch, A_HEADS // heads, nq),
        in_specs=[
            tiles,
            pl.BlockSpec((t, hw), lambda b, hh, i: (b, hh)),
            tiles,
            pl.BlockSpec((tq, hw), lambda b, hh, i: (b * nq + i, H_ZA // hw + hh)),
            pl.BlockSpec((1, LANES), lambda b, hh, i: (0, 0)),
            pl.BlockSpec((5, A_QK_DIM), lambda b, hh, i: (0, 0)),
        ],
        out_specs=pl.BlockSpec((tq, hw), lambda b, hh, i: (b * nq + i, hh)),
        scratch_shapes=[pltpu.VMEM((heads, 1, 2 * tq), F32), pltpu.VMEM((heads, 1, 2 * tq), F32),
                        pltpu.VMEM((heads, A_V_DIM, 2 * tq), F32),
                        pltpu.VMEM((heads, tq, 2 * tq), F32), pltpu.VMEM((heads, tq, 2 * tq), F32)],
        compiler_params=_params("arbitrary", "arbitrary", "arbitrary"),
        name="attn_prompt",
    )(qt, k, vt, h, gain, lamv)


def _chunk_mask(s, ts, q_pos0, k_pos0):
    q_pos = q_pos0 + lax.broadcasted_iota(jnp.int32, s.shape, 0) % ts
    k_pos = k_pos0 + lax.broadcasted_iota(jnp.int32, s.shape, 1)
    return jnp.where((k_pos // CHUNK) <= (q_pos // CHUNK), s, -jnp.inf)


def _rotate_lanes(blk, c, sa, sb):
    half = ROT_DIM // 2
    return blk * c + pltpu.roll(blk, half, 1) * sa + pltpu.roll(blk, LANES - half, 1) * sb


def _mix_sample_kernel(q_ref, k_ref, v_ref, za_ref, u_ref, zp_ref, qm_ref, zm_ref, cos_ref, sa_ref, sb_ref,
                       kct_ref, vc_ref, hist_ref, mk_ref, mv_ref, gain_ref, lam_ref, pw_ref, ps_ref,
                       o_ref, krot_ref, bp_ref, bm_ref, *, ts, past, mask_cache, mask_new):
    pos = past + lax.broadcasted_iota(jnp.int32, (ts, 1), 0)
    bp_ref[...] = _pool_mix(u_ref[...], hist_ref[...], zp_ref[...], pw_ref, ps_ref, pos).astype(bp_ref.dtype)
    _mem_attend(qm_ref[...].astype(BF16), lambda: zm_ref[...], mk_ref, mv_ref, bm_ref)

    c, sa, sb = cos_ref[...], sa_ref[...], sb_ref[...]
    nt = (((1,), (1,)), ((), ()))
    hs = [slice(n * LANES, (n + 1) * LANES) for n in range(A_HEADS)]
    scores = []
    for n in range(A_HEADS):
        q = (_rotate_lanes(q_ref[:, hs[n]], c, sa, sb) * A_SCALE).astype(BF16)
        k = _rotate_lanes(k_ref[:, hs[n]], c, sa, sb)
        krot_ref[:, hs[n]] = k
        qq = _split_components(q)
        s_c = jnp.dot(qq, kct_ref[n].astype(BF16), preferred_element_type=F32)
        s_n = lax.dot_general(qq, k.astype(BF16), nt, preferred_element_type=F32)
        if mask_cache:
            s_c = _chunk_mask(s_c, ts, past, 0)
        if mask_new:
            s_n = _chunk_mask(s_n, ts, past, past)
        scores.append((s_c, s_n))
    lam, lam_init = _lambda_values(lam_ref)
    gain = gain_ref[...] * (1.0 - lam_init)
    for n, (s_c, s_n) in enumerate(scores):
        m = jnp.maximum(jnp.max(s_c, axis=1, keepdims=True), jnp.max(s_n, axis=1, keepdims=True))
        p_c = jnp.exp(s_c - m)
        p_n = jnp.exp(s_n - m)
        l = jnp.sum(p_c, axis=1, keepdims=True) + jnp.sum(p_n, axis=1, keepdims=True)
        vc = vc_ref[pl.ds(n, past, stride=A_HEADS), :].astype(BF16)
        acc = (jnp.dot(p_c.astype(BF16), vc, preferred_element_type=F32)
               + jnp.dot(p_n.astype(BF16), v_ref[:, hs[n]].astype(BF16), preferred_element_type=F32))
        out = _attn_finish(acc, l, ts, lam, gain, za_ref[:, hs[n]].astype(F32))
        o_ref[:, hs[n]] = out.astype(o_ref.dtype)


def _needs_mask(q_pos, k_pos):
    return not bool(((k_pos[None, :] // CHUNK) <= (q_pos[:, None] // CHUNK)).all())


def _mix_sample(act, rot, kct, vc, hist, mk, mv, gain, lamv, pool_w, pool_scale, batch, ts, past):
    q_pos = past + np.arange(ts)
    seg = lambda col: pl.BlockSpec((ts, A_WIDTH), lambda b: (b, col // A_WIDTH))
    tab = pl.BlockSpec((ts, LANES), lambda b: (0, 0))
    row = pl.BlockSpec((ts, A_WIDTH), lambda b: (b, 0))
    mem = pl.BlockSpec((MEM_ROWS, LANES), lambda b: (b, 0))
    bshape = jax.ShapeDtypeStruct((batch * ts, A_WIDTH), BF16)
    return pl.pallas_call(
        functools.partial(_mix_sample_kernel, ts=ts, past=past,
                          mask_cache=_needs_mask(q_pos, np.arange(past)),
                          mask_new=_needs_mask(q_pos, q_pos)),
        out_shape=[bshape, jax.ShapeDtypeStruct((batch * ts, A_WIDTH), F32), bshape, bshape],
        grid=(batch,),
        in_specs=[
            seg(COL_Q), seg(COL_K), seg(COL_V), seg(COL_ZA), seg(COL_UP), seg(COL_ZP), seg(COL_QM), seg(COL_ZM),
            tab, tab, tab,
            pl.BlockSpec((None, A_HEADS, LANES, past), lambda b: (b, 0, 0, 0)),
            pl.BlockSpec((None, past * A_HEADS, LANES), lambda b: (b, 0, 0)),
            pl.BlockSpec((None, hist.shape[1], P_WIDTH), lambda b: (b, 0, 0)),
            mem, mem,
            pl.BlockSpec((1, LANES), lambda b: (0, 0)),
            pl.BlockSpec((5, A_QK_DIM), lambda b: (0, 0)),
            pl.BlockSpec((len(P_WINDOWS), P_GROUP_DIM, P_GROUP_DIM), lambda b: (0, 0, 0)),
            pl.BlockSpec((1, P_WIDTH), lambda b: (0, 0)),
        ],
        out_specs=[row, row, row, row],
        compiler_params=_params("arbitrary"),
        name="mix_sample",
    )(*([act] * 8), *rot, kct, vc, hist, mk, mv, gain, lamv, pool_w, pool_scale)


def _pool_mix(u, prev, zp, pw_ref, ps_ref, pos):
    tm = u.shape[0]
    ext = jnp.concatenate([prev, u], axis=0)
    hist_rows = prev.shape[0]
    parts = []
    for g, w in enumerate(P_WINDOWS):
        sl = slice(g * P_GROUP_DIM, (g + 1) * P_GROUP_DIM)
        tot = ext[:, sl]
        span = 1
        while span < w:
            tot = tot + pltpu.roll(tot, span, 0)
            span *= 2
        tot = tot[hist_rows:]
        cnt = jnp.minimum(pos + 1, w).astype(F32)
        pooled = tot / cnt - u[:, sl]
        parts.append(jnp.dot(pooled.astype(BF16), pw_ref[g].astype(BF16), preferred_element_type=F32))
    mixed = jnp.concatenate(parts, axis=1) * ps_ref[...]
    return mixed * _silu(zp)


def _mem_head(ref, h):
    halves = M_HEAD_DIM // LANES
    parts = [ref[pl.ds(c * M_HEADS + h, M_TOKENS, stride=halves * M_HEADS), :] for c in range(halves)]
    return jnp.concatenate(parts, axis=1)


def _mem_attend(q, gate_input, mk_ref, mv_ref, o_ref):
    nt = (((1,), (1,)), ((), ()))
    hs = [slice(n * M_HEAD_DIM, (n + 1) * M_HEAD_DIM) for n in range(M_HEADS)]
    scores = [lax.dot_general(q[:, sl], _mem_head(mk_ref, n).astype(BF16), nt,
                              preferred_element_type=F32) * M_SCALE for n, sl in enumerate(hs)]
    zm = gate_input()
    for n, (sl, s) in enumerate(zip(hs, scores)):
        m = jnp.max(s, axis=1, keepdims=True)
        p = jnp.exp(s - m)
        l = jnp.sum(p, axis=1, keepdims=True)
        o = jnp.dot(p.astype(BF16), _mem_head(mv_ref, n).astype(BF16), preferred_element_type=F32) / l
        o_ref[:, sl] = (o * _silu(zm[:, sl])).astype(o_ref.dtype)


def _proj_mem_kernel(x_ref, wq_ref, wz_ref, mk_ref, mv_ref, xs_ref, wt_ref, o_ref, act_ref):
    act_ref[...] = jnp.dot(xs_ref[...], wt_ref[...].astype(BF16), preferred_element_type=F32)
    xb = x_ref[...]
    q = jnp.dot(xb, wq_ref[...].astype(BF16), preferred_element_type=F32).astype(BF16)
    _mem_attend(q, lambda: jnp.dot(xb, wz_ref[...].astype(BF16), preferred_element_type=F32),
                mk_ref, mv_ref, o_ref)


def _proj_mem(xb, w, mk, mv, xs, batch, t, tm, tn):
    n, kdim = xb.shape
    ns = xs.shape[0]
    nt = t // tm
    tiles = w.shape[1] // tn
    assert batch * nt >= tiles, "not enough grid steps to carry the sample projection"
    rider = lambda b, i: (0, jnp.minimum(b * nt + i, tiles - 1))
    wspec = lambda col: pl.BlockSpec((kdim, M_WIDTH), lambda b, i: (0, col // M_WIDTH),
                                     pipeline_mode=pl.Buffered(1))
    mem = pl.BlockSpec((MEM_ROWS, LANES), lambda b, i: (b, 0))
    return pl.pallas_call(
        _proj_mem_kernel,
        out_shape=[jax.ShapeDtypeStruct((n, M_WIDTH), BF16), jax.ShapeDtypeStruct((ns, w.shape[1]), F32)],
        grid=(batch, nt),
        in_specs=[pl.BlockSpec((tm, kdim), lambda b, i: (b * nt + i, 0)), wspec(COL_QM), wspec(COL_ZM), mem, mem,
                  pl.BlockSpec((ns, kdim), lambda b, i: (0, 0), pipeline_mode=pl.Buffered(1)),
                  pl.BlockSpec((kdim, tn), rider)],
        out_specs=[pl.BlockSpec((tm, M_WIDTH), lambda b, i: (b * nt + i, 0)), pl.BlockSpec((ns, tn), rider)],
        compiler_params=_params("arbitrary", "arbitrary"),
        name="proj_mem",
    )(xb, w, w, mk, mv, xs, w)


def _merge_kernel(ba_ref, bp_ref, bm_ref, g0_ref, g1_ref, g2_ref, wb_ref, o_ref, wbb_ref, *, halves):
    @pl.when(pl.program_id(1) == 0)
    def _():
        wbb_ref[...] = wb_ref[...].astype(BF16)

    branches = ((ba_ref, g0_ref), (bp_ref, g1_ref), (bm_ref, g2_ref))
    rows = o_ref.shape[0] // halves
    projs = [[jnp.dot(b_ref[r * rows:(r + 1) * rows, :], wbb_ref[n], preferred_element_type=F32)
              for n, (b_ref, _) in enumerate(branches)] for r in range(halves)]
    for r in range(halves):
        sl = slice(r * rows, (r + 1) * rows)
        acc = None
        for n, (_, g_ref) in enumerate(branches):
            term = _sigmoid(g_ref[sl, :].astype(F32)) * projs[r][n]
            acc = term if acc is None else acc + term
        o_ref[sl, :] = acc.astype(o_ref.dtype)


def _merge(ba, bp, bm, h, g_col, wb, tm, tn):
    n = ba.shape[0]
    width = ba.shape[1]
    gspec = lambda k: pl.BlockSpec((tm, tn), lambda j, i: (i, (g_col + k * D_MODEL) // tn + j))
    bspec = pl.BlockSpec((tm, width), lambda j, i: (i, 0))
    return pl.pallas_call(
        functools.partial(_merge_kernel, halves=2 if tm % 32 == 0 else 1),
        out_shape=jax.ShapeDtypeStruct((n, D_MODEL), BF16),
        grid=(D_MODEL // tn, n // tm),
        in_specs=[bspec, bspec, bspec, gspec(0), gspec(1), gspec(2),
                  pl.BlockSpec((N_BRANCH, width, tn), lambda j, i: (0, 0, j))],
        out_specs=pl.BlockSpec((tm, tn), lambda j, i: (i, j)),
        scratch_shapes=[pltpu.VMEM((N_BRANCH, width, tn), BF16)],
        compiler_params=_params("arbitrary", "arbitrary"),
        name="merge",
    )(ba, bp, bm, h, h, h, wb)


def _out_kernel(m_ref, w_ref, x_ref, g_ref, b_ref, o_ref, wb_ref, *, halves, alpha):
    @pl.when(pl.program_id(0) == 0)
    def _():
        wb_ref[...] = w_ref[...].astype(BF16)

    rows = o_ref.shape[0] // halves
    outs = [jnp.dot(m_ref[r * rows:(r + 1) * rows, :], wb_ref[...], preferred_element_type=F32)
            for r in range(halves)]
    for r in range(halves):
        sl = slice(r * rows, (r + 1) * rows)
        z = alpha * x_ref[sl, :] + outs[r]
        mu = jnp.mean(z, axis=1, keepdims=True)
        zc = z - mu
        var = jnp.mean(zc * zc, axis=1, keepdims=True)
        o_ref[sl, :] = zc * lax.rsqrt(var + LN_EPS) * g_ref[...] + b_ref[...]


def _out(merged, w_out, x, ln_g, ln_b, alpha, tm):
    n = merged.shape[0]
    row = pl.BlockSpec((tm, D_MODEL), lambda i: (i, 0))
    vec = pl.BlockSpec((1, D_MODEL), lambda i: (0, 0))
    wspec = pl.BlockSpec((D_MODEL, D_MODEL), lambda i: (0, 0), pipeline_mode=pl.Buffered(1))
    return pl.pallas_call(
        functools.partial(_out_kernel, halves=4 if tm % 64 == 0 else 1, alpha=alpha),
        out_shape=jax.ShapeDtypeStruct((n, D_MODEL), F32),
        grid=(n // tm,),
        in_specs=[row, wspec, row, vec, vec],
        out_specs=row,
        scratch_shapes=[pltpu.VMEM((D_MODEL, D_MODEL), BF16)],
        compiler_params=_params("arbitrary"),
        name="out_ln",
    )(merged, w_out, x, ln_g, ln_b)


def _rest_col_block(tn):
    za_blocks = (COL_UP - COL_ZA) // tn
    return lambda j: jnp.where(j < za_blocks, COL_ZA // tn + j, COL_GM // tn + (j - za_blocks))


def _layer_prompt(x2d, batch, t, w, pool_hist, mk, mv, xs):
    ang = _rotary_angles(jnp.arange(t))
    cos_t, sin_t = jnp.cos(ang).T, jnp.sin(ang).T
    qt, kt, kb, xb = _proj_qk(x2d, w["w_in"], cos_t, sin_t, batch, t, ROW_TILE, KEY_TILE,
                              A_SCALE * math.log2(math.e))
    v, vt, u, b_p = _proj_vup(xb, w["w_in"], pool_hist, w["pool_w"], w["pool_scale"], batch, t, ROW_TILE, KEY_TILE)
    b_m, act_s = _proj_mem(xb, w["w_in"], mk, mv, xs, batch, t, ROW_TILE, COL_TILE)
    h = _proj(xb, w["w_in"], _rest_col_block(COL_TILE), H_COLS, BF16, GATE_ROW_TILE, COL_TILE, name="proj_rest")
    b_a = _attn_prompt(qt, kb, vt, h, w["gain"], w["lamv"], batch, t, KEY_TILE)
    merged = _merge(b_a, b_p, b_m, h, H_GM, w["w_branch"], ROW_TILE, COL_TILE)
    y = _out(merged, w["w_out"], x2d, w["ln_g"], w["ln_b"], w["alpha"], ROW_TILE)
    return y, kt, v, u, act_s


def _layer_sample(x2d, act, batch, t, past, w, kt_hist, v_hist, pool_hist, mk, mv):
    n = batch * t
    rot = _rotary_tables(past + jnp.arange(t))
    b_a, k, b_p, b_m = _mix_sample(act, rot, kt_hist, v_hist, pool_hist, mk, mv, w["gain"], w["lamv"],
                                   w["pool_w"], w["pool_scale"], batch, t, past)
    merged = _merge(b_a, b_p, b_m, act, COL_GM, w["w_branch"], n, COL_TILE)
    y = _out(merged, w["w_out"], x2d, w["ln_g"], w["ln_b"], w["alpha"], n)
    return y, k


def kernel(x_prompt, x_sample, cache_attn_k, cache_attn_v, cache_mem_k, cache_mem_v, state_pool, mem_prompt, w_in, w_mem_kv, lambda_q1, lambda_k1, lambda_q2, lambda_k2, subln_gain, pool_w, pool_scale, w_branch, w_out, ln_gain, ln_bias):
    bp, tp, d = x_prompt.shape
    bs, ts, _ = x_sample.shape
    past = cache_attn_k.shape[2]
    hist_rows = P_HIST + 1
    halves = M_HEAD_DIM // LANES

    def to_mem_layout(a):
        a = a.reshape(a.shape[0], M_TOKENS, M_HEADS, halves, LANES)
        return jnp.transpose(a, (0, 1, 3, 2, 4)).reshape(a.shape[0] * MEM_ROWS, LANES)

    def from_mem_layout(a, batch):
        a = a.reshape(batch, M_TOKENS, halves, M_HEADS, LANES)
        return jnp.transpose(a, (0, 1, 3, 2, 4)).reshape(batch, M_TOKENS, M_HEADS, M_HEAD_DIM)

    def run_layer(l, xp2d, xs2d):
        w = {
            "w_in": w_in[l],
            "gain": subln_gain[l].reshape(1, A_V_DIM),
            "lamv": jnp.stack([lambda_q1[l], lambda_k1[l], lambda_q2[l], lambda_k2[l],
                               jnp.full((A_QK_DIM,), 0.8 - 0.6 * math.exp(-0.3 * l), F32)]),
            "pool_w": pool_w[l],
            "pool_scale": pool_scale[l].reshape(1, P_WIDTH),
            "w_branch": w_branch[l],
            "w_out": w_out[l],
            "ln_g": ln_gain[l].reshape(1, d),
            "ln_b": ln_bias[l].reshape(1, d),
            "alpha": (2.0 * w_in.shape[0]) ** 0.25,
        }
        memf = mem_prompt.reshape(bp * M_TOKENS, d)
        mk, mv = _proj_memkv(memf, w_mem_kv[l], MEM_ROW_TILE)

        zero_hist = jnp.zeros((bp, hist_rows, P_WIDTH), F32)
        y_p, k_p, v_p, u_p, act_s = _layer_prompt(xp2d, bp, tp, w, zero_hist, mk, mv, xs2d.astype(BF16))

        hist_s = jnp.pad(state_pool[l], ((0, 0), (hist_rows - P_HIST, 0), (0, 0)))
        kt_hist = jnp.transpose(cache_attn_k[l], (0, 2, 3, 4, 1)).reshape(bs, A_HEADS, 2 * A_QK_DIM, past)
        y_s, k_s = _layer_sample(
            xs2d, act_s, bs, ts, past, w,
            kt_hist, cache_attn_v[l].reshape(bs, past * A_HEADS, A_V_DIM),
            hist_s, to_mem_layout(cache_mem_k[l]), to_mem_layout(cache_mem_v[l]))
        v_s = act_s[:, COL_V:COL_V + A_WIDTH]
        u_s = act_s[:, COL_UP:COL_UP + P_WIDTH]

        pool_p = jnp.concatenate([zero_hist, u_p.reshape(bp, tp, P_WIDTH)], axis=1)[:, -P_HIST:]
        pool_s = jnp.concatenate([hist_s, u_s.reshape(bs, ts, P_WIDTH)], axis=1)[:, -P_HIST:]
        k_p = jnp.transpose(k_p.reshape(bp, A_HEADS, 2, A_QK_DIM, tp), (0, 4, 1, 2, 3))
        outs = (k_p, v_p.reshape(bp, tp, A_HEADS, A_V_DIM),
                from_mem_layout(mk, bp), from_mem_layout(mv, bp),
                pool_p,
                k_s.reshape(bs, ts, A_HEADS, 2, A_QK_DIM), v_s.reshape(bs, ts, A_HEADS, A_V_DIM), pool_s)
        return y_p, y_s, outs

    yp = x_prompt.reshape(bp * tp, d)
    ys = x_sample.reshape(bs * ts, d)
    per_layer = []
    for l in range(w_in.shape[0]):
        yp, ys, outs = run_layer(l, yp, ys)
        per_layer.append(outs)
    stacked = [jnp.stack([o[n] for o in per_layer]) for n in range(8)]
    return (yp.reshape(bp, tp, d), ys.reshape(bs, ts, d), *stacked)
```

```python
import functools
import math

import numpy as np
import jax
import jax.numpy as jnp
from jax import lax
from jax.experimental import pallas as pl
from jax.experimental.pallas import tpu as pltpu

F32 = jnp.float32
BF16 = jnp.bfloat16

V7X_VMEM_LIMIT_BYTES = 56 * 1024 * 1024
LANES = 128

D_MODEL = 2048
CHUNK = 64
A_HEADS = 8
A_QK_DIM = 64
A_V_DIM = 128
A_WIDTH = A_HEADS * A_V_DIM
A_SCALE = A_QK_DIM ** -0.5
ROT_DIM = A_QK_DIM // 4
ROPE_THETA = 500000.0
SUBLN_EPS = 1e-5
P_WINDOWS = (2, 4, 8, 16)
P_GROUP_DIM = 256
P_WIDTH = len(P_WINDOWS) * P_GROUP_DIM
P_HIST = max(P_WINDOWS) - 1
M_TOKENS = 256
M_HEADS = 4
M_HEAD_DIM = 256
M_WIDTH = M_HEADS * M_HEAD_DIM
M_SCALE = M_HEAD_DIM ** -0.5
MEM_ROWS = M_TOKENS * M_HEADS * (M_HEAD_DIM // LANES)
N_BRANCH = 3
LN_EPS = 1e-5
ONES_ROWS = 16

COL_Q = 0
COL_K = COL_Q + A_WIDTH
COL_V = COL_K + A_WIDTH
COL_ZA = COL_V + A_WIDTH
COL_UP = COL_ZA + A_WIDTH
COL_ZP = COL_UP + P_WIDTH
COL_QM = COL_ZP + P_WIDTH
COL_ZM = COL_QM + M_WIDTH
COL_GM = COL_ZM + M_WIDTH
IN_COLS = COL_GM + N_BRANCH * D_MODEL
H_ZA = 0
H_GM = H_ZA + A_WIDTH
H_COLS = H_GM + N_BRANCH * D_MODEL

KEY_TILE = 256
ROW_TILE = 512
OUT_ROW_TILE = 256
COL_TILE = 1024
GATE_ROW_TILE = 2048
MEM_ROW_TILE = 512


def _params(*sem):
    return pltpu.CompilerParams(dimension_semantics=sem, vmem_limit_bytes=V7X_VMEM_LIMIT_BYTES)


def _sigmoid(z):
    return 1.0 / (1.0 + jnp.exp(-z))


def _silu(z):
    return z * _sigmoid(z)


def _proj_kernel(x_ref, w_ref, o_ref):
    acc = jnp.dot(x_ref[...].astype(BF16), w_ref[...].astype(BF16), preferred_element_type=F32)
    o_ref[...] = acc.astype(o_ref.dtype)


def _proj(x, w, col_block, ncols, out_dtype, tm, tn, name="proj"):
    n, kdim = x.shape
    return pl.pallas_call(
        _proj_kernel,
        out_shape=jax.ShapeDtypeStruct((n, ncols), out_dtype),
        grid=(n // tm, ncols // tn),
        in_specs=[pl.BlockSpec((tm, kdim), lambda i, j: (i, 0)),
                  pl.BlockSpec((kdim, tn), lambda i, j: (0, col_block(j)))],
        out_specs=pl.BlockSpec((tm, tn), lambda i, j: (i, j)),
        compiler_params=_params("arbitrary", "arbitrary"),
        name=name,
    )(x, w)


def _proj_memkv_kernel(x_ref, w_ref, mk_ref, mv_ref):
    kv = jnp.dot(x_ref[...].astype(BF16), w_ref[...].astype(BF16), preferred_element_type=F32)
    tm = kv.shape[0]
    halves = M_HEAD_DIM // LANES
    group = halves * M_HEADS

    def store(o_ref):
        for h in range(M_HEADS):
            for c in range(halves):
                col = h * M_HEAD_DIM + c * LANES
                o_ref[pl.ds(c * M_HEADS + h, tm, stride=group), :] = kv[:, col:col + LANES]

    @pl.when(pl.program_id(1) == 0)
    def _():
        store(mk_ref)

    @pl.when(pl.program_id(1) == 1)
    def _():
        store(mv_ref)


def _proj_memkv(mem, w, tm):
    n, kdim = mem.shape
    group = M_WIDTH // LANES
    row = pl.BlockSpec((tm * group, LANES), lambda i, j: (i, 0))
    return pl.pallas_call(
        _proj_memkv_kernel,
        out_shape=[jax.ShapeDtypeStruct((n * group, LANES), F32)] * 2,
        grid=(n // tm, 2),
        in_specs=[pl.BlockSpec((tm, kdim), lambda i, j: (i, 0)),
                  pl.BlockSpec((kdim, M_WIDTH), lambda i, j: (0, j))],
        out_specs=[row, row],
        compiler_params=_params("arbitrary", "arbitrary"),
        name="proj_memkv",
    )(mem, w)


def _rotate_sublanes(acc, c, s):
    half = ROT_DIM // 2
    pieces = []
    for g in range(acc.shape[0] // A_QK_DIM):
        b = g * A_QK_DIM
        lo, hi = acc[b:b + half], acc[b + half:b + ROT_DIM]
        pieces += [lo * c - hi * s, hi * c + lo * s, acc[b + ROT_DIM:b + A_QK_DIM]]
    return jnp.concatenate(pieces, axis=0)


def _proj_qk_kernel(x_ref, wq_ref, wk_ref, cos_ref, sin_ref, qt_ref, kt_ref, kb_ref, xb_ref, *, q_scale):
    xb = x_ref[...].astype(BF16)
    xb_ref[...] = xb
    c, s = cos_ref[...], sin_ref[...]
    q = _rotate_sublanes(jnp.dot(xb, wq_ref[...].astype(BF16), preferred_element_type=F32).T, c, s)
    q = (q * q_scale).astype(qt_ref.dtype)
    tile = qt_ref.shape[-1]
    for n in range(qt_ref.shape[0]):
        qt_ref[n] = q[:, n * tile:(n + 1) * tile]
    k = _rotate_sublanes(jnp.dot(xb, wk_ref[...].astype(BF16), preferred_element_type=F32).T, c, s)
    kt_ref[...] = k
    kb_ref[...] = k.T.astype(kb_ref.dtype)


def _proj_qk(x, w, cos_t, sin_t, batch, t, tm, q_tile, q_scale):
    n, kdim = x.shape
    width = A_WIDTH
    nt = t // tm
    tab = pl.BlockSpec((ROT_DIM // 2, tm), lambda i: (0, i % nt))
    wspec = lambda col: pl.BlockSpec((kdim, width), lambda i: (0, col // width), pipeline_mode=pl.Buffered(1))
    tspec = pl.BlockSpec((None, width, tm), lambda i: (i // nt, 0, i % nt))
    qspec = pl.BlockSpec((None, tm // q_tile, width, q_tile), lambda i: (i // nt, i % nt, 0, 0))
    return pl.pallas_call(
        functools.partial(_proj_qk_kernel, q_scale=q_scale),
        out_shape=[jax.ShapeDtypeStruct((batch, t // q_tile, width, q_tile), BF16),
                   jax.ShapeDtypeStruct((batch, width, t), F32),
                   jax.ShapeDtypeStruct((n, width), BF16), jax.ShapeDtypeStruct((n, kdim), BF16)],
        grid=(n // tm,),
        in_specs=[pl.BlockSpec((tm, kdim), lambda i: (i, 0)), wspec(COL_Q), wspec(COL_K), tab, tab],
        out_specs=[qspec, tspec, pl.BlockSpec((tm, width), lambda i: (i, 0)),
                   pl.BlockSpec((tm, kdim), lambda i: (i, 0))],
        compiler_params=_params("arbitrary"),
        name="proj_qk",
    )(x, w, w, cos_t, sin_t)


def _proj_vup_kernel(x_ref, wv_ref, wu_ref, wz_ref, hist_ref, pw_ref, ps_ref, v_ref, vt_ref, u_ref, bp_ref,
                     prev_ref, *, key_tile, nt):
    i = pl.program_id(0) % nt

    @pl.when(i == 0)
    def _():
        prev_ref[...] = hist_ref[...]

    xb = x_ref[...]
    tm = xb.shape[0]
    u = jnp.dot(xb, wu_ref[...].astype(BF16), preferred_element_type=F32)
    u_ref[...] = u
    zp = jnp.dot(xb, wz_ref[...].astype(BF16), preferred_element_type=F32)
    v = jnp.dot(xb, wv_ref[...].astype(BF16), preferred_element_type=F32)
    for h in range(A_HEADS):
        v_ref[pl.ds(h, tm, stride=A_HEADS), :] = v[:, h * A_V_DIM:(h + 1) * A_V_DIM]
    for c in range(tm // key_tile):
        vt_ref[c] = v[c * key_tile:(c + 1) * key_tile, :].T.astype(vt_ref.dtype)
    pos = i * tm + lax.broadcasted_iota(jnp.int32, (tm, 1), 0)
    bp_ref[...] = _pool_mix(u, prev_ref[...], zp, pw_ref, ps_ref, pos).astype(bp_ref.dtype)
    prev_ref[...] = u[tm - prev_ref.shape[0]:, :]


def _proj_vup(xb, w, hist, pool_w, pool_scale, batch, t, tm, key_tile):
    n, kdim = xb.shape
    width = A_WIDTH
    nt = t // tm
    hr = hist.shape[1]
    wspec = lambda col: pl.BlockSpec((kdim, width), lambda i: (0, col // width), pipeline_mode=pl.Buffered(1))
    row = pl.BlockSpec((tm, width), lambda i: (i, 0))
    return pl.pallas_call(
        functools.partial(_proj_vup_kernel, key_tile=key_tile, nt=nt),
        out_shape=[jax.ShapeDtypeStruct((n * A_HEADS, A_V_DIM), F32),
                   jax.ShapeDtypeStruct((batch, t // key_tile, width, key_tile), BF16),
                   jax.ShapeDtypeStruct((n, width), F32),
                   jax.ShapeDtypeStruct((n, width), BF16)],
        grid=(n // tm,),
        in_specs=[pl.BlockSpec((tm, kdim), lambda i: (i, 0)), wspec(COL_V), wspec(COL_UP), wspec(COL_ZP),
                  pl.BlockSpec((None, hr, P_WIDTH), lambda i: (i // nt, 0, 0)),
                  pl.BlockSpec((len(P_WINDOWS), P_GROUP_DIM, P_GROUP_DIM), lambda i: (0, 0, 0)),
                  pl.BlockSpec((1, P_WIDTH), lambda i: (0, 0))],
        out_specs=[pl.BlockSpec((tm * A_HEADS, A_V_DIM), lambda i: (i, 0)),
                   pl.BlockSpec((None, tm // key_tile, width, key_tile), lambda i: (i // nt, i % nt, 0, 0)),
                   row, row],
        scratch_shapes=[pltpu.VMEM((hr, P_WIDTH), F32)],
        compiler_params=_params("arbitrary"),
        name="proj_vup",
    )(xb, w, w, w, hist, pool_w, pool_scale)


def _rotary_angles(pos):
    half = ROT_DIM // 2
    inv = ROPE_THETA ** (-(jnp.arange(half, dtype=F32) * 2.0) / ROT_DIM)
    return pos.astype(F32)[:, None] * inv[None, :]


def _rotary_tables(pos):
    half = ROT_DIM // 2
    ang = _rotary_angles(pos)
    cos, sin = jnp.cos(ang), jnp.sin(ang)
    t = pos.shape[0]
    ones = jnp.ones((t, A_QK_DIM - ROT_DIM), F32)
    zeros = jnp.zeros((t, A_QK_DIM - ROT_DIM), F32)
    zh = jnp.zeros((t, half), F32)
    c64 = jnp.concatenate([cos, cos, ones], axis=1)
    sa64 = jnp.concatenate([zh, sin, zeros], axis=1)
    sb64 = jnp.concatenate([-sin, zh, zeros], axis=1)
    tile = lambda a: jnp.concatenate([a, a], axis=1)
    return tile(c64), tile(sa64), tile(sb64)


def _lambda_values(lam_ref):
    lv = lam_ref[...]
    a = jnp.sum(lv[0:1] * lv[1:2], axis=1, keepdims=True)
    b = jnp.sum(lv[2:3] * lv[3:4], axis=1, keepdims=True)
    lam_init = lv[4:5, 0:1]
    return jnp.exp(a) - jnp.exp(b) + lam_init, lam_init


def _split_components(q):
    lane = lax.broadcasted_iota(jnp.int32, q.shape, 1)
    qf = q.astype(F32)
    return jnp.concatenate([jnp.where(lane < A_QK_DIM, qf, 0.0),
                            jnp.where(lane >= A_QK_DIM, qf, 0.0)], axis=0).astype(q.dtype)


def _attn_finish(acc, l, tq, lam, gain, za):
    o = acc[:tq] / l[:tq] - lam * (acc[tq:] / l[tq:])
    o = o * lax.rsqrt(jnp.mean(o * o, axis=1, keepdims=True) + SUBLN_EPS) * gain
    return o * _silu(za)


def _attn_prompt_kernel(qt_ref, k_ref, vt_ref, za_ref, gain_ref, lam_ref, o_ref, m_ref, l_ref, acc_ref,
                        s0_ref, s1_ref, *, tq, heads, nq):
    i = pl.program_id(2)
    hs = [slice(n * LANES, (n + 1) * LANES) for n in range(heads)]

    def stacked_queries(qi):
        out = []
        for sl in hs:
            qt = qt_ref[qi, sl, :].astype(F32)
            row = lax.broadcasted_iota(jnp.int32, qt.shape, 0)
            out.append(jnp.concatenate([jnp.where(row < A_QK_DIM, qt, 0.0),
                                        jnp.where(row >= A_QK_DIM, qt, 0.0)], axis=1).astype(BF16))
        return out

    def score(j, qq):
        start = pl.multiple_of(j * tq, tq)
        sl, stacked = qq
        return jnp.dot(k_ref[pl.ds(start, tq), sl], stacked, preferred_element_type=F32)

    qqts = stacked_queries(i)

    @pl.when(i == 0)
    def _():
        for n in range(heads):
            s0_ref[n] = score(0, (hs[n], qqts[n]))

    m_ref[...] = jnp.full(m_ref.shape, -jnp.inf, F32)
    l_ref[...] = jnp.zeros(l_ref.shape, F32)
    acc_ref[...] = jnp.zeros(acc_ref.shape, F32)
    ones_rows = jnp.ones((ONES_ROWS, tq), BF16)

    def step(j, cur_ref, nxt_ref, masked):
        for n in range(heads):
            s = cur_ref[n]
            if masked:
                krow = lax.broadcasted_iota(jnp.int32, s.shape, 0)
                qcol = lax.broadcasted_iota(jnp.int32, s.shape, 1) % tq
                s = jnp.where((krow // CHUNK) <= (qcol // CHUNK), s, -jnp.inf)
            m = m_ref[n]
            m_new = jnp.maximum(m, jnp.max(s, axis=0, keepdims=True))
            alpha = jnp.exp2(m - m_new)
            p = jnp.exp2(s - m_new).astype(BF16)
            m_ref[n] = m_new
            lhs = jnp.concatenate([vt_ref[j, hs[n], :], ones_rows], axis=0)
            pv = jnp.dot(lhs, p, preferred_element_type=F32)
            acc_ref[n] = alpha * acc_ref[n] + pv[:A_V_DIM]
            l_ref[n] = alpha * l_ref[n] + pv[A_V_DIM:A_V_DIM + 1]
            if nxt_ref is not None:
                nxt_ref[n] = score(j + 1, (hs[n], qqts[n]))

    def pair(jj, carry):
        step(2 * jj, s0_ref, s1_ref, False)
        step(2 * jj + 1, s1_ref, s0_ref, False)
        return carry

    lax.fori_loop(0, i // 2, pair, 0)

    @pl.when(i % 2 == 1)
    def _():
        step(i - 1, s0_ref, s1_ref, False)
        step(i, s1_ref, None, True)

    @pl.when(i % 2 == 0)
    def _():
        step(i, s0_ref, None, True)

    nxt = stacked_queries(jnp.minimum(i + 1, nq - 1))
    lam, lam_init = _lambda_values(lam_ref)
    gain = gain_ref[...] * (1.0 - lam_init)
    for n in range(heads):
        s0_ref[n] = score(0, (hs[n], nxt[n]))
        inv, acc = 1.0 / l_ref[n], acc_ref[n]
        ot = acc[:, :tq] * inv[:, :tq] - lam * (acc[:, tq:] * inv[:, tq:])
        o = ot.T
        o = o * lax.rsqrt(jnp.mean(o * o, axis=1, keepdims=True) + SUBLN_EPS) * gain
        o_ref[:, hs[n]] = (o * _silu(za_ref[:, hs[n]].astype(F32))).astype(o_ref.dtype)


def _attn_prompt(qt, k, vt, h, gain, lamv, batch, t, tq, heads=A_HEADS):
    nq = t // tq
    n = batch * t
    hw = heads * LANES
    tiles = pl.BlockSpec((None, nq, hw, tq), lambda b, hh, i: (b, 0, hh, 0))
    return pl.pallas_call(
        functools.partial(_attn_prompt_kernel, tq=tq, heads=heads, nq=nq),
        out_shape=jax.ShapeDtypeStruct((n, A_WIDTH), BF16),
        grid=(batch, A_HEADS // heads, nq),
        in_specs=[
            tiles,
            pl.BlockSpec((t, hw), lambda b, hh, i: (b, hh)),
            tiles,
            pl.BlockSpec((tq, hw), lambda b, hh, i: (b * nq + i, H_ZA // hw + hh)),
            pl.BlockSpec((1, LANES), lambda b, hh, i: (0, 0)),
            pl.BlockSpec((5, A_QK_DIM), lambda b, hh, i: (0, 0)),
        ],
        out_specs=pl.BlockSpec((tq, hw), lambda b, hh, i: (b * nq + i, hh)),
        scratch_shapes=[pltpu.VMEM((heads, 1, 2 * tq), F32), pltpu.VMEM((heads, 1, 2 * tq), F32),
                        pltpu.VMEM((heads, A_V_DIM, 2 * tq), F32),
                        pltpu.VMEM((heads, tq, 2 * tq), F32), pltpu.VMEM((heads, tq, 2 * tq), F32)],
        compiler_params=_params("arbitrary", "arbitrary", "arbitrary"),
        name="attn_prompt",
    )(qt, k, vt, h, gain, lamv)


def _chunk_mask(s, ts, q_pos0, k_pos0):
    q_pos = q_pos0 + lax.broadcasted_iota(jnp.int32, s.shape, 0) % ts
    k_pos = k_pos0 + lax.broadcasted_iota(jnp.int32, s.shape, 1)
    return jnp.where((k_pos // CHUNK) <= (q_pos // CHUNK), s, -jnp.inf)


def _rotate_lanes(blk, c, sa, sb):
    half = ROT_DIM // 2
    return blk * c + pltpu.roll(blk, half, 1) * sa + pltpu.roll(blk, LANES - half, 1) * sb


def _mix_sample_kernel(q_ref, k_ref, v_ref, za_ref, u_ref, zp_ref, qm_ref, zm_ref, cos_ref, sa_ref, sb_ref,
                       kct_ref, vc_ref, hist_ref, mk_ref, mv_ref, gain_ref, lam_ref, pw_ref, ps_ref,
                       o_ref, krot_ref, bp_ref, bm_ref, *, ts, past, mask_cache, mask_new):
    pos = past + lax.broadcasted_iota(jnp.int32, (ts, 1), 0)
    bp_ref[...] = _pool_mix(u_ref[...], hist_ref[...], zp_ref[...], pw_ref, ps_ref, pos).astype(bp_ref.dtype)
    _mem_attend(qm_ref[...].astype(BF16), lambda: zm_ref[...], mk_ref, mv_ref, bm_ref)

    c, sa, sb = cos_ref[...], sa_ref[...], sb_ref[...]
    nt = (((1,), (1,)), ((), ()))
    hs = [slice(n * LANES, (n + 1) * LANES) for n in range(A_HEADS)]
    scores = []
    for n in range(A_HEADS):
        q = (_rotate_lanes(q_ref[:, hs[n]], c, sa, sb) * A_SCALE).astype(BF16)
        k = _rotate_lanes(k_ref[:, hs[n]], c, sa, sb)
        krot_ref[:, hs[n]] = k
        qq = _split_components(q)
        s_c = jnp.dot(qq, kct_ref[n].astype(BF16), preferred_element_type=F32)
        s_n = lax.dot_general(qq, k.astype(BF16), nt, preferred_element_type=F32)
        if mask_cache:
            s_c = _chunk_mask(s_c, ts, past, 0)
        if mask_new:
            s_n = _chunk_mask(s_n, ts, past, past)
        scores.append((s_c, s_n))
    lam, lam_init = _lambda_values(lam_ref)
    gain = gain_ref[...] * (1.0 - lam_init)
    for n, (s_c, s_n) in enumerate(scores):
        m = jnp.maximum(jnp.max(s_c, axis=1, keepdims=True), jnp.max(s_n, axis=1, keepdims=True))
        p_c = jnp.exp(s_c - m)
        p_n = jnp.exp(s_n - m)
        l = jnp.sum(p_c, axis=1, keepdims=True) + jnp.sum(p_n, axis=1, keepdims=True)
        vc = vc_ref[pl.ds(n, past, stride=A_HEADS), :].astype(BF16)
        acc = (jnp.dot(p_c.astype(BF16), vc, preferred_element_type=F32)
               + jnp.dot(p_n.astype(BF16), v_ref[:, hs[n]].astype(BF16), preferred_element_type=F32))
        out = _attn_finish(acc, l, ts, lam, gain, za_ref[:, hs[n]].astype(F32))
        o_ref[:, hs[n]] = out.astype(o_ref.dtype)


def _needs_mask(q_pos, k_pos):
    return not bool(((k_pos[None, :] // CHUNK) <= (q_pos[:, None] // CHUNK)).all())


def _mix_sample(act, rot, kct, vc, hist, mk, mv, gain, lamv, pool_w, pool_scale, batch, ts, past):
    q_pos = past + np.arange(ts)
    seg = lambda col: pl.BlockSpec((ts, A_WIDTH), lambda b: (b, col // A_WIDTH))
    tab = pl.BlockSpec((ts, LANES), lambda b: (0, 0))
    row = pl.BlockSpec((ts, A_WIDTH), lambda b: (b, 0))
    mem = pl.BlockSpec((MEM_ROWS, LANES), lambda b: (b, 0))
    bshape = jax.ShapeDtypeStruct((batch * ts, A_WIDTH), BF16)
    return pl.pallas_call(
        functools.partial(_mix_sample_kernel, ts=ts, past=past,
                          mask_cache=_needs_mask(q_pos, np.arange(past)),
                          mask_new=_needs_mask(q_pos, q_pos)),
        out_shape=[bshape, jax.ShapeDtypeStruct((batch * ts, A_WIDTH), F32), bshape, bshape],
        grid=(batch,),
        in_specs=[
            seg(COL_Q), seg(COL_K), seg(COL_V), seg(COL_ZA), seg(COL_UP), seg(COL_ZP), seg(COL_QM), seg(COL_ZM),
            tab, tab, tab,
            pl.BlockSpec((None, A_HEADS, LANES, past), lambda b: (b, 0, 0, 0)),
            pl.BlockSpec((None, past * A_HEADS, LANES), lambda b: (b, 0, 0)),
            pl.BlockSpec((None, hist.shape[1], P_WIDTH), lambda b: (b, 0, 0)),
            mem, mem,
            pl.BlockSpec((1, LANES), lambda b: (0, 0)),
            pl.BlockSpec((5, A_QK_DIM), lambda b: (0, 0)),
            pl.BlockSpec((len(P_WINDOWS), P_GROUP_DIM, P_GROUP_DIM), lambda b: (0, 0, 0)),
            pl.BlockSpec((1, P_WIDTH), lambda b: (0, 0)),
        ],
        out_specs=[row, row, row, row],
        compiler_params=_params("arbitrary"),
        name="mix_sample",
    )(*([act] * 8), *rot, kct, vc, hist, mk, mv, gain, lamv, pool_w, pool_scale)


def _pool_mix(u, prev, zp, pw_ref, ps_ref, pos):
    tm = u.shape[0]
    ext = jnp.concatenate([prev, u], axis=0)
    hist_rows = prev.shape[0]
    parts = []
    for g, w in enumerate(P_WINDOWS):
        sl = slice(g * P_GROUP_DIM, (g + 1) * P_GROUP_DIM)
        tot = ext[:, sl]
        span = 1
        while span < w:
            tot = tot + pltpu.roll(tot, span, 0)
            span *= 2
        tot = tot[hist_rows:]
        cnt = jnp.minimum(pos + 1, w).astype(F32)
        pooled = tot / cnt - u[:, sl]
        parts.append(jnp.dot(pooled.astype(BF16), pw_ref[g].astype(BF16), preferred_element_type=F32))
    mixed = jnp.concatenate(parts, axis=1) * ps_ref[...]
    return mixed * _silu(zp)


def _mem_head(ref, h):
    halves = M_HEAD_DIM // LANES
    parts = [ref[pl.ds(c * M_HEADS + h, M_TOKENS, stride=halves * M_HEADS), :] for c in range(halves)]
    return jnp.concatenate(parts, axis=1)


def _mem_attend(q, gate_input, mk_ref, mv_ref, o_ref):
    nt = (((1,), (1,)), ((), ()))
    hs = [slice(n * M_HEAD_DIM, (n + 1) * M_HEAD_DIM) for n in range(M_HEADS)]
    scores = [lax.dot_general(q[:, sl], _mem_head(mk_ref, n).astype(BF16), nt,
                              preferred_element_type=F32) * M_SCALE for n, sl in enumerate(hs)]
    zm = gate_input()
    for n, (sl, s) in enumerate(zip(hs, scores)):
        m = jnp.max(s, axis=1, keepdims=True)
        p = jnp.exp(s - m)
        l = jnp.sum(p, axis=1, keepdims=True)
        o = jnp.dot(p.astype(BF16), _mem_head(mv_ref, n).astype(BF16), preferred_element_type=F32) / l
        o_ref[:, sl] = (o * _silu(zm[:, sl])).astype(o_ref.dtype)


def _proj_mem_kernel(x_ref, wq_ref, wz_ref, mk_ref, mv_ref, xs_ref, wt_ref, o_ref, act_ref):
    act_ref[...] = jnp.dot(xs_ref[...], wt_ref[...].astype(BF16), preferred_element_type=F32)
    xb = x_ref[...]
    q = jnp.dot(xb, wq_ref[...].astype(BF16), preferred_element_type=F32).astype(BF16)
    _mem_attend(q, lambda: jnp.dot(xb, wz_ref[...].astype(BF16), preferred_element_type=F32),
                mk_ref, mv_ref, o_ref)


def _proj_mem(xb, w, mk, mv, xs, batch, t, tm, tn):
    n, kdim = xb.shape
    ns = xs.shape[0]
    nt = t // tm
    tiles = w.shape[1] // tn
    assert batch * nt >= tiles, "not enough grid steps to carry the sample projection"
    rider = lambda b, i: (0, jnp.minimum(b * nt + i, tiles - 1))
    wspec = lambda col: pl.BlockSpec((kdim, M_WIDTH), lambda b, i: (0, col // M_WIDTH),
                                     pipeline_mode=pl.Buffered(1))
    mem = pl.BlockSpec((MEM_ROWS, LANES), lambda b, i: (b, 0))
    return pl.pallas_call(
        _proj_mem_kernel,
        out_shape=[jax.ShapeDtypeStruct((n, M_WIDTH), BF16), jax.ShapeDtypeStruct((ns, w.shape[1]), F32)],
        grid=(batch, nt),
        in_specs=[pl.BlockSpec((tm, kdim), lambda b, i: (b * nt + i, 0)), wspec(COL_QM), wspec(COL_ZM), mem, mem,
                  pl.BlockSpec((ns, kdim), lambda b, i: (0, 0), pipeline_mode=pl.Buffered(1)),
                  pl.BlockSpec((kdim, tn), rider)],
        out_specs=[pl.BlockSpec((tm, M_WIDTH), lambda b, i: (b * nt + i, 0)), pl.BlockSpec((ns, tn), rider)],
        compiler_params=_params("arbitrary", "arbitrary"),
        name="proj_mem",
    )(xb, w, w, mk, mv, xs, w)


def _gated_sum(b_refs, g_refs, w_ref, rows):
    projs = [jnp.dot(b_ref[rows, :], w_ref[n], preferred_element_type=F32) for n, b_ref in enumerate(b_refs)]
    acc = None
    for g_ref, proj in zip(g_refs, projs):
        term = _sigmoid(g_ref[rows, :].astype(F32)) * proj
        acc = term if acc is None else acc + term
    return acc


def _merge_kernel(ba_ref, bp_ref, bm_ref, g0_ref, g1_ref, g2_ref, wb_ref,
                  sa_ref, sp_ref, sm_ref, sg0_ref, sg1_ref, sg2_ref, o_ref, so_ref, wbb_ref, *, halves):
    @pl.when(pl.program_id(1) == 0)
    def _():
        wbb_ref[...] = wb_ref[...].astype(BF16)
        so_ref[...] = _gated_sum((sa_ref, sp_ref, sm_ref), (sg0_ref, sg1_ref, sg2_ref), wbb_ref,
                                 slice(None)).astype(so_ref.dtype)

    b_refs, g_refs = (ba_ref, bp_ref, bm_ref), (g0_ref, g1_ref, g2_ref)
    rows = o_ref.shape[0] // halves
    projs = [[jnp.dot(b_ref[r * rows:(r + 1) * rows, :], wbb_ref[n], preferred_element_type=F32)
              for n, b_ref in enumerate(b_refs)] for r in range(halves)]
    for r in range(halves):
        sl = slice(r * rows, (r + 1) * rows)
        acc = None
        for n, g_ref in enumerate(g_refs):
            term = _sigmoid(g_ref[sl, :].astype(F32)) * projs[r][n]
            acc = term if acc is None else acc + term
        o_ref[sl, :] = acc.astype(o_ref.dtype)


def _merge(ba, bp, bm, h, g_col, sa, sp, sm, sh, sg_col, wb, tm, tn):
    n, width = ba.shape
    ns = sa.shape[0]
    gspec = lambda k: pl.BlockSpec((tm, tn), lambda j, i: (i, (g_col + k * D_MODEL) // tn + j))
    sgspec = lambda k: pl.BlockSpec((ns, tn), lambda j, i: (0, (sg_col + k * D_MODEL) // tn + j))
    bspec = pl.BlockSpec((tm, width), lambda j, i: (i, 0))
    sbspec = pl.BlockSpec((ns, width), lambda j, i: (0, 0), pipeline_mode=pl.Buffered(1))
    return pl.pallas_call(
        functools.partial(_merge_kernel, halves=2 if tm % 32 == 0 else 1),
        out_shape=[jax.ShapeDtypeStruct((n, D_MODEL), BF16), jax.ShapeDtypeStruct((ns, D_MODEL), BF16)],
        grid=(D_MODEL // tn, n // tm),
        in_specs=[bspec, bspec, bspec, gspec(0), gspec(1), gspec(2),
                  pl.BlockSpec((N_BRANCH, width, tn), lambda j, i: (0, 0, j), pipeline_mode=pl.Buffered(1)),
                  sbspec, sbspec, sbspec, sgspec(0), sgspec(1), sgspec(2)],
        out_specs=[pl.BlockSpec((tm, tn), lambda j, i: (i, j)), pl.BlockSpec((ns, tn), lambda j, i: (0, j))],
        scratch_shapes=[pltpu.VMEM((N_BRANCH, width, tn), BF16)],
        compiler_params=_params("arbitrary", "arbitrary"),
        name="merge",
    )(ba, bp, bm, h, h, h, wb, sa, sp, sm, sh, sh, sh)


def _layer_norm_residual(x, proj, g_ref, b_ref, alpha):
    z = alpha * x + proj
    mu = jnp.mean(z, axis=1, keepdims=True)
    zc = z - mu
    var = jnp.mean(zc * zc, axis=1, keepdims=True)
    return zc * lax.rsqrt(var + LN_EPS) * g_ref[...] + b_ref[...]


def _out_kernel(m_ref, w_ref, x_ref, g_ref, b_ref, sm_ref, sx_ref, o_ref, so_ref, wb_ref, *, halves, alpha):
    @pl.when(pl.program_id(0) == 0)
    def _():
        wb_ref[...] = w_ref[...].astype(BF16)
        so_ref[...] = _layer_norm_residual(
            sx_ref[...], jnp.dot(sm_ref[...], wb_ref[...], preferred_element_type=F32), g_ref, b_ref, alpha)

    rows = o_ref.shape[0] // halves
    outs = [jnp.dot(m_ref[r * rows:(r + 1) * rows, :], wb_ref[...], preferred_element_type=F32)
            for r in range(halves)]
    for r in range(halves):
        sl = slice(r * rows, (r + 1) * rows)
        o_ref[sl, :] = _layer_norm_residual(x_ref[sl, :], outs[r], g_ref, b_ref, alpha)


def _out(merged, x, s_merged, s_x, w_out, ln_g, ln_b, alpha, tm):
    n = merged.shape[0]
    ns = s_merged.shape[0]
    row = pl.BlockSpec((tm, D_MODEL), lambda i: (i, 0))
    srow = pl.BlockSpec((ns, D_MODEL), lambda i: (0, 0), pipeline_mode=pl.Buffered(1))
    vec = pl.BlockSpec((1, D_MODEL), lambda i: (0, 0))
    wspec = pl.BlockSpec((D_MODEL, D_MODEL), lambda i: (0, 0), pipeline_mode=pl.Buffered(1))
    return pl.pallas_call(
        functools.partial(_out_kernel, halves=2, alpha=alpha),
        out_shape=[jax.ShapeDtypeStruct((n, D_MODEL), F32), jax.ShapeDtypeStruct((ns, D_MODEL), F32)],
        grid=(n // tm,),
        in_specs=[row, wspec, row, vec, vec, srow, srow],
        out_specs=[row, pl.BlockSpec((ns, D_MODEL), lambda i: (0, 0))],
        scratch_shapes=[pltpu.VMEM((D_MODEL, D_MODEL), BF16)],
        compiler_params=_params("arbitrary"),
        name="out_ln",
    )(merged, w_out, x, ln_g, ln_b, s_merged, s_x)


def _rest_col_block(tn):
    za_blocks = (COL_UP - COL_ZA) // tn
    return lambda j: jnp.where(j < za_blocks, COL_ZA // tn + j, COL_GM // tn + (j - za_blocks))


def _layer(xp2d, xs2d, bp, tp, bs, ts, past, w, zero_hist, hist_s, mk, mv, kt_hist, v_hist, mk_s, mv_s):
    ang = _rotary_angles(jnp.arange(tp))
    cos_t, sin_t = jnp.cos(ang).T, jnp.sin(ang).T
    qt, kt, kb, xb = _proj_qk(xp2d, w["w_in"], cos_t, sin_t, bp, tp, ROW_TILE, KEY_TILE,
                              A_SCALE * math.log2(math.e))
    v, vt, u, b_p = _proj_vup(xb, w["w_in"], zero_hist, w["pool_w"], w["pool_scale"], bp, tp, ROW_TILE, KEY_TILE)
    b_m, act_s = _proj_mem(xb, w["w_in"], mk, mv, xs2d.astype(BF16), bp, tp, ROW_TILE, COL_TILE)
    h = _proj(xb, w["w_in"], _rest_col_block(COL_TILE), H_COLS, BF16, GATE_ROW_TILE, COL_TILE, name="proj_rest")
    b_a = _attn_prompt(qt, kb, vt, h, w["gain"], w["lamv"], bp, tp, KEY_TILE)

    rot_s = _rotary_tables(past + jnp.arange(ts))
    s_a, k_s, s_p, s_m = _mix_sample(act_s, rot_s, kt_hist, v_hist, hist_s, mk_s, mv_s, w["gain"], w["lamv"],
                                     w["pool_w"], w["pool_scale"], bs, ts, past)

    merged, s_merged = _merge(b_a, b_p, b_m, h, H_GM, s_a, s_p, s_m, act_s, COL_GM, w["w_branch"],
                              ROW_TILE, COL_TILE)
    y_p, y_s = _out(merged, xp2d, s_merged, xs2d, w["w_out"], w["ln_g"], w["ln_b"], w["alpha"], OUT_ROW_TILE)
    return y_p, y_s, kt, v, u, k_s, act_s


def kernel(x_prompt, x_sample, cache_attn_k, cache_attn_v, cache_mem_k, cache_mem_v, state_pool, mem_prompt, w_in, w_mem_kv, lambda_q1, lambda_k1, lambda_q2, lambda_k2, subln_gain, pool_w, pool_scale, w_branch, w_out, ln_gain, ln_bias):
    bp, tp, d = x_prompt.shape
    bs, ts, _ = x_sample.shape
    past = cache_attn_k.shape[2]
    hist_rows = P_HIST + 1
    halves = M_HEAD_DIM // LANES

    def to_mem_layout(a):
        a = a.reshape(a.shape[0], M_TOKENS, M_HEADS, halves, LANES)
        return jnp.transpose(a, (0, 1, 3, 2, 4)).reshape(a.shape[0] * MEM_ROWS, LANES)

    def from_mem_layout(a, batch):
        a = a.reshape(batch, M_TOKENS, halves, M_HEADS, LANES)
        return jnp.transpose(a, (0, 1, 3, 2, 4)).reshape(batch, M_TOKENS, M_HEADS, M_HEAD_DIM)

    def run_layer(l, xp2d, xs2d):
        w = {
            "w_in": w_in[l],
            "gain": subln_gain[l].reshape(1, A_V_DIM),
            "lamv": jnp.stack([lambda_q1[l], lambda_k1[l], lambda_q2[l], lambda_k2[l],
                               jnp.full((A_QK_DIM,), 0.8 - 0.6 * math.exp(-0.3 * l), F32)]),
            "pool_w": pool_w[l],
            "pool_scale": pool_scale[l].reshape(1, P_WIDTH),
            "w_branch": w_branch[l],
            "w_out": w_out[l],
            "ln_g": ln_gain[l].reshape(1, d),
            "ln_b": ln_bias[l].reshape(1, d),
            "alpha": (2.0 * w_in.shape[0]) ** 0.25,
        }
        memf = mem_prompt.reshape(bp * M_TOKENS, d)
        mk, mv = _proj_memkv(memf, w_mem_kv[l], MEM_ROW_TILE)

        zero_hist = jnp.zeros((bp, hist_rows, P_WIDTH), F32)
        hist_s = jnp.pad(state_pool[l], ((0, 0), (hist_rows - P_HIST, 0), (0, 0)))
        kt_hist = jnp.transpose(cache_attn_k[l], (0, 2, 3, 4, 1)).reshape(bs, A_HEADS, 2 * A_QK_DIM, past)
        y_p, y_s, k_p, v_p, u_p, k_s, act_s = _layer(
            xp2d, xs2d, bp, tp, bs, ts, past, w, zero_hist, hist_s, mk, mv,
            kt_hist, cache_attn_v[l].reshape(bs, past * A_HEADS, A_V_DIM),
            to_mem_layout(cache_mem_k[l]), to_mem_layout(cache_mem_v[l]))
        v_s = act_s[:, COL_V:COL_V + A_WIDTH]
        u_s = act_s[:, COL_UP:COL_UP + P_WIDTH]

        pool_p = jnp.concatenate([zero_hist, u_p.reshape(bp, tp, P_WIDTH)], axis=1)[:, -P_HIST:]
        pool_s = jnp.concatenate([hist_s, u_s.reshape(bs, ts, P_WIDTH)], axis=1)[:, -P_HIST:]
        k_p = jnp.transpose(k_p.reshape(bp, A_HEADS, 2, A_QK_DIM, tp), (0, 4, 1, 2, 3))
        outs = (k_p, v_p.reshape(bp, tp, A_HEADS, A_V_DIM),
                from_mem_layout(mk, bp), from_mem_layout(mv, bp),
                pool_p,
                k_s.reshape(bs, ts, A_HEADS, 2, A_QK_DIM), v_s.reshape(bs, ts, A_HEADS, A_V_DIM), pool_s)
        return y_p, y_s, outs

    yp = x_prompt.reshape(bp * tp, d)
    ys = x_sample.reshape(bs * ts, d)
    per_layer = []
    for l in range(w_in.shape[0]):
        yp, ys, outs = run_layer(l, yp, ys)
        per_layer.append(outs)
    stacked = [jnp.stack([o[n] for o in per_layer]) for n in range(8)]
    return (yp.reshape(bp, tp, d), ys.reshape(bs, ts, d), *stacked)
```

```python
import functools
import math

import numpy as np
import jax
import jax.numpy as jnp
from jax import lax
from jax.experimental import pallas as pl
from jax.experimental.pallas import tpu as pltpu

F32 = jnp.float32
BF16 = jnp.bfloat16

V7X_VMEM_LIMIT_BYTES = 56 * 1024 * 1024
LANES = 128

D_MODEL = 2048
CHUNK = 64
A_HEADS = 8
A_QK_DIM = 64
A_V_DIM = 128
A_WIDTH = A_HEADS * A_V_DIM
A_SCALE = A_QK_DIM ** -0.5
ROT_DIM = A_QK_DIM // 4
ROPE_THETA = 500000.0
SUBLN_EPS = 1e-5
P_WINDOWS = (2, 4, 8, 16)
P_GROUP_DIM = 256
P_WIDTH = len(P_WINDOWS) * P_GROUP_DIM
P_HIST = max(P_WINDOWS) - 1
M_TOKENS = 256
M_HEADS = 4
M_HEAD_DIM = 256
M_WIDTH = M_HEADS * M_HEAD_DIM
M_SCALE = M_HEAD_DIM ** -0.5
MEM_ROWS = M_TOKENS * M_HEADS * (M_HEAD_DIM // LANES)
N_BRANCH = 3
LN_EPS = 1e-5
ONES_ROWS = 16

COL_Q = 0
COL_K = COL_Q + A_WIDTH
COL_V = COL_K + A_WIDTH
COL_ZA = COL_V + A_WIDTH
COL_UP = COL_ZA + A_WIDTH
COL_ZP = COL_UP + P_WIDTH
COL_QM = COL_ZP + P_WIDTH
COL_ZM = COL_QM + M_WIDTH
COL_GM = COL_ZM + M_WIDTH
IN_COLS = COL_GM + N_BRANCH * D_MODEL
H_ZA = 0
H_GM = H_ZA + A_WIDTH
H_COLS = H_GM + N_BRANCH * D_MODEL

KEY_TILE = 256
ROW_TILE = 512
OUT_ROW_TILE = 256
COL_TILE = 1024
GATE_ROW_TILE = 2048
MEM_ROW_TILE = 512


def _params(*sem):
    return pltpu.CompilerParams(dimension_semantics=sem, vmem_limit_bytes=V7X_VMEM_LIMIT_BYTES)


def _sigmoid(z):
    return 1.0 / (1.0 + jnp.exp(-z))


def _silu(z):
    return z * _sigmoid(z)


def _proj_kernel(x_ref, w_ref, o_ref):
    acc = jnp.dot(x_ref[...].astype(BF16), w_ref[...].astype(BF16), preferred_element_type=F32)
    o_ref[...] = acc.astype(o_ref.dtype)


def _proj(x, w, col_block, ncols, out_dtype, tm, tn, name="proj"):
    n, kdim = x.shape
    return pl.pallas_call(
        _proj_kernel,
        out_shape=jax.ShapeDtypeStruct((n, ncols), out_dtype),
        grid=(n // tm, ncols // tn),
        in_specs=[pl.BlockSpec((tm, kdim), lambda i, j: (i, 0)),
                  pl.BlockSpec((kdim, tn), lambda i, j: (0, col_block(j)))],
        out_specs=pl.BlockSpec((tm, tn), lambda i, j: (i, j)),
        compiler_params=_params("arbitrary", "arbitrary"),
        name=name,
    )(x, w)


def _proj_memkv_kernel(x_ref, w_ref, mk_ref, mv_ref):
    kv = jnp.dot(x_ref[...].astype(BF16), w_ref[...].astype(BF16), preferred_element_type=F32)
    tm = kv.shape[0]
    halves = M_HEAD_DIM // LANES
    group = halves * M_HEADS

    def store(o_ref):
        for h in range(M_HEADS):
            for c in range(halves):
                col = h * M_HEAD_DIM + c * LANES
                o_ref[pl.ds(c * M_HEADS + h, tm, stride=group), :] = kv[:, col:col + LANES]

    @pl.when(pl.program_id(1) == 0)
    def _():
        store(mk_ref)

    @pl.when(pl.program_id(1) == 1)
    def _():
        store(mv_ref)


def _proj_memkv(mem, w, tm):
    n, kdim = mem.shape
    group = M_WIDTH // LANES
    row = pl.BlockSpec((tm * group, LANES), lambda i, j: (i, 0))
    return pl.pallas_call(
        _proj_memkv_kernel,
        out_shape=[jax.ShapeDtypeStruct((n * group, LANES), F32)] * 2,
        grid=(n // tm, 2),
        in_specs=[pl.BlockSpec((tm, kdim), lambda i, j: (i, 0)),
                  pl.BlockSpec((kdim, M_WIDTH), lambda i, j: (0, j))],
        out_specs=[row, row],
        compiler_params=_params("arbitrary", "arbitrary"),
        name="proj_memkv",
    )(mem, w)


def _rotate_sublanes(acc, c, s):
    half = ROT_DIM // 2
    pieces = []
    for g in range(acc.shape[0] // A_QK_DIM):
        b = g * A_QK_DIM
        lo, hi = acc[b:b + half], acc[b + half:b + ROT_DIM]
        pieces += [lo * c - hi * s, hi * c + lo * s, acc[b + ROT_DIM:b + A_QK_DIM]]
    return jnp.concatenate(pieces, axis=0)


def _proj_qk_kernel(x_ref, wq_ref, wk_ref, cos_ref, sin_ref, qt_ref, kt_ref, kb_ref, xb_ref, *, q_scale):
    xb = x_ref[...].astype(BF16)
    xb_ref[...] = xb
    c, s = cos_ref[...], sin_ref[...]
    q = _rotate_sublanes(jnp.dot(xb, wq_ref[...].astype(BF16), preferred_element_type=F32).T, c, s)
    q = (q * q_scale).astype(qt_ref.dtype)
    tile = qt_ref.shape[-1]
    for n in range(qt_ref.shape[0]):
        qt_ref[n] = q[:, n * tile:(n + 1) * tile]
    k = _rotate_sublanes(jnp.dot(xb, wk_ref[...].astype(BF16), preferred_element_type=F32).T, c, s)
    kt_ref[...] = k
    kb_ref[...] = k.T.astype(kb_ref.dtype)


def _proj_qk(x, w, cos_t, sin_t, batch, t, tm, q_tile, q_scale):
    n, kdim = x.shape
    width = A_WIDTH
    nt = t // tm
    tab = pl.BlockSpec((ROT_DIM // 2, tm), lambda i: (0, i % nt))
    wspec = lambda col: pl.BlockSpec((kdim, width), lambda i: (0, col // width), pipeline_mode=pl.Buffered(1))
    tspec = pl.BlockSpec((None, width, tm), lambda i: (i // nt, 0, i % nt))
    qspec = pl.BlockSpec((None, tm // q_tile, width, q_tile), lambda i: (i // nt, i % nt, 0, 0))
    return pl.pallas_call(
        functools.partial(_proj_qk_kernel, q_scale=q_scale),
        out_shape=[jax.ShapeDtypeStruct((batch, t // q_tile, width, q_tile), BF16),
                   jax.ShapeDtypeStruct((batch, width, t), F32),
                   jax.ShapeDtypeStruct((n, width), BF16), jax.ShapeDtypeStruct((n, kdim), BF16)],
        grid=(n // tm,),
        in_specs=[pl.BlockSpec((tm, kdim), lambda i: (i, 0)), wspec(COL_Q), wspec(COL_K), tab, tab],
        out_specs=[qspec, tspec, pl.BlockSpec((tm, width), lambda i: (i, 0)),
                   pl.BlockSpec((tm, kdim), lambda i: (i, 0))],
        compiler_params=_params("arbitrary"),
        name="proj_qk",
    )(x, w, w, cos_t, sin_t)


def _proj_vup_kernel(x_ref, wv_ref, wu_ref, wz_ref, hist_ref, pw_ref, ps_ref, v_ref, vt_ref, ut_ref, bp_ref,
                     prev_ref, *, key_tile, nt):
    i = pl.program_id(0) % nt

    @pl.when(i == 0)
    def _():
        prev_ref[...] = hist_ref[...]

    xb = x_ref[...]
    tm = xb.shape[0]
    u = jnp.dot(xb, wu_ref[...].astype(BF16), preferred_element_type=F32)
    ut_ref[...] = u[tm - ut_ref.shape[0]:, :]
    zp = jnp.dot(xb, wz_ref[...].astype(BF16), preferred_element_type=F32)
    v = jnp.dot(xb, wv_ref[...].astype(BF16), preferred_element_type=F32)
    for h in range(A_HEADS):
        v_ref[pl.ds(h, tm, stride=A_HEADS), :] = v[:, h * A_V_DIM:(h + 1) * A_V_DIM]
    for c in range(tm // key_tile):
        vt_ref[c] = v[c * key_tile:(c + 1) * key_tile, :].T.astype(vt_ref.dtype)
    pos = i * tm + lax.broadcasted_iota(jnp.int32, (tm, 1), 0)
    bp_ref[...] = _pool_mix(u, prev_ref[...], zp, pw_ref, ps_ref, pos).astype(bp_ref.dtype)
    prev_ref[...] = u[tm - prev_ref.shape[0]:, :]


def _proj_vup(xb, w, hist, pool_w, pool_scale, batch, t, tm, key_tile):
    assert tm >= hist.shape[1]
    n, kdim = xb.shape
    width = A_WIDTH
    nt = t // tm
    hr = hist.shape[1]
    wspec = lambda col: pl.BlockSpec((kdim, width), lambda i: (0, col // width), pipeline_mode=pl.Buffered(1))
    row = pl.BlockSpec((tm, width), lambda i: (i, 0))
    return pl.pallas_call(
        functools.partial(_proj_vup_kernel, key_tile=key_tile, nt=nt),
        out_shape=[jax.ShapeDtypeStruct((n * A_HEADS, A_V_DIM), F32),
                   jax.ShapeDtypeStruct((batch, t // key_tile, width, key_tile), BF16),
                   jax.ShapeDtypeStruct((batch, hr, width), F32),
                   jax.ShapeDtypeStruct((n, width), BF16)],
        grid=(n // tm,),
        in_specs=[pl.BlockSpec((tm, kdim), lambda i: (i, 0)), wspec(COL_V), wspec(COL_UP), wspec(COL_ZP),
                  pl.BlockSpec((None, hr, P_WIDTH), lambda i: (i // nt, 0, 0)),
                  pl.BlockSpec((len(P_WINDOWS), P_GROUP_DIM, P_GROUP_DIM), lambda i: (0, 0, 0)),
                  pl.BlockSpec((1, P_WIDTH), lambda i: (0, 0))],
        out_specs=[pl.BlockSpec((tm * A_HEADS, A_V_DIM), lambda i: (i, 0)),
                   pl.BlockSpec((None, tm // key_tile, width, key_tile), lambda i: (i // nt, i % nt, 0, 0)),
                   pl.BlockSpec((None, hr, width), lambda i: (i // nt, 0, 0)), row],
        scratch_shapes=[pltpu.VMEM((hr, P_WIDTH), F32)],
        compiler_params=_params("arbitrary"),
        name="proj_vup",
    )(xb, w, w, w, hist, pool_w, pool_scale)


def _rotary_angles(pos):
    half = ROT_DIM // 2
    inv = ROPE_THETA ** (-(jnp.arange(half, dtype=F32) * 2.0) / ROT_DIM)
    return pos.astype(F32)[:, None] * inv[None, :]


def _rotary_tables(pos):
    half = ROT_DIM // 2
    ang = _rotary_angles(pos)
    cos, sin = jnp.cos(ang), jnp.sin(ang)
    t = pos.shape[0]
    ones = jnp.ones((t, A_QK_DIM - ROT_DIM), F32)
    zeros = jnp.zeros((t, A_QK_DIM - ROT_DIM), F32)
    zh = jnp.zeros((t, half), F32)
    c64 = jnp.concatenate([cos, cos, ones], axis=1)
    sa64 = jnp.concatenate([zh, sin, zeros], axis=1)
    sb64 = jnp.concatenate([-sin, zh, zeros], axis=1)
    tile = lambda a: jnp.concatenate([a, a], axis=1)
    return tile(c64), tile(sa64), tile(sb64)


def _lambda_values(lam_ref):
    lv = lam_ref[...]
    a = jnp.sum(lv[0:1] * lv[1:2], axis=1, keepdims=True)
    b = jnp.sum(lv[2:3] * lv[3:4], axis=1, keepdims=True)
    lam_init = lv[4:5, 0:1]
    return jnp.exp(a) - jnp.exp(b) + lam_init, lam_init


def _split_components(q):
    lane = lax.broadcasted_iota(jnp.int32, q.shape, 1)
    qf = q.astype(F32)
    return jnp.concatenate([jnp.where(lane < A_QK_DIM, qf, 0.0),
                            jnp.where(lane >= A_QK_DIM, qf, 0.0)], axis=0).astype(q.dtype)


def _attn_finish(acc, l, tq, lam, gain, za):
    o = acc[:tq] / l[:tq] - lam * (acc[tq:] / l[tq:])
    o = o * lax.rsqrt(jnp.mean(o * o, axis=1, keepdims=True) + SUBLN_EPS) * gain
    return o * _silu(za)


def _attn_prompt_kernel(qt_ref, k_ref, vt_ref, za_ref, gain_ref, lam_ref, o_ref, m_ref, l_ref, acc_ref,
                        s0_ref, s1_ref, *, tq, heads, nq):
    i = pl.program_id(2)
    hs = [slice(n * LANES, (n + 1) * LANES) for n in range(heads)]

    def stacked_queries(qi):
        out = []
        for sl in hs:
            qt = qt_ref[qi, sl, :].astype(F32)
            row = lax.broadcasted_iota(jnp.int32, qt.shape, 0)
            out.append(jnp.concatenate([jnp.where(row < A_QK_DIM, qt, 0.0),
                                        jnp.where(row >= A_QK_DIM, qt, 0.0)], axis=1).astype(BF16))
        return out

    def score(j, qq):
        start = pl.multiple_of(j * tq, tq)
        sl, stacked = qq
        return jnp.dot(k_ref[pl.ds(start, tq), sl], stacked, preferred_element_type=F32)

    qqts = stacked_queries(i)

    @pl.when(i == 0)
    def _():
        for n in range(heads):
            s0_ref[n] = score(0, (hs[n], qqts[n]))

    m_ref[...] = jnp.full(m_ref.shape, -jnp.inf, F32)
    l_ref[...] = jnp.zeros(l_ref.shape, F32)
    acc_ref[...] = jnp.zeros(acc_ref.shape, F32)
    ones_rows = jnp.ones((ONES_ROWS, tq), BF16)

    def step(j, cur_ref, nxt_ref, masked):
        for n in range(heads):
            s = cur_ref[n]
            if masked:
                krow = lax.broadcasted_iota(jnp.int32, s.shape, 0)
                qcol = lax.broadcasted_iota(jnp.int32, s.shape, 1) % tq
                s = jnp.where((krow // CHUNK) <= (qcol // CHUNK), s, -jnp.inf)
            m = m_ref[n]
            m_new = jnp.maximum(m, jnp.max(s, axis=0, keepdims=True))
            alpha = jnp.exp2(m - m_new)
            p = jnp.exp2(s - m_new).astype(BF16)
            m_ref[n] = m_new
            lhs = jnp.concatenate([vt_ref[j, hs[n], :], ones_rows], axis=0)
            pv = jnp.dot(lhs, p, preferred_element_type=F32)
            acc_ref[n] = alpha * acc_ref[n] + pv[:A_V_DIM]
            l_ref[n] = alpha * l_ref[n] + pv[A_V_DIM:A_V_DIM + 1]
            if nxt_ref is not None:
                nxt_ref[n] = score(j + 1, (hs[n], qqts[n]))

    def pair(jj, carry):
        step(2 * jj, s0_ref, s1_ref, False)
        step(2 * jj + 1, s1_ref, s0_ref, False)
        return carry

    lax.fori_loop(0, i // 2, pair, 0)

    @pl.when(i % 2 == 1)
    def _():
        step(i - 1, s0_ref, s1_ref, False)
        step(i, s1_ref, None, True)

    @pl.when(i % 2 == 0)
    def _():
        step(i, s0_ref, None, True)

    nxt = stacked_queries(jnp.minimum(i + 1, nq - 1))
    lam, lam_init = _lambda_values(lam_ref)
    gain = gain_ref[...] * (1.0 - lam_init)
    for n in range(heads):
        s0_ref[n] = score(0, (hs[n], nxt[n]))
        inv, acc = 1.0 / l_ref[n], acc_ref[n]
        ot = acc[:, :tq] * inv[:, :tq] - lam * (acc[:, tq:] * inv[:, tq:])
        o = ot.T
        o = o * lax.rsqrt(jnp.mean(o * o, axis=1, keepdims=True) + SUBLN_EPS) * gain
        o_ref[:, hs[n]] = (o * _silu(za_ref[:, hs[n]].astype(F32))).astype(o_ref.dtype)


def _attn_prompt(qt, k, vt, h, gain, lamv, batch, t, tq, heads=A_HEADS):
    nq = t // tq
    n = batch * t
    hw = heads * LANES
    tiles = pl.BlockSpec((None, nq, hw, tq), lambda b, hh, i: (b, 0, hh, 0))
    return pl.pallas_call(
        functools.partial(_attn_prompt_kernel, tq=tq, heads=heads, nq=nq),
        out_shape=jax.ShapeDtypeStruct((n, A_WIDTH), BF16),
        grid=(batch, A_HEADS // heads, nq),
        in_specs=[
            tiles,
            pl.BlockSpec((t, hw), lambda b, hh, i: (b, hh)),
            tiles,
            pl.BlockSpec((tq, hw), lambda b, hh, i: (b * nq + i, H_ZA // hw + hh)),
            pl.BlockSpec((1, LANES), lambda b, hh, i: (0, 0)),
            pl.BlockSpec((5, A_QK_DIM), lambda b, hh, i: (0, 0)),
        ],
        out_specs=pl.BlockSpec((tq, hw), lambda b, hh, i: (b * nq + i, hh)),
        scratch_shapes=[pltpu.VMEM((heads, 1, 2 * tq), F32), pltpu.VMEM((heads, 1, 2 * tq), F32),
                        pltpu.VMEM((heads, A_V_DIM, 2 * tq), F32),
                        pltpu.VMEM((heads, tq, 2 * tq), F32), pltpu.VMEM((heads, tq, 2 * tq), F32)],
        compiler_params=_params("arbitrary", "arbitrary", "arbitrary"),
        name="attn_prompt",
    )(qt, k, vt, h, gain, lamv)


def _chunk_mask(s, ts, q_pos0, k_pos0):
    q_pos = q_pos0 + lax.broadcasted_iota(jnp.int32, s.shape, 0) % ts
    k_pos = k_pos0 + lax.broadcasted_iota(jnp.int32, s.shape, 1)
    return jnp.where((k_pos // CHUNK) <= (q_pos // CHUNK), s, -jnp.inf)


def _rotate_lanes(blk, c, sa, sb):
    half = ROT_DIM // 2
    return blk * c + pltpu.roll(blk, half, 1) * sa + pltpu.roll(blk, LANES - half, 1) * sb


def _mix_sample_kernel(q_ref, k_ref, v_ref, za_ref, u_ref, zp_ref, qm_ref, zm_ref, cos_ref, sa_ref, sb_ref,
                       kct_ref, vc_ref, hist_ref, mk_ref, mv_ref, gain_ref, lam_ref, pw_ref, ps_ref,
                       o_ref, krot_ref, bp_ref, bm_ref, *, ts, past, mask_cache, mask_new):
    pos = past + lax.broadcasted_iota(jnp.int32, (ts, 1), 0)
    bp_ref[...] = _pool_mix(u_ref[...], hist_ref[...], zp_ref[...], pw_ref, ps_ref, pos).astype(bp_ref.dtype)
    _mem_attend(qm_ref[...].astype(BF16), lambda: zm_ref[...], mk_ref, mv_ref, bm_ref)

    c, sa, sb = cos_ref[...], sa_ref[...], sb_ref[...]
    nt = (((1,), (1,)), ((), ()))
    hs = [slice(n * LANES, (n + 1) * LANES) for n in range(A_HEADS)]
    scores = []
    for n in range(A_HEADS):
        q = (_rotate_lanes(q_ref[:, hs[n]], c, sa, sb) * A_SCALE).astype(BF16)
        k = _rotate_lanes(k_ref[:, hs[n]], c, sa, sb)
        krot_ref[:, hs[n]] = k
        qq = _split_components(q)
        s_c = jnp.dot(qq, kct_ref[n].astype(BF16), preferred_element_type=F32)
        s_n = lax.dot_general(qq, k.astype(BF16), nt, preferred_element_type=F32)
        if mask_cache:
            s_c = _chunk_mask(s_c, ts, past, 0)
        if mask_new:
            s_n = _chunk_mask(s_n, ts, past, past)
        scores.append((s_c, s_n))
    lam, lam_init = _lambda_values(lam_ref)
    gain = gain_ref[...] * (1.0 - lam_init)
    for n, (s_c, s_n) in enumerate(scores):
        m = jnp.maximum(jnp.max(s_c, axis=1, keepdims=True), jnp.max(s_n, axis=1, keepdims=True))
        p_c = jnp.exp(s_c - m)
        p_n = jnp.exp(s_n - m)
        l = jnp.sum(p_c, axis=1, keepdims=True) + jnp.sum(p_n, axis=1, keepdims=True)
        vc = vc_ref[pl.ds(n, past, stride=A_HEADS), :].astype(BF16)
        acc = (jnp.dot(p_c.astype(BF16), vc, preferred_element_type=F32)
               + jnp.dot(p_n.astype(BF16), v_ref[:, hs[n]].astype(BF16), preferred_element_type=F32))
        out = _attn_finish(acc, l, ts, lam, gain, za_ref[:, hs[n]].astype(F32))
        o_ref[:, hs[n]] = out.astype(o_ref.dtype)


def _needs_mask(q_pos, k_pos):
    return not bool(((k_pos[None, :] // CHUNK) <= (q_pos[:, None] // CHUNK)).all())


def _mix_sample(act, rot, kct, vc, hist, mk, mv, gain, lamv, pool_w, pool_scale, batch, ts, past):
    q_pos = past + np.arange(ts)
    seg = lambda col: pl.BlockSpec((ts, A_WIDTH), lambda b: (b, col // A_WIDTH))
    tab = pl.BlockSpec((ts, LANES), lambda b: (0, 0))
    row = pl.BlockSpec((ts, A_WIDTH), lambda b: (b, 0))
    mem = pl.BlockSpec((MEM_ROWS, LANES), lambda b: (b, 0))
    bshape = jax.ShapeDtypeStruct((batch * ts, A_WIDTH), BF16)
    return pl.pallas_call(
        functools.partial(_mix_sample_kernel, ts=ts, past=past,
                          mask_cache=_needs_mask(q_pos, np.arange(past)),
                          mask_new=_needs_mask(q_pos, q_pos)),
        out_shape=[bshape, jax.ShapeDtypeStruct((batch * ts, A_WIDTH), F32), bshape, bshape],
        grid=(batch,),
        in_specs=[
            seg(COL_Q), seg(COL_K), seg(COL_V), seg(COL_ZA), seg(COL_UP), seg(COL_ZP), seg(COL_QM), seg(COL_ZM),
            tab, tab, tab,
            pl.BlockSpec((None, A_HEADS, LANES, past), lambda b: (b, 0, 0, 0)),
            pl.BlockSpec((None, past * A_HEADS, LANES), lambda b: (b, 0, 0)),
            pl.BlockSpec((None, hist.shape[1], P_WIDTH), lambda b: (b, 0, 0)),
            mem, mem,
            pl.BlockSpec((1, LANES), lambda b: (0, 0)),
            pl.BlockSpec((5, A_QK_DIM), lambda b: (0, 0)),
            pl.BlockSpec((len(P_WINDOWS), P_GROUP_DIM, P_GROUP_DIM), lambda b: (0, 0, 0)),
            pl.BlockSpec((1, P_WIDTH), lambda b: (0, 0)),
        ],
        out_specs=[row, row, row, row],
        compiler_params=_params("arbitrary"),
        name="mix_sample",
    )(*([act] * 8), *rot, kct, vc, hist, mk, mv, gain, lamv, pool_w, pool_scale)


def _pool_mix(u, prev, zp, pw_ref, ps_ref, pos):
    tm = u.shape[0]
    ext = jnp.concatenate([prev, u], axis=0)
    hist_rows = prev.shape[0]
    parts = []
    for g, w in enumerate(P_WINDOWS):
        sl = slice(g * P_GROUP_DIM, (g + 1) * P_GROUP_DIM)
        tot = ext[:, sl]
        span = 1
        while span < w:
            tot = tot + pltpu.roll(tot, span, 0)
            span *= 2
        tot = tot[hist_rows:]
        cnt = jnp.minimum(pos + 1, w).astype(F32)
        pooled = tot / cnt - u[:, sl]
        parts.append(jnp.dot(pooled.astype(BF16), pw_ref[g].astype(BF16), preferred_element_type=F32))
    mixed = jnp.concatenate(parts, axis=1) * ps_ref[...]
    return mixed * _silu(zp)


def _mem_head(ref, h):
    halves = M_HEAD_DIM // LANES
    parts = [ref[pl.ds(c * M_HEADS + h, M_TOKENS, stride=halves * M_HEADS), :] for c in range(halves)]
    return jnp.concatenate(parts, axis=1)


def _mem_attend(q, gate_input, mk_ref, mv_ref, o_ref):
    nt = (((1,), (1,)), ((), ()))
    hs = [slice(n * M_HEAD_DIM, (n + 1) * M_HEAD_DIM) for n in range(M_HEADS)]
    scores = [lax.dot_general(q[:, sl], _mem_head(mk_ref, n).astype(BF16), nt,
                              preferred_element_type=F32) * M_SCALE for n, sl in enumerate(hs)]
    zm = gate_input()
    for n, (sl, s) in enumerate(zip(hs, scores)):
        m = jnp.max(s, axis=1, keepdims=True)
        p = jnp.exp(s - m)
        l = jnp.sum(p, axis=1, keepdims=True)
        o = jnp.dot(p.astype(BF16), _mem_head(mv_ref, n).astype(BF16), preferred_element_type=F32) / l
        o_ref[:, sl] = (o * _silu(zm[:, sl])).astype(o_ref.dtype)


def _proj_mem_kernel(x_ref, wq_ref, wz_ref, mk_ref, mv_ref, xs_ref, wt_ref, o_ref, act_ref):
    act_ref[...] = jnp.dot(xs_ref[...], wt_ref[...].astype(BF16), preferred_element_type=F32)
    xb = x_ref[...]
    q = jnp.dot(xb, wq_ref[...].astype(BF16), preferred_element_type=F32).astype(BF16)
    _mem_attend(q, lambda: jnp.dot(xb, wz_ref[...].astype(BF16), preferred_element_type=F32),
                mk_ref, mv_ref, o_ref)


def _proj_mem(xb, w, mk, mv, xs, batch, t, tm, tn):
    n, kdim = xb.shape
    ns = xs.shape[0]
    nt = t // tm
    tiles = w.shape[1] // tn
    assert batch * nt >= tiles, "not enough grid steps to carry the sample projection"
    rider = lambda b, i: (0, jnp.minimum(b * nt + i, tiles - 1))
    wspec = lambda col: pl.BlockSpec((kdim, M_WIDTH), lambda b, i: (0, col // M_WIDTH),
                                     pipeline_mode=pl.Buffered(1))
    mem = pl.BlockSpec((MEM_ROWS, LANES), lambda b, i: (b, 0))
    return pl.pallas_call(
        _proj_mem_kernel,
        out_shape=[jax.ShapeDtypeStruct((n, M_WIDTH), BF16), jax.ShapeDtypeStruct((ns, w.shape[1]), F32)],
        grid=(batch, nt),
        in_specs=[pl.BlockSpec((tm, kdim), lambda b, i: (b * nt + i, 0)), wspec(COL_QM), wspec(COL_ZM), mem, mem,
                  pl.BlockSpec((ns, kdim), lambda b, i: (0, 0), pipeline_mode=pl.Buffered(1)),
                  pl.BlockSpec((kdim, tn), rider)],
        out_specs=[pl.BlockSpec((tm, M_WIDTH), lambda b, i: (b * nt + i, 0)), pl.BlockSpec((ns, tn), rider)],
        compiler_params=_params("arbitrary", "arbitrary"),
        name="proj_mem",
    )(xb, w, w, mk, mv, xs, w)


def _gated_sum(b_refs, g_refs, w_ref, rows):
    projs = [jnp.dot(b_ref[rows, :], w_ref[n], preferred_element_type=F32) for n, b_ref in enumerate(b_refs)]
    acc = None
    for g_ref, proj in zip(g_refs, projs):
        term = _sigmoid(g_ref[rows, :].astype(F32)) * proj
        acc = term if acc is None else acc + term
    return acc


def _merge_kernel(ba_ref, bp_ref, bm_ref, g0_ref, g1_ref, g2_ref, wb_ref,
                  sa_ref, sp_ref, sm_ref, sg0_ref, sg1_ref, sg2_ref, o_ref, so_ref, wbb_ref, *, halves):
    @pl.when(pl.program_id(1) == 0)
    def _():
        wbb_ref[...] = wb_ref[...].astype(BF16)
        so_ref[...] = _gated_sum((sa_ref, sp_ref, sm_ref), (sg0_ref, sg1_ref, sg2_ref), wbb_ref,
                                 slice(None)).astype(so_ref.dtype)

    b_refs, g_refs = (ba_ref, bp_ref, bm_ref), (g0_ref, g1_ref, g2_ref)
    rows = o_ref.shape[0] // halves
    projs = [[jnp.dot(b_ref[r * rows:(r + 1) * rows, :], wbb_ref[n], preferred_element_type=F32)
              for n, b_ref in enumerate(b_refs)] for r in range(halves)]
    for r in range(halves):
        sl = slice(r * rows, (r + 1) * rows)
        acc = None
        for n, g_ref in enumerate(g_refs):
            term = _sigmoid(g_ref[sl, :].astype(F32)) * projs[r][n]
            acc = term if acc is None else acc + term
        o_ref[sl, :] = acc.astype(o_ref.dtype)


def _merge(ba, bp, bm, h, g_col, sa, sp, sm, sh, sg_col, wb, tm, tn):
    n, width = ba.shape
    ns = sa.shape[0]
    gspec = lambda k: pl.BlockSpec((tm, tn), lambda j, i: (i, (g_col + k * D_MODEL) // tn + j))
    sgspec = lambda k: pl.BlockSpec((ns, tn), lambda j, i: (0, (sg_col + k * D_MODEL) // tn + j))
    bspec = pl.BlockSpec((tm, width), lambda j, i: (i, 0))
    sbspec = pl.BlockSpec((ns, width), lambda j, i: (0, 0), pipeline_mode=pl.Buffered(1))
    return pl.pallas_call(
        functools.partial(_merge_kernel, halves=2 if tm % 32 == 0 else 1),
        out_shape=[jax.ShapeDtypeStruct((n, D_MODEL), BF16), jax.ShapeDtypeStruct((ns, D_MODEL), BF16)],
        grid=(D_MODEL // tn, n // tm),
        in_specs=[bspec, bspec, bspec, gspec(0), gspec(1), gspec(2),
                  pl.BlockSpec((N_BRANCH, width, tn), lambda j, i: (0, 0, j), pipeline_mode=pl.Buffered(1)),
                  sbspec, sbspec, sbspec, sgspec(0), sgspec(1), sgspec(2)],
        out_specs=[pl.BlockSpec((tm, tn), lambda j, i: (i, j)), pl.BlockSpec((ns, tn), lambda j, i: (0, j))],
        scratch_shapes=[pltpu.VMEM((N_BRANCH, width, tn), BF16)],
        compiler_params=_params("arbitrary", "arbitrary"),
        name="merge",
    )(ba, bp, bm, h, h, h, wb, sa, sp, sm, sh, sh, sh)


def _layer_norm_residual(x, proj, g_ref, b_ref, alpha):
    z = alpha * x + proj
    mu = jnp.mean(z, axis=1, keepdims=True)
    zc = z - mu
    var = jnp.mean(zc * zc, axis=1, keepdims=True)
    return zc * lax.rsqrt(var + LN_EPS) * g_ref[...] + b_ref[...]


def _out_kernel(m_ref, w_ref, x_ref, g_ref, b_ref, sm_ref, sx_ref, o_ref, so_ref, wb_ref, *, halves, alpha):
    @pl.when(pl.program_id(0) == 0)
    def _():
        wb_ref[...] = w_ref[...].astype(BF16)
        so_ref[...] = _layer_norm_residual(
            sx_ref[...], jnp.dot(sm_ref[...], wb_ref[...], preferred_element_type=F32), g_ref, b_ref, alpha)

    rows = o_ref.shape[0] // halves
    outs = [jnp.dot(m_ref[r * rows:(r + 1) * rows, :], wb_ref[...], preferred_element_type=F32)
            for r in range(halves)]
    for r in range(halves):
        sl = slice(r * rows, (r + 1) * rows)
        o_ref[sl, :] = _layer_norm_residual(x_ref[sl, :], outs[r], g_ref, b_ref, alpha)


def _out(merged, x, s_merged, s_x, w_out, ln_g, ln_b, alpha, tm):
    n = merged.shape[0]
    ns = s_merged.shape[0]
    row = pl.BlockSpec((tm, D_MODEL), lambda i: (i, 0))
    srow = pl.BlockSpec((ns, D_MODEL), lambda i: (0, 0), pipeline_mode=pl.Buffered(1))
    vec = pl.BlockSpec((1, D_MODEL), lambda i: (0, 0))
    wspec = pl.BlockSpec((D_MODEL, D_MODEL), lambda i: (0, 0), pipeline_mode=pl.Buffered(1))
    return pl.pallas_call(
        functools.partial(_out_kernel, halves=2, alpha=alpha),
        out_shape=[jax.ShapeDtypeStruct((n, D_MODEL), F32), jax.ShapeDtypeStruct((ns, D_MODEL), F32)],
        grid=(n // tm,),
        in_specs=[row, wspec, row, vec, vec, srow, srow],
        out_specs=[row, pl.BlockSpec((ns, D_MODEL), lambda i: (0, 0))],
        scratch_shapes=[pltpu.VMEM((D_MODEL, D_MODEL), BF16)],
        compiler_params=_params("arbitrary"),
        name="out_ln",
    )(merged, w_out, x, ln_g, ln_b, s_merged, s_x)


def _rest_col_block(tn):
    za_blocks = (COL_UP - COL_ZA) // tn
    return lambda j: jnp.where(j < za_blocks, COL_ZA // tn + j, COL_GM // tn + (j - za_blocks))


def _layer(xp2d, xs2d, bp, tp, bs, ts, past, w, zero_hist, hist_s, mk, mv, kt_hist, v_hist, mk_s, mv_s):
    ang = _rotary_angles(jnp.arange(tp))
    cos_t, sin_t = jnp.cos(ang).T, jnp.sin(ang).T
    qt, kt, kb, xb = _proj_qk(xp2d, w["w_in"], cos_t, sin_t, bp, tp, ROW_TILE, KEY_TILE,
                              A_SCALE * math.log2(math.e))
    v, vt, u, b_p = _proj_vup(xb, w["w_in"], zero_hist, w["pool_w"], w["pool_scale"], bp, tp, ROW_TILE, KEY_TILE)
    b_m, act_s = _proj_mem(xb, w["w_in"], mk, mv, xs2d.astype(BF16), bp, tp, ROW_TILE, COL_TILE)
    h = _proj(xb, w["w_in"], _rest_col_block(COL_TILE), H_COLS, BF16, GATE_ROW_TILE, COL_TILE, name="proj_rest")
    b_a = _attn_prompt(qt, kb, vt, h, w["gain"], w["lamv"], bp, tp, KEY_TILE)

    rot_s = _rotary_tables(past + jnp.arange(ts))
    s_a, k_s, s_p, s_m = _mix_sample(act_s, rot_s, kt_hist, v_hist, hist_s, mk_s, mv_s, w["gain"], w["lamv"],
                                     w["pool_w"], w["pool_scale"], bs, ts, past)

    merged, s_merged = _merge(b_a, b_p, b_m, h, H_GM, s_a, s_p, s_m, act_s, COL_GM, w["w_branch"],
                              ROW_TILE, COL_TILE)
    y_p, y_s = _out(merged, xp2d, s_merged, xs2d, w["w_out"], w["ln_g"], w["ln_b"], w["alpha"], OUT_ROW_TILE)
    return y_p, y_s, kt, v, u, k_s, act_s


def kernel(x_prompt, x_sample, cache_attn_k, cache_attn_v, cache_mem_k, cache_mem_v, state_pool, mem_prompt, w_in, w_mem_kv, lambda_q1, lambda_k1, lambda_q2, lambda_k2, subln_gain, pool_w, pool_scale, w_branch, w_out, ln_gain, ln_bias):
    bp, tp, d = x_prompt.shape
    bs, ts, _ = x_sample.shape
    past = cache_attn_k.shape[2]
    hist_rows = P_HIST + 1
    halves = M_HEAD_DIM // LANES

    def to_mem_layout(a):
        a = a.reshape(a.shape[0], M_TOKENS, M_HEADS, halves, LANES)
        return jnp.transpose(a, (0, 1, 3, 2, 4)).reshape(a.shape[0] * MEM_ROWS, LANES)

    def from_mem_layout(a, batch):
        a = a.reshape(batch, M_TOKENS, halves, M_HEADS, LANES)
        return jnp.transpose(a, (0, 1, 3, 2, 4)).reshape(batch, M_TOKENS, M_HEADS, M_HEAD_DIM)

    def run_layer(l, xp2d, xs2d):
        w = {
            "w_in": w_in[l],
            "gain": subln_gain[l].reshape(1, A_V_DIM),
            "lamv": jnp.stack([lambda_q1[l], lambda_k1[l], lambda_q2[l], lambda_k2[l],
                               jnp.full((A_QK_DIM,), 0.8 - 0.6 * math.exp(-0.3 * l), F32)]),
            "pool_w": pool_w[l],
            "pool_scale": pool_scale[l].reshape(1, P_WIDTH),
            "w_branch": w_branch[l],
            "w_out": w_out[l],
            "ln_g": ln_gain[l].reshape(1, d),
            "ln_b": ln_bias[l].reshape(1, d),
            "alpha": (2.0 * w_in.shape[0]) ** 0.25,
        }
        memf = mem_prompt.reshape(bp * M_TOKENS, d)
        mk, mv = _proj_memkv(memf, w_mem_kv[l], MEM_ROW_TILE)

        zero_hist = jnp.zeros((bp, hist_rows, P_WIDTH), F32)
        hist_s = jnp.pad(state_pool[l], ((0, 0), (hist_rows - P_HIST, 0), (0, 0)))
        kt_hist = jnp.transpose(cache_attn_k[l], (0, 2, 3, 4, 1)).reshape(bs, A_HEADS, 2 * A_QK_DIM, past)
        y_p, y_s, k_p, v_p, u_p, k_s, act_s = _layer(
            xp2d, xs2d, bp, tp, bs, ts, past, w, zero_hist, hist_s, mk, mv,
            kt_hist, cache_attn_v[l].reshape(bs, past * A_HEADS, A_V_DIM),
            to_mem_layout(cache_mem_k[l]), to_mem_layout(cache_mem_v[l]))
        v_s = act_s[:, COL_V:COL_V + A_WIDTH]
        u_s = act_s[:, COL_UP:COL_UP + P_WIDTH]

        pool_p = u_p[:, -P_HIST:]
        pool_s = jnp.concatenate([hist_s, u_s.reshape(bs, ts, P_WIDTH)], axis=1)[:, -P_HIST:]
        k_p = jnp.transpose(k_p.reshape(bp, A_HEADS, 2, A_QK_DIM, tp), (0, 4, 1, 2, 3))
        outs = (k_p, v_p.reshape(bp, tp, A_HEADS, A_V_DIM),
                from_mem_layout(mk, bp), from_mem_layout(mv, bp),
                pool_p,
                k_s.reshape(bs, ts, A_HEADS, 2, A_QK_DIM), v_s.reshape(bs, ts, A_HEADS, A_V_DIM), pool_s)
        return y_p, y_s, outs

    yp = x_prompt.reshape(bp * tp, d)
    ys = x_sample.reshape(bs * ts, d)
    per_layer = []
    for l in range(w_in.shape[0]):
        yp, ys, outs = run_layer(l, yp, ys)
        per_layer.append(outs)
    stacked = [jnp.stack([o[n] for o in per_layer]) for n in range(8)]
    return (yp.reshape(bp, tp, d), ys.reshape(bs, ts, d), *stacked)
```

```python
import functools
import math

import numpy as np
import jax
import jax.numpy as jnp
from jax import lax
from jax.experimental import pallas as pl
from jax.experimental.pallas import tpu as pltpu

F32 = jnp.float32
BF16 = jnp.bfloat16

V7X_VMEM_LIMIT_BYTES = 56 * 1024 * 1024
LANES = 128

D_MODEL = 2048
CHUNK = 64
A_HEADS = 8
A_QK_DIM = 64
A_V_DIM = 128
A_WIDTH = A_HEADS * A_V_DIM
A_SCALE = A_QK_DIM ** -0.5
ROT_DIM = A_QK_DIM // 4
ROPE_THETA = 500000.0
SUBLN_EPS = 1e-5
P_WINDOWS = (2, 4, 8, 16)
P_GROUP_DIM = 256
P_WIDTH = len(P_WINDOWS) * P_GROUP_DIM
P_HIST = max(P_WINDOWS) - 1
M_TOKENS = 256
M_HEADS = 4
M_HEAD_DIM = 256
M_WIDTH = M_HEADS * M_HEAD_DIM
M_SCALE = M_HEAD_DIM ** -0.5
MEM_ROWS = M_TOKENS * M_HEADS * (M_HEAD_DIM // LANES)
N_BRANCH = 3
LN_EPS = 1e-5
ONES_ROWS = 16

COL_Q = 0
COL_K = COL_Q + A_WIDTH
COL_V = COL_K + A_WIDTH
COL_ZA = COL_V + A_WIDTH
COL_UP = COL_ZA + A_WIDTH
COL_ZP = COL_UP + P_WIDTH
COL_QM = COL_ZP + P_WIDTH
COL_ZM = COL_QM + M_WIDTH
COL_GM = COL_ZM + M_WIDTH
IN_COLS = COL_GM + N_BRANCH * D_MODEL
H_ZA = 0
H_GM = H_ZA + A_WIDTH
H_COLS = H_GM + N_BRANCH * D_MODEL

KEY_TILE = 256
ROW_TILE = 512
OUT_ROW_TILE = 256
COL_TILE = 1024
GATE_ROW_TILE = 2048
MEM_ROW_TILE = 512


def _params(*sem):
    return pltpu.CompilerParams(dimension_semantics=sem, vmem_limit_bytes=V7X_VMEM_LIMIT_BYTES)


def _sigmoid(z):
    return 1.0 / (1.0 + jnp.exp(-z))


def _silu(z):
    return z * _sigmoid(z)


def _proj_kernel(x_ref, w_ref, o_ref):
    acc = jnp.dot(x_ref[...].astype(BF16), w_ref[...].astype(BF16), preferred_element_type=F32)
    o_ref[...] = acc.astype(o_ref.dtype)


def _proj(x, w, col_block, ncols, out_dtype, tm, tn, name="proj"):
    n, kdim = x.shape
    return pl.pallas_call(
        _proj_kernel,
        out_shape=jax.ShapeDtypeStruct((n, ncols), out_dtype),
        grid=(n // tm, ncols // tn),
        in_specs=[pl.BlockSpec((tm, kdim), lambda i, j: (i, 0)),
                  pl.BlockSpec((kdim, tn), lambda i, j: (0, col_block(j)))],
        out_specs=pl.BlockSpec((tm, tn), lambda i, j: (i, j)),
        compiler_params=_params("arbitrary", "arbitrary"),
        name=name,
    )(x, w)


def _proj_memkv_kernel(x_ref, w_ref, mk_ref, mv_ref):
    kv = jnp.dot(x_ref[...].astype(BF16), w_ref[...].astype(BF16), preferred_element_type=F32)
    tm = kv.shape[0]
    halves = M_HEAD_DIM // LANES
    group = halves * M_HEADS

    def store(o_ref):
        for h in range(M_HEADS):
            for c in range(halves):
                col = h * M_HEAD_DIM + c * LANES
                o_ref[pl.ds(c * M_HEADS + h, tm, stride=group), :] = kv[:, col:col + LANES]

    @pl.when(pl.program_id(1) == 0)
    def _():
        store(mk_ref)

    @pl.when(pl.program_id(1) == 1)
    def _():
        store(mv_ref)


def _proj_memkv(mem, w, tm):
    n, kdim = mem.shape
    group = M_WIDTH // LANES
    row = pl.BlockSpec((tm * group, LANES), lambda i, j: (i, 0))
    return pl.pallas_call(
        _proj_memkv_kernel,
        out_shape=[jax.ShapeDtypeStruct((n * group, LANES), F32)] * 2,
        grid=(n // tm, 2),
        in_specs=[pl.BlockSpec((tm, kdim), lambda i, j: (i, 0)),
                  pl.BlockSpec((kdim, M_WIDTH), lambda i, j: (0, j))],
        out_specs=[row, row],
        compiler_params=_params("arbitrary", "arbitrary"),
        name="proj_memkv",
    )(mem, w)


def _rotate_sublanes(acc, c, s):
    half = ROT_DIM // 2
    pieces = []
    for g in range(acc.shape[0] // A_QK_DIM):
        b = g * A_QK_DIM
        lo, hi = acc[b:b + half], acc[b + half:b + ROT_DIM]
        pieces += [lo * c - hi * s, hi * c + lo * s, acc[b + ROT_DIM:b + A_QK_DIM]]
    return jnp.concatenate(pieces, axis=0)


def _proj_qk_kernel(x_ref, wq_ref, wk_ref, cos_ref, sin_ref, qt_ref, kt_ref, kb_ref, xb_ref, *, q_scale):
    xb = x_ref[...].astype(BF16)
    xb_ref[...] = xb
    c, s = cos_ref[...], sin_ref[...]
    k = _rotate_sublanes(jnp.dot(xb, wk_ref[...].astype(BF16), preferred_element_type=F32).T, c, s)
    kt_ref[...] = k
    kb_ref[...] = k.T.astype(kb_ref.dtype)
    q = _rotate_sublanes(jnp.dot(xb, wq_ref[...].astype(BF16), preferred_element_type=F32).T, c, s)
    q = (q * q_scale).astype(qt_ref.dtype)
    tile = qt_ref.shape[-1]
    for n in range(qt_ref.shape[0]):
        qt_ref[n] = q[:, n * tile:(n + 1) * tile]


def _proj_qk(x, w, cos_t, sin_t, batch, t, tm, q_tile, q_scale):
    n, kdim = x.shape
    width = A_WIDTH
    nt = t // tm
    tab = pl.BlockSpec((ROT_DIM // 2, tm), lambda i: (0, i % nt))
    wspec = lambda col: pl.BlockSpec((kdim, width), lambda i: (0, col // width), pipeline_mode=pl.Buffered(1))
    tspec = pl.BlockSpec((None, width, tm), lambda i: (i // nt, 0, i % nt))
    qspec = pl.BlockSpec((None, tm // q_tile, width, q_tile), lambda i: (i // nt, i % nt, 0, 0))
    return pl.pallas_call(
        functools.partial(_proj_qk_kernel, q_scale=q_scale),
        out_shape=[jax.ShapeDtypeStruct((batch, t // q_tile, width, q_tile), BF16),
                   jax.ShapeDtypeStruct((batch, width, t), F32),
                   jax.ShapeDtypeStruct((n, width), BF16), jax.ShapeDtypeStruct((n, kdim), BF16)],
        grid=(n // tm,),
        in_specs=[pl.BlockSpec((tm, kdim), lambda i: (i, 0)), wspec(COL_Q), wspec(COL_K), tab, tab],
        out_specs=[qspec, tspec, pl.BlockSpec((tm, width), lambda i: (i, 0)),
                   pl.BlockSpec((tm, kdim), lambda i: (i, 0))],
        compiler_params=_params("arbitrary"),
        name="proj_qk",
    )(x, w, w, cos_t, sin_t)


def _proj_vup_kernel(x_ref, wv_ref, wu_ref, wz_ref, hist_ref, pw_ref, ps_ref, v_ref, vt_ref, ut_ref, bp_ref,
                     prev_ref, *, key_tile, nt):
    i = pl.program_id(0) % nt

    @pl.when(i == 0)
    def _():
        prev_ref[...] = hist_ref[...]

    xb = x_ref[...]
    tm = xb.shape[0]
    u = jnp.dot(xb, wu_ref[...].astype(BF16), preferred_element_type=F32)
    ut_ref[...] = u[tm - ut_ref.shape[0]:, :]
    zp = jnp.dot(xb, wz_ref[...].astype(BF16), preferred_element_type=F32)
    v = jnp.dot(xb, wv_ref[...].astype(BF16), preferred_element_type=F32)
    for h in range(A_HEADS):
        v_ref[pl.ds(h, tm, stride=A_HEADS), :] = v[:, h * A_V_DIM:(h + 1) * A_V_DIM]
    for c in range(tm // key_tile):
        vt_ref[c] = v[c * key_tile:(c + 1) * key_tile, :].T.astype(vt_ref.dtype)
    pos = i * tm + lax.broadcasted_iota(jnp.int32, (tm, 1), 0)
    bp_ref[...] = _pool_mix(u, prev_ref[...], zp, pw_ref, ps_ref, pos).astype(bp_ref.dtype)
    prev_ref[...] = u[tm - prev_ref.shape[0]:, :]


def _proj_vup(xb, w, hist, pool_w, pool_scale, batch, t, tm, key_tile):
    assert tm >= hist.shape[1]
    n, kdim = xb.shape
    width = A_WIDTH
    nt = t // tm
    hr = hist.shape[1]
    wspec = lambda col: pl.BlockSpec((kdim, width), lambda i: (0, col // width), pipeline_mode=pl.Buffered(1))
    row = pl.BlockSpec((tm, width), lambda i: (i, 0))
    return pl.pallas_call(
        functools.partial(_proj_vup_kernel, key_tile=key_tile, nt=nt),
        out_shape=[jax.ShapeDtypeStruct((n * A_HEADS, A_V_DIM), F32),
                   jax.ShapeDtypeStruct((batch, t // key_tile, width, key_tile), BF16),
                   jax.ShapeDtypeStruct((batch, hr, width), F32),
                   jax.ShapeDtypeStruct((n, width), BF16)],
        grid=(n // tm,),
        in_specs=[pl.BlockSpec((tm, kdim), lambda i: (i, 0)), wspec(COL_V), wspec(COL_UP), wspec(COL_ZP),
                  pl.BlockSpec((None, hr, P_WIDTH), lambda i: (i // nt, 0, 0)),
                  pl.BlockSpec((len(P_WINDOWS), P_GROUP_DIM, P_GROUP_DIM), lambda i: (0, 0, 0)),
                  pl.BlockSpec((1, P_WIDTH), lambda i: (0, 0))],
        out_specs=[pl.BlockSpec((tm * A_HEADS, A_V_DIM), lambda i: (i, 0)),
                   pl.BlockSpec((None, tm // key_tile, width, key_tile), lambda i: (i // nt, i % nt, 0, 0)),
                   pl.BlockSpec((None, hr, width), lambda i: (i // nt, 0, 0)), row],
        scratch_shapes=[pltpu.VMEM((hr, P_WIDTH), F32)],
        compiler_params=_params("arbitrary"),
        name="proj_vup",
    )(xb, w, w, w, hist, pool_w, pool_scale)


def _rotary_angles(pos):
    half = ROT_DIM // 2
    inv = ROPE_THETA ** (-(jnp.arange(half, dtype=F32) * 2.0) / ROT_DIM)
    return pos.astype(F32)[:, None] * inv[None, :]


def _rotary_tables(pos):
    half = ROT_DIM // 2
    ang = _rotary_angles(pos)
    cos, sin = jnp.cos(ang), jnp.sin(ang)
    t = pos.shape[0]
    ones = jnp.ones((t, A_QK_DIM - ROT_DIM), F32)
    zeros = jnp.zeros((t, A_QK_DIM - ROT_DIM), F32)
    zh = jnp.zeros((t, half), F32)
    c64 = jnp.concatenate([cos, cos, ones], axis=1)
    sa64 = jnp.concatenate([zh, sin, zeros], axis=1)
    sb64 = jnp.concatenate([-sin, zh, zeros], axis=1)
    tile = lambda a: jnp.concatenate([a, a], axis=1)
    return tile(c64), tile(sa64), tile(sb64)


def _lambda_values(lam_ref):
    lv = lam_ref[...]
    a = jnp.sum(lv[0:1] * lv[1:2], axis=1, keepdims=True)
    b = jnp.sum(lv[2:3] * lv[3:4], axis=1, keepdims=True)
    lam_init = lv[4:5, 0:1]
    return jnp.exp(a) - jnp.exp(b) + lam_init, lam_init


def _split_components(q):
    lane = lax.broadcasted_iota(jnp.int32, q.shape, 1)
    qf = q.astype(F32)
    return jnp.concatenate([jnp.where(lane < A_QK_DIM, qf, 0.0),
                            jnp.where(lane >= A_QK_DIM, qf, 0.0)], axis=0).astype(q.dtype)


def _attn_finish(acc, l, tq, lam, gain, za):
    o = acc[:tq] / l[:tq] - lam * (acc[tq:] / l[tq:])
    o = o * lax.rsqrt(jnp.mean(o * o, axis=1, keepdims=True) + SUBLN_EPS) * gain
    return o * _silu(za)


def _attn_prompt_kernel(qt_ref, k_ref, vt_ref, za_ref, gain_ref, lam_ref, o_ref, m_ref, l_ref, acc_ref,
                        s0_ref, s1_ref, *, tq, heads, nq):
    i = pl.program_id(2)
    hs = [slice(n * LANES, (n + 1) * LANES) for n in range(heads)]

    def stacked_queries(qi):
        out = []
        for sl in hs:
            qt = qt_ref[qi, sl, :].astype(F32)
            row = lax.broadcasted_iota(jnp.int32, qt.shape, 0)
            out.append(jnp.concatenate([jnp.where(row < A_QK_DIM, qt, 0.0),
                                        jnp.where(row >= A_QK_DIM, qt, 0.0)], axis=1).astype(BF16))
        return out

    def score(j, qq):
        start = pl.multiple_of(j * tq, tq)
        sl, stacked = qq
        return jnp.dot(k_ref[pl.ds(start, tq), sl], stacked, preferred_element_type=F32)

    qqts = stacked_queries(i)

    @pl.when(i == 0)
    def _():
        for n in range(heads):
            s0_ref[n] = score(0, (hs[n], qqts[n]))

    m_ref[...] = jnp.full(m_ref.shape, -jnp.inf, F32)
    l_ref[...] = jnp.zeros(l_ref.shape, F32)
    acc_ref[...] = jnp.zeros(acc_ref.shape, F32)
    ones_rows = jnp.ones((ONES_ROWS, tq), BF16)

    def step(j, cur_ref, nxt_ref, masked):
        for n in range(heads):
            s = cur_ref[n]
            if masked:
                krow = lax.broadcasted_iota(jnp.int32, s.shape, 0)
                qcol = lax.broadcasted_iota(jnp.int32, s.shape, 1) % tq
                s = jnp.where((krow // CHUNK) <= (qcol // CHUNK), s, -jnp.inf)
            m = m_ref[n]
            m_new = jnp.maximum(m, jnp.max(s, axis=0, keepdims=True))
            alpha = jnp.exp2(m - m_new)
            p = jnp.exp2(s - m_new).astype(BF16)
            m_ref[n] = m_new
            lhs = jnp.concatenate([vt_ref[j, hs[n], :], ones_rows], axis=0)
            pv = jnp.dot(lhs, p, preferred_element_type=F32)
            acc_ref[n] = alpha * acc_ref[n] + pv[:A_V_DIM]
            l_ref[n] = alpha * l_ref[n] + pv[A_V_DIM:A_V_DIM + 1]
            if nxt_ref is not None:
                nxt_ref[n] = score(j + 1, (hs[n], qqts[n]))

    def pair(jj, carry):
        step(2 * jj, s0_ref, s1_ref, False)
        step(2 * jj + 1, s1_ref, s0_ref, False)
        return carry

    lax.fori_loop(0, i // 2, pair, 0)

    @pl.when(i % 2 == 1)
    def _():
        step(i - 1, s0_ref, s1_ref, False)
        step(i, s1_ref, None, True)

    @pl.when(i % 2 == 0)
    def _():
        step(i, s0_ref, None, True)

    nxt = stacked_queries(jnp.minimum(i + 1, nq - 1))
    lam, lam_init = _lambda_values(lam_ref)
    gain = gain_ref[...] * (1.0 - lam_init)
    for n in range(heads):
        s0_ref[n] = score(0, (hs[n], nxt[n]))
        inv, acc = 1.0 / l_ref[n], acc_ref[n]
        ot = acc[:, :tq] * inv[:, :tq] - lam * (acc[:, tq:] * inv[:, tq:])
        o = ot.T
        o = o * lax.rsqrt(jnp.mean(o * o, axis=1, keepdims=True) + SUBLN_EPS) * gain
        o_ref[:, hs[n]] = (o * _silu(za_ref[:, hs[n]].astype(F32))).astype(o_ref.dtype)


def _attn_prompt(qt, k, vt, h, gain, lamv, batch, t, tq, heads=A_HEADS):
    nq = t // tq
    n = batch * t
    hw = heads * LANES
    tiles = pl.BlockSpec((None, nq, hw, tq), lambda b, hh, i: (b, 0, hh, 0))
    return pl.pallas_call(
        functools.partial(_attn_prompt_kernel, tq=tq, heads=heads, nq=nq),
        out_shape=jax.ShapeDtypeStruct((n, A_WIDTH), BF16),
        grid=(batch, A_HEADS // heads, nq),
        in_specs=[
            tiles,
            pl.BlockSpec((t, hw), lambda b, hh, i: (b, hh)),
            tiles,
            pl.BlockSpec((tq, hw), lambda b, hh, i: (b * nq + i, H_ZA // hw + hh)),
            pl.BlockSpec((1, LANES), lambda b, hh, i: (0, 0)),
            pl.BlockSpec((5, A_QK_DIM), lambda b, hh, i: (0, 0)),
        ],
        out_specs=pl.BlockSpec((tq, hw), lambda b, hh, i: (b * nq + i, hh)),
        scratch_shapes=[pltpu.VMEM((heads, 1, 2 * tq), F32), pltpu.VMEM((heads, 1, 2 * tq), F32),
                        pltpu.VMEM((heads, A_V_DIM, 2 * tq), F32),
                        pltpu.VMEM((heads, tq, 2 * tq), F32), pltpu.VMEM((heads, tq, 2 * tq), F32)],
        compiler_params=_params("arbitrary", "arbitrary", "arbitrary"),
        name="attn_prompt",
    )(qt, k, vt, h, gain, lamv)


def _chunk_mask(s, ts, q_pos0, k_pos0):
    q_pos = q_pos0 + lax.broadcasted_iota(jnp.int32, s.shape, 0) % ts
    k_pos = k_pos0 + lax.broadcasted_iota(jnp.int32, s.shape, 1)
    return jnp.where((k_pos // CHUNK) <= (q_pos // CHUNK), s, -jnp.inf)


def _rotate_lanes(blk, c, sa, sb):
    half = ROT_DIM // 2
    return blk * c + pltpu.roll(blk, half, 1) * sa + pltpu.roll(blk, LANES - half, 1) * sb


def _mix_sample_kernel(q_ref, k_ref, v_ref, za_ref, u_ref, zp_ref, qm_ref, zm_ref, cos_ref, sa_ref, sb_ref,
                       kct_ref, vc_ref, hist_ref, mk_ref, mv_ref, gain_ref, lam_ref, pw_ref, ps_ref,
                       o_ref, krot_ref, bp_ref, bm_ref, *, ts, past, mask_cache, mask_new):
    pos = past + lax.broadcasted_iota(jnp.int32, (ts, 1), 0)
    bp_ref[...] = _pool_mix(u_ref[...], hist_ref[...], zp_ref[...], pw_ref, ps_ref, pos).astype(bp_ref.dtype)
    _mem_attend(qm_ref[...].astype(BF16), lambda: zm_ref[...], mk_ref, mv_ref, bm_ref)

    c, sa, sb = cos_ref[...], sa_ref[...], sb_ref[...]
    nt = (((1,), (1,)), ((), ()))
    hs = [slice(n * LANES, (n + 1) * LANES) for n in range(A_HEADS)]
    scores = []
    for n in range(A_HEADS):
        q = (_rotate_lanes(q_ref[:, hs[n]], c, sa, sb) * A_SCALE).astype(BF16)
        k = _rotate_lanes(k_ref[:, hs[n]], c, sa, sb)
        krot_ref[:, hs[n]] = k
        qq = _split_components(q)
        s_c = jnp.dot(qq, kct_ref[n].astype(BF16), preferred_element_type=F32)
        s_n = lax.dot_general(qq, k.astype(BF16), nt, preferred_element_type=F32)
        if mask_cache:
            s_c = _chunk_mask(s_c, ts, past, 0)
        if mask_new:
            s_n = _chunk_mask(s_n, ts, past, past)
        scores.append((s_c, s_n))
    lam, lam_init = _lambda_values(lam_ref)
    gain = gain_ref[...] * (1.0 - lam_init)
    for n, (s_c, s_n) in enumerate(scores):
        m = jnp.maximum(jnp.max(s_c, axis=1, keepdims=True), jnp.max(s_n, axis=1, keepdims=True))
        p_c = jnp.exp(s_c - m)
        p_n = jnp.exp(s_n - m)
        l = jnp.sum(p_c, axis=1, keepdims=True) + jnp.sum(p_n, axis=1, keepdims=True)
        vc = vc_ref[pl.ds(n, past, stride=A_HEADS), :].astype(BF16)
        acc = (jnp.dot(p_c.astype(BF16), vc, preferred_element_type=F32)
               + jnp.dot(p_n.astype(BF16), v_ref[:, hs[n]].astype(BF16), preferred_element_type=F32))
        out = _attn_finish(acc, l, ts, lam, gain, za_ref[:, hs[n]].astype(F32))
        o_ref[:, hs[n]] = out.astype(o_ref.dtype)


def _needs_mask(q_pos, k_pos):
    return not bool(((k_pos[None, :] // CHUNK) <= (q_pos[:, None] // CHUNK)).all())


def _mix_sample(act, rot, kct, vc, hist, mk, mv, gain, lamv, pool_w, pool_scale, batch, ts, past):
    q_pos = past + np.arange(ts)
    seg = lambda col: pl.BlockSpec((ts, A_WIDTH), lambda b: (b, col // A_WIDTH))
    tab = pl.BlockSpec((ts, LANES), lambda b: (0, 0))
    row = pl.BlockSpec((ts, A_WIDTH), lambda b: (b, 0))
    mem = pl.BlockSpec((MEM_ROWS, LANES), lambda b: (b, 0))
    bshape = jax.ShapeDtypeStruct((batch * ts, A_WIDTH), BF16)
    return pl.pallas_call(
        functools.partial(_mix_sample_kernel, ts=ts, past=past,
                          mask_cache=_needs_mask(q_pos, np.arange(past)),
                          mask_new=_needs_mask(q_pos, q_pos)),
        out_shape=[bshape, jax.ShapeDtypeStruct((batch * ts, A_WIDTH), F32), bshape, bshape],
        grid=(batch,),
        in_specs=[
            seg(COL_Q), seg(COL_K), seg(COL_V), seg(COL_ZA), seg(COL_UP), seg(COL_ZP), seg(COL_QM), seg(COL_ZM),
            tab, tab, tab,
            pl.BlockSpec((None, A_HEADS, LANES, past), lambda b: (b, 0, 0, 0)),
            pl.BlockSpec((None, past * A_HEADS, LANES), lambda b: (b, 0, 0)),
            pl.BlockSpec((None, hist.shape[1], P_WIDTH), lambda b: (b, 0, 0)),
            mem, mem,
            pl.BlockSpec((1, LANES), lambda b: (0, 0)),
            pl.BlockSpec((5, A_QK_DIM), lambda b: (0, 0)),
            pl.BlockSpec((len(P_WINDOWS), P_GROUP_DIM, P_GROUP_DIM), lambda b: (0, 0, 0)),
            pl.BlockSpec((1, P_WIDTH), lambda b: (0, 0)),
        ],
        out_specs=[row, row, row, row],
        compiler_params=_params("arbitrary"),
        name="mix_sample",
    )(*([act] * 8), *rot, kct, vc, hist, mk, mv, gain, lamv, pool_w, pool_scale)


def _pool_mix(u, prev, zp, pw_ref, ps_ref, pos):
    tm = u.shape[0]
    ext = jnp.concatenate([prev, u], axis=0)
    hist_rows = prev.shape[0]
    parts = []
    for g, w in enumerate(P_WINDOWS):
        sl = slice(g * P_GROUP_DIM, (g + 1) * P_GROUP_DIM)
        tot = ext[:, sl]
        span = 1
        while span < w:
            tot = tot + pltpu.roll(tot, span, 0)
            span *= 2
        tot = tot[hist_rows:]
        cnt = jnp.minimum(pos + 1, w).astype(F32)
        pooled = tot / cnt - u[:, sl]
        parts.append(jnp.dot(pooled.astype(BF16), pw_ref[g].astype(BF16), preferred_element_type=F32))
    mixed = jnp.concatenate(parts, axis=1) * ps_ref[...]
    return mixed * _silu(zp)


def _mem_head(ref, h):
    halves = M_HEAD_DIM // LANES
    parts = [ref[pl.ds(c * M_HEADS + h, M_TOKENS, stride=halves * M_HEADS), :] for c in range(halves)]
    return jnp.concatenate(parts, axis=1)


def _mem_attend(q, gate_input, mk_ref, mv_ref, o_ref):
    nt = (((1,), (1,)), ((), ()))
    hs = [slice(n * M_HEAD_DIM, (n + 1) * M_HEAD_DIM) for n in range(M_HEADS)]
    scores = [lax.dot_general(q[:, sl], _mem_head(mk_ref, n).astype(BF16), nt,
                              preferred_element_type=F32) * M_SCALE for n, sl in enumerate(hs)]
    zm = gate_input()
    for n, (sl, s) in enumerate(zip(hs, scores)):
        m = jnp.max(s, axis=1, keepdims=True)
        p = jnp.exp(s - m)
        l = jnp.sum(p, axis=1, keepdims=True)
        o = jnp.dot(p.astype(BF16), _mem_head(mv_ref, n).astype(BF16), preferred_element_type=F32) / l
        o_ref[:, sl] = (o * _silu(zm[:, sl])).astype(o_ref.dtype)


def _proj_mem_kernel(x_ref, wq_ref, wz_ref, mk_ref, mv_ref, xs_ref, wt_ref, o_ref, act_ref):
    act_ref[...] = jnp.dot(xs_ref[...], wt_ref[...].astype(BF16), preferred_element_type=F32)
    xb = x_ref[...]
    q = jnp.dot(xb, wq_ref[...].astype(BF16), preferred_element_type=F32).astype(BF16)
    _mem_attend(q, lambda: jnp.dot(xb, wz_ref[...].astype(BF16), preferred_element_type=F32),
                mk_ref, mv_ref, o_ref)


def _proj_mem(xb, w, mk, mv, xs, batch, t, tm, tn):
    n, kdim = xb.shape
    ns = xs.shape[0]
    nt = t // tm
    tiles = w.shape[1] // tn
    assert batch * nt >= tiles, "not enough grid steps to carry the sample projection"
    rider = lambda b, i: (0, jnp.minimum(b * nt + i, tiles - 1))
    wspec = lambda col: pl.BlockSpec((kdim, M_WIDTH), lambda b, i: (0, col // M_WIDTH),
                                     pipeline_mode=pl.Buffered(1))
    mem = pl.BlockSpec((MEM_ROWS, LANES), lambda b, i: (b, 0))
    return pl.pallas_call(
        _proj_mem_kernel,
        out_shape=[jax.ShapeDtypeStruct((n, M_WIDTH), BF16), jax.ShapeDtypeStruct((ns, w.shape[1]), F32)],
        grid=(batch, nt),
        in_specs=[pl.BlockSpec((tm, kdim), lambda b, i: (b * nt + i, 0)), wspec(COL_QM), wspec(COL_ZM), mem, mem,
                  pl.BlockSpec((ns, kdim), lambda b, i: (0, 0), pipeline_mode=pl.Buffered(1)),
                  pl.BlockSpec((kdim, tn), rider)],
        out_specs=[pl.BlockSpec((tm, M_WIDTH), lambda b, i: (b * nt + i, 0)), pl.BlockSpec((ns, tn), rider)],
        compiler_params=_params("arbitrary", "arbitrary"),
        name="proj_mem",
    )(xb, w, w, mk, mv, xs, w)


def _gated_sum(b_refs, g_refs, w_ref, rows):
    projs = [jnp.dot(b_ref[rows, :], w_ref[n], preferred_element_type=F32) for n, b_ref in enumerate(b_refs)]
    acc = None
    for g_ref, proj in zip(g_refs, projs):
        term = _sigmoid(g_ref[rows, :].astype(F32)) * proj
        acc = term if acc is None else acc + term
    return acc


def _merge_kernel(ba_ref, bp_ref, bm_ref, g0_ref, g1_ref, g2_ref, wb_ref,
                  sa_ref, sp_ref, sm_ref, sg0_ref, sg1_ref, sg2_ref, o_ref, so_ref, wbb_ref, *, halves):
    @pl.when(pl.program_id(1) == 0)
    def _():
        wbb_ref[...] = wb_ref[...].astype(BF16)
        so_ref[...] = _gated_sum((sa_ref, sp_ref, sm_ref), (sg0_ref, sg1_ref, sg2_ref), wbb_ref,
                                 slice(None)).astype(so_ref.dtype)

    b_refs, g_refs = (ba_ref, bp_ref, bm_ref), (g0_ref, g1_ref, g2_ref)
    rows = o_ref.shape[0] // halves
    projs = [[jnp.dot(b_ref[r * rows:(r + 1) * rows, :], wbb_ref[n], preferred_element_type=F32)
              for n, b_ref in enumerate(b_refs)] for r in range(halves)]
    for r in range(halves):
        sl = slice(r * rows, (r + 1) * rows)
        acc = None
        for n, g_ref in enumerate(g_refs):
            term = _sigmoid(g_ref[sl, :].astype(F32)) * projs[r][n]
            acc = term if acc is None else acc + term
        o_ref[sl, :] = acc.astype(o_ref.dtype)


def _merge(ba, bp, bm, h, g_col, sa, sp, sm, sh, sg_col, wb, tm, tn):
    n, width = ba.shape
    ns = sa.shape[0]
    gspec = lambda k: pl.BlockSpec((tm, tn), lambda j, i: (i, (g_col + k * D_MODEL) // tn + j))
    sgspec = lambda k: pl.BlockSpec((ns, tn), lambda j, i: (0, (sg_col + k * D_MODEL) // tn + j))
    bspec = pl.BlockSpec((tm, width), lambda j, i: (i, 0))
    sbspec = pl.BlockSpec((ns, width), lambda j, i: (0, 0), pipeline_mode=pl.Buffered(1))
    return pl.pallas_call(
        functools.partial(_merge_kernel, halves=2 if tm % 32 == 0 else 1),
        out_shape=[jax.ShapeDtypeStruct((n, D_MODEL), BF16), jax.ShapeDtypeStruct((ns, D_MODEL), BF16)],
        grid=(D_MODEL // tn, n // tm),
        in_specs=[bspec, bspec, bspec, gspec(0), gspec(1), gspec(2),
                  pl.BlockSpec((N_BRANCH, width, tn), lambda j, i: (0, 0, j), pipeline_mode=pl.Buffered(1)),
                  sbspec, sbspec, sbspec, sgspec(0), sgspec(1), sgspec(2)],
        out_specs=[pl.BlockSpec((tm, tn), lambda j, i: (i, j)), pl.BlockSpec((ns, tn), lambda j, i: (0, j))],
        scratch_shapes=[pltpu.VMEM((N_BRANCH, width, tn), BF16)],
        compiler_params=_params("arbitrary", "arbitrary"),
        name="merge",
    )(ba, bp, bm, h, h, h, wb, sa, sp, sm, sh, sh, sh)


def _layer_norm_residual(x, proj, g_ref, b_ref, alpha):
    z = alpha * x + proj
    mu = jnp.mean(z, axis=1, keepdims=True)
    zc = z - mu
    var = jnp.mean(zc * zc, axis=1, keepdims=True)
    return zc * lax.rsqrt(var + LN_EPS) * g_ref[...] + b_ref[...]


def _out_kernel(m_ref, w_ref, x_ref, g_ref, b_ref, sm_ref, sx_ref, o_ref, so_ref, wb_ref, *, halves, alpha):
    @pl.when(pl.program_id(0) == 0)
    def _():
        wb_ref[...] = w_ref[...].astype(BF16)
        so_ref[...] = _layer_norm_residual(
            sx_ref[...], jnp.dot(sm_ref[...], wb_ref[...], preferred_element_type=F32), g_ref, b_ref, alpha)

    rows = o_ref.shape[0] // halves
    outs = [jnp.dot(m_ref[r * rows:(r + 1) * rows, :], wb_ref[...], preferred_element_type=F32)
            for r in range(halves)]
    for r in range(halves):
        sl = slice(r * rows, (r + 1) * rows)
        o_ref[sl, :] = _layer_norm_residual(x_ref[sl, :], outs[r], g_ref, b_ref, alpha)


def _out(merged, x, s_merged, s_x, w_out, ln_g, ln_b, alpha, tm):
    n = merged.shape[0]
    ns = s_merged.shape[0]
    row = pl.BlockSpec((tm, D_MODEL), lambda i: (i, 0))
    srow = pl.BlockSpec((ns, D_MODEL), lambda i: (0, 0), pipeline_mode=pl.Buffered(1))
    vec = pl.BlockSpec((1, D_MODEL), lambda i: (0, 0))
    wspec = pl.BlockSpec((D_MODEL, D_MODEL), lambda i: (0, 0), pipeline_mode=pl.Buffered(1))
    return pl.pallas_call(
        functools.partial(_out_kernel, halves=2, alpha=alpha),
        out_shape=[jax.ShapeDtypeStruct((n, D_MODEL), F32), jax.ShapeDtypeStruct((ns, D_MODEL), F32)],
        grid=(n // tm,),
        in_specs=[row, wspec, row, vec, vec, srow, srow],
        out_specs=[row, pl.BlockSpec((ns, D_MODEL), lambda i: (0, 0))],
        scratch_shapes=[pltpu.VMEM((D_MODEL, D_MODEL), BF16)],
        compiler_params=_params("arbitrary"),
        name="out_ln",
    )(merged, w_out, x, ln_g, ln_b, s_merged, s_x)


def _rest_col_block(tn):
    za_blocks = (COL_UP - COL_ZA) // tn
    return lambda j: jnp.where(j < za_blocks, COL_ZA // tn + j, COL_GM // tn + (j - za_blocks))


def _layer(xp2d, xs2d, bp, tp, bs, ts, past, w, zero_hist, hist_s, mk, mv, kt_hist, v_hist, mk_s, mv_s):
    ang = _rotary_angles(jnp.arange(tp))
    cos_t, sin_t = jnp.cos(ang).T, jnp.sin(ang).T
    qt, kt, kb, xb = _proj_qk(xp2d, w["w_in"], cos_t, sin_t, bp, tp, ROW_TILE, KEY_TILE,
                              A_SCALE * math.log2(math.e))
    v, vt, u, b_p = _proj_vup(xb, w["w_in"], zero_hist, w["pool_w"], w["pool_scale"], bp, tp, ROW_TILE, KEY_TILE)
    b_m, act_s = _proj_mem(xb, w["w_in"], mk, mv, xs2d.astype(BF16), bp, tp, ROW_TILE, COL_TILE)
    h = _proj(xb, w["w_in"], _rest_col_block(COL_TILE), H_COLS, BF16, GATE_ROW_TILE, COL_TILE, name="proj_rest")
    b_a = _attn_prompt(qt, kb, vt, h, w["gain"], w["lamv"], bp, tp, KEY_TILE)

    rot_s = _rotary_tables(past + jnp.arange(ts))
    s_a, k_s, s_p, s_m = _mix_sample(act_s, rot_s, kt_hist, v_hist, hist_s, mk_s, mv_s, w["gain"], w["lamv"],
                                     w["pool_w"], w["pool_scale"], bs, ts, past)

    merged, s_merged = _merge(b_a, b_p, b_m, h, H_GM, s_a, s_p, s_m, act_s, COL_GM, w["w_branch"],
                              ROW_TILE, COL_TILE)
    y_p, y_s = _out(merged, xp2d, s_merged, xs2d, w["w_out"], w["ln_g"], w["ln_b"], w["alpha"], OUT_ROW_TILE)
    return y_p, y_s, kt, v, u, k_s, act_s


def kernel(x_prompt, x_sample, cache_attn_k, cache_attn_v, cache_mem_k, cache_mem_v, state_pool, mem_prompt, w_in, w_mem_kv, lambda_q1, lambda_k1, lambda_q2, lambda_k2, subln_gain, pool_w, pool_scale, w_branch, w_out, ln_gain, ln_bias):
    bp, tp, d = x_prompt.shape
    bs, ts, _ = x_sample.shape
    past = cache_attn_k.shape[2]
    hist_rows = P_HIST + 1
    halves = M_HEAD_DIM // LANES

    def to_mem_layout(a):
        a = a.reshape(a.shape[0], M_TOKENS, M_HEADS, halves, LANES)
        return jnp.transpose(a, (0, 1, 3, 2, 4)).reshape(a.shape[0] * MEM_ROWS, LANES)

    def from_mem_layout(a, batch):
        a = a.reshape(batch, M_TOKENS, halves, M_HEADS, LANES)
        return jnp.transpose(a, (0, 1, 3, 2, 4)).reshape(batch, M_TOKENS, M_HEADS, M_HEAD_DIM)

    def run_layer(l, xp2d, xs2d):
        w = {
            "w_in": w_in[l],
            "gain": subln_gain[l].reshape(1, A_V_DIM),
            "lamv": jnp.stack([lambda_q1[l], lambda_k1[l], lambda_q2[l], lambda_k2[l],
                               jnp.full((A_QK_DIM,), 0.8 - 0.6 * math.exp(-0.3 * l), F32)]),
            "pool_w": pool_w[l],
            "pool_scale": pool_scale[l].reshape(1, P_WIDTH),
            "w_branch": w_branch[l],
            "w_out": w_out[l],
            "ln_g": ln_gain[l].reshape(1, d),
            "ln_b": ln_bias[l].reshape(1, d),
            "alpha": (2.0 * w_in.shape[0]) ** 0.25,
        }
        memf = mem_prompt.reshape(bp * M_TOKENS, d)
        mk, mv = _proj_memkv(memf, w_mem_kv[l], MEM_ROW_TILE)

        zero_hist = jnp.zeros((bp, hist_rows, P_WIDTH), F32)
        hist_s = jnp.pad(state_pool[l], ((0, 0), (hist_rows - P_HIST, 0), (0, 0)))
        kt_hist = jnp.transpose(cache_attn_k[l], (0, 2, 3, 4, 1)).reshape(bs, A_HEADS, 2 * A_QK_DIM, past)
        y_p, y_s, k_p, v_p, u_p, k_s, act_s = _layer(
            xp2d, xs2d, bp, tp, bs, ts, past, w, zero_hist, hist_s, mk, mv,
            kt_hist, cache_attn_v[l].reshape(bs, past * A_HEADS, A_V_DIM),
            to_mem_layout(cache_mem_k[l]), to_mem_layout(cache_mem_v[l]))
        v_s = act_s[:, COL_V:COL_V + A_WIDTH]
        u_s = act_s[:, COL_UP:COL_UP + P_WIDTH]

        pool_p = u_p[:, -P_HIST:]
        pool_s = jnp.concatenate([hist_s, u_s.reshape(bs, ts, P_WIDTH)], axis=1)[:, -P_HIST:]
        k_p = jnp.transpose(k_p.reshape(bp, A_HEADS, 2, A_QK_DIM, tp), (0, 4, 1, 2, 3))
        outs = (k_p, v_p.reshape(bp, tp, A_HEADS, A_V_DIM),
                from_mem_layout(mk, bp), from_mem_layout(mv, bp),
                pool_p,
                k_s.reshape(bs, ts, A_HEADS, 2, A_QK_DIM), v_s.reshape(bs, ts, A_HEADS, A_V_DIM), pool_s)
        return y_p, y_s, outs

    yp = x_prompt.reshape(bp * tp, d)
    ys = x_sample.reshape(bs * ts, d)
    per_layer = []
    for l in range(w_in.shape[0]):
        yp, ys, outs = run_layer(l, yp, ys)
        per_layer.append(outs)
    stacked = [jnp.stack([o[n] for o in per_layer]) for n in range(8)]
    return (yp.reshape(bp, tp, d), ys.reshape(bs, ts, d), *stacked)
```

```python
import functools
import math

import numpy as np
import jax
import jax.numpy as jnp
from jax import lax
from jax.experimental import pallas as pl
from jax.experimental.pallas import tpu as pltpu

F32 = jnp.float32
BF16 = jnp.bfloat16

V7X_VMEM_LIMIT_BYTES = 56 * 1024 * 1024
LANES = 128

D_MODEL = 2048
CHUNK = 64
A_HEADS = 8
A_QK_DIM = 64
A_V_DIM = 128
A_WIDTH = A_HEADS * A_V_DIM
A_SCALE = A_QK_DIM ** -0.5
ROT_DIM = A_QK_DIM // 4
ROPE_THETA = 500000.0
SUBLN_EPS = 1e-5
P_WINDOWS = (2, 4, 8, 16)
P_GROUP_DIM = 256
P_WIDTH = len(P_WINDOWS) * P_GROUP_DIM
P_HIST = max(P_WINDOWS) - 1
M_TOKENS = 256
M_HEADS = 4
M_HEAD_DIM = 256
M_WIDTH = M_HEADS * M_HEAD_DIM
M_SCALE = M_HEAD_DIM ** -0.5
MEM_ROWS = M_TOKENS * M_HEADS * (M_HEAD_DIM // LANES)
N_BRANCH = 3
LN_EPS = 1e-5
ONES_ROWS = 16

COL_Q = 0
COL_K = COL_Q + A_WIDTH
COL_V = COL_K + A_WIDTH
COL_ZA = COL_V + A_WIDTH
COL_UP = COL_ZA + A_WIDTH
COL_ZP = COL_UP + P_WIDTH
COL_QM = COL_ZP + P_WIDTH
COL_ZM = COL_QM + M_WIDTH
COL_GM = COL_ZM + M_WIDTH
IN_COLS = COL_GM + N_BRANCH * D_MODEL
H_ZA = 0
H_GM = H_ZA + A_WIDTH
H_COLS = H_GM + N_BRANCH * D_MODEL

KEY_TILE = 256
ROW_TILE = 512
OUT_ROW_TILE = 256
COL_TILE = 1024
GATE_ROW_TILE = 2048
MEM_ROW_TILE = 512
CACHE_SLOTS = 3


def _params(*sem):
    return pltpu.CompilerParams(dimension_semantics=sem, vmem_limit_bytes=V7X_VMEM_LIMIT_BYTES)


def _sigmoid(z):
    return 1.0 / (1.0 + jnp.exp(-z))


def _silu(z):
    return z * _sigmoid(z)


def _proj_kernel(x_ref, w_ref, o_ref):
    acc = jnp.dot(x_ref[...].astype(BF16), w_ref[...].astype(BF16), preferred_element_type=F32)
    o_ref[...] = acc.astype(o_ref.dtype)


def _proj(x, w, col_block, ncols, out_dtype, tm, tn, name="proj"):
    n, kdim = x.shape
    return pl.pallas_call(
        _proj_kernel,
        out_shape=jax.ShapeDtypeStruct((n, ncols), out_dtype),
        grid=(n // tm, ncols // tn),
        in_specs=[pl.BlockSpec((tm, kdim), lambda i, j: (i, 0)),
                  pl.BlockSpec((kdim, tn), lambda i, j: (0, col_block(j)))],
        out_specs=pl.BlockSpec((tm, tn), lambda i, j: (i, j)),
        compiler_params=_params("arbitrary", "arbitrary"),
        name=name,
    )(x, w)


def _proj_memkv_kernel(x_ref, w_ref, mk_ref, mv_ref):
    kv = jnp.dot(x_ref[...].astype(BF16), w_ref[...].astype(BF16), preferred_element_type=F32)
    tm = kv.shape[0]
    halves = M_HEAD_DIM // LANES
    group = halves * M_HEADS

    def store(o_ref):
        for h in range(M_HEADS):
            for c in range(halves):
                col = h * M_HEAD_DIM + c * LANES
                o_ref[pl.ds(c * M_HEADS + h, tm, stride=group), :] = kv[:, col:col + LANES]

    @pl.when(pl.program_id(1) == 0)
    def _():
        store(mk_ref)

    @pl.when(pl.program_id(1) == 1)
    def _():
        store(mv_ref)


def _proj_memkv(mem, w, tm):
    n, kdim = mem.shape
    group = M_WIDTH // LANES
    row = pl.BlockSpec((tm * group, LANES), lambda i, j: (i, 0))
    return pl.pallas_call(
        _proj_memkv_kernel,
        out_shape=[jax.ShapeDtypeStruct((n * group, LANES), F32)] * 2,
        grid=(n // tm, 2),
        in_specs=[pl.BlockSpec((tm, kdim), lambda i, j: (i, 0)),
                  pl.BlockSpec((kdim, M_WIDTH), lambda i, j: (0, j))],
        out_specs=[row, row],
        compiler_params=_params("arbitrary", "arbitrary"),
        name="proj_memkv",
    )(mem, w)


def _rotate_sublanes(acc, c, s):
    half = ROT_DIM // 2
    pieces = []
    for g in range(acc.shape[0] // A_QK_DIM):
        b = g * A_QK_DIM
        lo, hi = acc[b:b + half], acc[b + half:b + ROT_DIM]
        pieces += [lo * c - hi * s, hi * c + lo * s, acc[b + ROT_DIM:b + A_QK_DIM]]
    return jnp.concatenate(pieces, axis=0)


def _proj_qk_kernel(x_ref, wq_ref, wk_ref, cos_ref, sin_ref, qt_ref, kt_ref, kb_ref, xb_ref, *, q_scale):
    xb = x_ref[...].astype(BF16)
    xb_ref[...] = xb
    c, s = cos_ref[...], sin_ref[...]
    q = _rotate_sublanes(jnp.dot(xb, wq_ref[...].astype(BF16), preferred_element_type=F32).T, c, s)
    q = (q * q_scale).astype(qt_ref.dtype)
    tile = qt_ref.shape[-1]
    for n in range(qt_ref.shape[0]):
        qt_ref[n] = q[:, n * tile:(n + 1) * tile]
    k = _rotate_sublanes(jnp.dot(xb, wk_ref[...].astype(BF16), preferred_element_type=F32).T, c, s)
    kt_ref[...] = k
    kb_ref[...] = k.T.astype(kb_ref.dtype)


def _proj_qk(x, w, cos_t, sin_t, batch, t, tm, q_tile, q_scale):
    n, kdim = x.shape
    width = A_WIDTH
    nt = t // tm
    tab = pl.BlockSpec((ROT_DIM // 2, tm), lambda i: (0, i % nt))
    wspec = lambda col: pl.BlockSpec((kdim, width), lambda i: (0, col // width), pipeline_mode=pl.Buffered(1))
    tspec = pl.BlockSpec((None, width, tm), lambda i: (i // nt, 0, i % nt))
    qspec = pl.BlockSpec((None, tm // q_tile, width, q_tile), lambda i: (i // nt, i % nt, 0, 0))
    return pl.pallas_call(
        functools.partial(_proj_qk_kernel, q_scale=q_scale),
        out_shape=[jax.ShapeDtypeStruct((batch, t // q_tile, width, q_tile), BF16),
                   jax.ShapeDtypeStruct((batch, width, t), F32),
                   jax.ShapeDtypeStruct((n, width), BF16), jax.ShapeDtypeStruct((n, kdim), BF16)],
        grid=(n // tm,),
        in_specs=[pl.BlockSpec((tm, kdim), lambda i: (i, 0)), wspec(COL_Q), wspec(COL_K), tab, tab],
        out_specs=[qspec, tspec, pl.BlockSpec((tm, width), lambda i: (i, 0)),
                   pl.BlockSpec((tm, kdim), lambda i: (i, 0))],
        compiler_params=_params("arbitrary"),
        name="proj_qk",
    )(x, w, w, cos_t, sin_t)


def _proj_vup_kernel(x_ref, wv_ref, wu_ref, wz_ref, hist_ref, pw_ref, ps_ref, v_ref, vt_ref, ut_ref, bp_ref,
                     prev_ref, *, key_tile, nt):
    i = pl.program_id(0) % nt

    @pl.when(i == 0)
    def _():
        prev_ref[...] = hist_ref[...]

    xb = x_ref[...]
    tm = xb.shape[0]
    u = jnp.dot(xb, wu_ref[...].astype(BF16), preferred_element_type=F32)
    ut_ref[...] = u[tm - ut_ref.shape[0]:, :]
    zp = jnp.dot(xb, wz_ref[...].astype(BF16), preferred_element_type=F32)
    v = jnp.dot(xb, wv_ref[...].astype(BF16), preferred_element_type=F32)
    for h in range(A_HEADS):
        v_ref[pl.ds(h, tm, stride=A_HEADS), :] = v[:, h * A_V_DIM:(h + 1) * A_V_DIM]
    for c in range(tm // key_tile):
        vt_ref[c] = v[c * key_tile:(c + 1) * key_tile, :].T.astype(vt_ref.dtype)
    pos = i * tm + lax.broadcasted_iota(jnp.int32, (tm, 1), 0)
    bp_ref[...] = _pool_mix(u, prev_ref[...], zp, pw_ref, ps_ref, pos).astype(bp_ref.dtype)
    prev_ref[...] = u[tm - prev_ref.shape[0]:, :]


def _proj_vup(xb, w, hist, pool_w, pool_scale, batch, t, tm, key_tile):
    assert tm >= hist.shape[1]
    n, kdim = xb.shape
    width = A_WIDTH
    nt = t // tm
    hr = hist.shape[1]
    wspec = lambda col: pl.BlockSpec((kdim, width), lambda i: (0, col // width), pipeline_mode=pl.Buffered(1))
    row = pl.BlockSpec((tm, width), lambda i: (i, 0))
    return pl.pallas_call(
        functools.partial(_proj_vup_kernel, key_tile=key_tile, nt=nt),
        out_shape=[jax.ShapeDtypeStruct((n * A_HEADS, A_V_DIM), F32),
                   jax.ShapeDtypeStruct((batch, t // key_tile, width, key_tile), BF16),
                   jax.ShapeDtypeStruct((batch, hr, width), F32),
                   jax.ShapeDtypeStruct((n, width), BF16)],
        grid=(n // tm,),
        in_specs=[pl.BlockSpec((tm, kdim), lambda i: (i, 0)), wspec(COL_V), wspec(COL_UP), wspec(COL_ZP),
                  pl.BlockSpec((None, hr, P_WIDTH), lambda i: (i // nt, 0, 0)),
                  pl.BlockSpec((len(P_WINDOWS), P_GROUP_DIM, P_GROUP_DIM), lambda i: (0, 0, 0)),
                  pl.BlockSpec((1, P_WIDTH), lambda i: (0, 0))],
        out_specs=[pl.BlockSpec((tm * A_HEADS, A_V_DIM), lambda i: (i, 0)),
                   pl.BlockSpec((None, tm // key_tile, width, key_tile), lambda i: (i // nt, i % nt, 0, 0)),
                   pl.BlockSpec((None, hr, width), lambda i: (i // nt, 0, 0)), row],
        scratch_shapes=[pltpu.VMEM((hr, P_WIDTH), F32)],
        compiler_params=_params("arbitrary"),
        name="proj_vup",
    )(xb, w, w, w, hist, pool_w, pool_scale)


def _rotary_angles(pos):
    half = ROT_DIM // 2
    inv = ROPE_THETA ** (-(jnp.arange(half, dtype=F32) * 2.0) / ROT_DIM)
    return pos.astype(F32)[:, None] * inv[None, :]


def _rotary_tables(pos):
    half = ROT_DIM // 2
    ang = _rotary_angles(pos)
    cos, sin = jnp.cos(ang), jnp.sin(ang)
    t = pos.shape[0]
    ones = jnp.ones((t, A_QK_DIM - ROT_DIM), F32)
    zeros = jnp.zeros((t, A_QK_DIM - ROT_DIM), F32)
    zh = jnp.zeros((t, half), F32)
    c64 = jnp.concatenate([cos, cos, ones], axis=1)
    sa64 = jnp.concatenate([zh, sin, zeros], axis=1)
    sb64 = jnp.concatenate([-sin, zh, zeros], axis=1)
    tile = lambda a: jnp.concatenate([a, a], axis=1)
    return tile(c64), tile(sa64), tile(sb64)


def _lambda_values(lam_ref):
    lv = lam_ref[...]
    a = jnp.sum(lv[0:1] * lv[1:2], axis=1, keepdims=True)
    b = jnp.sum(lv[2:3] * lv[3:4], axis=1, keepdims=True)
    lam_init = lv[4:5, 0:1]
    return jnp.exp(a) - jnp.exp(b) + lam_init, lam_init


def _split_components(q):
    lane = lax.broadcasted_iota(jnp.int32, q.shape, 1)
    qf = q.astype(F32)
    return jnp.concatenate([jnp.where(lane < A_QK_DIM, qf, 0.0),
                            jnp.where(lane >= A_QK_DIM, qf, 0.0)], axis=0).astype(q.dtype)


def _attn_finish(acc, l, tq, lam, gain, za):
    o = acc[:tq] / l[:tq] - lam * (acc[tq:] / l[tq:])
    o = o * lax.rsqrt(jnp.mean(o * o, axis=1, keepdims=True) + SUBLN_EPS) * gain
    return o * _silu(za)


def _attn_prompt_kernel(qt_ref, k_ref, vt_ref, za_ref, gain_ref, lam_ref, o_ref, m_ref, l_ref, acc_ref,
                        s0_ref, s1_ref, *, tq, heads, nq):
    i = pl.program_id(2)
    hs = [slice(n * LANES, (n + 1) * LANES) for n in range(heads)]

    def stacked_queries(qi):
        out = []
        for sl in hs:
            qt = qt_ref[qi, sl, :].astype(F32)
            row = lax.broadcasted_iota(jnp.int32, qt.shape, 0)
            out.append(jnp.concatenate([jnp.where(row < A_QK_DIM, qt, 0.0),
                                        jnp.where(row >= A_QK_DIM, qt, 0.0)], axis=1).astype(BF16))
        return out

    def score(j, qq):
        start = pl.multiple_of(j * tq, tq)
        sl, stacked = qq
        return jnp.dot(k_ref[pl.ds(start, tq), sl], stacked, preferred_element_type=F32)

    qqts = stacked_queries(i)

    @pl.when(i == 0)
    def _():
        for n in range(heads):
            s0_ref[n] = score(0, (hs[n], qqts[n]))

    m_ref[...] = jnp.full(m_ref.shape, -jnp.inf, F32)
    l_ref[...] = jnp.zeros(l_ref.shape, F32)
    acc_ref[...] = jnp.zeros(acc_ref.shape, F32)
    ones_rows = jnp.ones((ONES_ROWS, tq), BF16)

    def step(j, cur_ref, nxt_ref, masked):
        for n in range(heads):
            s = cur_ref[n]
            if masked:
                krow = lax.broadcasted_iota(jnp.int32, s.shape, 0)
                qcol = lax.broadcasted_iota(jnp.int32, s.shape, 1) % tq
                s = jnp.where((krow // CHUNK) <= (qcol // CHUNK), s, -jnp.inf)
            m = m_ref[n]
            m_new = jnp.maximum(m, jnp.max(s, axis=0, keepdims=True))
            alpha = jnp.exp2(m - m_new)
            p = jnp.exp2(s - m_new).astype(BF16)
            m_ref[n] = m_new
            lhs = jnp.concatenate([vt_ref[j, hs[n], :], ones_rows], axis=0)
            pv = jnp.dot(lhs, p, preferred_element_type=F32)
            acc_ref[n] = alpha * acc_ref[n] + pv[:A_V_DIM]
            l_ref[n] = alpha * l_ref[n] + pv[A_V_DIM:A_V_DIM + 1]
            if nxt_ref is not None:
                nxt_ref[n] = score(j + 1, (hs[n], qqts[n]))

    def pair(jj, carry):
        step(2 * jj, s0_ref, s1_ref, False)
        step(2 * jj + 1, s1_ref, s0_ref, False)
        return carry

    lax.fori_loop(0, i // 2, pair, 0)

    @pl.when(i % 2 == 1)
    def _():
        step(i - 1, s0_ref, s1_ref, False)
        step(i, s1_ref, None, True)

    @pl.when(i % 2 == 0)
    def _():
        step(i, s0_ref, None, True)

    nxt = stacked_queries(jnp.minimum(i + 1, nq - 1))
    lam, lam_init = _lambda_values(lam_ref)
    gain = gain_ref[...] * (1.0 - lam_init)
    for n in range(heads):
        s0_ref[n] = score(0, (hs[n], nxt[n]))
        inv, acc = 1.0 / l_ref[n], acc_ref[n]
        ot = acc[:, :tq] * inv[:, :tq] - lam * (acc[:, tq:] * inv[:, tq:])
        o = ot.T
        o = o * lax.rsqrt(jnp.mean(o * o, axis=1, keepdims=True) + SUBLN_EPS) * gain
        o_ref[:, hs[n]] = (o * _silu(za_ref[:, hs[n]].astype(F32))).astype(o_ref.dtype)


def _attn_prompt(qt, k, vt, h, gain, lamv, batch, t, tq, heads=A_HEADS):
    nq = t // tq
    n = batch * t
    hw = heads * LANES
    tiles = pl.BlockSpec((None, nq, hw, tq), lambda b, hh, i: (b, 0, hh, 0))
    return pl.pallas_call(
        functools.partial(_attn_prompt_kernel, tq=tq, heads=heads, nq=nq),
        out_shape=jax.ShapeDtypeStruct((n, A_WIDTH), BF16),
        grid=(batch, A_HEADS // heads, nq),
        in_specs=[
            tiles,
            pl.BlockSpec((t, hw), lambda b, hh, i: (b, hh)),
            tiles,
            pl.BlockSpec((tq, hw), lambda b, hh, i: (b * nq + i, H_ZA // hw + hh)),
            pl.BlockSpec((1, LANES), lambda b, hh, i: (0, 0)),
            pl.BlockSpec((5, A_QK_DIM), lambda b, hh, i: (0, 0)),
        ],
        out_specs=pl.BlockSpec((tq, hw), lambda b, hh, i: (b * nq + i, hh)),
        scratch_shapes=[pltpu.VMEM((heads, 1, 2 * tq), F32), pltpu.VMEM((heads, 1, 2 * tq), F32),
                        pltpu.VMEM((heads, A_V_DIM, 2 * tq), F32),
                        pltpu.VMEM((heads, tq, 2 * tq), F32), pltpu.VMEM((heads, tq, 2 * tq), F32)],
        compiler_params=_params("arbitrary", "arbitrary", "arbitrary"),
        name="attn_prompt",
    )(qt, k, vt, h, gain, lamv)


def _chunk_mask(s, ts, q_pos0, k_pos0):
    q_pos = q_pos0 + lax.broadcasted_iota(jnp.int32, s.shape, 0) % ts
    k_pos = k_pos0 + lax.broadcasted_iota(jnp.int32, s.shape, 1)
    return jnp.where((k_pos // CHUNK) <= (q_pos // CHUNK), s, -jnp.inf)


def _rotate_lanes(blk, c, sa, sb):
    half = ROT_DIM // 2
    return blk * c + pltpu.roll(blk, half, 1) * sa + pltpu.roll(blk, LANES - half, 1) * sb


def _mix_sample_kernel(q_ref, k_ref, v_ref, za_ref, u_ref, zp_ref, qm_ref, zm_ref, cos_ref, sa_ref, sb_ref,
                       kct_hbm, vc_hbm, hist_ref, mk_ref, mv_ref, gain_ref, lam_ref, pw_ref, ps_ref,
                       o_ref, krot_ref, bp_ref, bm_ref, kbuf_ref, vbuf_ref, sem, *,
                       ts, past, mask_cache, mask_new, batch):
    b = pl.program_id(0)

    def cache_copies(bb, slot):
        return (pltpu.make_async_copy(kct_hbm.at[bb], kbuf_ref.at[slot], sem.at[0, slot]),
                pltpu.make_async_copy(vc_hbm.at[bb], vbuf_ref.at[slot], sem.at[1, slot]))

    @pl.when(b == 0)
    def _():
        for bb in range(min(CACHE_SLOTS - 1, batch)):
            for cp in cache_copies(bb, bb):
                cp.start()

    @pl.when(b + CACHE_SLOTS - 1 < batch)
    def _():
        nxt = b + CACHE_SLOTS - 1
        for cp in cache_copies(nxt, nxt % CACHE_SLOTS):
            cp.start()

    slot = b % CACHE_SLOTS
    for cp in cache_copies(b, slot):
        cp.wait()
    kct_ref, vc_ref = kbuf_ref.at[slot], vbuf_ref.at[slot]

    pos = past + lax.broadcasted_iota(jnp.int32, (ts, 1), 0)
    bp_ref[...] = _pool_mix(u_ref[...], hist_ref[...], zp_ref[...], pw_ref, ps_ref, pos).astype(bp_ref.dtype)
    _mem_attend(qm_ref[...].astype(BF16), lambda: zm_ref[...], mk_ref, mv_ref, bm_ref)

    c, sa, sb = cos_ref[...], sa_ref[...], sb_ref[...]
    nt = (((1,), (1,)), ((), ()))
    hs = [slice(n * LANES, (n + 1) * LANES) for n in range(A_HEADS)]
    scores = []
    for n in range(A_HEADS):
        q = (_rotate_lanes(q_ref[:, hs[n]], c, sa, sb) * A_SCALE).astype(BF16)
        k = _rotate_lanes(k_ref[:, hs[n]], c, sa, sb)
        krot_ref[:, hs[n]] = k
        qq = _split_components(q)
        s_c = jnp.dot(qq, kct_ref[n].astype(BF16), preferred_element_type=F32)
        s_n = lax.dot_general(qq, k.astype(BF16), nt, preferred_element_type=F32)
        if mask_cache:
            s_c = _chunk_mask(s_c, ts, past, 0)
        if mask_new:
            s_n = _chunk_mask(s_n, ts, past, past)
        scores.append((s_c, s_n))
    lam, lam_init = _lambda_values(lam_ref)
    gain = gain_ref[...] * (1.0 - lam_init)
    for n, (s_c, s_n) in enumerate(scores):
        m = jnp.maximum(jnp.max(s_c, axis=1, keepdims=True), jnp.max(s_n, axis=1, keepdims=True))
        p_c = jnp.exp(s_c - m)
        p_n = jnp.exp(s_n - m)
        l = jnp.sum(p_c, axis=1, keepdims=True) + jnp.sum(p_n, axis=1, keepdims=True)
        vc = vc_ref[pl.ds(n, past, stride=A_HEADS), :].astype(BF16)
        acc = (jnp.dot(p_c.astype(BF16), vc, preferred_element_type=F32)
               + jnp.dot(p_n.astype(BF16), v_ref[:, hs[n]].astype(BF16), preferred_element_type=F32))
        out = _attn_finish(acc, l, ts, lam, gain, za_ref[:, hs[n]].astype(F32))
        o_ref[:, hs[n]] = out.astype(o_ref.dtype)


def _needs_mask(q_pos, k_pos):
    return not bool(((k_pos[None, :] // CHUNK) <= (q_pos[:, None] // CHUNK)).all())


def _mix_sample(act, rot, kct, vc, hist, mk, mv, gain, lamv, pool_w, pool_scale, batch, ts, past):
    q_pos = past + np.arange(ts)
    seg = lambda col: pl.BlockSpec((ts, A_WIDTH), lambda b: (b, col // A_WIDTH))
    tab = pl.BlockSpec((ts, LANES), lambda b: (0, 0))
    row = pl.BlockSpec((ts, A_WIDTH), lambda b: (b, 0))
    mem = pl.BlockSpec((MEM_ROWS, LANES), lambda b: (b, 0))
    bshape = jax.ShapeDtypeStruct((batch * ts, A_WIDTH), BF16)
    return pl.pallas_call(
        functools.partial(_mix_sample_kernel, ts=ts, past=past,
                          mask_cache=_needs_mask(q_pos, np.arange(past)),
                          mask_new=_needs_mask(q_pos, q_pos), batch=batch),
        out_shape=[bshape, jax.ShapeDtypeStruct((batch * ts, A_WIDTH), F32), bshape, bshape],
        grid=(batch,),
        in_specs=[
            seg(COL_Q), seg(COL_K), seg(COL_V), seg(COL_ZA), seg(COL_UP), seg(COL_ZP), seg(COL_QM), seg(COL_ZM),
            tab, tab, tab,
            pl.BlockSpec(memory_space=pl.ANY),
            pl.BlockSpec(memory_space=pl.ANY),
            pl.BlockSpec((None, hist.shape[1], P_WIDTH), lambda b: (b, 0, 0)),
            mem, mem,
            pl.BlockSpec((1, LANES), lambda b: (0, 0)),
            pl.BlockSpec((5, A_QK_DIM), lambda b: (0, 0)),
            pl.BlockSpec((len(P_WINDOWS), P_GROUP_DIM, P_GROUP_DIM), lambda b: (0, 0, 0)),
            pl.BlockSpec((1, P_WIDTH), lambda b: (0, 0)),
        ],
        out_specs=[row, row, row, row],
        scratch_shapes=[pltpu.VMEM((CACHE_SLOTS, A_HEADS, LANES, past), F32),
                        pltpu.VMEM((CACHE_SLOTS, past * A_HEADS, LANES), F32),
                        pltpu.SemaphoreType.DMA((2, CACHE_SLOTS))],
        compiler_params=_params("arbitrary"),
        name="mix_sample",
    )(*([act] * 8), *rot, kct, vc, hist, mk, mv, gain, lamv, pool_w, pool_scale)


def _pool_mix(u, prev, zp, pw_ref, ps_ref, pos):
    tm = u.shape[0]
    ext = jnp.concatenate([prev, u], axis=0)
    hist_rows = prev.shape[0]
    parts = []
    for g, w in enumerate(P_WINDOWS):
        sl = slice(g * P_GROUP_DIM, (g + 1) * P_GROUP_DIM)
        tot = ext[:, sl]
        span = 1
        while span < w:
            tot = tot + pltpu.roll(tot, span, 0)
            span *= 2
        tot = tot[hist_rows:]
        cnt = jnp.minimum(pos + 1, w).astype(F32)
        pooled = tot / cnt - u[:, sl]
        parts.append(jnp.dot(pooled.astype(BF16), pw_ref[g].astype(BF16), preferred_element_type=F32))
    mixed = jnp.concatenate(parts, axis=1) * ps_ref[...]
    return mixed * _silu(zp)


def _mem_head(ref, h):
    halves = M_HEAD_DIM // LANES
    parts = [ref[pl.ds(c * M_HEADS + h, M_TOKENS, stride=halves * M_HEADS), :] for c in range(halves)]
    return jnp.concatenate(parts, axis=1)


def _mem_attend(q, gate_input, mk_ref, mv_ref, o_ref):
    nt = (((1,), (1,)), ((), ()))
    hs = [slice(n * M_HEAD_DIM, (n + 1) * M_HEAD_DIM) for n in range(M_HEADS)]
    scores = [lax.dot_general(q[:, sl], _mem_head(mk_ref, n).astype(BF16), nt,
                              preferred_element_type=F32) * M_SCALE for n, sl in enumerate(hs)]
    zm = gate_input()
    for n, (sl, s) in enumerate(zip(hs, scores)):
        m = jnp.max(s, axis=1, keepdims=True)
        p = jnp.exp(s - m)
        l = jnp.sum(p, axis=1, keepdims=True)
        o = jnp.dot(p.astype(BF16), _mem_head(mv_ref, n).astype(BF16), preferred_element_type=F32) / l
        o_ref[:, sl] = (o * _silu(zm[:, sl])).astype(o_ref.dtype)


def _proj_mem_kernel(x_ref, wq_ref, wz_ref, mk_ref, mv_ref, xs_ref, wt_ref, o_ref, act_ref):
    act_ref[...] = jnp.dot(xs_ref[...], wt_ref[...].astype(BF16), preferred_element_type=F32)
    xb = x_ref[...]
    q = jnp.dot(xb, wq_ref[...].astype(BF16), preferred_element_type=F32).astype(BF16)
    _mem_attend(q, lambda: jnp.dot(xb, wz_ref[...].astype(BF16), preferred_element_type=F32),
                mk_ref, mv_ref, o_ref)


def _proj_mem(xb, w, mk, mv, xs, batch, t, tm, tn):
    n, kdim = xb.shape
    ns = xs.shape[0]
    nt = t // tm
    tiles = w.shape[1] // tn
    assert batch * nt >= tiles, "not enough grid steps to carry the sample projection"
    rider = lambda b, i: (0, jnp.minimum(b * nt + i, tiles - 1))
    wspec = lambda col: pl.BlockSpec((kdim, M_WIDTH), lambda b, i: (0, col // M_WIDTH),
                                     pipeline_mode=pl.Buffered(1))
    mem = pl.BlockSpec((MEM_ROWS, LANES), lambda b, i: (b, 0))
    return pl.pallas_call(
        _proj_mem_kernel,
        out_shape=[jax.ShapeDtypeStruct((n, M_WIDTH), BF16), jax.ShapeDtypeStruct((ns, w.shape[1]), F32)],
        grid=(batch, nt),
        in_specs=[pl.BlockSpec((tm, kdim), lambda b, i: (b * nt + i, 0)), wspec(COL_QM), wspec(COL_ZM), mem, mem,
                  pl.BlockSpec((ns, kdim), lambda b, i: (0, 0), pipeline_mode=pl.Buffered(1)),
                  pl.BlockSpec((kdim, tn), rider)],
        out_specs=[pl.BlockSpec((tm, M_WIDTH), lambda b, i: (b * nt + i, 0)), pl.BlockSpec((ns, tn), rider)],
        compiler_params=_params("arbitrary", "arbitrary"),
        name="proj_mem",
    )(xb, w, w, mk, mv, xs, w)


def _gated_sum(b_refs, g_refs, w_ref, rows):
    projs = [jnp.dot(b_ref[rows, :], w_ref[n], preferred_element_type=F32) for n, b_ref in enumerate(b_refs)]
    acc = None
    for g_ref, proj in zip(g_refs, projs):
        term = _sigmoid(g_ref[rows, :].astype(F32)) * proj
        acc = term if acc is None else acc + term
    return acc


def _merge_kernel(ba_ref, bp_ref, bm_ref, g0_ref, g1_ref, g2_ref, wb_ref,
                  sa_ref, sp_ref, sm_ref, sg0_ref, sg1_ref, sg2_ref, o_ref, so_ref, wbb_ref, *, halves):
    @pl.when(pl.program_id(1) == 0)
    def _():
        wbb_ref[...] = wb_ref[...].astype(BF16)
        so_ref[...] = _gated_sum((sa_ref, sp_ref, sm_ref), (sg0_ref, sg1_ref, sg2_ref), wbb_ref,
                                 slice(None)).astype(so_ref.dtype)

    b_refs, g_refs = (ba_ref, bp_ref, bm_ref), (g0_ref, g1_ref, g2_ref)
    rows = o_ref.shape[0] // halves
    projs = [[jnp.dot(b_ref[r * rows:(r + 1) * rows, :], wbb_ref[n], preferred_element_type=F32)
              for n, b_ref in enumerate(b_refs)] for r in range(halves)]
    for r in range(halves):
        sl = slice(r * rows, (r + 1) * rows)
        acc = None
        for n, g_ref in enumerate(g_refs):
            term = _sigmoid(g_ref[sl, :].astype(F32)) * projs[r][n]
            acc = term if acc is None else acc + term
        o_ref[sl, :] = acc.astype(o_ref.dtype)


def _merge(ba, bp, bm, h, g_col, sa, sp, sm, sh, sg_col, wb, tm, tn):
    n, width = ba.shape
    ns = sa.shape[0]
    gspec = lambda k: pl.BlockSpec((tm, tn), lambda j, i: (i, (g_col + k * D_MODEL) // tn + j))
    sgspec = lambda k: pl.BlockSpec((ns, tn), lambda j, i: (0, (sg_col + k * D_MODEL) // tn + j))
    bspec = pl.BlockSpec((tm, width), lambda j, i: (i, 0))
    sbspec = pl.BlockSpec((ns, width), lambda j, i: (0, 0), pipeline_mode=pl.Buffered(1))
    return pl.pallas_call(
        functools.partial(_merge_kernel, halves=2 if tm % 32 == 0 else 1),
        out_shape=[jax.ShapeDtypeStruct((n, D_MODEL), BF16), jax.ShapeDtypeStruct((ns, D_MODEL), BF16)],
        grid=(D_MODEL // tn, n // tm),
        in_specs=[bspec, bspec, bspec, gspec(0), gspec(1), gspec(2),
                  pl.BlockSpec((N_BRANCH, width, tn), lambda j, i: (0, 0, j), pipeline_mode=pl.Buffered(1)),
                  sbspec, sbspec, sbspec, sgspec(0), sgspec(1), sgspec(2)],
        out_specs=[pl.BlockSpec((tm, tn), lambda j, i: (i, j)), pl.BlockSpec((ns, tn), lambda j, i: (0, j))],
        scratch_shapes=[pltpu.VMEM((N_BRANCH, width, tn), BF16)],
        compiler_params=_params("arbitrary", "arbitrary"),
        name="merge",
    )(ba, bp, bm, h, h, h, wb, sa, sp, sm, sh, sh, sh)


def _layer_norm_residual(x, proj, g_ref, b_ref, alpha):
    z = alpha * x + proj
    mu = jnp.mean(z, axis=1, keepdims=True)
    zc = z - mu
    var = jnp.mean(zc * zc, axis=1, keepdims=True)
    return zc * lax.rsqrt(var + LN_EPS) * g_ref[...] + b_ref[...]


def _out_kernel(m_ref, w_ref, x_ref, g_ref, b_ref, sm_ref, sx_ref, o_ref, so_ref, wb_ref, *, halves, alpha):
    @pl.when(pl.program_id(0) == 0)
    def _():
        wb_ref[...] = w_ref[...].astype(BF16)
        so_ref[...] = _layer_norm_residual(
            sx_ref[...], jnp.dot(sm_ref[...], wb_ref[...], preferred_element_type=F32), g_ref, b_ref, alpha)

    rows = o_ref.shape[0] // halves
    outs = [jnp.dot(m_ref[r * rows:(r + 1) * rows, :], wb_ref[...], preferred_element_type=F32)
            for r in range(halves)]
    for r in range(halves):
        sl = slice(r * rows, (r + 1) * rows)
        o_ref[sl, :] = _layer_norm_residual(x_ref[sl, :], outs[r], g_ref, b_ref, alpha)


def _out(merged, x, s_merged, s_x, w_out, ln_g, ln_b, alpha, tm):
    n = merged.shape[0]
    ns = s_merged.shape[0]
    row = pl.BlockSpec((tm, D_MODEL), lambda i: (i, 0))
    srow = pl.BlockSpec((ns, D_MODEL), lambda i: (0, 0), pipeline_mode=pl.Buffered(1))
    vec = pl.BlockSpec((1, D_MODEL), lambda i: (0, 0))
    wspec = pl.BlockSpec((D_MODEL, D_MODEL), lambda i: (0, 0), pipeline_mode=pl.Buffered(1))
    return pl.pallas_call(
        functools.partial(_out_kernel, halves=2, alpha=alpha),
        out_shape=[jax.ShapeDtypeStruct((n, D_MODEL), F32), jax.ShapeDtypeStruct((ns, D_MODEL), F32)],
        grid=(n // tm,),
        in_specs=[row, wspec, row, vec, vec, srow, srow],
        out_specs=[row, pl.BlockSpec((ns, D_MODEL), lambda i: (0, 0))],
        scratch_shapes=[pltpu.VMEM((D_MODEL, D_MODEL), BF16)],
        compiler_params=_params("arbitrary"),
        name="out_ln",
    )(merged, w_out, x, ln_g, ln_b, s_merged, s_x)


def _rest_col_block(tn):
    za_blocks = (COL_UP - COL_ZA) // tn
    return lambda j: jnp.where(j < za_blocks, COL_ZA // tn + j, COL_GM // tn + (j - za_blocks))


def _layer(xp2d, xs2d, bp, tp, bs, ts, past, w, zero_hist, hist_s, mk, mv, kt_hist, v_hist, mk_s, mv_s):
    ang = _rotary_angles(jnp.arange(tp))
    cos_t, sin_t = jnp.cos(ang).T, jnp.sin(ang).T
    qt, kt, kb, xb = _proj_qk(xp2d, w["w_in"], cos_t, sin_t, bp, tp, ROW_TILE, KEY_TILE,
                              A_SCALE * math.log2(math.e))
    v, vt, u, b_p = _proj_vup(xb, w["w_in"], zero_hist, w["pool_w"], w["pool_scale"], bp, tp, ROW_TILE, KEY_TILE)
    b_m, act_s = _proj_mem(xb, w["w_in"], mk, mv, xs2d.astype(BF16), bp, tp, ROW_TILE, COL_TILE)
    h = _proj(xb, w["w_in"], _rest_col_block(COL_TILE), H_COLS, BF16, GATE_ROW_TILE, COL_TILE, name="proj_rest")
    b_a = _attn_prompt(qt, kb, vt, h, w["gain"], w["lamv"], bp, tp, KEY_TILE)

    rot_s = _rotary_tables(past + jnp.arange(ts))
    s_a, k_s, s_p, s_m = _mix_sample(act_s, rot_s, kt_hist, v_hist, hist_s, mk_s, mv_s, w["gain"], w["lamv"],
                                     w["pool_w"], w["pool_scale"], bs, ts, past)

    merged, s_merged = _merge(b_a, b_p, b_m, h, H_GM, s_a, s_p, s_m, act_s, COL_GM, w["w_branch"],
                              ROW_TILE, COL_TILE)
    y_p, y_s = _out(merged, xp2d, s_merged, xs2d, w["w_out"], w["ln_g"], w["ln_b"], w["alpha"], OUT_ROW_TILE)
    return y_p, y_s, kt, v, u, k_s, act_s


def kernel(x_prompt, x_sample, cache_attn_k, cache_attn_v, cache_mem_k, cache_mem_v, state_pool, mem_prompt, w_in, w_mem_kv, lambda_q1, lambda_k1, lambda_q2, lambda_k2, subln_gain, pool_w, pool_scale, w_branch, w_out, ln_gain, ln_bias):
    bp, tp, d = x_prompt.shape
    bs, ts, _ = x_sample.shape
    past = cache_attn_k.shape[2]
    hist_rows = P_HIST + 1
    halves = M_HEAD_DIM // LANES

    def to_mem_layout(a):
        a = a.reshape(a.shape[0], M_TOKENS, M_HEADS, halves, LANES)
        return jnp.transpose(a, (0, 1, 3, 2, 4)).reshape(a.shape[0] * MEM_ROWS, LANES)

    def from_mem_layout(a, batch):
        a = a.reshape(batch, M_TOKENS, halves, M_HEADS, LANES)
        return jnp.transpose(a, (0, 1, 3, 2, 4)).reshape(batch, M_TOKENS, M_HEADS, M_HEAD_DIM)

    def run_layer(l, xp2d, xs2d):
        w = {
            "w_in": w_in[l],
            "gain": subln_gain[l].reshape(1, A_V_DIM),
            "lamv": jnp.stack([lambda_q1[l], lambda_k1[l], lambda_q2[l], lambda_k2[l],
                               jnp.full((A_QK_DIM,), 0.8 - 0.6 * math.exp(-0.3 * l), F32)]),
            "pool_w": pool_w[l],
            "pool_scale": pool_scale[l].reshape(1, P_WIDTH),
            "w_branch": w_branch[l],
            "w_out": w_out[l],
            "ln_g": ln_gain[l].reshape(1, d),
            "ln_b": ln_bias[l].reshape(1, d),
            "alpha": (2.0 * w_in.shape[0]) ** 0.25,
        }
        memf = mem_prompt.reshape(bp * M_TOKENS, d)
        mk, mv = _proj_memkv(memf, w_mem_kv[l], MEM_ROW_TILE)

        zero_hist = jnp.zeros((bp, hist_rows, P_WIDTH), F32)
        hist_s = jnp.pad(state_pool[l], ((0, 0), (hist_rows - P_HIST, 0), (0, 0)))
        kt_hist = jnp.transpose(cache_attn_k[l], (0, 2, 3, 4, 1)).reshape(bs, A_HEADS, 2 * A_QK_DIM, past)
        y_p, y_s, k_p, v_p, u_p, k_s, act_s = _layer(
            xp2d, xs2d, bp, tp, bs, ts, past, w, zero_hist, hist_s, mk, mv,
            kt_hist, cache_attn_v[l].reshape(bs, past * A_HEADS, A_V_DIM),
            to_mem_layout(cache_mem_k[l]), to_mem_layout(cache_mem_v[l]))
        v_s = act_s[:, COL_V:COL_V + A_WIDTH]
        u_s = act_s[:, COL_UP:COL_UP + P_WIDTH]

        pool_p = u_p[:, -P_HIST:]
        pool_s = jnp.concatenate([hist_s, u_s.reshape(bs, ts, P_WIDTH)], axis=1)[:, -P_HIST:]
        k_p = jnp.transpose(k_p.reshape(bp, A_HEADS, 2, A_QK_DIM, tp), (0, 4, 1, 2, 3))
        outs = (k_p, v_p.reshape(bp, tp, A_HEADS, A_V_DIM),
                from_mem_layout(mk, bp), from_mem_layout(mv, bp),
                pool_p,
                k_s.reshape(bs, ts, A_HEADS, 2, A_QK_DIM), v_s.reshape(bs, ts, A_HEADS, A_V_DIM), pool_s)
        return y_p, y_s, outs

    yp = x_prompt.reshape(bp * tp, d)
    ys = x_sample.reshape(bs * ts, d)
    per_layer = []
    for l in range(w_in.shape[0]):
        yp, ys, outs = run_layer(l, yp, ys)
        per_layer.append(outs)
    stacked = [jnp.stack([o[n] for o in per_layer]) for n in range(8)]
    return (yp.reshape(bp, tp, d), ys.reshape(bs, ts, d), *stacked)
```

```python
import functools
import math

import numpy as np
import jax
import jax.numpy as jnp
from jax import lax
from jax.experimental import pallas as pl
from jax.experimental.pallas import tpu as pltpu

F32 = jnp.float32
BF16 = jnp.bfloat16

V7X_VMEM_LIMIT_BYTES = 56 * 1024 * 1024
LANES = 128

D_MODEL = 2048
CHUNK = 64
A_HEADS = 8
A_QK_DIM = 64
A_V_DIM = 128
A_WIDTH = A_HEADS * A_V_DIM
A_SCALE = A_QK_DIM ** -0.5
ROT_DIM = A_QK_DIM // 4
ROPE_THETA = 500000.0
SUBLN_EPS = 1e-5
P_WINDOWS = (2, 4, 8, 16)
P_GROUP_DIM = 256
P_WIDTH = len(P_WINDOWS) * P_GROUP_DIM
P_HIST = max(P_WINDOWS) - 1
M_TOKENS = 256
M_HEADS = 4
M_HEAD_DIM = 256
M_WIDTH = M_HEADS * M_HEAD_DIM
M_SCALE = M_HEAD_DIM ** -0.5
MEM_ROWS = M_TOKENS * M_HEADS * (M_HEAD_DIM // LANES)
N_BRANCH = 3
LN_EPS = 1e-5
ONES_ROWS = 16

COL_Q = 0
COL_K = COL_Q + A_WIDTH
COL_V = COL_K + A_WIDTH
COL_ZA = COL_V + A_WIDTH
COL_UP = COL_ZA + A_WIDTH
COL_ZP = COL_UP + P_WIDTH
COL_QM = COL_ZP + P_WIDTH
COL_ZM = COL_QM + M_WIDTH
COL_GM = COL_ZM + M_WIDTH
IN_COLS = COL_GM + N_BRANCH * D_MODEL
H_ZA = 0
H_GM = H_ZA + A_WIDTH
H_COLS = H_GM + N_BRANCH * D_MODEL

KEY_TILE = 256
ROW_TILE = 512
OUT_ROW_TILE = 256
COL_TILE = 1024
GATE_ROW_TILE = 2048
MEM_ROW_TILE = 512


def _params(*sem):
    return pltpu.CompilerParams(dimension_semantics=sem, vmem_limit_bytes=V7X_VMEM_LIMIT_BYTES)


def _sigmoid(z):
    return 1.0 / (1.0 + jnp.exp(-z))


def _silu(z):
    return z * _sigmoid(z)


def _proj_kernel(x_ref, w_ref, o_ref):
    acc = jnp.dot(x_ref[...].astype(BF16), w_ref[...].astype(BF16), preferred_element_type=F32)
    o_ref[...] = acc.astype(o_ref.dtype)


def _proj(x, w, col_block, ncols, out_dtype, tm, tn, name="proj"):
    n, kdim = x.shape
    return pl.pallas_call(
        _proj_kernel,
        out_shape=jax.ShapeDtypeStruct((n, ncols), out_dtype),
        grid=(n // tm, ncols // tn),
        in_specs=[pl.BlockSpec((tm, kdim), lambda i, j: (i, 0)),
                  pl.BlockSpec((kdim, tn), lambda i, j: (0, col_block(j)))],
        out_specs=pl.BlockSpec((tm, tn), lambda i, j: (i, j)),
        compiler_params=_params("arbitrary", "arbitrary"),
        name=name,
    )(x, w)


def _proj_memkv_kernel(x_ref, w_ref, mk_ref, mv_ref):
    kv = jnp.dot(x_ref[...].astype(BF16), w_ref[...].astype(BF16), preferred_element_type=F32)
    tm = kv.shape[0]
    halves = M_HEAD_DIM // LANES
    group = halves * M_HEADS

    def store(o_ref):
        for h in range(M_HEADS):
            for c in range(halves):
                col = h * M_HEAD_DIM + c * LANES
                o_ref[pl.ds(c * M_HEADS + h, tm, stride=group), :] = kv[:, col:col + LANES]

    @pl.when(pl.program_id(1) == 0)
    def _():
        store(mk_ref)

    @pl.when(pl.program_id(1) == 1)
    def _():
        store(mv_ref)


def _proj_memkv(mem, w, tm):
    n, kdim = mem.shape
    group = M_WIDTH // LANES
    row = pl.BlockSpec((tm * group, LANES), lambda i, j: (i, 0))
    return pl.pallas_call(
        _proj_memkv_kernel,
        out_shape=[jax.ShapeDtypeStruct((n * group, LANES), F32)] * 2,
        grid=(n // tm, 2),
        in_specs=[pl.BlockSpec((tm, kdim), lambda i, j: (i, 0)),
                  pl.BlockSpec((kdim, M_WIDTH), lambda i, j: (0, j))],
        out_specs=[row, row],
        compiler_params=_params("arbitrary", "arbitrary"),
        name="proj_memkv",
    )(mem, w)


def _rotate_sublanes(acc, c, s):
    half = ROT_DIM // 2
    pieces = []
    for g in range(acc.shape[0] // A_QK_DIM):
        b = g * A_QK_DIM
        lo, hi = acc[b:b + half], acc[b + half:b + ROT_DIM]
        pieces += [lo * c - hi * s, hi * c + lo * s, acc[b + ROT_DIM:b + A_QK_DIM]]
    return jnp.concatenate(pieces, axis=0)


def _proj_qk_kernel(x_ref, wq_ref, wk_ref, cos_ref, sin_ref, qt_ref, kt_ref, kb_ref, xb_ref, *, q_scale):
    xb = x_ref[...].astype(BF16)
    xb_ref[...] = xb
    c, s = cos_ref[...], sin_ref[...]
    q = _rotate_sublanes(jnp.dot(xb, wq_ref[...].astype(BF16), preferred_element_type=F32).T, c, s)
    q = (q * q_scale).astype(qt_ref.dtype)
    tile = qt_ref.shape[-1]
    for n in range(qt_ref.shape[0]):
        qt_ref[n] = q[:, n * tile:(n + 1) * tile]
    k = _rotate_sublanes(jnp.dot(xb, wk_ref[...].astype(BF16), preferred_element_type=F32).T, c, s)
    kt_ref[...] = k
    kb_ref[...] = k.T.astype(kb_ref.dtype)


def _proj_qk(x, w, cos_t, sin_t, batch, t, tm, q_tile, q_scale):
    n, kdim = x.shape
    width = A_WIDTH
    nt = t // tm
    tab = pl.BlockSpec((ROT_DIM // 2, tm), lambda i: (0, i % nt))
    wspec = lambda col: pl.BlockSpec((kdim, width), lambda i: (0, col // width), pipeline_mode=pl.Buffered(1))
    tspec = pl.BlockSpec((None, width, tm), lambda i: (i // nt, 0, i % nt))
    qspec = pl.BlockSpec((None, tm // q_tile, width, q_tile), lambda i: (i // nt, i % nt, 0, 0))
    return pl.pallas_call(
        functools.partial(_proj_qk_kernel, q_scale=q_scale),
        out_shape=[jax.ShapeDtypeStruct((batch, t // q_tile, width, q_tile), BF16),
                   jax.ShapeDtypeStruct((batch, width, t), F32),
                   jax.ShapeDtypeStruct((n, width), BF16), jax.ShapeDtypeStruct((n, kdim), BF16)],
        grid=(n // tm,),
        in_specs=[pl.BlockSpec((tm, kdim), lambda i: (i, 0)), wspec(COL_Q), wspec(COL_K), tab, tab],
        out_specs=[qspec, tspec, pl.BlockSpec((tm, width), lambda i: (i, 0)),
                   pl.BlockSpec((tm, kdim), lambda i: (i, 0))],
        compiler_params=_params("arbitrary"),
        name="proj_qk",
    )(x, w, w, cos_t, sin_t)


def _proj_vup_kernel(x_ref, w_hbm, wu_ref, hist_ref, pw_ref, ps_ref, v_ref, vt_ref, ut_ref, bp_ref,
                     prev_ref, wz_ref, wv_ref, sem, *, key_tile, nt):
    i = pl.program_id(0) % nt
    width = wz_ref.shape[1]

    def weight_copies():
        return (pltpu.make_async_copy(w_hbm.at[:, pl.ds(COL_ZP, width)], wz_ref, sem.at[0]),
                pltpu.make_async_copy(w_hbm.at[:, pl.ds(COL_V, width)], wv_ref, sem.at[1]))

    @pl.when(pl.program_id(0) == 0)
    def _():
        for cp in weight_copies():
            cp.start()

    @pl.when(i == 0)
    def _():
        prev_ref[...] = hist_ref[...]

    xb = x_ref[...]
    tm = xb.shape[0]
    u = jnp.dot(xb, wu_ref[...].astype(BF16), preferred_element_type=F32)
    ut_ref[...] = u[tm - ut_ref.shape[0]:, :]

    @pl.when(pl.program_id(0) == 0)
    def _():
        for cp in weight_copies():
            cp.wait()

    zp = jnp.dot(xb, wz_ref[...].astype(BF16), preferred_element_type=F32)
    v = jnp.dot(xb, wv_ref[...].astype(BF16), preferred_element_type=F32)
    for h in range(A_HEADS):
        v_ref[pl.ds(h, tm, stride=A_HEADS), :] = v[:, h * A_V_DIM:(h + 1) * A_V_DIM]
    for c in range(tm // key_tile):
        vt_ref[c] = v[c * key_tile:(c + 1) * key_tile, :].T.astype(vt_ref.dtype)
    pos = i * tm + lax.broadcasted_iota(jnp.int32, (tm, 1), 0)
    bp_ref[...] = _pool_mix(u, prev_ref[...], zp, pw_ref, ps_ref, pos).astype(bp_ref.dtype)
    prev_ref[...] = u[tm - prev_ref.shape[0]:, :]


def _proj_vup(xb, w, hist, pool_w, pool_scale, batch, t, tm, key_tile):
    assert tm >= hist.shape[1]
    n, kdim = xb.shape
    width = A_WIDTH
    nt = t // tm
    hr = hist.shape[1]
    wspec = lambda col: pl.BlockSpec((kdim, width), lambda i: (0, col // width), pipeline_mode=pl.Buffered(1))
    row = pl.BlockSpec((tm, width), lambda i: (i, 0))
    return pl.pallas_call(
        functools.partial(_proj_vup_kernel, key_tile=key_tile, nt=nt),
        out_shape=[jax.ShapeDtypeStruct((n * A_HEADS, A_V_DIM), F32),
                   jax.ShapeDtypeStruct((batch, t // key_tile, width, key_tile), BF16),
                   jax.ShapeDtypeStruct((batch, hr, width), F32),
                   jax.ShapeDtypeStruct((n, width), BF16)],
        grid=(n // tm,),
        in_specs=[pl.BlockSpec((tm, kdim), lambda i: (i, 0)), pl.BlockSpec(memory_space=pl.ANY), wspec(COL_UP),
                  pl.BlockSpec((None, hr, P_WIDTH), lambda i: (i // nt, 0, 0)),
                  pl.BlockSpec((len(P_WINDOWS), P_GROUP_DIM, P_GROUP_DIM), lambda i: (0, 0, 0)),
                  pl.BlockSpec((1, P_WIDTH), lambda i: (0, 0))],
        out_specs=[pl.BlockSpec((tm * A_HEADS, A_V_DIM), lambda i: (i, 0)),
                   pl.BlockSpec((None, tm // key_tile, width, key_tile), lambda i: (i // nt, i % nt, 0, 0)),
                   pl.BlockSpec((None, hr, width), lambda i: (i // nt, 0, 0)), row],
        scratch_shapes=[pltpu.VMEM((hr, P_WIDTH), F32), pltpu.VMEM((kdim, width), F32),
                        pltpu.VMEM((kdim, width), F32), pltpu.SemaphoreType.DMA((2,))],
        compiler_params=_params("arbitrary"),
        name="proj_vup",
    )(xb, w, w, hist, pool_w, pool_scale)


def _rotary_angles(pos):
    half = ROT_DIM // 2
    inv = ROPE_THETA ** (-(jnp.arange(half, dtype=F32) * 2.0) / ROT_DIM)
    return pos.astype(F32)[:, None] * inv[None, :]


def _rotary_tables(pos):
    half = ROT_DIM // 2
    ang = _rotary_angles(pos)
    cos, sin = jnp.cos(ang), jnp.sin(ang)
    t = pos.shape[0]
    ones = jnp.ones((t, A_QK_DIM - ROT_DIM), F32)
    zeros = jnp.zeros((t, A_QK_DIM - ROT_DIM), F32)
    zh = jnp.zeros((t, half), F32)
    c64 = jnp.concatenate([cos, cos, ones], axis=1)
    sa64 = jnp.concatenate([zh, sin, zeros], axis=1)
    sb64 = jnp.concatenate([-sin, zh, zeros], axis=1)
    tile = lambda a: jnp.concatenate([a, a], axis=1)
    return tile(c64), tile(sa64), tile(sb64)


def _lambda_values(lam_ref):
    lv = lam_ref[...]
    a = jnp.sum(lv[0:1] * lv[1:2], axis=1, keepdims=True)
    b = jnp.sum(lv[2:3] * lv[3:4], axis=1, keepdims=True)
    lam_init = lv[4:5, 0:1]
    return jnp.exp(a) - jnp.exp(b) + lam_init, lam_init


def _split_components(q):
    lane = lax.broadcasted_iota(jnp.int32, q.shape, 1)
    qf = q.astype(F32)
    return jnp.concatenate([jnp.where(lane < A_QK_DIM, qf, 0.0),
                            jnp.where(lane >= A_QK_DIM, qf, 0.0)], axis=0).astype(q.dtype)


def _attn_finish(acc, l, tq, lam, gain, za):
    o = acc[:tq] / l[:tq] - lam * (acc[tq:] / l[tq:])
    o = o * lax.rsqrt(jnp.mean(o * o, axis=1, keepdims=True) + SUBLN_EPS) * gain
    return o * _silu(za)


def _attn_prompt_kernel(qt_ref, k_ref, vt_ref, za_ref, gain_ref, lam_ref, o_ref, m_ref, l_ref, acc_ref,
                        s0_ref, s1_ref, *, tq, heads, nq):
    i = pl.program_id(2)
    hs = [slice(n * LANES, (n + 1) * LANES) for n in range(heads)]

    def stacked_queries(qi):
        out = []
        for sl in hs:
            qt = qt_ref[qi, sl, :].astype(F32)
            row = lax.broadcasted_iota(jnp.int32, qt.shape, 0)
            out.append(jnp.concatenate([jnp.where(row < A_QK_DIM, qt, 0.0),
                                        jnp.where(row >= A_QK_DIM, qt, 0.0)], axis=1).astype(BF16))
        return out

    def score(j, qq):
        start = pl.multiple_of(j * tq, tq)
        sl, stacked = qq
        return jnp.dot(k_ref[pl.ds(start, tq), sl], stacked, preferred_element_type=F32)

    qqts = stacked_queries(i)

    @pl.when(i == 0)
    def _():
        for n in range(heads):
            s0_ref[n] = score(0, (hs[n], qqts[n]))

    m_ref[...] = jnp.full(m_ref.shape, -jnp.inf, F32)
    l_ref[...] = jnp.zeros(l_ref.shape, F32)
    acc_ref[...] = jnp.zeros(acc_ref.shape, F32)
    ones_rows = jnp.ones((ONES_ROWS, tq), BF16)

    def step(j, cur_ref, nxt_ref, masked):
        for n in range(heads):
            s = cur_ref[n]
            if masked:
                krow = lax.broadcasted_iota(jnp.int32, s.shape, 0)
                qcol = lax.broadcasted_iota(jnp.int32, s.shape, 1) % tq
                s = jnp.where((krow // CHUNK) <= (qcol // CHUNK), s, -jnp.inf)
            m = m_ref[n]
            m_new = jnp.maximum(m, jnp.max(s, axis=0, keepdims=True))
            alpha = jnp.exp2(m - m_new)
            p = jnp.exp2(s - m_new).astype(BF16)
            m_ref[n] = m_new
            lhs = jnp.concatenate([vt_ref[j, hs[n], :], ones_rows], axis=0)
            pv = jnp.dot(lhs, p, preferred_element_type=F32)
            acc_ref[n] = alpha * acc_ref[n] + pv[:A_V_DIM]
            l_ref[n] = alpha * l_ref[n] + pv[A_V_DIM:A_V_DIM + 1]
            if nxt_ref is not None:
                nxt_ref[n] = score(j + 1, (hs[n], qqts[n]))

    def pair(jj, carry):
        step(2 * jj, s0_ref, s1_ref, False)
        step(2 * jj + 1, s1_ref, s0_ref, False)
        return carry

    lax.fori_loop(0, i // 2, pair, 0)

    @pl.when(i % 2 == 1)
    def _():
        step(i - 1, s0_ref, s1_ref, False)
        step(i, s1_ref, None, True)

    @pl.when(i % 2 == 0)
    def _():
        step(i, s0_ref, None, True)

    nxt = stacked_queries(jnp.minimum(i + 1, nq - 1))
    lam, lam_init = _lambda_values(lam_ref)
    gain = gain_ref[...] * (1.0 - lam_init)
    for n in range(heads):
        s0_ref[n] = score(0, (hs[n], nxt[n]))
        inv, acc = 1.0 / l_ref[n], acc_ref[n]
        ot = acc[:, :tq] * inv[:, :tq] - lam * (acc[:, tq:] * inv[:, tq:])
        o = ot.T
        o = o * lax.rsqrt(jnp.mean(o * o, axis=1, keepdims=True) + SUBLN_EPS) * gain
        o_ref[:, hs[n]] = (o * _silu(za_ref[:, hs[n]].astype(F32))).astype(o_ref.dtype)


def _attn_prompt(qt, k, vt, h, gain, lamv, batch, t, tq, heads=A_HEADS):
    nq = t // tq
    n = batch * t
    hw = heads * LANES
    tiles = pl.BlockSpec((None, nq, hw, tq), lambda b, hh, i: (b, 0, hh, 0))
    return pl.pallas_call(
        functools.partial(_attn_prompt_kernel, tq=tq, heads=heads, nq=nq),
        out_shape=jax.ShapeDtypeStruct((n, A_WIDTH), BF16),
        grid=(batch, A_HEADS // heads, nq),
        in_specs=[
            tiles,
            pl.BlockSpec((t, hw), lambda b, hh, i: (b, hh)),
            tiles,
            pl.BlockSpec((tq, hw), lambda b, hh, i: (b * nq + i, H_ZA // hw + hh)),
            pl.BlockSpec((1, LANES), lambda b, hh, i: (0, 0)),
            pl.BlockSpec((5, A_QK_DIM), lambda b, hh, i: (0, 0)),
        ],
        out_specs=pl.BlockSpec((tq, hw), lambda b, hh, i: (b * nq + i, hh)),
        scratch_shapes=[pltpu.VMEM((heads, 1, 2 * tq), F32), pltpu.VMEM((heads, 1, 2 * tq), F32),
                        pltpu.VMEM((heads, A_V_DIM, 2 * tq), F32),
                        pltpu.VMEM((heads, tq, 2 * tq), F32), pltpu.VMEM((heads, tq, 2 * tq), F32)],
        compiler_params=_params("arbitrary", "arbitrary", "arbitrary"),
        name="attn_prompt",
    )(qt, k, vt, h, gain, lamv)


def _chunk_mask(s, ts, q_pos0, k_pos0):
    q_pos = q_pos0 + lax.broadcasted_iota(jnp.int32, s.shape, 0) % ts
    k_pos = k_pos0 + lax.broadcasted_iota(jnp.int32, s.shape, 1)
    return jnp.where((k_pos // CHUNK) <= (q_pos // CHUNK), s, -jnp.inf)


def _rotate_lanes(blk, c, sa, sb):
    half = ROT_DIM // 2
    return blk * c + pltpu.roll(blk, half, 1) * sa + pltpu.roll(blk, LANES - half, 1) * sb


def _mix_sample_kernel(q_ref, k_ref, v_ref, za_ref, u_ref, zp_ref, qm_ref, zm_ref, cos_ref, sa_ref, sb_ref,
                       kct_ref, vc_ref, hist_ref, mk_ref, mv_ref, gain_ref, lam_ref, pw_ref, ps_ref,
                       o_ref, krot_ref, bp_ref, bm_ref, *, ts, past, mask_cache, mask_new):
    pos = past + lax.broadcasted_iota(jnp.int32, (ts, 1), 0)
    bp_ref[...] = _pool_mix(u_ref[...], hist_ref[...], zp_ref[...], pw_ref, ps_ref, pos).astype(bp_ref.dtype)
    _mem_attend(qm_ref[...].astype(BF16), lambda: zm_ref[...], mk_ref, mv_ref, bm_ref)

    c, sa, sb = cos_ref[...], sa_ref[...], sb_ref[...]
    nt = (((1,), (1,)), ((), ()))
    hs = [slice(n * LANES, (n + 1) * LANES) for n in range(A_HEADS)]
    scores = []
    for n in range(A_HEADS):
        q = (_rotate_lanes(q_ref[:, hs[n]], c, sa, sb) * A_SCALE).astype(BF16)
        k = _rotate_lanes(k_ref[:, hs[n]], c, sa, sb)
        krot_ref[:, hs[n]] = k
        qq = _split_components(q)
        s_c = jnp.dot(qq, kct_ref[n].astype(BF16), preferred_element_type=F32)
        s_n = lax.dot_general(qq, k.astype(BF16), nt, preferred_element_type=F32)
        if mask_cache:
            s_c = _chunk_mask(s_c, ts, past, 0)
        if mask_new:
            s_n = _chunk_mask(s_n, ts, past, past)
        scores.append((s_c, s_n))
    lam, lam_init = _lambda_values(lam_ref)
    gain = gain_ref[...] * (1.0 - lam_init)
    for n, (s_c, s_n) in enumerate(scores):
        m = jnp.maximum(jnp.max(s_c, axis=1, keepdims=True), jnp.max(s_n, axis=1, keepdims=True))
        p_c = jnp.exp(s_c - m)
        p_n = jnp.exp(s_n - m)
        l = jnp.sum(p_c, axis=1, keepdims=True) + jnp.sum(p_n, axis=1, keepdims=True)
        vc = vc_ref[pl.ds(n, past, stride=A_HEADS), :].astype(BF16)
        acc = (jnp.dot(p_c.astype(BF16), vc, preferred_element_type=F32)
               + jnp.dot(p_n.astype(BF16), v_ref[:, hs[n]].astype(BF16), preferred_element_type=F32))
        out = _attn_finish(acc, l, ts, lam, gain, za_ref[:, hs[n]].astype(F32))
        o_ref[:, hs[n]] = out.astype(o_ref.dtype)


def _needs_mask(q_pos, k_pos):
    return not bool(((k_pos[None, :] // CHUNK) <= (q_pos[:, None] // CHUNK)).all())


def _mix_sample(act, rot, kct, vc, hist, mk, mv, gain, lamv, pool_w, pool_scale, batch, ts, past):
    q_pos = past + np.arange(ts)
    seg = lambda col: pl.BlockSpec((ts, A_WIDTH), lambda b: (b, col // A_WIDTH))
    tab = pl.BlockSpec((ts, LANES), lambda b: (0, 0))
    row = pl.BlockSpec((ts, A_WIDTH), lambda b: (b, 0))
    mem = pl.BlockSpec((MEM_ROWS, LANES), lambda b: (b, 0))
    bshape = jax.ShapeDtypeStruct((batch * ts, A_WIDTH), BF16)
    return pl.pallas_call(
        functools.partial(_mix_sample_kernel, ts=ts, past=past,
                          mask_cache=_needs_mask(q_pos, np.arange(past)),
                          mask_new=_needs_mask(q_pos, q_pos)),
        out_shape=[bshape, jax.ShapeDtypeStruct((batch * ts, A_WIDTH), F32), bshape, bshape],
        grid=(batch,),
        in_specs=[
            seg(COL_Q), seg(COL_K), seg(COL_V), seg(COL_ZA), seg(COL_UP), seg(COL_ZP), seg(COL_QM), seg(COL_ZM),
            tab, tab, tab,
            pl.BlockSpec((None, A_HEADS, LANES, past), lambda b: (b, 0, 0, 0)),
            pl.BlockSpec((None, past * A_HEADS, LANES), lambda b: (b, 0, 0)),
            pl.BlockSpec((None, hist.shape[1], P_WIDTH), lambda b: (b, 0, 0)),
            mem, mem,
            pl.BlockSpec((1, LANES), lambda b: (0, 0)),
            pl.BlockSpec((5, A_QK_DIM), lambda b: (0, 0)),
            pl.BlockSpec((len(P_WINDOWS), P_GROUP_DIM, P_GROUP_DIM), lambda b: (0, 0, 0)),
            pl.BlockSpec((1, P_WIDTH), lambda b: (0, 0)),
        ],
        out_specs=[row, row, row, row],
        compiler_params=_params("arbitrary"),
        name="mix_sample",
    )(*([act] * 8), *rot, kct, vc, hist, mk, mv, gain, lamv, pool_w, pool_scale)


def _pool_mix(u, prev, zp, pw_ref, ps_ref, pos):
    tm = u.shape[0]
    ext = jnp.concatenate([prev, u], axis=0)
    hist_rows = prev.shape[0]
    parts = []
    for g, w in enumerate(P_WINDOWS):
        sl = slice(g * P_GROUP_DIM, (g + 1) * P_GROUP_DIM)
        tot = ext[:, sl]
        span = 1
        while span < w:
            tot = tot + pltpu.roll(tot, span, 0)
            span *= 2
        tot = tot[hist_rows:]
        cnt = jnp.minimum(pos + 1, w).astype(F32)
        pooled = tot / cnt - u[:, sl]
        parts.append(jnp.dot(pooled.astype(BF16), pw_ref[g].astype(BF16), preferred_element_type=F32))
    mixed = jnp.concatenate(parts, axis=1) * ps_ref[...]
    return mixed * _silu(zp)


def _mem_head(ref, h):
    halves = M_HEAD_DIM // LANES
    parts = [ref[pl.ds(c * M_HEADS + h, M_TOKENS, stride=halves * M_HEADS), :] for c in range(halves)]
    return jnp.concatenate(parts, axis=1)


def _mem_attend(q, gate_input, mk_ref, mv_ref, o_ref):
    nt = (((1,), (1,)), ((), ()))
    hs = [slice(n * M_HEAD_DIM, (n + 1) * M_HEAD_DIM) for n in range(M_HEADS)]
    scores = [lax.dot_general(q[:, sl], _mem_head(mk_ref, n).astype(BF16), nt,
                              preferred_element_type=F32) * M_SCALE for n, sl in enumerate(hs)]
    zm = gate_input()
    for n, (sl, s) in enumerate(zip(hs, scores)):
        m = jnp.max(s, axis=1, keepdims=True)
        p = jnp.exp(s - m)
        l = jnp.sum(p, axis=1, keepdims=True)
        o = jnp.dot(p.astype(BF16), _mem_head(mv_ref, n).astype(BF16), preferred_element_type=F32) / l
        o_ref[:, sl] = (o * _silu(zm[:, sl])).astype(o_ref.dtype)


def _proj_mem_kernel(x_ref, wq_ref, wz_ref, mk_ref, mv_ref, xs_ref, wt_ref, o_ref, act_ref):
    act_ref[...] = jnp.dot(xs_ref[...], wt_ref[...].astype(BF16), preferred_element_type=F32)
    xb = x_ref[...]
    q = jnp.dot(xb, wq_ref[...].astype(BF16), preferred_element_type=F32).astype(BF16)
    _mem_attend(q, lambda: jnp.dot(xb, wz_ref[...].astype(BF16), preferred_element_type=F32),
                mk_ref, mv_ref, o_ref)


def _proj_mem(xb, w, mk, mv, xs, batch, t, tm, tn):
    n, kdim = xb.shape
    ns = xs.shape[0]
    nt = t // tm
    tiles = w.shape[1] // tn
    assert batch * nt >= tiles, "not enough grid steps to carry the sample projection"
    rider = lambda b, i: (0, jnp.minimum(b * nt + i, tiles - 1))
    wspec = lambda col: pl.BlockSpec((kdim, M_WIDTH), lambda b, i: (0, col // M_WIDTH),
                                     pipeline_mode=pl.Buffered(1))
    mem = pl.BlockSpec((MEM_ROWS, LANES), lambda b, i: (b, 0))
    return pl.pallas_call(
        _proj_mem_kernel,
        out_shape=[jax.ShapeDtypeStruct((n, M_WIDTH), BF16), jax.ShapeDtypeStruct((ns, w.shape[1]), F32)],
        grid=(batch, nt),
        in_specs=[pl.BlockSpec((tm, kdim), lambda b, i: (b * nt + i, 0)), wspec(COL_QM), wspec(COL_ZM), mem, mem,
                  pl.BlockSpec((ns, kdim), lambda b, i: (0, 0), pipeline_mode=pl.Buffered(1)),
                  pl.BlockSpec((kdim, tn), rider)],
        out_specs=[pl.BlockSpec((tm, M_WIDTH), lambda b, i: (b * nt + i, 0)), pl.BlockSpec((ns, tn), rider)],
        compiler_params=_params("arbitrary", "arbitrary"),
        name="proj_mem",
    )(xb, w, w, mk, mv, xs, w)


def _gated_sum(b_refs, g_refs, w_ref, rows):
    projs = [jnp.dot(b_ref[rows, :], w_ref[n], preferred_element_type=F32) for n, b_ref in enumerate(b_refs)]
    acc = None
    for g_ref, proj in zip(g_refs, projs):
        term = _sigmoid(g_ref[rows, :].astype(F32)) * proj
        acc = term if acc is None else acc + term
    return acc


def _merge_kernel(ba_ref, bp_ref, bm_ref, g0_ref, g1_ref, g2_ref, wb_ref,
                  sa_ref, sp_ref, sm_ref, sg0_ref, sg1_ref, sg2_ref, o_ref, so_ref, wbb_ref, *, halves):
    @pl.when(pl.program_id(1) == 0)
    def _():
        wbb_ref[...] = wb_ref[...].astype(BF16)
        so_ref[...] = _gated_sum((sa_ref, sp_ref, sm_ref), (sg0_ref, sg1_ref, sg2_ref), wbb_ref,
                                 slice(None)).astype(so_ref.dtype)

    b_refs, g_refs = (ba_ref, bp_ref, bm_ref), (g0_ref, g1_ref, g2_ref)
    rows = o_ref.shape[0] // halves
    projs = [[jnp.dot(b_ref[r * rows:(r + 1) * rows, :], wbb_ref[n], preferred_element_type=F32)
              for n, b_ref in enumerate(b_refs)] for r in range(halves)]
    for r in range(halves):
        sl = slice(r * rows, (r + 1) * rows)
        acc = None
        for n, g_ref in enumerate(g_refs):
            term = _sigmoid(g_ref[sl, :].astype(F32)) * projs[r][n]
            acc = term if acc is None else acc + term
        o_ref[sl, :] = acc.astype(o_ref.dtype)


def _merge(ba, bp, bm, h, g_col, sa, sp, sm, sh, sg_col, wb, tm, tn):
    n, width = ba.shape
    ns = sa.shape[0]
    gspec = lambda k: pl.BlockSpec((tm, tn), lambda j, i: (i, (g_col + k * D_MODEL) // tn + j))
    sgspec = lambda k: pl.BlockSpec((ns, tn), lambda j, i: (0, (sg_col + k * D_MODEL) // tn + j))
    bspec = pl.BlockSpec((tm, width), lambda j, i: (i, 0))
    sbspec = pl.BlockSpec((ns, width), lambda j, i: (0, 0), pipeline_mode=pl.Buffered(1))
    return pl.pallas_call(
        functools.partial(_merge_kernel, halves=2 if tm % 32 == 0 else 1),
        out_shape=[jax.ShapeDtypeStruct((n, D_MODEL), BF16), jax.ShapeDtypeStruct((ns, D_MODEL), BF16)],
        grid=(D_MODEL // tn, n // tm),
        in_specs=[bspec, bspec, bspec, gspec(0), gspec(1), gspec(2),
                  pl.BlockSpec((N_BRANCH, width, tn), lambda j, i: (0, 0, j), pipeline_mode=pl.Buffered(1)),
                  sbspec, sbspec, sbspec, sgspec(0), sgspec(1), sgspec(2)],
        out_specs=[pl.BlockSpec((tm, tn), lambda j, i: (i, j)), pl.BlockSpec((ns, tn), lambda j, i: (0, j))],
        scratch_shapes=[pltpu.VMEM((N_BRANCH, width, tn), BF16)],
        compiler_params=_params("arbitrary", "arbitrary"),
        name="merge",
    )(ba, bp, bm, h, h, h, wb, sa, sp, sm, sh, sh, sh)


def _layer_norm_residual(x, proj, g_ref, b_ref, alpha):
    z = alpha * x + proj
    mu = jnp.mean(z, axis=1, keepdims=True)
    zc = z - mu
    var = jnp.mean(zc * zc, axis=1, keepdims=True)
    return zc * lax.rsqrt(var + LN_EPS) * g_ref[...] + b_ref[...]


def _out_kernel(m_ref, w_ref, x_ref, g_ref, b_ref, sm_ref, sx_ref, o_ref, so_ref, wb_ref, *, halves, alpha):
    @pl.when(pl.program_id(0) == 0)
    def _():
        wb_ref[...] = w_ref[...].astype(BF16)
        so_ref[...] = _layer_norm_residual(
            sx_ref[...], jnp.dot(sm_ref[...], wb_ref[...], preferred_element_type=F32), g_ref, b_ref, alpha)

    rows = o_ref.shape[0] // halves
    outs = [jnp.dot(m_ref[r * rows:(r + 1) * rows, :], wb_ref[...], preferred_element_type=F32)
            for r in range(halves)]
    for r in range(halves):
        sl = slice(r * rows, (r + 1) * rows)
        o_ref[sl, :] = _layer_norm_residual(x_ref[sl, :], outs[r], g_ref, b_ref, alpha)


def _out(merged, x, s_merged, s_x, w_out, ln_g, ln_b, alpha, tm):
    n = merged.shape[0]
    ns = s_merged.shape[0]
    row = pl.BlockSpec((tm, D_MODEL), lambda i: (i, 0))
    srow = pl.BlockSpec((ns, D_MODEL), lambda i: (0, 0), pipeline_mode=pl.Buffered(1))
    vec = pl.BlockSpec((1, D_MODEL), lambda i: (0, 0))
    wspec = pl.BlockSpec((D_MODEL, D_MODEL), lambda i: (0, 0), pipeline_mode=pl.Buffered(1))
    return pl.pallas_call(
        functools.partial(_out_kernel, halves=2, alpha=alpha),
        out_shape=[jax.ShapeDtypeStruct((n, D_MODEL), F32), jax.ShapeDtypeStruct((ns, D_MODEL), F32)],
        grid=(n // tm,),
        in_specs=[row, wspec, row, vec, vec, srow, srow],
        out_specs=[row, pl.BlockSpec((ns, D_MODEL), lambda i: (0, 0))],
        scratch_shapes=[pltpu.VMEM((D_MODEL, D_MODEL), BF16)],
        compiler_params=_params("arbitrary"),
        name="out_ln",
    )(merged, w_out, x, ln_g, ln_b, s_merged, s_x)


def _rest_col_block(tn):
    za_blocks = (COL_UP - COL_ZA) // tn
    return lambda j: jnp.where(j < za_blocks, COL_ZA // tn + j, COL_GM // tn + (j - za_blocks))


def _layer(xp2d, xs2d, bp, tp, bs, ts, past, w, zero_hist, hist_s, mk, mv, kt_hist, v_hist, mk_s, mv_s):
    ang = _rotary_angles(jnp.arange(tp))
    cos_t, sin_t = jnp.cos(ang).T, jnp.sin(ang).T
    qt, kt, kb, xb = _proj_qk(xp2d, w["w_in"], cos_t, sin_t, bp, tp, ROW_TILE, KEY_TILE,
                              A_SCALE * math.log2(math.e))
    v, vt, u, b_p = _proj_vup(xb, w["w_in"], zero_hist, w["pool_w"], w["pool_scale"], bp, tp, ROW_TILE, KEY_TILE)
    b_m, act_s = _proj_mem(xb, w["w_in"], mk, mv, xs2d.astype(BF16), bp, tp, ROW_TILE, COL_TILE)
    h = _proj(xb, w["w_in"], _rest_col_block(COL_TILE), H_COLS, BF16, GATE_ROW_TILE, COL_TILE, name="proj_rest")
    b_a = _attn_prompt(qt, kb, vt, h, w["gain"], w["lamv"], bp, tp, KEY_TILE)

    rot_s = _rotary_tables(past + jnp.arange(ts))
    s_a, k_s, s_p, s_m = _mix_sample(act_s, rot_s, kt_hist, v_hist, hist_s, mk_s, mv_s, w["gain"], w["lamv"],
                                     w["pool_w"], w["pool_scale"], bs, ts, past)

    merged, s_merged = _merge(b_a, b_p, b_m, h, H_GM, s_a, s_p, s_m, act_s, COL_GM, w["w_branch"],
                              ROW_TILE, COL_TILE)
    y_p, y_s = _out(merged, xp2d, s_merged, xs2d, w["w_out"], w["ln_g"], w["ln_b"], w["alpha"], OUT_ROW_TILE)
    return y_p, y_s, kt, v, u, k_s, act_s


def kernel(x_prompt, x_sample, cache_attn_k, cache_attn_v, cache_mem_k, cache_mem_v, state_pool, mem_prompt, w_in, w_mem_kv, lambda_q1, lambda_k1, lambda_q2, lambda_k2, subln_gain, pool_w, pool_scale, w_branch, w_out, ln_gain, ln_bias):
    bp, tp, d = x_prompt.shape
    bs, ts, _ = x_sample.shape
    past = cache_attn_k.shape[2]
    hist_rows = P_HIST + 1
    halves = M_HEAD_DIM // LANES

    def to_mem_layout(a):
        a = a.reshape(a.shape[0], M_TOKENS, M_HEADS, halves, LANES)
        return jnp.transpose(a, (0, 1, 3, 2, 4)).reshape(a.shape[0] * MEM_ROWS, LANES)

    def from_mem_layout(a, batch):
        a = a.reshape(batch, M_TOKENS, halves, M_HEADS, LANES)
        return jnp.transpose(a, (0, 1, 3, 2, 4)).reshape(batch, M_TOKENS, M_HEADS, M_HEAD_DIM)

    def run_layer(l, xp2d, xs2d):
        w = {
            "w_in": w_in[l],
            "gain": subln_gain[l].reshape(1, A_V_DIM),
            "lamv": jnp.stack([lambda_q1[l], lambda_k1[l], lambda_q2[l], lambda_k2[l],
                               jnp.full((A_QK_DIM,), 0.8 - 0.6 * math.exp(-0.3 * l), F32)]),
            "pool_w": pool_w[l],
            "pool_scale": pool_scale[l].reshape(1, P_WIDTH),
            "w_branch": w_branch[l],
            "w_out": w_out[l],
            "ln_g": ln_gain[l].reshape(1, d),
            "ln_b": ln_bias[l].reshape(1, d),
            "alpha": (2.0 * w_in.shape[0]) ** 0.25,
        }
        memf = mem_prompt.reshape(bp * M_TOKENS, d)
        mk, mv = _proj_memkv(memf, w_mem_kv[l], MEM_ROW_TILE)

        zero_hist = jnp.zeros((bp, hist_rows, P_WIDTH), F32)
        hist_s = jnp.pad(state_pool[l], ((0, 0), (hist_rows - P_HIST, 0), (0, 0)))
        kt_hist = jnp.transpose(cache_attn_k[l], (0, 2, 3, 4, 1)).reshape(bs, A_HEADS, 2 * A_QK_DIM, past)
        y_p, y_s, k_p, v_p, u_p, k_s, act_s = _layer(
            xp2d, xs2d, bp, tp, bs, ts, past, w, zero_hist, hist_s, mk, mv,
            kt_hist, cache_attn_v[l].reshape(bs, past * A_HEADS, A_V_DIM),
            to_mem_layout(cache_mem_k[l]), to_mem_layout(cache_mem_v[l]))
        v_s = act_s[:, COL_V:COL_V + A_WIDTH]
        u_s = act_s[:, COL_UP:COL_UP + P_WIDTH]

        pool_p = u_p[:, -P_HIST:]
        pool_s = jnp.concatenate([hist_s, u_s.reshape(bs, ts, P_WIDTH)], axis=1)[:, -P_HIST:]
        k_p = jnp.transpose(k_p.reshape(bp, A_HEADS, 2, A_QK_DIM, tp), (0, 4, 1, 2, 3))
        outs = (k_p, v_p.reshape(bp, tp, A_HEADS, A_V_DIM),
                from_mem_layout(mk, bp), from_mem_layout(mv, bp),
                pool_p,
                k_s.reshape(bs, ts, A_HEADS, 2, A_QK_DIM), v_s.reshape(bs, ts, A_HEADS, A_V_DIM), pool_s)
        return y_p, y_s, outs

    yp = x_prompt.reshape(bp * tp, d)
    ys = x_sample.reshape(bs * ts, d)
    per_layer = []
    for l in range(w_in.shape[0]):
        yp, ys, outs = run_layer(l, yp, ys)
        per_layer.append(outs)
    stacked = [jnp.stack([o[n] for o in per_layer]) for n in range(8)]
    return (yp.reshape(bp, tp, d), ys.reshape(bs, ts, d), *stacked)
```
